```python
import jax, jax.numpy as jnp
from jax import lax
import numpy as np

D_MODEL = 1024
BATCH = 1
SEQ = 16384
DEPTH = 1

D_MIX = D_MODEL
DN_HEADS = 4
DN_HEAD_DIM = D_MODEL // 8
DN_WIDTH = DN_HEADS * DN_HEAD_DIM
CONV_K = 4
CHUNK = 64
AT_HEADS = 8
AT_HEAD_DIM = D_MODEL // 16
AT_WIDTH = AT_HEADS * AT_HEAD_DIM
PATTERNS = ((128, 1), (512, 4), (2048, 16))
Q_BLOCK = 128
ROPE_THETA = 10000.0
EPS = 1e-6
SPLIT_SIZES = (3 * DN_WIDTH, DN_WIDTH, DN_HEADS, DN_HEADS, AT_WIDTH, AT_WIDTH, AT_WIDTH, AT_WIDTH)
IN_COLS = 4 * DN_WIDTH + 2 * DN_HEADS + 4 * AT_WIDTH

kernel_name = "hybrid_deltanet_dilated_swa_adaln"


def rmsnorm(x, w):
    xf = x.astype(jnp.float32)
    xf = xf * lax.rsqrt(jnp.mean(xf * xf, axis=-1, keepdims=True) + EPS)
    return xf.astype(x.dtype) * w


def l2norm(x):
    xf = x.astype(jnp.float32)
    return xf * lax.rsqrt(jnp.sum(xf * xf, axis=-1, keepdims=True) + EPS)


def rope(x, positions):
    hd = x.shape[-1]
    half = hd // 2
    inv_freq = ROPE_THETA ** (-jnp.arange(half, dtype=jnp.float32) / half)
    ang = positions.astype(jnp.float32)[..., None] * inv_freq
    cos = jnp.cos(ang)[:, :, None, :]
    sin = jnp.sin(ang)[:, :, None, :]
    x1, x2 = x[..., :half], x[..., half:]
    out = jnp.concatenate([x1 * cos - x2 * sin, x2 * cos + x1 * sin], axis=-1)
    return out.astype(x.dtype)


def causal_short_conv(x, w):
    K = w.shape[0]
    S = x.shape[1]
    xp = jnp.pad(x, ((0, 0), (K - 1, 0), (0, 0)))
    out = xp[:, 0:S] * w[0]
    for j in range(1, K):
        out = out + xp[:, j:j + S] * w[j]
    return out


def gated_delta_rule(q, k, v, g, beta):
    B, S, H, dk = q.shape
    dv = v.shape[-1]
    nc = S // CHUNK

    def chunks4(t):
        return t.reshape(B, nc, CHUNK, H, t.shape[-1]).transpose(0, 3, 1, 2, 4)

    def chunks3(t):
        return t.reshape(B, nc, CHUNK, H).transpose(0, 3, 1, 2)

    q = chunks4(q) * (dk ** -0.5)
    k = chunks4(k)
    v = chunks4(v)
    beta = chunks3(beta)
    gc = jnp.cumsum(chunks3(g), axis=-1)

    tril = jnp.tril(jnp.ones((CHUNK, CHUNK), dtype=bool))
    strict = tril & ~jnp.eye(CHUNK, dtype=bool)
    decay_mat = jnp.exp(jnp.where(tril, gc[..., :, None] - gc[..., None, :], -jnp.inf))

    kb = k * beta[..., None]
    vb = v * beta[..., None]
    a_low = jnp.where(strict, jnp.einsum('bhncd,bhnsd->bhncs', kb, k) * decay_mat, 0.0)
    ia = a_low + jnp.eye(CHUNK, dtype=jnp.float32)
    u = lax.linalg.triangular_solve(ia, vb, left_side=True, lower=True, unit_diagonal=True)
    w = lax.linalg.triangular_solve(ia, kb * jnp.exp(gc)[..., None],
                                    left_side=True, lower=True, unit_diagonal=True)
    attn_intra = jnp.where(tril, jnp.einsum('bhncd,bhnsd->bhncs', q, k) * decay_mat, 0.0)
    q_dec = q * jnp.exp(gc)[..., None]
    k_dec = k * jnp.exp(gc[..., -1:] - gc)[..., None]
    g_last = jnp.exp(gc[..., -1])

    xs = (jnp.moveaxis(u, 2, 0), jnp.moveaxis(w, 2, 0), jnp.moveaxis(q_dec, 2, 0),
          jnp.moveaxis(k_dec, 2, 0), jnp.moveaxis(attn_intra, 2, 0), jnp.moveaxis(g_last, 2, 0))

    def step(state, inp):
        u_n, w_n, qd_n, kd_n, at_n, gl_n = inp
        v_new = u_n - jnp.einsum('bhcd,bhde->bhce', w_n, state)
        o = jnp.einsum('bhcd,bhde->bhce', qd_n, state) + jnp.einsum('bhcs,bhse->bhce', at_n, v_new)
        state = state * gl_n[..., None, None] + jnp.einsum('bhcd,bhce->bhde', kd_n, v_new)
        return state, o

    s0 = jnp.zeros((B, H, dk, dv), dtype=jnp.float32)
    _, o = lax.scan(step, s0, xs)
    return o.transpose(1, 0, 3, 2, 4).reshape(B, S, H, dv)


def strided_window_attention(q, k, v, dilation, w_sub):
    B, S, H, hd = q.shape
    L = S // dilation
    nb = -(-L // Q_BLOCK)
    Lp = nb * Q_BLOCK

    def split(t):
        t = t.reshape(B, L, dilation, H, hd).transpose(0, 2, 1, 3, 4)
        return jnp.pad(t, ((0, 0), (0, 0), (0, Lp - L), (0, 0), (0, 0)))

    def band(t):
        tp = jnp.pad(t, ((0, 0), (0, 0), (Q_BLOCK, 0), (0, 0), (0, 0)))
        tp = tp.reshape(B, dilation, nb + 1, Q_BLOCK, H, hd)
        return jnp.concatenate([tp[:, :, :-1], tp[:, :, 1:]], axis=3)

    qb = split(q).reshape(B, dilation, nb, Q_BLOCK, H, hd)
    kb = band(split(k))
    vb = band(split(v))

    s = jnp.einsum('bdnqhe,bdnkhe->bdnhqk', qb, kb).astype(jnp.float32) * (hd ** -0.5)
    qi = jnp.arange(Q_BLOCK)[:, None]
    kj = jnp.arange(2 * Q_BLOCK)[None, :]
    rel = Q_BLOCK + qi - kj
    kidx = jnp.arange(nb)[:, None] * Q_BLOCK - Q_BLOCK + jnp.arange(2 * Q_BLOCK)[None, :]
    mask = ((rel >= 0) & (rel <= w_sub))[None] & (kidx >= 0)[:, None, :]
    s = jnp.where(mask[None, None, :, None], s, -jnp.inf)
    m = jnp.max(s, axis=-1, keepdims=True)
    p = jnp.exp(s - m)
    l = jnp.sum(p, axis=-1, keepdims=True)
    o = jnp.einsum('bdnhqk,bdnkhe->bdnqhe', p / l, vb.astype(jnp.float32))
    lse = (m + jnp.log(l))[..., 0]

    o = o.reshape(B, dilation, Lp, H, hd)[:, :, :L].transpose(0, 2, 1, 3, 4).reshape(B, S, H, hd)
    lse = lse.transpose(0, 1, 2, 4, 3).reshape(B, dilation, Lp, H)[:, :, :L]
    lse = lse.transpose(0, 2, 1, 3).reshape(B, S, H)
    return o, lse


def dilated_attention(q, k, v):
    outs, lses = [], []
    for window, dilation in PATTERNS:
        o, lse = strided_window_attention(q, k, v, dilation, window // dilation)
        outs.append(o)
        lses.append(lse)
    wts = jax.nn.softmax(jnp.stack(lses, axis=0), axis=0)
    return jnp.sum(wts[..., None] * jnp.stack(outs, axis=0), axis=0)


def setup_inputs(seed: int = 0) -> dict:
    key = jax.random.key(seed)
    ks = jax.random.split(key, 16)
    f32 = jnp.float32
    x = jax.random.normal(ks[0], (BATCH, SEQ, D_MODEL), f32)
    c = jax.random.normal(ks[1], (BATCH, D_MODEL), f32)
    positions = jnp.broadcast_to(jnp.arange(SEQ, dtype=jnp.int32)[None], (BATCH, SEQ))
    w_mod = jax.random.normal(ks[2], (DEPTH, D_MODEL, 3 * D_MODEL), f32) * (0.2 * D_MODEL ** -0.5)
    b_mod = jax.random.normal(ks[3], (DEPTH, 3 * D_MODEL), f32) * 0.01
    norm_w = 1.0 + 0.01 * jax.random.normal(ks[4], (DEPTH, D_MODEL), f32)
    w_in = jax.random.normal(ks[5], (DEPTH, D_MODEL, IN_COLS), f32) * (D_MODEL ** -0.5)
    conv_w = jax.random.normal(ks[6], (DEPTH, CONV_K, 3 * DN_WIDTH), f32) * (CONV_K ** -0.5)
    a_log = jnp.log(jax.random.uniform(ks[7], (DEPTH, DN_HEADS), f32, 1.0, 16.0))
    dt = jnp.exp(jax.random.uniform(ks[8], (DEPTH, DN_HEADS), f32, jnp.log(1e-3), jnp.log(1e-1)))
    dt_bias = dt + jnp.log(-jnp.expm1(-dt))
    dn_norm_w = 1.0 + 0.01 * jax.random.normal(ks[9], (DEPTH, DN_HEAD_DIM), f32)
    at_norm_w = 1.0 + 0.01 * jax.random.normal(ks[10], (DEPTH, AT_HEAD_DIM), f32)
    w_out = jax.random.normal(ks[11], (DEPTH, D_MIX, D_MODEL), f32) * (D_MIX ** -0.5)
    final_norm_w = 1.0 + 0.01 * jax.random.normal(ks[12], (D_MODEL,), f32)
    return {"x": x, "c": c, "positions": positions, "w_mod": w_mod, "b_mod": b_mod,
            "norm_w": norm_w, "w_in": w_in, "conv_w": conv_w, "a_log": a_log,
            "dt_bias": dt_bias, "dn_norm_w": dn_norm_w, "at_norm_w": at_norm_w,
            "w_out": w_out, "final_norm_w": final_norm_w}


def reference(x, c, positions, w_mod, b_mod, norm_w, w_in, conv_w, a_log, dt_bias,
              dn_norm_w, at_norm_w, w_out, final_norm_w):
    B, S, _ = x.shape
    offsets = tuple(int(o) for o in np.cumsum(SPLIT_SIZES)[:-1])
    for layer in range(DEPTH):
        mod = jax.nn.silu(c) @ w_mod[layer] + b_mod[layer]
        shift, scale, gate = jnp.split(mod, 3, axis=-1)
        h = rmsnorm(x, norm_w[layer]) * (1.0 + scale[:, None]) + shift[:, None]

        proj = h @ w_in[layer]
        dn_qkv, dn_z, dn_b, dn_a, at_q, at_k, at_v, at_z = jnp.split(proj, offsets, axis=-1)

        dn_qkv = jax.nn.silu(causal_short_conv(dn_qkv, conv_w[layer]))
        dq, dk_, dvv = jnp.split(dn_qkv, 3, axis=-1)
        dq = l2norm(dq.reshape(B, S, DN_HEADS, DN_HEAD_DIM))
        dk_ = l2norm(dk_.reshape(B, S, DN_HEADS, DN_HEAD_DIM))
        dvv = dvv.reshape(B, S, DN_HEADS, DN_HEAD_DIM).astype(jnp.float32)
        beta = jax.nn.sigmoid(dn_b.astype(jnp.float32))
        g = -jnp.exp(a_log[layer].astype(jnp.float32)) * jax.nn.softplus(
            (dn_a + dt_bias[layer]).astype(jnp.float32))
        o_dn = gated_delta_rule(dq, dk_, dvv, g, beta).astype(x.dtype)
        o_dn = rmsnorm(o_dn, dn_norm_w[layer]) * jax.nn.silu(dn_z.reshape(B, S, DN_HEADS, DN_HEAD_DIM))
        o_dn = o_dn.reshape(B, S, DN_WIDTH)

        aq = rope(at_q.reshape(B, S, AT_HEADS, AT_HEAD_DIM), positions)
        ak = rope(at_k.reshape(B, S, AT_HEADS, AT_HEAD_DIM), positions)
        av = at_v.reshape(B, S, AT_HEADS, AT_HEAD_DIM)
        o_at = dilated_attention(aq, ak, av).astype(x.dtype)
        o_at = rmsnorm(o_at, at_norm_w[layer]) * jax.nn.silu(at_z.reshape(B, S, AT_HEADS, AT_HEAD_DIM))
        o_at = o_at.reshape(B, S, AT_WIDTH)

        mix = jnp.concatenate([o_dn, o_at], axis=-1) @ w_out[layer]
        x = x + gate[:, None] * mix
    return rmsnorm(x, final_norm_w)
```

```python
import functools
import math

import numpy as np
import jax
import jax.numpy as jnp
from jax import lax
from jax.experimental import pallas as pl
from jax.experimental.pallas import tpu as pltpu

D_MODEL = 1024
DN_HEADS = 4
DN_HD = 128
DN_W = DN_HEADS * DN_HD
AT_HEADS = 8
AT_HD = 64
AT_W = AT_HEADS * AT_HD
CONV_K = 4
CHUNK = 64
DILATIONS = (1, 4, 16)
W_SUB = 128
Q_BLOCK = 128
ROPE_THETA = 10000.0
EPS = 1e-6
NEG = -1e30

LANES = 128
SUBLANES = 8
VMEM_LIMIT = 48 * 1024 * 1024

C_QKV, C_Z, C_AQ, C_AK, C_AV, C_AZ, C_BA = 0, 1536, 2048, 2560, 3072, 3584, 4096
W_COLS = C_BA + LANES

F32 = jnp.float32
BF16 = jnp.bfloat16
HI = lax.Precision.HIGHEST
NT_DIMS = (((1,), (1,)), ((), ()))


def _sigmoid(v):
    return 1.0 / (1.0 + jnp.exp(-v))


def _iota(shape, dim):
    return lax.broadcasted_iota(jnp.int32, shape, dim)


def _params(sem):
    return pltpu.CompilerParams(dimension_semantics=sem, vmem_limit_bytes=VMEM_LIMIT)


def _mod_body(c_ref, w_ref, b_ref, o_ref):
    c = c_ref[...]
    sc = c * _sigmoid(c)
    o_ref[...] = jnp.sum(sc * w_ref[...], axis=0, keepdims=True) + b_ref[...]


def _mod_call(c_col, w_mod, b_mod):
    d, n = w_mod.shape
    tn = 512
    return pl.pallas_call(
        _mod_body,
        out_shape=jax.ShapeDtypeStruct((1, n), F32),
        grid=(n // tn,),
        in_specs=[pl.BlockSpec((d, 1), lambda j: (0, 0)),
                  pl.BlockSpec((d, tn), lambda j: (0, j)),
                  pl.BlockSpec((1, tn), lambda j: (0, j))],
        out_specs=pl.BlockSpec((1, tn), lambda j: (0, j)),
        compiler_params=_params(("arbitrary",)),
        name="mod",
    )(c_col, w_mod, b_mod)


def _proj_body(x_ref, pos_ref, nw_ref, sc_ref, sh_ref, w_ref, cw_ref, alog_ref, dtb_ref, invf_ref,
               sgn_ref, qn_ref, kn_ref, v_ref, zdn_ref, bg_ref, aq_ref, ak_ref, av_ref, zat_ref,
               cbuf_ref):
    i = pl.program_id(0)
    tm = x_ref.shape[0]
    x = x_ref[...]
    ms = jnp.mean(x * x, axis=-1, keepdims=True)
    h = (x * lax.rsqrt(ms + EPS)) * (nw_ref[...] * (1.0 + sc_ref[...])) + sh_ref[...]
    hb = h.astype(BF16)

    def proj(c0, width):
        return jnp.dot(hb, w_ref[:, c0:c0 + width], preferred_element_type=F32)

    @pl.when(i == 0)
    def _():
        cbuf_ref[0:SUBLANES, :] = jnp.zeros((SUBLANES, 3 * DN_W), F32)

    cbuf_ref[SUBLANES:SUBLANES + tm, :] = proj(C_QKV, 3 * DN_W)
    base = SUBLANES - (CONV_K - 1)
    conv = cbuf_ref[base:base + tm, :] * cw_ref[0:1, :]
    for j in range(1, CONV_K):
        conv = conv + cbuf_ref[base + j:base + j + tm, :] * cw_ref[j:j + 1, :]
    cbuf_ref[0:SUBLANES, :] = cbuf_ref[tm:tm + SUBLANES, :]
    y = conv * _sigmoid(conv)
    for hd in range(DN_HEADS):
        sl = slice(hd * DN_HD, (hd + 1) * DN_HD)
        qh = y[:, sl]
        kh = y[:, DN_W + hd * DN_HD:DN_W + (hd + 1) * DN_HD]
        qs = lax.rsqrt(jnp.sum(qh * qh, axis=-1, keepdims=True) + EPS) * (DN_HD ** -0.5)
        ks = lax.rsqrt(jnp.sum(kh * kh, axis=-1, keepdims=True) + EPS)
        qn_ref[:, sl] = (qh * qs).astype(BF16)
        kn_ref[:, sl] = (kh * ks).astype(BF16)
    v_ref[...] = y[:, 2 * DN_W:3 * DN_W].astype(BF16)

    z = proj(C_Z, DN_W)
    zdn_ref[...] = (z * _sigmoid(z)).astype(BF16)

    ba = proj(C_BA, LANES)
    lane = _iota(ba.shape, 1)
    a = ba + dtb_ref[...]
    softplus = jnp.maximum(a, 0.0) + jnp.log(1.0 + jnp.exp(-jnp.abs(a)))
    g = -jnp.exp(alog_ref[...]) * softplus
    bg_ref[...] = jnp.where(lane < DN_HEADS, _sigmoid(ba), jnp.where(lane < 2 * DN_HEADS, g, 0.0))

    ang = invf_ref[...] * pos_ref[...].astype(F32)
    cos_t = jnp.cos(ang)
    sin_t = jnp.sin(ang)
    reps = LANES // (AT_HD // 2)
    cos = jnp.concatenate([cos_t] * reps, axis=0).T
    sin = jnp.concatenate([sin_t] * reps, axis=0).T * sgn_ref[...]
    qscale = (AT_HD ** -0.5) * math.log2(math.e)
    aq = proj(C_AQ, AT_W)
    ak = proj(C_AK, AT_W)
    for gp in range(AT_W // LANES):
        sl = slice(gp * LANES, (gp + 1) * LANES)
        xq = aq[:, sl]
        xk = ak[:, sl]
        rq = xq * cos + pltpu.roll(xq, LANES // 2, axis=1) * sin
        rk = xk * cos + pltpu.roll(xk, LANES // 2, axis=1) * sin
        aq_ref[:, sl] = (rq * qscale).astype(BF16)
        ak_ref[:, sl] = rk.astype(BF16)
    av_ref[...] = proj(C_AV, AT_W).astype(BF16)
    az = proj(C_AZ, AT_W)
    zat_ref[...] = (az * _sigmoid(az)).astype(BF16)


def _proj_call(x2, pos_row, nw, scale, shift, w_all, cw, alog_row, dtb_row, invf, sgn, tm):
    s, d = x2.shape
    row = lambda n: pl.BlockSpec((1, n), lambda i: (0, 0))
    full = lambda a: pl.BlockSpec(a.shape, lambda i: (0, 0))
    tile = lambda n: pl.BlockSpec((tm, n), lambda i: (i, 0))
    outs = [jax.ShapeDtypeStruct((s, DN_W), BF16)] * 4 + [jax.ShapeDtypeStruct((s, LANES), F32)] + \
           [jax.ShapeDtypeStruct((s, AT_W), BF16)] * 4
    out_specs = [tile(DN_W)] * 4 + [tile(LANES)] + [tile(AT_W)] * 4
    return pl.pallas_call(
        _proj_body,
        out_shape=outs,
        grid=(s // tm,),
        in_specs=[tile(d), pl.BlockSpec((1, tm), lambda i: (0, i)), row(d), row(d), row(d),
                  full(w_all), full(cw), row(LANES), row(LANES), full(invf), row(LANES)],
        out_specs=out_specs,
        scratch_shapes=[pltpu.VMEM((tm + SUBLANES, 3 * DN_W), F32)],
        compiler_params=_params(("arbitrary",)),
        name="proj",
    )(x2, pos_row, nw, scale, shift, w_all, cw, alog_row, dtb_row, invf, sgn)


def _dn_prep_body(q_ref, k_ref, v_ref, bg_ref, pq_ref, bm_ref, op_ref, gl_ref):
    rows = q_ref.shape[0]
    nchunk = rows // CHUNK
    bg = bg_ref[...]
    ri = _iota((rows, rows), 0)
    ci = _iota((rows, rows), 1)
    tril_blk = jnp.where((ri // CHUNK == ci // CHUNK) & (ri >= ci), 1.0, 0.0).astype(F32)
    gcum = jnp.dot(tril_blk, bg, precision=HI, preferred_element_type=F32)
    li = _iota((LANES, DN_W), 0)
    hi_ = _iota((LANES, DN_W), 1) // DN_HD
    e_beta = jnp.where(li == hi_, 1.0, 0.0).astype(F32)
    e_g = jnp.where(li == hi_ + DN_HEADS, 1.0, 0.0).astype(F32)
    bfull = jnp.dot(bg, e_beta, precision=HI, preferred_element_type=F32)
    gfull = jnp.dot(gcum, e_g, precision=HI, preferred_element_type=F32)
    eg = jnp.exp(gfull)
    q = q_ref[...].astype(F32)
    k = k_ref[...].astype(F32)
    kb = k * bfull
    vb = v_ref[...].astype(F32) * bfull
    q_dec = q * eg
    kbg = kb * eg

    sel = jnp.where(_iota((DN_HEADS * CHUNK, LANES), 1) == _iota((DN_HEADS * CHUNK, LANES), 0) // CHUNK + DN_HEADS,
                    1.0, 0.0).astype(F32)
    r64 = _iota((CHUNK, CHUNK), 0)
    c64 = _iota((CHUNK, CHUNK), 1)
    tril = r64 >= c64
    strict = r64 > c64
    eye = jnp.where(r64 == c64, 1.0, 0.0).astype(F32)

    for c in range(nchunk):
        rs = slice(c * CHUNK, (c + 1) * CHUNK)
        g_last = gfull[c * CHUNK + CHUNK - 1:(c + 1) * CHUNK, :]
        k_dec = k[rs] * jnp.exp(g_last - gfull[rs])
        gl_rows = []
        r_all = lax.dot_general(sel, gcum[rs], NT_DIMS, precision=HI, preferred_element_type=F32)
        for hd in range(DN_HEADS):
            hs = slice(hd * DN_HD, (hd + 1) * DN_HD)
            diff = gfull[rs, hd * DN_HD:hd * DN_HD + CHUNK] - r_all[hd * CHUNK:(hd + 1) * CHUNK]
            dec = jnp.exp(jnp.where(tril, diff, NEG))
            kh = k_ref[rs, hs]
            stacked = jnp.concatenate([kb[rs, hs].astype(BF16), q_ref[rs, hs]], axis=0)
            kq = lax.dot_general(stacked, kh, NT_DIMS, preferred_element_type=F32)
            a_low = jnp.where(strict, kq[0:CHUNK] * dec, 0.0)
            attn = jnp.where(tril, kq[CHUNK:2 * CHUNK] * dec, 0.0)
            xinv = eye - a_low
            pw = a_low
            for _ in range(5):
                pwb = pw.astype(BF16)
                pw = jnp.dot(pwb, pwb, preferred_element_type=F32)
                xinv = xinv + jnp.dot(xinv.astype(BF16), pw.astype(BF16), preferred_element_type=F32)
            rhs = jnp.concatenate([kbg[rs, hs], vb[rs, hs]], axis=1).astype(BF16)
            wu = jnp.dot(xinv.astype(BF16), rhs, preferred_element_type=F32).astype(BF16)
            kdt = k_dec[:, hs].T.astype(BF16)
            pb = jnp.dot(kdt, wu, preferred_element_type=F32)
            aw = jnp.dot(attn.astype(BF16), wu, preferred_element_type=F32)
            pq_ref[c, hd, 0:DN_HD, :] = pb[:, 0:DN_HD].astype(BF16)
            pq_ref[c, hd, DN_HD:DN_HD + CHUNK, :] = (q_dec[rs, hs] - aw[:, 0:DN_HD]).astype(BF16)
            bm_ref[c, hd] = pb[:, DN_HD:2 * DN_HD]
            op_ref[rs, hs] = aw[:, DN_HD:2 * DN_HD]
            gl_rows.append(jnp.exp(g_last[:, hs]))
        gl_rows.append(jnp.zeros((SUBLANES - DN_HEADS, DN_HD), F32))
        gl_ref[c] = jnp.concatenate(gl_rows, axis=0)


def _dn_prep_call(qn, kn, v, bg, rows):
    s = qn.shape[0]
    nct = s // CHUNK
    ncb = rows // CHUNK
    tile = lambda n: pl.BlockSpec((rows, n), lambda i: (i, 0))
    return pl.pallas_call(
        _dn_prep_body,
        out_shape=[jax.ShapeDtypeStruct((nct, DN_HEADS, DN_HD + CHUNK, DN_HD), BF16),
                   jax.ShapeDtypeStruct((nct, DN_HEADS, DN_HD, DN_HD), F32),
                   jax.ShapeDtypeStruct((s, DN_W), F32),
                   jax.ShapeDtypeStruct((nct, SUBLANES, DN_HD), F32)],
        grid=(s // rows,),
        in_specs=[tile(DN_W), tile(DN_W), tile(DN_W), tile(LANES)],
        out_specs=[pl.BlockSpec((ncb, DN_HEADS, DN_HD + CHUNK, DN_HD), lambda i: (i, 0, 0, 0)),
                   pl.BlockSpec((ncb, DN_HEADS, DN_HD, DN_HD), lambda i: (i, 0, 0, 0)),
                   tile(DN_W),
                   pl.BlockSpec((ncb, SUBLANES, DN_HD), lambda i: (i, 0, 0))],
        compiler_params=_params(("arbitrary",)),
        name="dn_prep",
    )(qn, kn, v, bg)


def _dn_scan_body(pq_ref, bm_ref, op_ref, gl_ref, z_ref, nw_ref, o_ref, s_ref):
    ncb = pq_ref.shape[0]

    @pl.when(pl.program_id(0) == 0)
    def _():
        s_ref[...] = jnp.zeros(s_ref.shape, F32)

    def chunk(c, carry):
        rs = pl.ds(pl.multiple_of(c * CHUNK, CHUNK), CHUNK)
        for hd in range(DN_HEADS):
            hs = slice(hd * DN_HD, (hd + 1) * DN_HD)
            state = s_ref[hd]
            r = jnp.dot(pq_ref[c, hd], state.astype(BF16), preferred_element_type=F32)
            o = r[DN_HD:DN_HD + CHUNK] + op_ref[rs, hs]
            s_ref[hd] = gl_ref[c, hd:hd + 1, :] * state - r[0:DN_HD] + bm_ref[c, hd]
            ms = jnp.mean(o * o, axis=-1, keepdims=True)
            o_ref[rs, hs] = (o * lax.rsqrt(ms + EPS) * nw_ref[...] * z_ref[rs, hs].astype(F32)).astype(BF16)
        return carry

    lax.fori_loop(0, ncb, chunk, 0)


def _dn_scan_call(pq, bm, op, gl, zdn, nw_row, ncb):
    nct = pq.shape[0]
    s = op.shape[0]
    rows = ncb * CHUNK
    return pl.pallas_call(
        _dn_scan_body,
        out_shape=jax.ShapeDtypeStruct((s, DN_W), BF16),
        grid=(nct // ncb,),
        in_specs=[pl.BlockSpec((ncb, DN_HEADS, DN_HD + CHUNK, DN_HD), lambda i: (i, 0, 0, 0)),
                  pl.BlockSpec((ncb, DN_HEADS, DN_HD, DN_HD), lambda i: (i, 0, 0, 0)),
                  pl.BlockSpec((rows, DN_W), lambda i: (i, 0)),
                  pl.BlockSpec((ncb, SUBLANES, DN_HD), lambda i: (i, 0, 0)),
                  pl.BlockSpec((rows, DN_W), lambda i: (i, 0)),
                  pl.BlockSpec((1, DN_HD), lambda i: (0, 0))],
        out_specs=pl.BlockSpec((rows, DN_W), lambda i: (i, 0)),
        scratch_shapes=[pltpu.VMEM((DN_HEADS, DN_HD, DN_HD), F32)],
        compiler_params=_params(("arbitrary",)),
        name="dn_scan",
    )(pq, bm, op, gl, zdn, nw_row)


def _attn_body(q_ref, kp_ref, kc_ref, vp_ref, vc_ref, acc_ref, ml_ref):
    n = pl.program_id(1)
    qi = _iota((Q_BLOCK, 2 * Q_BLOCK), 0)
    kj = _iota((Q_BLOCK, 2 * Q_BLOCK), 1)
    rel = Q_BLOCK + qi - kj
    valid = (rel >= 0) & (rel <= W_SUB) & ((kj >= Q_BLOCK) | (n > 0))
    bias = jnp.where(valid, 0.0, NEG).astype(F32)
    lane_q = _iota((Q_BLOCK, LANES), 1)
    lane_v = _iota((2 * Q_BLOCK, LANES), 1)
    ml = jnp.zeros((Q_BLOCK, LANES), F32)
    for gp in range(AT_W // LANES):
        sl = slice(gp * LANES, (gp + 1) * LANES)
        qg = q_ref[:, sl]
        kband = jnp.concatenate([kp_ref[:, sl], kc_ref[:, sl]], axis=0)
        vband = jnp.concatenate([vp_ref[:, sl], vc_ref[:, sl]], axis=0)
        res = []
        for hh in range(2):
            qm = jnp.where((lane_q % AT_HD) // (AT_HD // 2) == hh, qg, jnp.zeros_like(qg))
            s = lax.dot_general(qm, kband, NT_DIMS, preferred_element_type=F32) + bias
            m = jnp.max(s, axis=-1, keepdims=True)
            p = jnp.exp2(s - m)
            vaug = jnp.where(lane_v // AT_HD == hh, vband, jnp.ones_like(vband))
            r = jnp.dot(p.astype(BF16), vaug, preferred_element_type=F32)
            res.append((m, r))
        (m0, r0), (m1, r1) = res
        acc_ref[:, sl] = jnp.where(lane_q < AT_HD, r0, r1)
        h0 = 2 * gp
        ml = jnp.where(lane_q == h0, m0, ml)
        ml = jnp.where(lane_q == h0 + 1, m1, ml)
        ml = jnp.where(lane_q == AT_HEADS + h0, pltpu.roll(r0, AT_HD, axis=1), ml)
        ml = jnp.where(lane_q == AT_HEADS + h0 + 1, r1, ml)
    ml_ref[...] = ml


def _attn_call(aq, ak, av, dil):
    s = aq.shape[0]
    sub = s // dil
    nb = sub // Q_BLOCK
    view = lambda a: a.reshape(sub, dil * a.shape[1])
    cur = lambda w: pl.BlockSpec((Q_BLOCK, w), lambda r, n: (n, r))
    prev = lambda w: pl.BlockSpec((Q_BLOCK, w), lambda r, n: (jnp.maximum(n - 1, 0), r))
    acc, ml = pl.pallas_call(
        _attn_body,
        out_shape=[jax.ShapeDtypeStruct((sub, dil * AT_W), F32),
                   jax.ShapeDtypeStruct((sub, dil * LANES), F32)],
        grid=(dil, nb),
        in_specs=[cur(AT_W), prev(AT_W), cur(AT_W), prev(AT_W), cur(AT_W)],
        out_specs=[cur(AT_W), cur(LANES)],
        compiler_params=_params(("arbitrary", "arbitrary")),
        name=f"attn_d{dil}",
    )(view(aq), view(ak), view(ak), view(av), view(av))
    return acc.reshape(s, AT_W), ml.reshape(s, LANES)


def _out_body(x_ref, odn_ref, a1_ref, a2_ref, a3_ref, m1_ref, m2_ref, m3_ref, z_ref, gate_ref, w_ref,
              anw_ref, fnw_ref, o_ref):
    li = _iota((LANES, AT_W), 0)
    hi_ = _iota((LANES, AT_W), 1) // AT_HD
    e_m = jnp.where(li == hi_, 1.0, 0.0).astype(F32)
    e_l = jnp.where(li == hi_ + AT_HEADS, 1.0, 0.0).astype(F32)
    mls = [r[...] for r in (m1_ref, m2_ref, m3_ref)]
    ms_ = [jnp.dot(v, e_m, precision=HI, preferred_element_type=F32) for v in mls]
    ls_ = [jnp.dot(v, e_l, precision=HI, preferred_element_type=F32) for v in mls]
    mx = jnp.maximum(jnp.maximum(ms_[0], ms_[1]), ms_[2])
    num = jnp.zeros_like(mx)
    den = jnp.zeros_like(mx)
    for a_ref, mp, lp in zip((a1_ref, a2_ref, a3_ref), ms_, ls_):
        wgt = jnp.exp2(mp - mx)
        num = num + wgt * a_ref[...]
        den = den + wgt * lp
    o = num / den
    bi = _iota((AT_W, AT_W), 0) // AT_HD
    bj = _iota((AT_W, AT_W), 1) // AT_HD
    head_mean = jnp.where(bi == bj, 1.0 / AT_HD, 0.0).astype(F32)
    msq = jnp.dot(o * o, head_mean, precision=HI, preferred_element_type=F32)
    o_at = o * lax.rsqrt(msq + EPS) * anw_ref[...] * z_ref[...].astype(F32)
    mixin = jnp.concatenate([odn_ref[...], o_at.astype(BF16)], axis=1)
    mix = jnp.dot(mixin, w_ref[...], preferred_element_type=F32)
    y = x_ref[...] + gate_ref[...] * mix
    ms = jnp.mean(y * y, axis=-1, keepdims=True)
    o_ref[...] = y * lax.rsqrt(ms + EPS) * fnw_ref[...]


def _out_call(x2, odn, accs, mls, zat, gate, w_out, anw_row, fnw_row, tm):
    s, d = x2.shape
    tile = lambda n: pl.BlockSpec((tm, n), lambda i: (i, 0))
    row = lambda n: pl.BlockSpec((1, n), lambda i: (0, 0))
    return pl.pallas_call(
        _out_body,
        out_shape=jax.ShapeDtypeStruct((s, d), F32),
        grid=(s // tm,),
        in_specs=[tile(d), tile(DN_W)] + [tile(AT_W)] * 3 + [tile(LANES)] * 3 +
                 [tile(AT_W), row(d), pl.BlockSpec(w_out.shape, lambda i: (0, 0)), row(AT_W), row(d)],
        out_specs=tile(d),
        compiler_params=_params(("arbitrary",)),
        name="out",
    )(x2, odn, *accs, *mls, zat, gate, w_out, anw_row, fnw_row)


def _rope_pair_perm():
    half = AT_HD // 2
    idx = []
    for gp in range(AT_W // LANES):
        for l in range(LANES):
            part, hh, j = l // AT_HD, (l % AT_HD) // half, l % half
            idx.append((2 * gp + hh) * AT_HD + part * half + j)
    return np.asarray(idx, np.int32)


def kernel(x, c, positions, w_mod, b_mod, norm_w, w_in, conv_w, a_log, dt_bias, dn_norm_w, at_norm_w,
           w_out, final_norm_w):
    b, s, d = x.shape
    assert b == 1 and d == D_MODEL and w_mod.shape[0] == 1
    x2 = x.reshape(s, d)

    mod = _mod_call(c.reshape(d, 1), w_mod[0], b_mod[0].reshape(1, 3 * d))
    shift, scale, gate = mod[:, 0:d], mod[:, d:2 * d], mod[:, 2 * d:3 * d]

    wi = w_in[0]
    o_z, o_b, o_a, o_aq = 3 * DN_W, 4 * DN_W, 4 * DN_W + DN_HEADS, 4 * DN_W + 2 * DN_HEADS
    perm = _rope_pair_perm()
    w_aq = wi[:, o_aq:o_aq + AT_W][:, perm]
    w_ak = wi[:, o_aq + AT_W:o_aq + 2 * AT_W][:, perm]
    w_ba = jnp.pad(wi[:, o_b:o_b + 2 * DN_HEADS], ((0, 0), (0, LANES - 2 * DN_HEADS)))
    w_all = jnp.concatenate([wi[:, 0:o_z], wi[:, o_z:o_b], w_aq, w_ak,
                             wi[:, o_aq + 2 * AT_W:o_aq + 4 * AT_W], w_ba], axis=1).astype(BF16)
    assert w_all.shape[1] == W_COLS
    pad_heads = lambda v: jnp.pad(v.reshape(1, DN_HEADS), ((0, 0), (DN_HEADS, LANES - 2 * DN_HEADS)))
    half = AT_HD // 2
    invf = (ROPE_THETA ** (-jnp.arange(half, dtype=F32) / half)).reshape(half, 1)
    sgn = jnp.where(jnp.arange(LANES) < LANES // 2, -1.0, 1.0).astype(F32).reshape(1, LANES)

    qn, kn, v, zdn, bg, aq, ak, av, zat = _proj_call(
        x2, positions.reshape(1, s), norm_w[0].reshape(1, d), scale, shift, w_all, conv_w[0],
        pad_heads(a_log[0]), pad_heads(dt_bias[0]), invf, sgn, tm=256)

    pq, bm, op, gl = _dn_prep_call(qn, kn, v, bg, rows=256)
    odn = _dn_scan_call(pq, bm, op, gl, zdn, dn_norm_w[0].reshape(1, DN_HD), ncb=8)

    accs, mls = [], []
    for dil in DILATIONS:
        acc, ml = _attn_call(aq, ak, av, dil)
        accs.append(acc)
        mls.append(ml)

    out = _out_call(x2, odn, accs, mls, zat, gate, w_out[0].astype(BF16),
                    jnp.tile(at_norm_w[0], AT_HEADS).reshape(1, AT_W), final_norm_w.reshape(1, d), tm=256)
    return out.reshape(b, s, d)
```

```python
import functools
import math

import numpy as np
import jax
import jax.numpy as jnp
from jax import lax
from jax.experimental import pallas as pl
from jax.experimental.pallas import tpu as pltpu

D_MODEL = 1024
DN_HEADS = 4
DN_HD = 128
DN_W = DN_HEADS * DN_HD
AT_HEADS = 8
AT_HD = 64
AT_W = AT_HEADS * AT_HD
CONV_K = 4
CHUNK = 64
DILATIONS = (1, 4, 16)
W_SUB = 128
Q_BLOCK = 128
ROPE_THETA = 10000.0
EPS = 1e-6
NEG = -1e30

LANES = 128
SUBLANES = 8
VMEM_LIMIT = 48 * 1024 * 1024

C_QKV, C_Z, C_AQ, C_AK, C_AV, C_AZ, C_BA = 0, 1536, 2048, 2560, 3072, 3584, 4096
W_COLS = C_BA + LANES

F32 = jnp.float32
BF16 = jnp.bfloat16
HI = lax.Precision.HIGHEST
NT_DIMS = (((1,), (1,)), ((), ()))


def _sigmoid(v):
    return 1.0 / (1.0 + jnp.exp(-v))


def _iota(shape, dim):
    return lax.broadcasted_iota(jnp.int32, shape, dim)


def _params(sem):
    return pltpu.CompilerParams(dimension_semantics=sem, vmem_limit_bytes=VMEM_LIMIT)


def _mod_body(c_ref, w_ref, b_ref, o_ref):
    c = c_ref[...]
    sc = c * _sigmoid(c)
    o_ref[...] = jnp.sum(sc * w_ref[...], axis=0, keepdims=True) + b_ref[...]


def _mod_call(c_col, w_mod, b_mod):
    d, n = w_mod.shape
    tn = 512
    return pl.pallas_call(
        _mod_body,
        out_shape=jax.ShapeDtypeStruct((1, n), F32),
        grid=(n // tn,),
        in_specs=[pl.BlockSpec((d, 1), lambda j: (0, 0)),
                  pl.BlockSpec((d, tn), lambda j: (0, j)),
                  pl.BlockSpec((1, tn), lambda j: (0, j))],
        out_specs=pl.BlockSpec((1, tn), lambda j: (0, j)),
        compiler_params=_params(("arbitrary",)),
        name="mod",
    )(c_col, w_mod, b_mod)


def _proj_body(x_ref, pos_ref, nw_ref, sc_ref, sh_ref, w_ref, cw_ref, alog_ref, dtb_ref, invf_ref,
               sgn_ref, qn_ref, kn_ref, v_ref, zdn_ref, bg_ref, aq_ref, ak_ref, av_ref, zat_ref,
               cbuf_ref):
    i = pl.program_id(0)
    tm = x_ref.shape[0]
    x = x_ref[...]
    ms = jnp.mean(x * x, axis=-1, keepdims=True)
    h = (x * lax.rsqrt(ms + EPS)) * (nw_ref[...] * (1.0 + sc_ref[...])) + sh_ref[...]
    hb = h.astype(BF16)

    def proj(c0, width):
        return jnp.dot(hb, w_ref[:, c0:c0 + width], preferred_element_type=F32)

    @pl.when(i == 0)
    def _():
        cbuf_ref[0:SUBLANES, :] = jnp.zeros((SUBLANES, 3 * DN_W), F32)

    cbuf_ref[SUBLANES:SUBLANES + tm, :] = proj(C_QKV, 3 * DN_W)
    base = SUBLANES - (CONV_K - 1)
    conv = cbuf_ref[base:base + tm, :] * cw_ref[0:1, :]
    for j in range(1, CONV_K):
        conv = conv + cbuf_ref[base + j:base + j + tm, :] * cw_ref[j:j + 1, :]
    cbuf_ref[0:SUBLANES, :] = cbuf_ref[tm:tm + SUBLANES, :]
    y = conv * _sigmoid(conv)
    for hd in range(DN_HEADS):
        sl = slice(hd * DN_HD, (hd + 1) * DN_HD)
        qh = y[:, sl]
        kh = y[:, DN_W + hd * DN_HD:DN_W + (hd + 1) * DN_HD]
        qs = lax.rsqrt(jnp.sum(qh * qh, axis=-1, keepdims=True) + EPS) * (DN_HD ** -0.5)
        ks = lax.rsqrt(jnp.sum(kh * kh, axis=-1, keepdims=True) + EPS)
        qn_ref[:, sl] = (qh * qs).astype(BF16)
        kn_ref[:, sl] = (kh * ks).astype(BF16)
    v_ref[...] = y[:, 2 * DN_W:3 * DN_W].astype(BF16)

    z = proj(C_Z, DN_W)
    zdn_ref[...] = (z * _sigmoid(z)).astype(BF16)

    ba = proj(C_BA, LANES)
    lane = _iota(ba.shape, 1)
    a = ba + dtb_ref[...]
    softplus = jnp.maximum(a, 0.0) + jnp.log(1.0 + jnp.exp(-jnp.abs(a)))
    g = -jnp.exp(alog_ref[...]) * softplus
    bg_ref[...] = jnp.where(lane < DN_HEADS, _sigmoid(ba), jnp.where(lane < 2 * DN_HEADS, g, 0.0))

    ang = invf_ref[...] * pos_ref[...].astype(F32)
    cos_t = jnp.cos(ang)
    sin_t = jnp.sin(ang)
    reps = LANES // (AT_HD // 2)
    cos = jnp.concatenate([cos_t] * reps, axis=0).T
    sin = jnp.concatenate([sin_t] * reps, axis=0).T * sgn_ref[...]
    qscale = (AT_HD ** -0.5) * math.log2(math.e)
    aq = proj(C_AQ, AT_W)
    ak = proj(C_AK, AT_W)
    for gp in range(AT_W // LANES):
        sl = slice(gp * LANES, (gp + 1) * LANES)
        xq = aq[:, sl]
        xk = ak[:, sl]
        rq = xq * cos + pltpu.roll(xq, LANES // 2, axis=1) * sin
        rk = xk * cos + pltpu.roll(xk, LANES // 2, axis=1) * sin
        aq_ref[:, sl] = (rq * qscale).astype(BF16)
        ak_ref[:, sl] = rk.astype(BF16)
    av_ref[...] = proj(C_AV, AT_W).astype(BF16)
    az = proj(C_AZ, AT_W)
    zat_ref[...] = (az * _sigmoid(az)).astype(BF16)


def _proj_call(x2, pos_row, nw, scale, shift, w_all, cw, alog_row, dtb_row, invf, sgn, tm):
    s, d = x2.shape
    row = lambda n: pl.BlockSpec((1, n), lambda i: (0, 0))
    full = lambda a: pl.BlockSpec(a.shape, lambda i: (0, 0))
    tile = lambda n: pl.BlockSpec((tm, n), lambda i: (i, 0))
    outs = [jax.ShapeDtypeStruct((s, DN_W), BF16)] * 4 + [jax.ShapeDtypeStruct((s, LANES), F32)] + \
           [jax.ShapeDtypeStruct((s, AT_W), BF16)] * 4
    out_specs = [tile(DN_W)] * 4 + [tile(LANES)] + [tile(AT_W)] * 4
    return pl.pallas_call(
        _proj_body,
        out_shape=outs,
        grid=(s // tm,),
        in_specs=[tile(d), pl.BlockSpec((1, tm), lambda i: (0, i)), row(d), row(d), row(d),
                  full(w_all), full(cw), row(LANES), row(LANES), full(invf), row(LANES)],
        out_specs=out_specs,
        scratch_shapes=[pltpu.VMEM((tm + SUBLANES, 3 * DN_W), F32)],
        compiler_params=_params(("arbitrary",)),
        name="proj",
    )(x2, pos_row, nw, scale, shift, w_all, cw, alog_row, dtb_row, invf, sgn)


def _dn_prep_body(q_ref, k_ref, v_ref, bg_ref, pq_ref, bm_ref, op_ref, gl_ref):
    rows = q_ref.shape[0]
    nchunk = rows // CHUNK
    bg = bg_ref[...]
    ri = _iota((rows, rows), 0)
    ci = _iota((rows, rows), 1)
    tril_blk = jnp.where((ri // CHUNK == ci // CHUNK) & (ri >= ci), 1.0, 0.0).astype(BF16)
    g_hi = bg.astype(BF16)
    g_r1 = bg - g_hi.astype(F32)
    g_mid = g_r1.astype(BF16)
    g_lo = (g_r1 - g_mid.astype(F32)).astype(BF16)
    csum = jnp.dot(tril_blk, jnp.concatenate([g_hi, g_mid, g_lo], axis=1), preferred_element_type=F32)
    gcum = csum[:, 0:LANES] + csum[:, LANES:2 * LANES] + csum[:, 2 * LANES:3 * LANES]
    bfull = jnp.concatenate([jnp.broadcast_to(bg[:, hd:hd + 1], (rows, DN_HD)) for hd in range(DN_HEADS)], axis=1)
    gfull = jnp.concatenate([jnp.broadcast_to(gcum[:, DN_HEADS + hd:DN_HEADS + hd + 1], (rows, DN_HD))
                             for hd in range(DN_HEADS)], axis=1)
    eg = jnp.exp(gfull)
    q = q_ref[...].astype(F32)
    k = k_ref[...].astype(F32)
    kb = k * bfull
    vb = v_ref[...].astype(F32) * bfull
    q_dec = q * eg
    kbg = kb * eg

    r64 = _iota((CHUNK, CHUNK), 0)
    c64 = _iota((CHUNK, CHUNK), 1)
    tril = r64 >= c64
    strict = r64 > c64
    eye = jnp.where(r64 == c64, 1.0, 0.0).astype(F32)

    items = [(c, hd) for c in range(nchunk) for hd in range(DN_HEADS)]
    rsl = lambda c: slice(c * CHUNK, (c + 1) * CHUNK)
    hsl = lambda hd: slice(hd * DN_HD, (hd + 1) * DN_HD)
    g_last = [gfull[c * CHUNK + CHUNK - 1:(c + 1) * CHUNK, :] for c in range(nchunk)]
    k_dec = [k[rsl(c)] * jnp.exp(g_last[c] - gfull[rsl(c)]) for c in range(nchunk)]
    g_row = [gcum[rsl(c)].T for c in range(nchunk)]
    kq = [lax.dot_general(jnp.concatenate([kb[rsl(c), hsl(hd)].astype(BF16), q_ref[rsl(c), hsl(hd)]], axis=0),
                          k_ref[rsl(c), hsl(hd)], NT_DIMS, preferred_element_type=F32)
          for c, hd in items]
    dec = [jnp.exp(jnp.where(tril, gfull[rsl(c), hd * DN_HD:hd * DN_HD + CHUNK]
                             - g_row[c][DN_HEADS + hd:DN_HEADS + hd + 1, :], NEG)) for c, hd in items]
    a_low = [jnp.where(strict, m[0:CHUNK] * d, 0.0) for m, d in zip(kq, dec)]
    attn = [jnp.where(tril, m[CHUNK:2 * CHUNK] * d, 0.0).astype(BF16) for m, d in zip(kq, dec)]
    xinv = [eye - a for a in a_low]
    pw = [a.astype(BF16) for a in a_low]
    for _ in range(5):
        pw = [jnp.dot(p, p, preferred_element_type=F32).astype(BF16) for p in pw]
        xinv = [xi + jnp.dot(xi.astype(BF16), p, preferred_element_type=F32) for xi, p in zip(xinv, pw)]
    wu = [jnp.dot(xi.astype(BF16),
                  jnp.concatenate([kbg[rsl(c), hsl(hd)], vb[rsl(c), hsl(hd)]], axis=1).astype(BF16),
                  preferred_element_type=F32).astype(BF16) for xi, (c, hd) in zip(xinv, items)]
    pb = [jnp.dot(k_dec[c][:, hsl(hd)].T.astype(BF16), m, preferred_element_type=F32)
          for m, (c, hd) in zip(wu, items)]
    aw = [jnp.dot(a, m, preferred_element_type=F32) for a, m in zip(attn, wu)]
    for (c, hd), pbi, awi in zip(items, pb, aw):
        pq_ref[c, hd, 0:DN_HD, :] = pbi[:, 0:DN_HD].astype(BF16)
        pq_ref[c, hd, DN_HD:DN_HD + CHUNK, :] = (q_dec[rsl(c), hsl(hd)] - awi[:, 0:DN_HD]).astype(BF16)
        bm_ref[c, hd] = pbi[:, DN_HD:2 * DN_HD]
        op_ref[rsl(c), hsl(hd)] = awi[:, DN_HD:2 * DN_HD]
    for c in range(nchunk):
        gl_rows = [jnp.exp(g_last[c][:, hsl(hd)]) for hd in range(DN_HEADS)]
        gl_rows.append(jnp.zeros((SUBLANES - DN_HEADS, DN_HD), F32))
        gl_ref[c] = jnp.concatenate(gl_rows, axis=0)


def _dn_prep_call(qn, kn, v, bg, rows):
    s = qn.shape[0]
    nct = s // CHUNK
    ncb = rows // CHUNK
    tile = lambda n: pl.BlockSpec((rows, n), lambda i: (i, 0))
    return pl.pallas_call(
        _dn_prep_body,
        out_shape=[jax.ShapeDtypeStruct((nct, DN_HEADS, DN_HD + CHUNK, DN_HD), BF16),
                   jax.ShapeDtypeStruct((nct, DN_HEADS, DN_HD, DN_HD), F32),
                   jax.ShapeDtypeStruct((s, DN_W), F32),
                   jax.ShapeDtypeStruct((nct, SUBLANES, DN_HD), F32)],
        grid=(s // rows,),
        in_specs=[tile(DN_W), tile(DN_W), tile(DN_W), tile(LANES)],
        out_specs=[pl.BlockSpec((ncb, DN_HEADS, DN_HD + CHUNK, DN_HD), lambda i: (i, 0, 0, 0)),
                   pl.BlockSpec((ncb, DN_HEADS, DN_HD, DN_HD), lambda i: (i, 0, 0, 0)),
                   tile(DN_W),
                   pl.BlockSpec((ncb, SUBLANES, DN_HD), lambda i: (i, 0, 0))],
        compiler_params=_params(("arbitrary",)),
        name="dn_prep",
    )(qn, kn, v, bg)


def _dn_scan_body(pq_ref, bm_ref, op_ref, gl_ref, z_ref, nw_ref, o_ref, s_ref):
    ncb = pq_ref.shape[0]

    @pl.when(pl.program_id(0) == 0)
    def _():
        s_ref[...] = jnp.zeros(s_ref.shape, F32)

    def chunk(c, carry):
        rs = pl.ds(pl.multiple_of(c * CHUNK, CHUNK), CHUNK)
        for hd in range(DN_HEADS):
            hs = slice(hd * DN_HD, (hd + 1) * DN_HD)
            state = s_ref[hd]
            r = jnp.dot(pq_ref[c, hd], state.astype(BF16), preferred_element_type=F32)
            o = r[DN_HD:DN_HD + CHUNK] + op_ref[rs, hs]
            s_ref[hd] = gl_ref[c, hd:hd + 1, :] * state - r[0:DN_HD] + bm_ref[c, hd]
            ms = jnp.mean(o * o, axis=-1, keepdims=True)
            o_ref[rs, hs] = (o * lax.rsqrt(ms + EPS) * nw_ref[...] * z_ref[rs, hs].astype(F32)).astype(BF16)
        return carry

    lax.fori_loop(0, ncb, chunk, 0)


def _dn_scan_call(pq, bm, op, gl, zdn, nw_row, ncb):
    nct = pq.shape[0]
    s = op.shape[0]
    rows = ncb * CHUNK
    return pl.pallas_call(
        _dn_scan_body,
        out_shape=jax.ShapeDtypeStruct((s, DN_W), BF16),
        grid=(nct // ncb,),
        in_specs=[pl.BlockSpec((ncb, DN_HEADS, DN_HD + CHUNK, DN_HD), lambda i: (i, 0, 0, 0)),
                  pl.BlockSpec((ncb, DN_HEADS, DN_HD, DN_HD), lambda i: (i, 0, 0, 0)),
                  pl.BlockSpec((rows, DN_W), lambda i: (i, 0)),
                  pl.BlockSpec((ncb, SUBLANES, DN_HD), lambda i: (i, 0, 0)),
                  pl.BlockSpec((rows, DN_W), lambda i: (i, 0)),
                  pl.BlockSpec((1, DN_HD), lambda i: (0, 0))],
        out_specs=pl.BlockSpec((rows, DN_W), lambda i: (i, 0)),
        scratch_shapes=[pltpu.VMEM((DN_HEADS, DN_HD, DN_HD), F32)],
        compiler_params=_params(("arbitrary",)),
        name="dn_scan",
    )(pq, bm, op, gl, zdn, nw_row)


def _attn_body(q_ref, kp_ref, kc_ref, vp_ref, vc_ref, acc_ref, ml_ref):
    n = pl.program_id(1)
    qi = _iota((Q_BLOCK, 2 * Q_BLOCK), 0)
    kj = _iota((Q_BLOCK, 2 * Q_BLOCK), 1)
    rel = Q_BLOCK + qi - kj
    valid = (rel >= 0) & (rel <= W_SUB) & ((kj >= Q_BLOCK) | (n > 0))
    bias = jnp.where(valid, 0.0, NEG).astype(F32)
    lane_q = _iota((Q_BLOCK, LANES), 1)
    lane_v = _iota((2 * Q_BLOCK, LANES), 1)
    ml = jnp.zeros((Q_BLOCK, LANES), F32)
    for gp in range(AT_W // LANES):
        sl = slice(gp * LANES, (gp + 1) * LANES)
        qg = q_ref[:, sl]
        kband = jnp.concatenate([kp_ref[:, sl], kc_ref[:, sl]], axis=0)
        vband = jnp.concatenate([vp_ref[:, sl], vc_ref[:, sl]], axis=0)
        res = []
        for hh in range(2):
            qm = jnp.where((lane_q % AT_HD) // (AT_HD // 2) == hh, qg, jnp.zeros_like(qg))
            s = lax.dot_general(qm, kband, NT_DIMS, preferred_element_type=F32) + bias
            m = jnp.max(s, axis=-1, keepdims=True)
            p = jnp.exp2(s - m)
            vaug = jnp.where(lane_v // AT_HD == hh, vband, jnp.ones_like(vband))
            r = jnp.dot(p.astype(BF16), vaug, preferred_element_type=F32)
            res.append((m, r))
        (m0, r0), (m1, r1) = res
        acc_ref[:, sl] = jnp.where(lane_q < AT_HD, r0, r1)
        h0 = 2 * gp
        ml = jnp.where(lane_q == h0, m0, ml)
        ml = jnp.where(lane_q == h0 + 1, m1, ml)
        ml = jnp.where(lane_q == AT_HEADS + h0, pltpu.roll(r0, AT_HD, axis=1), ml)
        ml = jnp.where(lane_q == AT_HEADS + h0 + 1, r1, ml)
    ml_ref[...] = ml


def _attn_call(aq, ak, av, dil):
    s = aq.shape[0]
    sub = s // dil
    nb = sub // Q_BLOCK
    view = lambda a: a.reshape(sub, dil * a.shape[1])
    cur = lambda w: pl.BlockSpec((Q_BLOCK, w), lambda r, n: (n, r))
    prev = lambda w: pl.BlockSpec((Q_BLOCK, w), lambda r, n: (jnp.maximum(n - 1, 0), r))
    acc, ml = pl.pallas_call(
        _attn_body,
        out_shape=[jax.ShapeDtypeStruct((sub, dil * AT_W), F32),
                   jax.ShapeDtypeStruct((sub, dil * LANES), F32)],
        grid=(dil, nb),
        in_specs=[cur(AT_W), prev(AT_W), cur(AT_W), prev(AT_W), cur(AT_W)],
        out_specs=[cur(AT_W), cur(LANES)],
        compiler_params=_params(("arbitrary", "arbitrary")),
        name=f"attn_d{dil}",
    )(view(aq), view(ak), view(ak), view(av), view(av))
    return acc.reshape(s, AT_W), ml.reshape(s, LANES)


def _out_body(x_ref, odn_ref, a1_ref, a2_ref, a3_ref, m1_ref, m2_ref, m3_ref, z_ref, gate_ref, w_ref,
              anw_ref, fnw_ref, o_ref):
    li = _iota((LANES, AT_W), 0)
    hi_ = _iota((LANES, AT_W), 1) // AT_HD
    e_m = jnp.where(li == hi_, 1.0, 0.0).astype(F32)
    e_l = jnp.where(li == hi_ + AT_HEADS, 1.0, 0.0).astype(F32)
    mls = [r[...] for r in (m1_ref, m2_ref, m3_ref)]
    ms_ = [jnp.dot(v, e_m, precision=HI, preferred_element_type=F32) for v in mls]
    ls_ = [jnp.dot(v, e_l, precision=HI, preferred_element_type=F32) for v in mls]
    mx = jnp.maximum(jnp.maximum(ms_[0], ms_[1]), ms_[2])
    num = jnp.zeros_like(mx)
    den = jnp.zeros_like(mx)
    for a_ref, mp, lp in zip((a1_ref, a2_ref, a3_ref), ms_, ls_):
        wgt = jnp.exp2(mp - mx)
        num = num + wgt * a_ref[...]
        den = den + wgt * lp
    o = num / den
    bi = _iota((AT_W, AT_W), 0) // AT_HD
    bj = _iota((AT_W, AT_W), 1) // AT_HD
    head_mean = jnp.where(bi == bj, 1.0 / AT_HD, 0.0).astype(F32)
    msq = jnp.dot(o * o, head_mean, precision=HI, preferred_element_type=F32)
    o_at = o * lax.rsqrt(msq + EPS) * anw_ref[...] * z_ref[...].astype(F32)
    mixin = jnp.concatenate([odn_ref[...], o_at.astype(BF16)], axis=1)
    mix = jnp.dot(mixin, w_ref[...], preferred_element_type=F32)
    y = x_ref[...] + gate_ref[...] * mix
    ms = jnp.mean(y * y, axis=-1, keepdims=True)
    o_ref[...] = y * lax.rsqrt(ms + EPS) * fnw_ref[...]


def _out_call(x2, odn, accs, mls, zat, gate, w_out, anw_row, fnw_row, tm):
    s, d = x2.shape
    tile = lambda n: pl.BlockSpec((tm, n), lambda i: (i, 0))
    row = lambda n: pl.BlockSpec((1, n), lambda i: (0, 0))
    return pl.pallas_call(
        _out_body,
        out_shape=jax.ShapeDtypeStruct((s, d), F32),
        grid=(s // tm,),
        in_specs=[tile(d), tile(DN_W)] + [tile(AT_W)] * 3 + [tile(LANES)] * 3 +
                 [tile(AT_W), row(d), pl.BlockSpec(w_out.shape, lambda i: (0, 0)), row(AT_W), row(d)],
        out_specs=tile(d),
        compiler_params=_params(("arbitrary",)),
        name="out",
    )(x2, odn, *accs, *mls, zat, gate, w_out, anw_row, fnw_row)


def _rope_pair_perm():
    half = AT_HD // 2
    idx = []
    for gp in range(AT_W // LANES):
        for l in range(LANES):
            part, hh, j = l // AT_HD, (l % AT_HD) // half, l % half
            idx.append((2 * gp + hh) * AT_HD + part * half + j)
    return np.asarray(idx, np.int32)


def kernel(x, c, positions, w_mod, b_mod, norm_w, w_in, conv_w, a_log, dt_bias, dn_norm_w, at_norm_w,
           w_out, final_norm_w):
    b, s, d = x.shape
    assert b == 1 and d == D_MODEL and w_mod.shape[0] == 1
    x2 = x.reshape(s, d)

    mod = _mod_call(c.reshape(d, 1), w_mod[0], b_mod[0].reshape(1, 3 * d))
    shift, scale, gate = mod[:, 0:d], mod[:, d:2 * d], mod[:, 2 * d:3 * d]

    wi = w_in[0]
    o_z, o_b, o_a, o_aq = 3 * DN_W, 4 * DN_W, 4 * DN_W + DN_HEADS, 4 * DN_W + 2 * DN_HEADS
    perm = _rope_pair_perm()
    w_aq = wi[:, o_aq:o_aq + AT_W][:, perm]
    w_ak = wi[:, o_aq + AT_W:o_aq + 2 * AT_W][:, perm]
    w_ba = jnp.pad(wi[:, o_b:o_b + 2 * DN_HEADS], ((0, 0), (0, LANES - 2 * DN_HEADS)))
    w_all = jnp.concatenate([wi[:, 0:o_z], wi[:, o_z:o_b], w_aq, w_ak,
                             wi[:, o_aq + 2 * AT_W:o_aq + 4 * AT_W], w_ba], axis=1).astype(BF16)
    assert w_all.shape[1] == W_COLS
    pad_heads = lambda v: jnp.pad(v.reshape(1, DN_HEADS), ((0, 0), (DN_HEADS, LANES - 2 * DN_HEADS)))
    half = AT_HD // 2
    invf = (ROPE_THETA ** (-jnp.arange(half, dtype=F32) / half)).reshape(half, 1)
    sgn = jnp.where(jnp.arange(LANES) < LANES // 2, -1.0, 1.0).astype(F32).reshape(1, LANES)

    qn, kn, v, zdn, bg, aq, ak, av, zat = _proj_call(
        x2, positions.reshape(1, s), norm_w[0].reshape(1, d), scale, shift, w_all, conv_w[0],
        pad_heads(a_log[0]), pad_heads(dt_bias[0]), invf, sgn, tm=256)

    pq, bm, op, gl = _dn_prep_call(qn, kn, v, bg, rows=256)
    odn = _dn_scan_call(pq, bm, op, gl, zdn, dn_norm_w[0].reshape(1, DN_HD), ncb=8)

    accs, mls = [], []
    for dil in DILATIONS:
        acc, ml = _attn_call(aq, ak, av, dil)
        accs.append(acc)
        mls.append(ml)

    out = _out_call(x2, odn, accs, mls, zat, gate, w_out[0].astype(BF16),
                    jnp.tile(at_norm_w[0], AT_HEADS).reshape(1, AT_W), final_norm_w.reshape(1, d), tm=256)
    return out.reshape(b, s, d)
```

```python
import functools
import math

import numpy as np
import jax
import jax.numpy as jnp
from jax import lax
from jax.experimental import pallas as pl
from jax.experimental.pallas import tpu as pltpu

D_MODEL = 1024
DN_HEADS = 4
DN_HD = 128
DN_W = DN_HEADS * DN_HD
AT_HEADS = 8
AT_HD = 64
AT_W = AT_HEADS * AT_HD
CONV_K = 4
CHUNK = 64
DILATIONS = (1, 4, 16)
W_SUB = 128
Q_BLOCK = 128
SUPER = Q_BLOCK * DILATIONS[-1]
ATT_GROUP = 4
ROPE_THETA = 10000.0
EPS = 1e-6
NEG = -1e30

LANES = 128
SUBLANES = 8
VMEM_LIMIT = 48 * 1024 * 1024
PROJ_VMEM_LIMIT = 56 * 1024 * 1024

C_QKV, C_Z, C_AQ, C_AK, C_AV, C_AZ, C_BA = 0, 1536, 2048, 2560, 3072, 3584, 4096
W_COLS = C_BA + LANES

F32 = jnp.float32
BF16 = jnp.bfloat16
HI = lax.Precision.HIGHEST
NT_DIMS = (((1,), (1,)), ((), ()))


def _sigmoid(v):
    return 1.0 / (1.0 + jnp.exp(-v))


def _iota(shape, dim):
    return lax.broadcasted_iota(jnp.int32, shape, dim)


def _params(sem):
    return pltpu.CompilerParams(dimension_semantics=sem, vmem_limit_bytes=VMEM_LIMIT)


def _mod_body(c_ref, w_ref, b_ref, o_ref):
    c = c_ref[...]
    sc = c * _sigmoid(c)
    o_ref[...] = jnp.sum(sc * w_ref[...], axis=0, keepdims=True) + b_ref[...]


def _mod_call(c_col, w_mod, b_mod):
    d, n = w_mod.shape
    tn = 512
    return pl.pallas_call(
        _mod_body,
        out_shape=jax.ShapeDtypeStruct((1, n), F32),
        grid=(n // tn,),
        in_specs=[pl.BlockSpec((d, 1), lambda j: (0, 0)),
                  pl.BlockSpec((d, tn), lambda j: (0, j)),
                  pl.BlockSpec((1, tn), lambda j: (0, j))],
        out_specs=pl.BlockSpec((1, tn), lambda j: (0, j)),
        compiler_params=_params(("arbitrary",)),
        name="mod",
    )(c_col, w_mod, b_mod)


def _proj_body(x_ref, pos_ref, nw_ref, sc_ref, sh_ref, w_ref, cw_ref, alog_ref, dtb_ref, invf_ref,
               sgn_ref, qn_ref, kn_ref, v_ref, zdn_ref, bg_ref, zat_ref,
               aq1_ref, aq4_ref, aq16_ref, ak1_ref, ak4_ref, ak16_ref, av1_ref, av4_ref, av16_ref,
               cbuf_ref, pbuf_ref):
    i = pl.program_id(0)
    tm = x_ref.shape[0]
    x = x_ref[...]
    ms = jnp.mean(x * x, axis=-1, keepdims=True)
    h = (x * lax.rsqrt(ms + EPS)) * (nw_ref[...] * (1.0 + sc_ref[...])) + sh_ref[...]
    hb = h.astype(BF16)

    def proj(c0, width):
        return jnp.dot(hb, w_ref[:, c0:c0 + width], preferred_element_type=F32)

    @pl.when(i == 0)
    def _():
        cbuf_ref[0:SUBLANES, :] = jnp.zeros((SUBLANES, 3 * DN_W), F32)

    cbuf_ref[SUBLANES:SUBLANES + tm, :] = proj(C_QKV, 3 * DN_W)
    base = SUBLANES - (CONV_K - 1)
    conv = cbuf_ref[base:base + tm, :] * cw_ref[0:1, :]
    for j in range(1, CONV_K):
        conv = conv + cbuf_ref[base + j:base + j + tm, :] * cw_ref[j:j + 1, :]
    cbuf_ref[0:SUBLANES, :] = cbuf_ref[tm:tm + SUBLANES, :]
    y = conv * _sigmoid(conv)
    for hd in range(DN_HEADS):
        sl = slice(hd * DN_HD, (hd + 1) * DN_HD)
        qh = y[:, sl]
        kh = y[:, DN_W + hd * DN_HD:DN_W + (hd + 1) * DN_HD]
        qs = lax.rsqrt(jnp.sum(qh * qh, axis=-1, keepdims=True) + EPS) * (DN_HD ** -0.5)
        ks = lax.rsqrt(jnp.sum(kh * kh, axis=-1, keepdims=True) + EPS)
        qn_ref[:, sl] = (qh * qs).astype(BF16)
        kn_ref[:, sl] = (kh * ks).astype(BF16)
    v_ref[...] = y[:, 2 * DN_W:3 * DN_W].astype(BF16)

    z = proj(C_Z, DN_W)
    zdn_ref[...] = (z * _sigmoid(z)).astype(BF16)

    ba = proj(C_BA, LANES)
    lane = _iota(ba.shape, 1)
    a = ba + dtb_ref[...]
    softplus = jnp.maximum(a, 0.0) + jnp.log(1.0 + jnp.exp(-jnp.abs(a)))
    g = -jnp.exp(alog_ref[...]) * softplus
    bg_ref[...] = jnp.where(lane < DN_HEADS, _sigmoid(ba), jnp.where(lane < 2 * DN_HEADS, g, 0.0))

    ang = invf_ref[...] * pos_ref[...].astype(F32)
    cos_t = jnp.cos(ang)
    sin_t = jnp.sin(ang)
    reps = LANES // (AT_HD // 2)
    cos = jnp.concatenate([cos_t] * reps, axis=0).T
    sin = jnp.concatenate([sin_t] * reps, axis=0).T * sgn_ref[...]
    qscale = (AT_HD ** -0.5) * math.log2(math.e)
    step_in_super = i % (SUPER // tm)

    def emit(nat_ref, p4_ref, p16_ref):
        for gp in range(AT_W // LANES):
            sl = slice(gp * LANES, (gp + 1) * LANES)
            nat_ref[:, sl] = pbuf_ref[gp].astype(BF16)
            for dil, p_ref in ((DILATIONS[1], p4_ref), (DILATIONS[2], p16_ref)):
                n = tm // dil
                for r in range(dil):
                    dst = pl.multiple_of(r * (SUPER // dil) + step_in_super * n, n)
                    p_ref[pl.ds(dst, n), sl] = pbuf_ref[gp, pl.ds(r, n, stride=dil), :].astype(BF16)

    aq = proj(C_AQ, AT_W)
    for gp in range(AT_W // LANES):
        xq = aq[:, gp * LANES:(gp + 1) * LANES]
        pbuf_ref[gp] = (xq * cos + pltpu.roll(xq, LANES // 2, axis=1) * sin) * qscale
    emit(aq1_ref, aq4_ref, aq16_ref)
    ak = proj(C_AK, AT_W)
    for gp in range(AT_W // LANES):
        xk = ak[:, gp * LANES:(gp + 1) * LANES]
        pbuf_ref[gp] = xk * cos + pltpu.roll(xk, LANES // 2, axis=1) * sin
    emit(ak1_ref, ak4_ref, ak16_ref)
    av = proj(C_AV, AT_W)
    for gp in range(AT_W // LANES):
        pbuf_ref[gp] = av[:, gp * LANES:(gp + 1) * LANES]
    emit(av1_ref, av4_ref, av16_ref)
    az = proj(C_AZ, AT_W)
    zat_ref[...] = (az * _sigmoid(az)).astype(BF16)


def _proj_call(x2, pos_row, nw, scale, shift, w_all, cw, alog_row, dtb_row, invf, sgn, tm):
    s, d = x2.shape
    row = lambda n: pl.BlockSpec((1, n), lambda i: (0, 0))
    full = lambda a: pl.BlockSpec(a.shape, lambda i: (0, 0))
    tile = lambda n: pl.BlockSpec((tm, n), lambda i: (i, 0))
    sup = pl.BlockSpec((SUPER, AT_W), lambda i: (i // (SUPER // tm), 0))
    bf = lambda n: jax.ShapeDtypeStruct((s, n), BF16)
    outs = [bf(DN_W)] * 4 + [jax.ShapeDtypeStruct((s, LANES), F32), bf(AT_W)] + [bf(AT_W)] * 9
    out_specs = [tile(DN_W)] * 4 + [tile(LANES), tile(AT_W)] + [tile(AT_W), sup, sup] * 3
    return pl.pallas_call(
        _proj_body,
        out_shape=outs,
        grid=(s // tm,),
        in_specs=[tile(d), pl.BlockSpec((1, tm), lambda i: (0, i)), row(d), row(d), row(d),
                  pl.BlockSpec(w_all.shape, lambda i: (0, 0), pipeline_mode=pl.Buffered(1)),
                  full(cw), row(LANES), row(LANES), full(invf), row(LANES)],
        out_specs=out_specs,
        scratch_shapes=[pltpu.VMEM((tm + SUBLANES, 3 * DN_W), F32), pltpu.VMEM((AT_W // LANES, tm, LANES), F32)],
        compiler_params=pltpu.CompilerParams(dimension_semantics=("arbitrary",),
                                             vmem_limit_bytes=PROJ_VMEM_LIMIT),
        name="proj",
    )(x2, pos_row, nw, scale, shift, w_all, cw, alog_row, dtb_row, invf, sgn)


def _dn_prep_body(q_ref, k_ref, v_ref, bg_ref, pq_ref, bm_ref, op_ref, gl_ref):
    rows = q_ref.shape[0]
    nchunk = rows // CHUNK
    bg = bg_ref[...]
    ri = _iota((rows, rows), 0)
    ci = _iota((rows, rows), 1)
    tril_blk = jnp.where((ri // CHUNK == ci // CHUNK) & (ri >= ci), 1.0, 0.0).astype(BF16)
    g_hi = bg.astype(BF16)
    g_r1 = bg - g_hi.astype(F32)
    g_mid = g_r1.astype(BF16)
    g_lo = (g_r1 - g_mid.astype(F32)).astype(BF16)
    csum = jnp.dot(tril_blk, jnp.concatenate([g_hi, g_mid, g_lo], axis=1), preferred_element_type=F32)
    gcum = csum[:, 0:LANES] + csum[:, LANES:2 * LANES] + csum[:, 2 * LANES:3 * LANES]
    bfull = jnp.concatenate([jnp.broadcast_to(bg[:, hd:hd + 1], (rows, DN_HD)) for hd in range(DN_HEADS)], axis=1)
    gfull = jnp.concatenate([jnp.broadcast_to(gcum[:, DN_HEADS + hd:DN_HEADS + hd + 1], (rows, DN_HD))
                             for hd in range(DN_HEADS)], axis=1)
    eg = jnp.exp(gfull)
    q = q_ref[...].astype(F32)
    k = k_ref[...].astype(F32)
    kb = k * bfull
    vb = v_ref[...].astype(F32) * bfull
    q_dec = q * eg
    kbg = kb * eg

    r64 = _iota((CHUNK, CHUNK), 0)
    c64 = _iota((CHUNK, CHUNK), 1)
    tril = r64 >= c64
    strict = r64 > c64
    eye = jnp.where(r64 == c64, 1.0, 0.0).astype(F32)

    items = [(c, hd) for c in range(nchunk) for hd in range(DN_HEADS)]
    rsl = lambda c: slice(c * CHUNK, (c + 1) * CHUNK)
    hsl = lambda hd: slice(hd * DN_HD, (hd + 1) * DN_HD)
    g_last = [gfull[c * CHUNK + CHUNK - 1:(c + 1) * CHUNK, :] for c in range(nchunk)]
    k_dec = [k[rsl(c)] * jnp.exp(g_last[c] - gfull[rsl(c)]) for c in range(nchunk)]
    g_row = [gcum[rsl(c)].T for c in range(nchunk)]
    kq = [lax.dot_general(jnp.concatenate([kb[rsl(c), hsl(hd)].astype(BF16), q_ref[rsl(c), hsl(hd)]], axis=0),
                          k_ref[rsl(c), hsl(hd)], NT_DIMS, preferred_element_type=F32)
          for c, hd in items]
    dec = [jnp.exp(jnp.where(tril, gfull[rsl(c), hd * DN_HD:hd * DN_HD + CHUNK]
                             - g_row[c][DN_HEADS + hd:DN_HEADS + hd + 1, :], NEG)) for c, hd in items]
    a_low = [jnp.where(strict, m[0:CHUNK] * d, 0.0) for m, d in zip(kq, dec)]
    attn = [jnp.where(tril, m[CHUNK:2 * CHUNK] * d, 0.0).astype(BF16) for m, d in zip(kq, dec)]
    xinv = [eye - a for a in a_low]
    pw = [a.astype(BF16) for a in a_low]
    for _ in range(5):
        pw = [jnp.dot(p, p, preferred_element_type=F32).astype(BF16) for p in pw]
        xinv = [xi + jnp.dot(xi.astype(BF16), p, preferred_element_type=F32) for xi, p in zip(xinv, pw)]
    wu = [jnp.dot(xi.astype(BF16),
                  jnp.concatenate([kbg[rsl(c), hsl(hd)], vb[rsl(c), hsl(hd)]], axis=1).astype(BF16),
                  preferred_element_type=F32).astype(BF16) for xi, (c, hd) in zip(xinv, items)]
    pb = [jnp.dot(k_dec[c][:, hsl(hd)].T.astype(BF16), m, preferred_element_type=F32)
          for m, (c, hd) in zip(wu, items)]
    aw = [jnp.dot(a, m, preferred_element_type=F32) for a, m in zip(attn, wu)]
    for (c, hd), pbi, awi in zip(items, pb, aw):
        pq_ref[c, hd, 0:DN_HD, :] = pbi[:, 0:DN_HD].astype(BF16)
        pq_ref[c, hd, DN_HD:DN_HD + CHUNK, :] = (q_dec[rsl(c), hsl(hd)] - awi[:, 0:DN_HD]).astype(BF16)
        bm_ref[c, hd] = pbi[:, DN_HD:2 * DN_HD]
        op_ref[rsl(c), hsl(hd)] = awi[:, DN_HD:2 * DN_HD]
    for c in range(nchunk):
        gl_rows = [jnp.exp(g_last[c][:, hsl(hd)]) for hd in range(DN_HEADS)]
        gl_rows.append(jnp.zeros((SUBLANES - DN_HEADS, DN_HD), F32))
        gl_ref[c] = jnp.concatenate(gl_rows, axis=0)


def _dn_prep_call(qn, kn, v, bg, rows):
    s = qn.shape[0]
    nct = s // CHUNK
    ncb = rows // CHUNK
    tile = lambda n: pl.BlockSpec((rows, n), lambda i: (i, 0))
    return pl.pallas_call(
        _dn_prep_body,
        out_shape=[jax.ShapeDtypeStruct((nct, DN_HEADS, DN_HD + CHUNK, DN_HD), BF16),
                   jax.ShapeDtypeStruct((nct, DN_HEADS, DN_HD, DN_HD), F32),
                   jax.ShapeDtypeStruct((s, DN_W), F32),
                   jax.ShapeDtypeStruct((nct, SUBLANES, DN_HD), F32)],
        grid=(s // rows,),
        in_specs=[tile(DN_W), tile(DN_W), tile(DN_W), tile(LANES)],
        out_specs=[pl.BlockSpec((ncb, DN_HEADS, DN_HD + CHUNK, DN_HD), lambda i: (i, 0, 0, 0)),
                   pl.BlockSpec((ncb, DN_HEADS, DN_HD, DN_HD), lambda i: (i, 0, 0, 0)),
                   tile(DN_W),
                   pl.BlockSpec((ncb, SUBLANES, DN_HD), lambda i: (i, 0, 0))],
        compiler_params=_params(("arbitrary",)),
        name="dn_prep",
    )(qn, kn, v, bg)


def _dn_scan_body(pq_ref, bm_ref, op_ref, gl_ref, z_ref, nw_ref, o_ref, s_ref):
    ncb = pq_ref.shape[0]

    @pl.when(pl.program_id(0) == 0)
    def _():
        s_ref[...] = jnp.zeros(s_ref.shape, F32)

    def chunk(c, carry):
        rs = pl.ds(pl.multiple_of(c * CHUNK, CHUNK), CHUNK)
        for hd in range(DN_HEADS):
            hs = slice(hd * DN_HD, (hd + 1) * DN_HD)
            state = s_ref[hd]
            r = jnp.dot(pq_ref[c, hd], state.astype(BF16), preferred_element_type=F32)
            o = r[DN_HD:DN_HD + CHUNK] + op_ref[rs, hs]
            s_ref[hd] = gl_ref[c, hd:hd + 1, :] * state - r[0:DN_HD] + bm_ref[c, hd]
            ms = jnp.mean(o * o, axis=-1, keepdims=True)
            o_ref[rs, hs] = (o * lax.rsqrt(ms + EPS) * nw_ref[...] * z_ref[rs, hs].astype(F32)).astype(BF16)
        return carry

    lax.fori_loop(0, ncb, chunk, 0)


def _dn_scan_call(pq, bm, op, gl, zdn, nw_row, ncb):
    nct = pq.shape[0]
    s = op.shape[0]
    rows = ncb * CHUNK
    return pl.pallas_call(
        _dn_scan_body,
        out_shape=jax.ShapeDtypeStruct((s, DN_W), BF16),
        grid=(nct // ncb,),
        in_specs=[pl.BlockSpec((ncb, DN_HEADS, DN_HD + CHUNK, DN_HD), lambda i: (i, 0, 0, 0)),
                  pl.BlockSpec((ncb, DN_HEADS, DN_HD, DN_HD), lambda i: (i, 0, 0, 0)),
                  pl.BlockSpec((rows, DN_W), lambda i: (i, 0)),
                  pl.BlockSpec((ncb, SUBLANES, DN_HD), lambda i: (i, 0, 0)),
                  pl.BlockSpec((rows, DN_W), lambda i: (i, 0)),
                  pl.BlockSpec((1, DN_HD), lambda i: (0, 0))],
        out_specs=pl.BlockSpec((rows, DN_W), lambda i: (i, 0)),
        scratch_shapes=[pltpu.VMEM((DN_HEADS, DN_HD, DN_HD), F32)],
        compiler_params=_params(("arbitrary",)),
        name="dn_scan",
    )(pq, bm, op, gl, zdn, nw_row)


def _attn_body(q1, q4, q16, k1, k4, k16, v1, v4, v16, kp1, vp1, kp4a, kp4b, kp4c, kp4d,
               vp4a, vp4b, vp4c, vp4d, kp16, vp16, z_ref, anw_ref, o_ref, m_s, l_s, a_s):
    d4, d16 = DILATIONS[1], DILATIONS[2]
    kp4 = (kp4a, kp4b, kp4c, kp4d)
    vp4 = (vp4a, vp4b, vp4c, vp4d)
    qi = _iota((Q_BLOCK, 2 * Q_BLOCK), 0)
    kj = _iota((Q_BLOCK, 2 * Q_BLOCK), 1)
    rel = Q_BLOCK + qi - kj
    band = (rel >= 0) & (rel <= W_SUB)
    bias_in = jnp.where(band, 0.0, NEG).astype(F32)
    bias_first = jnp.where(band & (kj >= Q_BLOCK), 0.0, NEG).astype(F32)
    bias_edge = jnp.where(pl.program_id(0) == 0, bias_first, bias_in)
    lane = _iota((Q_BLOCK, LANES), 1)
    lane_k = _iota((2 * Q_BLOCK, LANES), 1)
    head_a_q = (lane % AT_HD) < (AT_HD // 2)
    head_a = lane < AT_HD
    head_a_k = lane_k < AT_HD
    ro = _iota((4 * Q_BLOCK, LANES), 0)
    ones_rhs = jnp.where((ro < 2 * Q_BLOCK) == (_iota((4 * Q_BLOCK, LANES), 1) < AT_HD), 1.0, 0.0).astype(BF16)
    sr = _iota((2 * LANES, LANES), 0)
    ssq_rhs = jnp.where((sr % LANES) // AT_HD == _iota((2 * LANES, LANES), 1) // AT_HD, 1.0, 0.0).astype(BF16)
    zq = jnp.zeros((Q_BLOCK, LANES), BF16)
    zv = jnp.zeros((2 * Q_BLOCK, LANES), BF16)

    def attend(blocks):
        q2 = [jnp.concatenate([jnp.where(head_a_q, q, zq), jnp.where(head_a_q, zq, q)], axis=0)
              for q, _, _, _ in blocks]
        s = [lax.dot_general(qq, blk[1], NT_DIMS, preferred_element_type=F32) for qq, blk in zip(q2, blocks)]
        s = [si + jnp.concatenate([blk[3], blk[3]], axis=0) for si, blk in zip(s, blocks)]
        m = [jnp.max(si, axis=-1, keepdims=True) for si in s]
        p = [jnp.exp2(si - mi).astype(BF16) for si, mi in zip(s, m)]
        pc = [jnp.concatenate([pi[0:Q_BLOCK], pi[Q_BLOCK:2 * Q_BLOCK]], axis=1) for pi in p]
        rhs = [jnp.concatenate([jnp.concatenate([jnp.where(head_a_k, blk[2], zv),
                                                 jnp.where(head_a_k, zv, blk[2])], axis=0), ones_rhs], axis=1)
               for blk in blocks]
        r = [jnp.dot(pci, rh, preferred_element_type=F32) for pci, rh in zip(pc, rhs)]
        return [(ri[:, 0:LANES], ri[:, LANES:2 * LANES],
                 jnp.where(head_a, mi[0:Q_BLOCK], mi[Q_BLOCK:2 * Q_BLOCK])) for ri, mi in zip(r, m)]

    def merged(res, rows):
        acc, l, m = res
        m_old = m_s[rows, :]
        m_new = jnp.maximum(m_old, m)
        w_old = jnp.exp2(m_old - m_new)
        w_new = jnp.exp2(m - m_new)
        return a_s[rows, :] * w_old + acc * w_new, l_s[rows, :] * w_old + l * w_new, m_new

    def d16_group(gi, carry):
        blocks = []
        for t in range(ATT_GROUP):
            row0 = pl.multiple_of((gi * ATT_GROUP + t) * Q_BLOCK, Q_BLOCK)
            rs = pl.ds(row0, Q_BLOCK)
            blocks.append((q16[rs, :], jnp.concatenate([kp16[rs, :], k16[rs, :]], axis=0),
                           jnp.concatenate([vp16[rs, :], v16[rs, :]], axis=0), bias_edge))
        for t, (acc, l, m) in enumerate(attend(blocks)):
            rows = pl.ds(gi * ATT_GROUP + t, Q_BLOCK, stride=d16)
            a_s[rows, :] = acc
            l_s[rows, :] = l
            m_s[rows, :] = m
        return carry

    lax.fori_loop(0, d16 // ATT_GROUP, d16_group, 0)

    def d4_group(b, edge):
        blocks = []
        for r in range(d4):
            base = r * (SUPER // d4)
            q = q4[pl.ds(pl.multiple_of(base + b * Q_BLOCK, Q_BLOCK), Q_BLOCK), :]
            if edge:
                kb = jnp.concatenate([kp4[r][...], k4[base:base + Q_BLOCK, :]], axis=0)
                vb = jnp.concatenate([vp4[r][...], v4[base:base + Q_BLOCK, :]], axis=0)
            else:
                ks = pl.ds(pl.multiple_of(base + (b - 1) * Q_BLOCK, Q_BLOCK), 2 * Q_BLOCK)
                kb, vb = k4[ks, :], v4[ks, :]
            blocks.append((q, kb, vb, bias_edge if edge else bias_in))
        for r, res in enumerate(attend(blocks)):
            rows = pl.ds(b * (Q_BLOCK * d4) + r, Q_BLOCK, stride=d4)
            acc, l, m = merged(res, rows)
            a_s[rows, :] = acc
            l_s[rows, :] = l
            m_s[rows, :] = m

    d4_group(0, True)

    def d4_loop(b, carry):
        d4_group(b, False)
        return carry

    lax.fori_loop(1, SUPER // d4 // Q_BLOCK, d4_loop, 0)

    def d1_group(b0, edge):
        blocks = []
        for t in range(ATT_GROUP):
            b = b0 + t
            q = q1[pl.ds(pl.multiple_of(b * Q_BLOCK, Q_BLOCK), Q_BLOCK), :]
            if edge and t == 0:
                kb = jnp.concatenate([kp1[...], k1[0:Q_BLOCK, :]], axis=0)
                vb = jnp.concatenate([vp1[...], v1[0:Q_BLOCK, :]], axis=0)
            else:
                ks = pl.ds(pl.multiple_of((b - 1) * Q_BLOCK, Q_BLOCK), 2 * Q_BLOCK)
                kb, vb = k1[ks, :], v1[ks, :]
            blocks.append((q, kb, vb, bias_edge if (edge and t == 0) else bias_in))
        rows = [pl.ds(pl.multiple_of((b0 + t) * Q_BLOCK, Q_BLOCK), Q_BLOCK) for t in range(ATT_GROUP)]
        outs = []
        for res, rs in zip(attend(blocks), rows):
            acc, l, _ = merged(res, rs)
            outs.append(acc / l)
        sq = [o * o for o in outs]
        hi = [v.astype(BF16) for v in sq]
        lo = [(v - h.astype(F32)).astype(BF16) for v, h in zip(sq, hi)]
        ssq = [jnp.dot(jnp.concatenate([h, lw], axis=1), ssq_rhs, preferred_element_type=F32)
               for h, lw in zip(hi, lo)]
        for o, sm, rs in zip(outs, ssq, rows):
            o_ref[rs, :] = (o * lax.rsqrt(sm * (1.0 / AT_HD) + EPS) * anw_ref[...]
                            * z_ref[rs, :].astype(F32)).astype(BF16)

    d1_group(0, True)

    def d1_loop(gi, carry):
        d1_group(gi * ATT_GROUP, False)
        return carry

    lax.fori_loop(1, SUPER // Q_BLOCK // ATT_GROUP, d1_loop, 0)


def _attn_call(aq, ak, av, zat, anw_row):
    s = zat.shape[0]
    d4 = DILATIONS[1]
    per_super = SUPER // Q_BLOCK
    cur = pl.BlockSpec((SUPER, LANES), lambda m, g: (m, g))
    prev16 = pl.BlockSpec((SUPER, LANES), lambda m, g: (jnp.maximum(m - 1, 0), g))
    prev1 = pl.BlockSpec((Q_BLOCK, LANES), lambda m, g: (jnp.maximum(m * per_super - 1, 0), g))
    prev4 = [pl.BlockSpec((Q_BLOCK, LANES),
                          lambda m, g, r=r: (jnp.maximum((m - 1) * per_super + (r + 1) * (per_super // d4) - 1, 0), g))
             for r in range(d4)]
    state = pltpu.VMEM((SUPER, LANES), F32)
    return pl.pallas_call(
        _attn_body,
        out_shape=jax.ShapeDtypeStruct((s, AT_W), BF16),
        grid=(s // SUPER, AT_W // LANES),
        in_specs=[cur] * 9 + [prev1, prev1] + prev4 + prev4 + [prev16, prev16, cur,
                                                               pl.BlockSpec((1, LANES), lambda m, g: (0, g))],
        out_specs=cur,
        scratch_shapes=[state, state, state],
        compiler_params=_params(("arbitrary", "arbitrary")),
        name="attn",
    )(*aq, *ak, *av, ak[0], av[0], *([ak[1]] * d4), *([av[1]] * d4), ak[2], av[2], zat, anw_row)


def _out_body(x_ref, odn_ref, oat_ref, gate_ref, w_ref, fnw_ref, o_ref):
    mixin = jnp.concatenate([odn_ref[...], oat_ref[...]], axis=1)
    mix = jnp.dot(mixin, w_ref[...], preferred_element_type=F32)
    y = x_ref[...] + gate_ref[...] * mix
    ms = jnp.mean(y * y, axis=-1, keepdims=True)
    o_ref[...] = y * lax.rsqrt(ms + EPS) * fnw_ref[...]


def _out_call(x2, odn, oat, gate, w_out, fnw_row, tm):
    s, d = x2.shape
    tile = lambda n: pl.BlockSpec((tm, n), lambda i: (i, 0))
    row = lambda n: pl.BlockSpec((1, n), lambda i: (0, 0))
    return pl.pallas_call(
        _out_body,
        out_shape=jax.ShapeDtypeStruct((s, d), F32),
        grid=(s // tm,),
        in_specs=[tile(d), tile(DN_W), tile(AT_W), row(d), pl.BlockSpec(w_out.shape, lambda i: (0, 0)), row(d)],
        out_specs=tile(d),
        compiler_params=_params(("arbitrary",)),
        name="out",
    )(x2, odn, oat, gate, w_out, fnw_row)


def _rope_pair_perm():
    half = AT_HD // 2
    idx = []
    for gp in range(AT_W // LANES):
        for l in range(LANES):
            part, hh, j = l // AT_HD, (l % AT_HD) // half, l % half
            idx.append((2 * gp + hh) * AT_HD + part * half + j)
    return np.asarray(idx, np.int32)


def kernel(x, c, positions, w_mod, b_mod, norm_w, w_in, conv_w, a_log, dt_bias, dn_norm_w, at_norm_w,
           w_out, final_norm_w):
    b, s, d = x.shape
    assert b == 1 and d == D_MODEL and w_mod.shape[0] == 1
    x2 = x.reshape(s, d)

    mod = _mod_call(c.reshape(d, 1), w_mod[0], b_mod[0].reshape(1, 3 * d))
    shift, scale, gate = mod[:, 0:d], mod[:, d:2 * d], mod[:, 2 * d:3 * d]

    wi = w_in[0]
    o_z, o_b, o_a, o_aq = 3 * DN_W, 4 * DN_W, 4 * DN_W + DN_HEADS, 4 * DN_W + 2 * DN_HEADS
    perm = _rope_pair_perm()
    w_aq = wi[:, o_aq:o_aq + AT_W][:, perm]
    w_ak = wi[:, o_aq + AT_W:o_aq + 2 * AT_W][:, perm]
    w_ba = jnp.pad(wi[:, o_b:o_b + 2 * DN_HEADS], ((0, 0), (0, LANES - 2 * DN_HEADS)))
    w_all = jnp.concatenate([wi[:, 0:o_z], wi[:, o_z:o_b], w_aq, w_ak,
                             wi[:, o_aq + 2 * AT_W:o_aq + 4 * AT_W], w_ba], axis=1).astype(BF16)
    assert w_all.shape[1] == W_COLS
    pad_heads = lambda v: jnp.pad(v.reshape(1, DN_HEADS), ((0, 0), (DN_HEADS, LANES - 2 * DN_HEADS)))
    half = AT_HD // 2
    invf = (ROPE_THETA ** (-jnp.arange(half, dtype=F32) / half)).reshape(half, 1)
    sgn = jnp.where(jnp.arange(LANES) < LANES // 2, -1.0, 1.0).astype(F32).reshape(1, LANES)

    assert s % SUPER == 0
    outs = _proj_call(x2, positions.reshape(1, s), norm_w[0].reshape(1, d), scale, shift, w_all, conv_w[0],
                      pad_heads(a_log[0]), pad_heads(dt_bias[0]), invf, sgn, tm=256)
    qn, kn, v, zdn, bg, zat = outs[0:6]
    aq, ak, av = outs[6:9], outs[9:12], outs[12:15]

    pq, bm, op, gl = _dn_prep_call(qn, kn, v, bg, rows=256)
    odn = _dn_scan_call(pq, bm, op, gl, zdn, dn_norm_w[0].reshape(1, DN_HD), ncb=8)
    oat = _attn_call(aq, ak, av, zat, jnp.tile(at_norm_w[0], AT_HEADS).reshape(1, AT_W))

    out = _out_call(x2, odn, oat, gate, w_out[0].astype(BF16), final_norm_w.reshape(1, d), tm=512)
    return out.reshape(b, s, d)
```

```python
import functools
import math

import numpy as np
import jax
import jax.numpy as jnp
from jax import lax
from jax.experimental import pallas as pl
from jax.experimental.pallas import tpu as pltpu

D_MODEL = 1024
DN_HEADS = 4
DN_HD = 128
DN_W = DN_HEADS * DN_HD
AT_HEADS = 8
AT_HD = 64
AT_W = AT_HEADS * AT_HD
CONV_K = 4
CHUNK = 64
DILATIONS = (1, 4, 16)
W_SUB = 128
Q_BLOCK = 128
SUPER = Q_BLOCK * DILATIONS[-1]
ATT_GROUP = 4
ROPE_THETA = 10000.0
EPS = 1e-6
NEG = -1e30

LANES = 128
SUBLANES = 8
MXU_COLS = 256
VMEM_LIMIT = 48 * 1024 * 1024
PROJ_VMEM_LIMIT = 56 * 1024 * 1024

C_QKV, C_Z, C_AQ, C_AK, C_AV, C_AZ, C_BA = 0, 1536, 2048, 2560, 3072, 3584, 4096
W_COLS = C_BA + LANES

F32 = jnp.float32
BF16 = jnp.bfloat16
HI = lax.Precision.HIGHEST
NT_DIMS = (((1,), (1,)), ((), ()))


def _sigmoid(v):
    return 1.0 / (1.0 + jnp.exp(-v))


def _iota(shape, dim):
    return lax.broadcasted_iota(jnp.int32, shape, dim)


def _params(sem):
    return pltpu.CompilerParams(dimension_semantics=sem, vmem_limit_bytes=VMEM_LIMIT)


def _mod_body(c_ref, w_ref, b_ref, o_ref):
    c = c_ref[...]
    sc = c * _sigmoid(c)
    o_ref[...] = jnp.sum(sc * w_ref[...], axis=0, keepdims=True) + b_ref[...]


def _mod_call(c_col, w_mod, b_mod):
    d, n = w_mod.shape
    tn = 512
    return pl.pallas_call(
        _mod_body,
        out_shape=jax.ShapeDtypeStruct((1, n), F32),
        grid=(n // tn,),
        in_specs=[pl.BlockSpec((d, 1), lambda j: (0, 0)),
                  pl.BlockSpec((d, tn), lambda j: (0, j)),
                  pl.BlockSpec((1, tn), lambda j: (0, j))],
        out_specs=pl.BlockSpec((1, tn), lambda j: (0, j)),
        compiler_params=_params(("arbitrary",)),
        name="mod",
    )(c_col, w_mod, b_mod)


def _proj_body(x_ref, pos_ref, nw_ref, sc_ref, sh_ref, w_ref, cw_ref, alog_ref, dtb_ref, invf_ref,
               sgn_ref, qn_ref, kn_ref, v_ref, zdn_ref, bg_ref, zat_ref,
               aq1_ref, aq4_ref, aq16_ref, ak1_ref, ak4_ref, ak16_ref, av1_ref, av4_ref, av16_ref,
               cbuf_ref, hb_ref, pbuf_ref, p4buf_ref):
    i = pl.program_id(0)
    tm = x_ref.shape[0]
    x = x_ref[...]
    ms = jnp.mean(x * x, axis=-1, keepdims=True)
    hb_ref[...] = ((x * lax.rsqrt(ms + EPS)) * (nw_ref[...] * (1.0 + sc_ref[...])) + sh_ref[...]).astype(BF16)

    def proj(c0, width=MXU_COLS):
        return jnp.dot(hb_ref[...], w_ref[:, c0:c0 + width], preferred_element_type=F32)

    def silu(v):
        return v * _sigmoid(v)

    @pl.when(i == 0)
    def _():
        cbuf_ref[...] = jnp.zeros((SUBLANES, 3 * DN_W), F32)

    sub = _iota((SUBLANES, MXU_COLS), 0)
    for c0 in range(0, 3 * DN_W, MXU_COLS):
        cs = slice(c0, c0 + MXU_COLS)
        xg = proj(C_QKV + c0)
        prev = cbuf_ref[:, cs]
        conv = xg * cw_ref[CONV_K - 1:CONV_K, cs]
        for j in range(1, CONV_K):
            sh = pltpu.roll(xg, j, axis=0)
            head = jnp.where(sub < j, pltpu.roll(prev, j, axis=0), sh[0:SUBLANES])
            conv = conv + jnp.concatenate([head, sh[SUBLANES:tm]], axis=0) * cw_ref[CONV_K - 1 - j:CONV_K - j, cs]
        cbuf_ref[:, cs] = xg[tm - SUBLANES:tm]
        y = silu(conv)
        kind, off = c0 // DN_W, c0 % DN_W
        for hh in range(MXU_COLS // DN_HD):
            yh = y[:, hh * DN_HD:(hh + 1) * DN_HD]
            dst = slice(off + hh * DN_HD, off + (hh + 1) * DN_HD)
            if kind == 2:
                v_ref[:, dst] = yh.astype(BF16)
            else:
                inv = lax.rsqrt(jnp.sum(yh * yh, axis=-1, keepdims=True) + EPS)
                if kind == 0:
                    qn_ref[:, dst] = (yh * (inv * (DN_HD ** -0.5))).astype(BF16)
                else:
                    kn_ref[:, dst] = (yh * inv).astype(BF16)

    for c0 in range(0, DN_W, MXU_COLS):
        zdn_ref[:, c0:c0 + MXU_COLS] = silu(proj(C_Z + c0)).astype(BF16)
        zat_ref[:, c0:c0 + MXU_COLS] = silu(proj(C_AZ + c0)).astype(BF16)

    ba = proj(C_BA, LANES)
    lane = _iota(ba.shape, 1)
    a = ba + dtb_ref[...]
    softplus = jnp.maximum(a, 0.0) + jnp.log(1.0 + jnp.exp(-jnp.abs(a)))
    g = -jnp.exp(alog_ref[...]) * softplus
    bg_ref[...] = jnp.where(lane < DN_HEADS, _sigmoid(ba), jnp.where(lane < 2 * DN_HEADS, g, 0.0))

    ang = invf_ref[...] * pos_ref[...].astype(F32)
    reps = LANES // (AT_HD // 2)
    cos = jnp.concatenate([jnp.cos(ang)] * reps, axis=0).T
    sin = jnp.concatenate([jnp.sin(ang)] * reps, axis=0).T * sgn_ref[...]
    qscale = (AT_HD ** -0.5) * math.log2(math.e)
    step_in_super = i % (SUPER // tm)
    d4, d16 = DILATIONS[1], DILATIONS[2]
    n4, n16 = tm // d4, tm // d16

    def emit(val, gp, nat_ref, p4_ref, p16_ref):
        sl = slice(gp * LANES, (gp + 1) * LANES)
        nat_ref[:, sl] = val.astype(BF16)
        pbuf_ref[gp] = val
        for c in range(d4):
            seg = pbuf_ref[gp, pl.ds(c, n4, stride=d4), :]
            dst = pl.multiple_of(c * (SUPER // d4) + step_in_super * n4, n4)
            p4_ref[pl.ds(dst, n4), sl] = seg.astype(BF16)
            p4buf_ref[gp, c * n4:(c + 1) * n4, :] = seg
        for r in range(d16):
            seg = p4buf_ref[gp, pl.ds((r % d4) * n4 + r // d4, n16, stride=d4), :]
            dst = pl.multiple_of(r * (SUPER // d16) + step_in_super * n16, n16)
            p16_ref[pl.ds(dst, n16), sl] = seg.astype(BF16)

    def rope(v):
        return v * cos + pltpu.roll(v, LANES // 2, axis=1) * sin

    for c0 in range(0, AT_W, MXU_COLS):
        aq = proj(C_AQ + c0)
        ak = proj(C_AK + c0)
        av = proj(C_AV + c0)
        for hh in range(MXU_COLS // LANES):
            gp = c0 // LANES + hh
            ls = slice(hh * LANES, (hh + 1) * LANES)
            emit(rope(aq[:, ls]) * qscale, gp, aq1_ref, aq4_ref, aq16_ref)
            emit(rope(ak[:, ls]), gp, ak1_ref, ak4_ref, ak16_ref)
            emit(av[:, ls], gp, av1_ref, av4_ref, av16_ref)


def _proj_call(x2, pos_row, nw, scale, shift, w_all, cw, alog_row, dtb_row, invf, sgn, tm):
    s, d = x2.shape
    row = lambda n: pl.BlockSpec((1, n), lambda i: (0, 0))
    full = lambda a: pl.BlockSpec(a.shape, lambda i: (0, 0))
    tile = lambda n: pl.BlockSpec((tm, n), lambda i: (i, 0))
    sup = pl.BlockSpec((SUPER, AT_W), lambda i: (i // (SUPER // tm), 0))
    bf = lambda n: jax.ShapeDtypeStruct((s, n), BF16)
    outs = [bf(DN_W)] * 4 + [jax.ShapeDtypeStruct((s, LANES), F32), bf(AT_W)] + [bf(AT_W)] * 9
    out_specs = [tile(DN_W)] * 4 + [tile(LANES), tile(AT_W)] + [tile(AT_W), sup, sup] * 3
    return pl.pallas_call(
        _proj_body,
        out_shape=outs,
        grid=(s // tm,),
        in_specs=[tile(d), pl.BlockSpec((1, tm), lambda i: (0, i)), row(d), row(d), row(d),
                  pl.BlockSpec(w_all.shape, lambda i: (0, 0), pipeline_mode=pl.Buffered(1)),
                  full(cw), row(LANES), row(LANES), full(invf), row(LANES)],
        out_specs=out_specs,
        scratch_shapes=[pltpu.VMEM((SUBLANES, 3 * DN_W), F32), pltpu.VMEM((tm, d), BF16),
                        pltpu.VMEM((AT_W // LANES, tm, LANES), F32),
                        pltpu.VMEM((AT_W // LANES, tm, LANES), F32)],
        compiler_params=pltpu.CompilerParams(dimension_semantics=("arbitrary",),
                                             vmem_limit_bytes=PROJ_VMEM_LIMIT),
        name="proj",
    )(x2, pos_row, nw, scale, shift, w_all, cw, alog_row, dtb_row, invf, sgn)


def _dn_prep_body(q_ref, k_ref, v_ref, bg_ref, pq_ref, bm_ref, op_ref, gl_ref):
    rows = q_ref.shape[0]
    nchunk = rows // CHUNK
    bg = bg_ref[...]
    ri = _iota((rows, rows), 0)
    ci = _iota((rows, rows), 1)
    tril_blk = jnp.where((ri // CHUNK == ci // CHUNK) & (ri >= ci), 1.0, 0.0).astype(BF16)
    g_hi = bg.astype(BF16)
    g_r1 = bg - g_hi.astype(F32)
    g_mid = g_r1.astype(BF16)
    g_lo = (g_r1 - g_mid.astype(F32)).astype(BF16)
    csum = jnp.dot(tril_blk, jnp.concatenate([g_hi, g_mid, g_lo], axis=1), preferred_element_type=F32)
    gcum = csum[:, 0:LANES] + csum[:, LANES:2 * LANES] + csum[:, 2 * LANES:3 * LANES]
    bfull = jnp.concatenate([jnp.broadcast_to(bg[:, hd:hd + 1], (rows, DN_HD)) for hd in range(DN_HEADS)], axis=1)
    gfull = jnp.concatenate([jnp.broadcast_to(gcum[:, DN_HEADS + hd:DN_HEADS + hd + 1], (rows, DN_HD))
                             for hd in range(DN_HEADS)], axis=1)
    eg = jnp.exp(gfull)
    q = q_ref[...].astype(F32)
    k = k_ref[...].astype(F32)
    kb = k * bfull
    vb = v_ref[...].astype(F32) * bfull
    q_dec = q * eg
    kbg = kb * eg

    r64 = _iota((CHUNK, CHUNK), 0)
    c64 = _iota((CHUNK, CHUNK), 1)
    tril = r64 >= c64
    strict = r64 > c64
    eye = jnp.where(r64 == c64, 1.0, 0.0).astype(F32)

    items = [(c, hd) for c in range(nchunk) for hd in range(DN_HEADS)]
    rsl = lambda c: slice(c * CHUNK, (c + 1) * CHUNK)
    hsl = lambda hd: slice(hd * DN_HD, (hd + 1) * DN_HD)
    g_last = [gfull[c * CHUNK + CHUNK - 1:(c + 1) * CHUNK, :] for c in range(nchunk)]
    k_dec = [k[rsl(c)] * jnp.exp(g_last[c] - gfull[rsl(c)]) for c in range(nchunk)]
    g_row = [gcum[rsl(c)].T for c in range(nchunk)]
    kq = [lax.dot_general(jnp.concatenate([kb[rsl(c), hsl(hd)].astype(BF16), q_ref[rsl(c), hsl(hd)]], axis=0),
                          k_ref[rsl(c), hsl(hd)], NT_DIMS, preferred_element_type=F32)
          for c, hd in items]
    dec = [jnp.exp(jnp.where(tril, gfull[rsl(c), hd * DN_HD:hd * DN_HD + CHUNK]
                             - g_row[c][DN_HEADS + hd:DN_HEADS + hd + 1, :], NEG)) for c, hd in items]
    a_low = [jnp.where(strict, m[0:CHUNK] * d, 0.0) for m, d in zip(kq, dec)]
    attn = [jnp.where(tril, m[CHUNK:2 * CHUNK] * d, 0.0).astype(BF16) for m, d in zip(kq, dec)]
    xinv = [eye - a for a in a_low]
    pw = [a.astype(BF16) for a in a_low]
    for _ in range(5):
        pw = [jnp.dot(p, p, preferred_element_type=F32).astype(BF16) for p in pw]
        xinv = [xi + jnp.dot(xi.astype(BF16), p, preferred_element_type=F32) for xi, p in zip(xinv, pw)]
    wu = [jnp.dot(xi.astype(BF16),
                  jnp.concatenate([kbg[rsl(c), hsl(hd)], vb[rsl(c), hsl(hd)]], axis=1).astype(BF16),
                  preferred_element_type=F32).astype(BF16) for xi, (c, hd) in zip(xinv, items)]
    pb = [jnp.dot(k_dec[c][:, hsl(hd)].T.astype(BF16), m, preferred_element_type=F32)
          for m, (c, hd) in zip(wu, items)]
    aw = [jnp.dot(a, m, preferred_element_type=F32) for a, m in zip(attn, wu)]
    for (c, hd), pbi, awi in zip(items, pb, aw):
        pq_ref[c, hd, 0:DN_HD, :] = pbi[:, 0:DN_HD].astype(BF16)
        pq_ref[c, hd, DN_HD:DN_HD + CHUNK, :] = (q_dec[rsl(c), hsl(hd)] - awi[:, 0:DN_HD]).astype(BF16)
        bm_ref[c, hd] = pbi[:, DN_HD:2 * DN_HD]
        op_ref[rsl(c), hsl(hd)] = awi[:, DN_HD:2 * DN_HD]
    for c in range(nchunk):
        gl_rows = [jnp.exp(g_last[c][:, hsl(hd)]) for hd in range(DN_HEADS)]
        gl_rows.append(jnp.zeros((SUBLANES - DN_HEADS, DN_HD), F32))
        gl_ref[c] = jnp.concatenate(gl_rows, axis=0)


def _dn_prep_call(qn, kn, v, bg, rows):
    s = qn.shape[0]
    nct = s // CHUNK
    ncb = rows // CHUNK
    tile = lambda n: pl.BlockSpec((rows, n), lambda i: (i, 0))
    return pl.pallas_call(
        _dn_prep_body,
        out_shape=[jax.ShapeDtypeStruct((nct, DN_HEADS, DN_HD + CHUNK, DN_HD), BF16),
                   jax.ShapeDtypeStruct((nct, DN_HEADS, DN_HD, DN_HD), F32),
                   jax.ShapeDtypeStruct((s, DN_W), F32),
                   jax.ShapeDtypeStruct((nct, SUBLANES, DN_HD), F32)],
        grid=(s // rows,),
        in_specs=[tile(DN_W), tile(DN_W), tile(DN_W), tile(LANES)],
        out_specs=[pl.BlockSpec((ncb, DN_HEADS, DN_HD + CHUNK, DN_HD), lambda i: (i, 0, 0, 0)),
                   pl.BlockSpec((ncb, DN_HEADS, DN_HD, DN_HD), lambda i: (i, 0, 0, 0)),
                   tile(DN_W),
                   pl.BlockSpec((ncb, SUBLANES, DN_HD), lambda i: (i, 0, 0))],
        compiler_params=_params(("arbitrary",)),
        name="dn_prep",
    )(qn, kn, v, bg)


def _dn_scan_body(pq_ref, bm_ref, op_ref, gl_ref, z_ref, nw_ref, o_ref, s_ref):
    ncb = pq_ref.shape[0]

    @pl.when(pl.program_id(0) == 0)
    def _():
        s_ref[...] = jnp.zeros(s_ref.shape, F32)

    def chunk(c, carry):
        rs = pl.ds(pl.multiple_of(c * CHUNK, CHUNK), CHUNK)
        for hd in range(DN_HEADS):
            hs = slice(hd * DN_HD, (hd + 1) * DN_HD)
            state = s_ref[hd]
            r = jnp.dot(pq_ref[c, hd], state.astype(BF16), preferred_element_type=F32)
            o = r[DN_HD:DN_HD + CHUNK] + op_ref[rs, hs]
            s_ref[hd] = gl_ref[c, hd:hd + 1, :] * state - r[0:DN_HD] + bm_ref[c, hd]
            ms = jnp.mean(o * o, axis=-1, keepdims=True)
            o_ref[rs, hs] = (o * lax.rsqrt(ms + EPS) * nw_ref[...] * z_ref[rs, hs].astype(F32)).astype(BF16)
        return carry

    lax.fori_loop(0, ncb, chunk, 0)


def _dn_scan_call(pq, bm, op, gl, zdn, nw_row, ncb):
    nct = pq.shape[0]
    s = op.shape[0]
    rows = ncb * CHUNK
    return pl.pallas_call(
        _dn_scan_body,
        out_shape=jax.ShapeDtypeStruct((s, DN_W), BF16),
        grid=(nct // ncb,),
        in_specs=[pl.BlockSpec((ncb, DN_HEADS, DN_HD + CHUNK, DN_HD), lambda i: (i, 0, 0, 0)),
                  pl.BlockSpec((ncb, DN_HEADS, DN_HD, DN_HD), lambda i: (i, 0, 0, 0)),
                  pl.BlockSpec((rows, DN_W), lambda i: (i, 0)),
                  pl.BlockSpec((ncb, SUBLANES, DN_HD), lambda i: (i, 0, 0)),
                  pl.BlockSpec((rows, DN_W), lambda i: (i, 0)),
                  pl.BlockSpec((1, DN_HD), lambda i: (0, 0))],
        out_specs=pl.BlockSpec((rows, DN_W), lambda i: (i, 0)),
        scratch_shapes=[pltpu.VMEM((DN_HEADS, DN_HD, DN_HD), F32)],
        compiler_params=_params(("arbitrary",)),
        name="dn_scan",
    )(pq, bm, op, gl, zdn, nw_row)


def _attn_body(q1, q4, q16, k1, k4, k16, v1, v4, v16, kp1, vp1, kp4a, kp4b, kp4c, kp4d,
               vp4a, vp4b, vp4c, vp4d, kp16, vp16, z_ref, anw_ref, o_ref, m_s, l_s, a_s):
    d4, d16 = DILATIONS[1], DILATIONS[2]
    kp4 = (kp4a, kp4b, kp4c, kp4d)
    vp4 = (vp4a, vp4b, vp4c, vp4d)
    qi = _iota((Q_BLOCK, 2 * Q_BLOCK), 0)
    kj = _iota((Q_BLOCK, 2 * Q_BLOCK), 1)
    rel = Q_BLOCK + qi - kj
    band = (rel >= 0) & (rel <= W_SUB)
    bias_in = jnp.where(band, 0.0, NEG).astype(F32)
    bias_first = jnp.where(band & (kj >= Q_BLOCK), 0.0, NEG).astype(F32)
    bias_edge = jnp.where(pl.program_id(0) == 0, bias_first, bias_in)
    lane = _iota((Q_BLOCK, LANES), 1)
    lane_k = _iota((2 * Q_BLOCK, LANES), 1)
    head_a_q = (lane % AT_HD) < (AT_HD // 2)
    head_a = lane < AT_HD
    head_a_k = lane_k < AT_HD
    ro = _iota((4 * Q_BLOCK, LANES), 0)
    ones_rhs = jnp.where((ro < 2 * Q_BLOCK) == (_iota((4 * Q_BLOCK, LANES), 1) < AT_HD), 1.0, 0.0).astype(BF16)
    sr = _iota((2 * LANES, LANES), 0)
    ssq_rhs = jnp.where((sr % LANES) // AT_HD == _iota((2 * LANES, LANES), 1) // AT_HD, 1.0, 0.0).astype(BF16)
    zq = jnp.zeros((Q_BLOCK, LANES), BF16)
    zv = jnp.zeros((2 * Q_BLOCK, LANES), BF16)

    def attend(blocks):
        q2 = [jnp.concatenate([jnp.where(head_a_q, q, zq), jnp.where(head_a_q, zq, q)], axis=0)
              for q, _, _, _ in blocks]
        s = [lax.dot_general(qq, blk[1], NT_DIMS, preferred_element_type=F32) for qq, blk in zip(q2, blocks)]
        s = [si + jnp.concatenate([blk[3], blk[3]], axis=0) for si, blk in zip(s, blocks)]
        m = [jnp.max(si, axis=-1, keepdims=True) for si in s]
        p = [jnp.exp2(si - mi).astype(BF16) for si, mi in zip(s, m)]
        pc = [jnp.concatenate([pi[0:Q_BLOCK], pi[Q_BLOCK:2 * Q_BLOCK]], axis=1) for pi in p]
        rhs = [jnp.concatenate([jnp.concatenate([jnp.where(head_a_k, blk[2], zv),
                                                 jnp.where(head_a_k, zv, blk[2])], axis=0), ones_rhs], axis=1)
               for blk in blocks]
        r = [jnp.dot(pci, rh, preferred_element_type=F32) for pci, rh in zip(pc, rhs)]
        return [(ri[:, 0:LANES], ri[:, LANES:2 * LANES],
                 jnp.where(head_a, mi[0:Q_BLOCK], mi[Q_BLOCK:2 * Q_BLOCK])) for ri, mi in zip(r, m)]

    def merged(res, rows):
        acc, l, m = res
        m_old = m_s[rows, :]
        m_new = jnp.maximum(m_old, m)
        w_old = jnp.exp2(m_old - m_new)
        w_new = jnp.exp2(m - m_new)
        return a_s[rows, :] * w_old + acc * w_new, l_s[rows, :] * w_old + l * w_new, m_new

    def d16_group(gi, carry):
        blocks = []
        for t in range(ATT_GROUP):
            row0 = pl.multiple_of((gi * ATT_GROUP + t) * Q_BLOCK, Q_BLOCK)
            rs = pl.ds(row0, Q_BLOCK)
            blocks.append((q16[rs, :], jnp.concatenate([kp16[rs, :], k16[rs, :]], axis=0),
                           jnp.concatenate([vp16[rs, :], v16[rs, :]], axis=0), bias_edge))
        for t, (acc, l, m) in enumerate(attend(blocks)):
            rows = pl.ds(gi * ATT_GROUP + t, Q_BLOCK, stride=d16)
            a_s[rows, :] = acc
            l_s[rows, :] = l
            m_s[rows, :] = m
        return carry

    lax.fori_loop(0, d16 // ATT_GROUP, d16_group, 0)

    def d4_group(b, edge):
        blocks = []
        for r in range(d4):
            base = r * (SUPER // d4)
            q = q4[pl.ds(pl.multiple_of(base + b * Q_BLOCK, Q_BLOCK), Q_BLOCK), :]
            if edge:
                kb = jnp.concatenate([kp4[r][...], k4[base:base + Q_BLOCK, :]], axis=0)
                vb = jnp.concatenate([vp4[r][...], v4[base:base + Q_BLOCK, :]], axis=0)
            else:
                ks = pl.ds(pl.multiple_of(base + (b - 1) * Q_BLOCK, Q_BLOCK), 2 * Q_BLOCK)
                kb, vb = k4[ks, :], v4[ks, :]
            blocks.append((q, kb, vb, bias_edge if edge else bias_in))
        for r, res in enumerate(attend(blocks)):
            rows = pl.ds(b * (Q_BLOCK * d4) + r, Q_BLOCK, stride=d4)
            acc, l, m = merged(res, rows)
            a_s[rows, :] = acc
            l_s[rows, :] = l
            m_s[rows, :] = m

    d4_group(0, True)

    def d4_loop(b, carry):
        d4_group(b, False)
        return carry

    lax.fori_loop(1, SUPER // d4 // Q_BLOCK, d4_loop, 0)

    def d1_group(b0, edge):
        blocks = []
        for t in range(ATT_GROUP):
            b = b0 + t
            q = q1[pl.ds(pl.multiple_of(b * Q_BLOCK, Q_BLOCK), Q_BLOCK), :]
            if edge and t == 0:
                kb = jnp.concatenate([kp1[...], k1[0:Q_BLOCK, :]], axis=0)
                vb = jnp.concatenate([vp1[...], v1[0:Q_BLOCK, :]], axis=0)
            else:
                ks = pl.ds(pl.multiple_of((b - 1) * Q_BLOCK, Q_BLOCK), 2 * Q_BLOCK)
                kb, vb = k1[ks, :], v1[ks, :]
            blocks.append((q, kb, vb, bias_edge if (edge and t == 0) else bias_in))
        rows = [pl.ds(pl.multiple_of((b0 + t) * Q_BLOCK, Q_BLOCK), Q_BLOCK) for t in range(ATT_GROUP)]
        outs = []
        for res, rs in zip(attend(blocks), rows):
            acc, l, _ = merged(res, rs)
            outs.append(acc / l)
        sq = [o * o for o in outs]
        hi = [v.astype(BF16) for v in sq]
        lo = [(v - h.astype(F32)).astype(BF16) for v, h in zip(sq, hi)]
        ssq = [jnp.dot(jnp.concatenate([h, lw], axis=1), ssq_rhs, preferred_element_type=F32)
               for h, lw in zip(hi, lo)]
        for o, sm, rs in zip(outs, ssq, rows):
            o_ref[rs, :] = (o * lax.rsqrt(sm * (1.0 / AT_HD) + EPS) * anw_ref[...]
                            * z_ref[rs, :].astype(F32)).astype(BF16)

    d1_group(0, True)

    def d1_loop(gi, carry):
        d1_group(gi * ATT_GROUP, False)
        return carry

    lax.fori_loop(1, SUPER // Q_BLOCK // ATT_GROUP, d1_loop, 0)


def _attn_call(aq, ak, av, zat, anw_row):
    s = zat.shape[0]
    d4 = DILATIONS[1]
    per_super = SUPER // Q_BLOCK
    cur = pl.BlockSpec((SUPER, LANES), lambda m, g: (m, g))
    prev16 = pl.BlockSpec((SUPER, LANES), lambda m, g: (jnp.maximum(m - 1, 0), g))
    prev1 = pl.BlockSpec((Q_BLOCK, LANES), lambda m, g: (jnp.maximum(m * per_super - 1, 0), g))
    prev4 = [pl.BlockSpec((Q_BLOCK, LANES),
                          lambda m, g, r=r: (jnp.maximum((m - 1) * per_super + (r + 1) * (per_super // d4) - 1, 0), g))
             for r in range(d4)]
    state = pltpu.VMEM((SUPER, LANES), F32)
    return pl.pallas_call(
        _attn_body,
        out_shape=jax.ShapeDtypeStruct((s, AT_W), BF16),
        grid=(s // SUPER, AT_W // LANES),
        in_specs=[cur] * 9 + [prev1, prev1] + prev4 + prev4 + [prev16, prev16, cur,
                                                               pl.BlockSpec((1, LANES), lambda m, g: (0, g))],
        out_specs=cur,
        scratch_shapes=[state, state, state],
        compiler_params=_params(("arbitrary", "arbitrary")),
        name="attn",
    )(*aq, *ak, *av, ak[0], av[0], *([ak[1]] * d4), *([av[1]] * d4), ak[2], av[2], zat, anw_row)


def _out_body(x_ref, odn_ref, oat_ref, gate_ref, w_ref, fnw_ref, o_ref):
    mixin = jnp.concatenate([odn_ref[...], oat_ref[...]], axis=1)
    mix = jnp.dot(mixin, w_ref[...], preferred_element_type=F32)
    y = x_ref[...] + gate_ref[...] * mix
    ms = jnp.mean(y * y, axis=-1, keepdims=True)
    o_ref[...] = y * lax.rsqrt(ms + EPS) * fnw_ref[...]


def _out_call(x2, odn, oat, gate, w_out, fnw_row, tm):
    s, d = x2.shape
    tile = lambda n: pl.BlockSpec((tm, n), lambda i: (i, 0))
    row = lambda n: pl.BlockSpec((1, n), lambda i: (0, 0))
    return pl.pallas_call(
        _out_body,
        out_shape=jax.ShapeDtypeStruct((s, d), F32),
        grid=(s // tm,),
        in_specs=[tile(d), tile(DN_W), tile(AT_W), row(d), pl.BlockSpec(w_out.shape, lambda i: (0, 0)), row(d)],
        out_specs=tile(d),
        compiler_params=_params(("arbitrary",)),
        name="out",
    )(x2, odn, oat, gate, w_out, fnw_row)


def _rope_pair_perm():
    half = AT_HD // 2
    idx = []
    for gp in range(AT_W // LANES):
        for l in range(LANES):
            part, hh, j = l // AT_HD, (l % AT_HD) // half, l % half
            idx.append((2 * gp + hh) * AT_HD + part * half + j)
    return np.asarray(idx, np.int32)


def kernel(x, c, positions, w_mod, b_mod, norm_w, w_in, conv_w, a_log, dt_bias, dn_norm_w, at_norm_w,
           w_out, final_norm_w):
    b, s, d = x.shape
    assert b == 1 and d == D_MODEL and w_mod.shape[0] == 1
    x2 = x.reshape(s, d)

    mod = _mod_call(c.reshape(d, 1), w_mod[0], b_mod[0].reshape(1, 3 * d))
    shift, scale, gate = mod[:, 0:d], mod[:, d:2 * d], mod[:, 2 * d:3 * d]

    wi = w_in[0]
    o_z, o_b, o_a, o_aq = 3 * DN_W, 4 * DN_W, 4 * DN_W + DN_HEADS, 4 * DN_W + 2 * DN_HEADS
    perm = _rope_pair_perm()
    w_aq = wi[:, o_aq:o_aq + AT_W][:, perm]
    w_ak = wi[:, o_aq + AT_W:o_aq + 2 * AT_W][:, perm]
    w_ba = jnp.pad(wi[:, o_b:o_b + 2 * DN_HEADS], ((0, 0), (0, LANES - 2 * DN_HEADS)))
    w_all = jnp.concatenate([wi[:, 0:o_z], wi[:, o_z:o_b], w_aq, w_ak,
                             wi[:, o_aq + 2 * AT_W:o_aq + 4 * AT_W], w_ba], axis=1).astype(BF16)
    assert w_all.shape[1] == W_COLS
    pad_heads = lambda v: jnp.pad(v.reshape(1, DN_HEADS), ((0, 0), (DN_HEADS, LANES - 2 * DN_HEADS)))
    half = AT_HD // 2
    invf = (ROPE_THETA ** (-jnp.arange(half, dtype=F32) / half)).reshape(half, 1)
    sgn = jnp.where(jnp.arange(LANES) < LANES // 2, -1.0, 1.0).astype(F32).reshape(1, LANES)

    assert s % SUPER == 0
    outs = _proj_call(x2, positions.reshape(1, s), norm_w[0].reshape(1, d), scale, shift, w_all, conv_w[0],
                      pad_heads(a_log[0]), pad_heads(dt_bias[0]), invf, sgn, tm=256)
    qn, kn, v, zdn, bg, zat = outs[0:6]
    aq, ak, av = outs[6:9], outs[9:12], outs[12:15]

    pq, bm, op, gl = _dn_prep_call(qn, kn, v, bg, rows=256)
    odn = _dn_scan_call(pq, bm, op, gl, zdn, dn_norm_w[0].reshape(1, DN_HD), ncb=8)
    oat = _attn_call(aq, ak, av, zat, jnp.tile(at_norm_w[0], AT_HEADS).reshape(1, AT_W))

    out = _out_call(x2, odn, oat, gate, w_out[0].astype(BF16), final_norm_w.reshape(1, d), tm=512)
    return out.reshape(b, s, d)
```

```python
import functools
import math

import numpy as np
import jax
import jax.numpy as jnp
from jax import lax
from jax.experimental import pallas as pl
from jax.experimental.pallas import tpu as pltpu

D_MODEL = 1024
DN_HEADS = 4
DN_HD = 128
DN_W = DN_HEADS * DN_HD
AT_HEADS = 8
AT_HD = 64
AT_W = AT_HEADS * AT_HD
CONV_K = 4
CHUNK = 64
DILATIONS = (1, 4, 16)
W_SUB = 128
Q_BLOCK = 128
SUPER = Q_BLOCK * DILATIONS[-1]
ATT_LAG = 2
ROPE_THETA = 10000.0
EPS = 1e-6
NEG = -1e30

LANES = 128
SUBLANES = 8
MXU_COLS = 256
VMEM_LIMIT = 48 * 1024 * 1024
PROJ_VMEM_LIMIT = 56 * 1024 * 1024

C_QKV, C_Z, C_AQ, C_AK, C_AV, C_AZ, C_BA = 0, 1536, 2048, 2560, 3072, 3584, 4096
W_COLS = C_BA + LANES

F32 = jnp.float32
BF16 = jnp.bfloat16
HI = lax.Precision.HIGHEST
NT_DIMS = (((1,), (1,)), ((), ()))


def _sigmoid(v):
    return 1.0 / (1.0 + jnp.exp(-v))


def _iota(shape, dim):
    return lax.broadcasted_iota(jnp.int32, shape, dim)


def _params(sem):
    return pltpu.CompilerParams(dimension_semantics=sem, vmem_limit_bytes=VMEM_LIMIT)


def _mod_body(c_ref, w_ref, b_ref, o_ref):
    c = c_ref[...]
    sc = c * _sigmoid(c)
    o_ref[...] = jnp.sum(sc * w_ref[...], axis=0, keepdims=True) + b_ref[...]


def _mod_call(c_col, w_mod, b_mod):
    d, n = w_mod.shape
    tn = 512
    return pl.pallas_call(
        _mod_body,
        out_shape=jax.ShapeDtypeStruct((1, n), F32),
        grid=(n // tn,),
        in_specs=[pl.BlockSpec((d, 1), lambda j: (0, 0)),
                  pl.BlockSpec((d, tn), lambda j: (0, j)),
                  pl.BlockSpec((1, tn), lambda j: (0, j))],
        out_specs=pl.BlockSpec((1, tn), lambda j: (0, j)),
        compiler_params=_params(("arbitrary",)),
        name="mod",
    )(c_col, w_mod, b_mod)


def _proj_body(x_ref, pos_ref, nw_ref, sc_ref, sh_ref, w_ref, cw_ref, alog_ref, dtb_ref, invf_ref,
               sgn_ref, qn_ref, kn_ref, v_ref, zdn_ref, bg_ref, zat_ref,
               aq1_ref, aq4_ref, aq16_ref, ak1_ref, ak4_ref, ak16_ref, av1_ref, av4_ref, av16_ref,
               cbuf_ref, hb_ref, pbuf_ref, p4buf_ref):
    i = pl.program_id(0)
    tm = x_ref.shape[0]
    x = x_ref[...]
    ms = jnp.mean(x * x, axis=-1, keepdims=True)
    hb_ref[...] = ((x * lax.rsqrt(ms + EPS)) * (nw_ref[...] * (1.0 + sc_ref[...])) + sh_ref[...]).astype(BF16)

    def proj(c0, width=MXU_COLS):
        return jnp.dot(hb_ref[...], w_ref[:, c0:c0 + width], preferred_element_type=F32)

    def silu(v):
        return v * _sigmoid(v)

    @pl.when(i == 0)
    def _():
        cbuf_ref[...] = jnp.zeros((SUBLANES, 3 * DN_W), F32)

    sub = _iota((SUBLANES, MXU_COLS), 0)
    for c0 in range(0, 3 * DN_W, MXU_COLS):
        cs = slice(c0, c0 + MXU_COLS)
        xg = proj(C_QKV + c0)
        prev = cbuf_ref[:, cs]
        conv = xg * cw_ref[CONV_K - 1:CONV_K, cs]
        for j in range(1, CONV_K):
            sh = pltpu.roll(xg, j, axis=0)
            head = jnp.where(sub < j, pltpu.roll(prev, j, axis=0), sh[0:SUBLANES])
            conv = conv + jnp.concatenate([head, sh[SUBLANES:tm]], axis=0) * cw_ref[CONV_K - 1 - j:CONV_K - j, cs]
        cbuf_ref[:, cs] = xg[tm - SUBLANES:tm]
        y = silu(conv)
        kind, off = c0 // DN_W, c0 % DN_W
        for hh in range(MXU_COLS // DN_HD):
            yh = y[:, hh * DN_HD:(hh + 1) * DN_HD]
            dst = slice(off + hh * DN_HD, off + (hh + 1) * DN_HD)
            if kind == 2:
                v_ref[:, dst] = yh.astype(BF16)
            else:
                inv = lax.rsqrt(jnp.sum(yh * yh, axis=-1, keepdims=True) + EPS)
                if kind == 0:
                    qn_ref[:, dst] = (yh * (inv * (DN_HD ** -0.5))).astype(BF16)
                else:
                    kn_ref[:, dst] = (yh * inv).astype(BF16)

    for c0 in range(0, DN_W, MXU_COLS):
        zdn_ref[:, c0:c0 + MXU_COLS] = silu(proj(C_Z + c0)).astype(BF16)
        zat_ref[:, c0:c0 + MXU_COLS] = silu(proj(C_AZ + c0)).astype(BF16)

    ba = proj(C_BA, LANES)
    lane = _iota(ba.shape, 1)
    a = ba + dtb_ref[...]
    softplus = jnp.maximum(a, 0.0) + jnp.log(1.0 + jnp.exp(-jnp.abs(a)))
    g = -jnp.exp(alog_ref[...]) * softplus
    bg_ref[...] = jnp.where(lane < DN_HEADS, _sigmoid(ba), jnp.where(lane < 2 * DN_HEADS, g, 0.0))

    ang = invf_ref[...] * pos_ref[...].astype(F32)
    reps = LANES // (AT_HD // 2)
    cos = jnp.concatenate([jnp.cos(ang)] * reps, axis=0).T
    sin = jnp.concatenate([jnp.sin(ang)] * reps, axis=0).T * sgn_ref[...]
    qscale = (AT_HD ** -0.5) * math.log2(math.e)
    step_in_super = i % (SUPER // tm)
    d4, d16 = DILATIONS[1], DILATIONS[2]
    n4, n16 = tm // d4, tm // d16

    def emit(val, gp, nat_ref, p4_ref, p16_ref):
        sl = slice(gp * LANES, (gp + 1) * LANES)
        nat_ref[:, sl] = val.astype(BF16)
        pbuf_ref[gp] = val
        for c in range(d4):
            seg = pbuf_ref[gp, pl.ds(c, n4, stride=d4), :]
            dst = pl.multiple_of(c * (SUPER // d4) + step_in_super * n4, n4)
            p4_ref[pl.ds(dst, n4), sl] = seg.astype(BF16)
            p4buf_ref[gp, c * n4:(c + 1) * n4, :] = seg
        for r in range(d16):
            seg = p4buf_ref[gp, pl.ds((r % d4) * n4 + r // d4, n16, stride=d4), :]
            dst = pl.multiple_of(r * (SUPER // d16) + step_in_super * n16, n16)
            p16_ref[pl.ds(dst, n16), sl] = seg.astype(BF16)

    def rope(v):
        return v * cos + pltpu.roll(v, LANES // 2, axis=1) * sin

    for c0 in range(0, AT_W, MXU_COLS):
        aq = proj(C_AQ + c0)
        ak = proj(C_AK + c0)
        av = proj(C_AV + c0)
        for hh in range(MXU_COLS // LANES):
            gp = c0 // LANES + hh
            ls = slice(hh * LANES, (hh + 1) * LANES)
            emit(rope(aq[:, ls]) * qscale, gp, aq1_ref, aq4_ref, aq16_ref)
            emit(rope(ak[:, ls]), gp, ak1_ref, ak4_ref, ak16_ref)
            emit(av[:, ls], gp, av1_ref, av4_ref, av16_ref)


def _proj_call(x2, pos_row, nw, scale, shift, w_all, cw, alog_row, dtb_row, invf, sgn, tm):
    s, d = x2.shape
    row = lambda n: pl.BlockSpec((1, n), lambda i: (0, 0))
    full = lambda a: pl.BlockSpec(a.shape, lambda i: (0, 0))
    tile = lambda n: pl.BlockSpec((tm, n), lambda i: (i, 0))
    sup = pl.BlockSpec((SUPER, AT_W), lambda i: (i // (SUPER // tm), 0))
    bf = lambda n: jax.ShapeDtypeStruct((s, n), BF16)
    outs = [bf(DN_W)] * 4 + [jax.ShapeDtypeStruct((s, LANES), F32), bf(AT_W)] + [bf(AT_W)] * 9
    out_specs = [tile(DN_W)] * 4 + [tile(LANES), tile(AT_W)] + [tile(AT_W), sup, sup] * 3
    return pl.pallas_call(
        _proj_body,
        out_shape=outs,
        grid=(s // tm,),
        in_specs=[tile(d), pl.BlockSpec((1, tm), lambda i: (0, i)), row(d), row(d), row(d),
                  pl.BlockSpec(w_all.shape, lambda i: (0, 0), pipeline_mode=pl.Buffered(1)),
                  full(cw), row(LANES), row(LANES), full(invf), row(LANES)],
        out_specs=out_specs,
        scratch_shapes=[pltpu.VMEM((SUBLANES, 3 * DN_W), F32), pltpu.VMEM((tm, d), BF16),
                        pltpu.VMEM((AT_W // LANES, tm, LANES), F32),
                        pltpu.VMEM((AT_W // LANES, tm, LANES), F32)],
        compiler_params=pltpu.CompilerParams(dimension_semantics=("arbitrary",),
                                             vmem_limit_bytes=PROJ_VMEM_LIMIT),
        name="proj",
    )(x2, pos_row, nw, scale, shift, w_all, cw, alog_row, dtb_row, invf, sgn)


def _dn_prep_body(q_ref, k_ref, v_ref, bg_ref, pq_ref, bm_ref, op_ref, gl_ref):
    rows = q_ref.shape[0]
    nchunk = rows // CHUNK
    bg = bg_ref[...]
    ri = _iota((rows, rows), 0)
    ci = _iota((rows, rows), 1)
    tril_blk = jnp.where((ri // CHUNK == ci // CHUNK) & (ri >= ci), 1.0, 0.0).astype(BF16)
    g_hi = bg.astype(BF16)
    g_r1 = bg - g_hi.astype(F32)
    g_mid = g_r1.astype(BF16)
    g_lo = (g_r1 - g_mid.astype(F32)).astype(BF16)
    csum = jnp.dot(tril_blk, jnp.concatenate([g_hi, g_mid, g_lo], axis=1), preferred_element_type=F32)
    gcum = csum[:, 0:LANES] + csum[:, LANES:2 * LANES] + csum[:, 2 * LANES:3 * LANES]
    bfull = jnp.concatenate([jnp.broadcast_to(bg[:, hd:hd + 1], (rows, DN_HD)) for hd in range(DN_HEADS)], axis=1)
    gfull = jnp.concatenate([jnp.broadcast_to(gcum[:, DN_HEADS + hd:DN_HEADS + hd + 1], (rows, DN_HD))
                             for hd in range(DN_HEADS)], axis=1)
    eg = jnp.exp(gfull)
    q = q_ref[...].astype(F32)
    k = k_ref[...].astype(F32)
    kb = k * bfull
    vb = v_ref[...].astype(F32) * bfull
    q_dec = q * eg
    kbg = kb * eg

    r64 = _iota((CHUNK, CHUNK), 0)
    c64 = _iota((CHUNK, CHUNK), 1)
    tril = r64 >= c64
    strict = r64 > c64
    eye = jnp.where(r64 == c64, 1.0, 0.0).astype(F32)

    items = [(c, hd) for c in range(nchunk) for hd in range(DN_HEADS)]
    rsl = lambda c: slice(c * CHUNK, (c + 1) * CHUNK)
    hsl = lambda hd: slice(hd * DN_HD, (hd + 1) * DN_HD)
    g_last = [gfull[c * CHUNK + CHUNK - 1:(c + 1) * CHUNK, :] for c in range(nchunk)]
    k_dec = [k[rsl(c)] * jnp.exp(g_last[c] - gfull[rsl(c)]) for c in range(nchunk)]
    g_row = [gcum[rsl(c)].T for c in range(nchunk)]
    kq = [lax.dot_general(jnp.concatenate([kb[rsl(c), hsl(hd)].astype(BF16), q_ref[rsl(c), hsl(hd)]], axis=0),
                          k_ref[rsl(c), hsl(hd)], NT_DIMS, preferred_element_type=F32)
          for c, hd in items]
    dec = [jnp.exp(jnp.where(tril, gfull[rsl(c), hd * DN_HD:hd * DN_HD + CHUNK]
                             - g_row[c][DN_HEADS + hd:DN_HEADS + hd + 1, :], NEG)) for c, hd in items]
    a_low = [jnp.where(strict, m[0:CHUNK] * d, 0.0) for m, d in zip(kq, dec)]
    attn = [jnp.where(tril, m[CHUNK:2 * CHUNK] * d, 0.0).astype(BF16) for m, d in zip(kq, dec)]
    xinv = [eye - a for a in a_low]
    pw = [a.astype(BF16) for a in a_low]
    for _ in range(5):
        pw = [jnp.dot(p, p, preferred_element_type=F32).astype(BF16) for p in pw]
        xinv = [xi + jnp.dot(xi.astype(BF16), p, preferred_element_type=F32) for xi, p in zip(xinv, pw)]
    wu = [jnp.dot(xi.astype(BF16),
                  jnp.concatenate([kbg[rsl(c), hsl(hd)], vb[rsl(c), hsl(hd)]], axis=1).astype(BF16),
                  preferred_element_type=F32).astype(BF16) for xi, (c, hd) in zip(xinv, items)]
    pb = [jnp.dot(k_dec[c][:, hsl(hd)].T.astype(BF16), m, preferred_element_type=F32)
          for m, (c, hd) in zip(wu, items)]
    aw = [jnp.dot(a, m, preferred_element_type=F32) for a, m in zip(attn, wu)]
    for (c, hd), pbi, awi in zip(items, pb, aw):
        pq_ref[c, hd, 0:DN_HD, :] = pbi[:, 0:DN_HD].astype(BF16)
        pq_ref[c, hd, DN_HD:DN_HD + CHUNK, :] = (q_dec[rsl(c), hsl(hd)] - awi[:, 0:DN_HD]).astype(BF16)
        bm_ref[c, hd] = pbi[:, DN_HD:2 * DN_HD]
        op_ref[rsl(c), hsl(hd)] = awi[:, DN_HD:2 * DN_HD]
    for c in range(nchunk):
        gl_rows = [jnp.exp(g_last[c][:, hsl(hd)]) for hd in range(DN_HEADS)]
        gl_rows.append(jnp.zeros((SUBLANES - DN_HEADS, DN_HD), F32))
        gl_ref[c] = jnp.concatenate(gl_rows, axis=0)


def _dn_prep_call(qn, kn, v, bg, rows):
    s = qn.shape[0]
    nct = s // CHUNK
    ncb = rows // CHUNK
    tile = lambda n: pl.BlockSpec((rows, n), lambda i: (i, 0))
    return pl.pallas_call(
        _dn_prep_body,
        out_shape=[jax.ShapeDtypeStruct((nct, DN_HEADS, DN_HD + CHUNK, DN_HD), BF16),
                   jax.ShapeDtypeStruct((nct, DN_HEADS, DN_HD, DN_HD), F32),
                   jax.ShapeDtypeStruct((s, DN_W), F32),
                   jax.ShapeDtypeStruct((nct, SUBLANES, DN_HD), F32)],
        grid=(s // rows,),
        in_specs=[tile(DN_W), tile(DN_W), tile(DN_W), tile(LANES)],
        out_specs=[pl.BlockSpec((ncb, DN_HEADS, DN_HD + CHUNK, DN_HD), lambda i: (i, 0, 0, 0)),
                   pl.BlockSpec((ncb, DN_HEADS, DN_HD, DN_HD), lambda i: (i, 0, 0, 0)),
                   tile(DN_W),
                   pl.BlockSpec((ncb, SUBLANES, DN_HD), lambda i: (i, 0, 0))],
        compiler_params=_params(("arbitrary",)),
        name="dn_prep",
    )(qn, kn, v, bg)


def _dn_scan_body(pq_ref, bm_ref, op_ref, gl_ref, z_ref, nw_ref, o_ref, s_ref):
    ncb = pq_ref.shape[0]

    @pl.when(pl.program_id(0) == 0)
    def _():
        s_ref[...] = jnp.zeros(s_ref.shape, F32)

    def chunk(c, carry):
        rs = pl.ds(pl.multiple_of(c * CHUNK, CHUNK), CHUNK)
        for hd in range(DN_HEADS):
            hs = slice(hd * DN_HD, (hd + 1) * DN_HD)
            state = s_ref[hd]
            r = jnp.dot(pq_ref[c, hd], state.astype(BF16), preferred_element_type=F32)
            o = r[DN_HD:DN_HD + CHUNK] + op_ref[rs, hs]
            s_ref[hd] = gl_ref[c, hd:hd + 1, :] * state - r[0:DN_HD] + bm_ref[c, hd]
            ms = jnp.mean(o * o, axis=-1, keepdims=True)
            o_ref[rs, hs] = (o * lax.rsqrt(ms + EPS) * nw_ref[...] * z_ref[rs, hs].astype(F32)).astype(BF16)
        return carry

    lax.fori_loop(0, ncb, chunk, 0)


def _dn_scan_call(pq, bm, op, gl, zdn, nw_row, ncb):
    nct = pq.shape[0]
    s = op.shape[0]
    rows = ncb * CHUNK
    return pl.pallas_call(
        _dn_scan_body,
        out_shape=jax.ShapeDtypeStruct((s, DN_W), BF16),
        grid=(nct // ncb,),
        in_specs=[pl.BlockSpec((ncb, DN_HEADS, DN_HD + CHUNK, DN_HD), lambda i: (i, 0, 0, 0)),
                  pl.BlockSpec((ncb, DN_HEADS, DN_HD, DN_HD), lambda i: (i, 0, 0, 0)),
                  pl.BlockSpec((rows, DN_W), lambda i: (i, 0)),
                  pl.BlockSpec((ncb, SUBLANES, DN_HD), lambda i: (i, 0, 0)),
                  pl.BlockSpec((rows, DN_W), lambda i: (i, 0)),
                  pl.BlockSpec((1, DN_HD), lambda i: (0, 0))],
        out_specs=pl.BlockSpec((rows, DN_W), lambda i: (i, 0)),
        scratch_shapes=[pltpu.VMEM((DN_HEADS, DN_HD, DN_HD), F32)],
        compiler_params=_params(("arbitrary",)),
        name="dn_scan",
    )(pq, bm, op, gl, zdn, nw_row)


def _attn_body(q1, q4, q16, k1, k4, k16, v1, v4, v16, kp1, vp1, kp4a, kp4b, kp4c, kp4d,
               vp4a, vp4b, vp4c, vp4d, kp16, vp16, z_ref, anw_ref, o_ref, m_s, l_s, a_s, m_n, l_n, a_n):
    d4, d16 = DILATIONS[1], DILATIONS[2]
    seg4 = SUPER // d4
    kp4 = (kp4a, kp4b, kp4c, kp4d)
    vp4 = (vp4a, vp4b, vp4c, vp4d)
    kj = _iota((2 * Q_BLOCK, Q_BLOCK), 0)
    qi = _iota((2 * Q_BLOCK, Q_BLOCK), 1)
    rel = Q_BLOCK + qi - kj
    band = (rel >= 0) & (rel <= W_SUB)
    bias_in = jnp.where(band, 0.0, NEG).astype(BF16)
    bias_first = jnp.where(band & (kj >= Q_BLOCK), 0.0, NEG).astype(BF16)
    bias_edge = jnp.where(pl.program_id(0) == 0, bias_first, bias_in)
    one_hot = jnp.where(kj % Q_BLOCK == qi, 1.0, 0.0).astype(BF16)
    lane = _iota((Q_BLOCK, LANES), 1)
    lane_k = _iota((2 * Q_BLOCK, LANES), 1)
    head_a_q = (lane % AT_HD) < (AT_HD // 2)
    head_a = lane < AT_HD
    head_a_k = lane_k < AT_HD
    ro = _iota((4 * Q_BLOCK, LANES), 0)
    ones_rhs = jnp.where((ro < 2 * Q_BLOCK) == (_iota((4 * Q_BLOCK, LANES), 1) < AT_HD), 1.0, 0.0).astype(BF16)
    sr = _iota((2 * LANES, LANES), 0)
    ssq_rhs = jnp.where((sr % LANES) // AT_HD == _iota((2 * LANES, LANES), 1) // AT_HD, 1.0, 0.0).astype(BF16)
    zq = jnp.zeros((Q_BLOCK, LANES), BF16)
    zv = jnp.zeros((2 * Q_BLOCK, LANES), BF16)

    def scores(load):
        q, kb, vb, bias = load()
        q2 = jnp.concatenate([jnp.where(head_a_q, q, zq), jnp.where(head_a_q, zq, q)], axis=0)
        s = lax.dot_general(jnp.concatenate([q2, one_hot], axis=1), jnp.concatenate([kb, bias], axis=1),
                            NT_DIMS, preferred_element_type=F32)
        m = jnp.max(s, axis=-1, keepdims=True)
        return jnp.exp2(s - m).astype(BF16), m, vb

    def values(p, m, vb):
        pc = jnp.concatenate([p[0:Q_BLOCK], p[Q_BLOCK:2 * Q_BLOCK]], axis=1)
        rhs = jnp.concatenate([jnp.concatenate([jnp.where(head_a_k, vb, zv), jnp.where(head_a_k, zv, vb)], axis=0),
                               ones_rhs], axis=1)
        r = jnp.dot(pc, rhs, preferred_element_type=F32)
        return r[:, 0:LANES], r[:, LANES:2 * LANES], jnp.where(head_a, m[0:Q_BLOCK], m[Q_BLOCK:2 * Q_BLOCK])

    def run(loads, finish, post=None):
        n = len(loads)
        pending, done = {}, {}
        for i in range(n + ATT_LAG + 1):
            if i < n:
                pending[i] = scores(loads[i])
            j = i - ATT_LAG
            if 0 <= j < n:
                done[j] = finish(j, *values(*pending.pop(j)))
            if post is not None and 0 <= j - 1 < n:
                post(j - 1, done.pop(j - 1))

    def merged(acc, l, m, m_old, l_old, a_old):
        m_new = jnp.maximum(m_old, m)
        w_old = jnp.exp2(m_old - m_new)
        w_new = jnp.exp2(m - m_new)
        return a_old * w_old + acc * w_new, l_old * w_old + l * w_new, m_new

    def load16(r):
        rs = slice(r * Q_BLOCK, (r + 1) * Q_BLOCK)
        return lambda: (q16[rs, :], jnp.concatenate([kp16[rs, :], k16[rs, :]], axis=0),
                        jnp.concatenate([vp16[rs, :], v16[rs, :]], axis=0), bias_edge)

    def finish16(r, acc, l, m):
        rows = pl.ds((r % d4) * seg4 + r // d4, Q_BLOCK, stride=d4)
        a_s[rows, :] = acc
        l_s[rows, :] = l
        m_s[rows, :] = m

    run([load16(r) for r in range(d16)], finish16)

    def load4(r, b):
        base = r * seg4 + b * Q_BLOCK
        if b == 0:
            return lambda: (q4[base:base + Q_BLOCK, :],
                            jnp.concatenate([kp4[r][...], k4[base:base + Q_BLOCK, :]], axis=0),
                            jnp.concatenate([vp4[r][...], v4[base:base + Q_BLOCK, :]], axis=0), bias_edge)
        return lambda: (q4[base:base + Q_BLOCK, :], k4[base - Q_BLOCK:base + Q_BLOCK, :],
                        v4[base - Q_BLOCK:base + Q_BLOCK, :], bias_in)

    blocks4 = [(r, b) for b in range(seg4 // Q_BLOCK) for r in range(d4)]

    def finish4(j, acc, l, m):
        r, b = blocks4[j]
        rows = slice(r * seg4 + b * Q_BLOCK, r * seg4 + (b + 1) * Q_BLOCK)
        acc, l, m = merged(acc, l, m, m_s[rows, :], l_s[rows, :], a_s[rows, :])
        a_s[rows, :] = acc
        l_s[rows, :] = l
        m_s[rows, :] = m

    run([load4(r, b) for r, b in blocks4], finish4)

    for c in range(d4):
        src = slice(c * seg4, (c + 1) * seg4)
        dst = pl.ds(c, seg4, stride=d4)
        m_n[dst, :] = m_s[src, :]
        l_n[dst, :] = l_s[src, :]
        a_n[dst, :] = a_s[src, :]

    def load1(b):
        rs = slice(b * Q_BLOCK, (b + 1) * Q_BLOCK)
        if b == 0:
            return lambda: (q1[rs, :], jnp.concatenate([kp1[...], k1[rs, :]], axis=0),
                            jnp.concatenate([vp1[...], v1[rs, :]], axis=0), bias_edge)
        ks = slice((b - 1) * Q_BLOCK, (b + 1) * Q_BLOCK)
        return lambda: (q1[rs, :], k1[ks, :], v1[ks, :], bias_in)

    def finish1(b, acc, l, m):
        rs = slice(b * Q_BLOCK, (b + 1) * Q_BLOCK)
        acc, l, _ = merged(acc, l, m, m_n[rs, :], l_n[rs, :], a_n[rs, :])
        o = acc / l
        sq = o * o
        hi = sq.astype(BF16)
        return o, jnp.concatenate([hi, (sq - hi.astype(F32)).astype(BF16)], axis=1)

    def post1(b, res):
        o, sq2 = res
        rs = slice(b * Q_BLOCK, (b + 1) * Q_BLOCK)
        ssq = jnp.dot(sq2, ssq_rhs, preferred_element_type=F32)
        o_ref[rs, :] = (o * lax.rsqrt(ssq * (1.0 / AT_HD) + EPS) * anw_ref[...]
                        * z_ref[rs, :].astype(F32)).astype(BF16)

    run([load1(b) for b in range(SUPER // Q_BLOCK)], finish1, post1)


def _attn_call(aq, ak, av, zat, anw_row):
    s = zat.shape[0]
    d4 = DILATIONS[1]
    per_super = SUPER // Q_BLOCK
    cur = pl.BlockSpec((SUPER, LANES), lambda m, g: (m, g))
    prev16 = pl.BlockSpec((SUPER, LANES), lambda m, g: (jnp.maximum(m - 1, 0), g))
    prev1 = pl.BlockSpec((Q_BLOCK, LANES), lambda m, g: (jnp.maximum(m * per_super - 1, 0), g))
    prev4 = [pl.BlockSpec((Q_BLOCK, LANES),
                          lambda m, g, r=r: (jnp.maximum((m - 1) * per_super + (r + 1) * (per_super // d4) - 1, 0), g))
             for r in range(d4)]
    state = pltpu.VMEM((SUPER, LANES), F32)
    return pl.pallas_call(
        _attn_body,
        out_shape=jax.ShapeDtypeStruct((s, AT_W), BF16),
        grid=(s // SUPER, AT_W // LANES),
        in_specs=[cur] * 9 + [prev1, prev1] + prev4 + prev4 + [prev16, prev16, cur,
                                                               pl.BlockSpec((1, LANES), lambda m, g: (0, g))],
        out_specs=cur,
        scratch_shapes=[state] * 6,
        compiler_params=_params(("arbitrary", "arbitrary")),
        name="attn",
    )(*aq, *ak, *av, ak[0], av[0], *([ak[1]] * d4), *([av[1]] * d4), ak[2], av[2], zat, anw_row)


def _out_body(x_ref, odn_ref, oat_ref, gate_ref, w_ref, fnw_ref, o_ref):
    mixin = jnp.concatenate([odn_ref[...], oat_ref[...]], axis=1)
    mix = jnp.dot(mixin, w_ref[...], preferred_element_type=F32)
    y = x_ref[...] + gate_ref[...] * mix
    ms = jnp.mean(y * y, axis=-1, keepdims=True)
    o_ref[...] = y * lax.rsqrt(ms + EPS) * fnw_ref[...]


def _out_call(x2, odn, oat, gate, w_out, fnw_row, tm):
    s, d = x2.shape
    tile = lambda n: pl.BlockSpec((tm, n), lambda i: (i, 0))
    row = lambda n: pl.BlockSpec((1, n), lambda i: (0, 0))
    return pl.pallas_call(
        _out_body,
        out_shape=jax.ShapeDtypeStruct((s, d), F32),
        grid=(s // tm,),
        in_specs=[tile(d), tile(DN_W), tile(AT_W), row(d), pl.BlockSpec(w_out.shape, lambda i: (0, 0)), row(d)],
        out_specs=tile(d),
        compiler_params=_params(("arbitrary",)),
        name="out",
    )(x2, odn, oat, gate, w_out, fnw_row)


def _rope_pair_perm():
    half = AT_HD // 2
    idx = []
    for gp in range(AT_W // LANES):
        for l in range(LANES):
            part, hh, j = l // AT_HD, (l % AT_HD) // half, l % half
            idx.append((2 * gp + hh) * AT_HD + part * half + j)
    return np.asarray(idx, np.int32)


def kernel(x, c, positions, w_mod, b_mod, norm_w, w_in, conv_w, a_log, dt_bias, dn_norm_w, at_norm_w,
           w_out, final_norm_w):
    b, s, d = x.shape
    assert b == 1 and d == D_MODEL and w_mod.shape[0] == 1
    x2 = x.reshape(s, d)

    mod = _mod_call(c.reshape(d, 1), w_mod[0], b_mod[0].reshape(1, 3 * d))
    shift, scale, gate = mod[:, 0:d], mod[:, d:2 * d], mod[:, 2 * d:3 * d]

    wi = w_in[0]
    o_z, o_b, o_a, o_aq = 3 * DN_W, 4 * DN_W, 4 * DN_W + DN_HEADS, 4 * DN_W + 2 * DN_HEADS
    perm = _rope_pair_perm()
    w_aq = wi[:, o_aq:o_aq + AT_W][:, perm]
    w_ak = wi[:, o_aq + AT_W:o_aq + 2 * AT_W][:, perm]
    w_ba = jnp.pad(wi[:, o_b:o_b + 2 * DN_HEADS], ((0, 0), (0, LANES - 2 * DN_HEADS)))
    w_all = jnp.concatenate([wi[:, 0:o_z], wi[:, o_z:o_b], w_aq, w_ak,
                             wi[:, o_aq + 2 * AT_W:o_aq + 4 * AT_W], w_ba], axis=1).astype(BF16)
    assert w_all.shape[1] == W_COLS
    pad_heads = lambda v: jnp.pad(v.reshape(1, DN_HEADS), ((0, 0), (DN_HEADS, LANES - 2 * DN_HEADS)))
    half = AT_HD // 2
    invf = (ROPE_THETA ** (-jnp.arange(half, dtype=F32) / half)).reshape(half, 1)
    sgn = jnp.where(jnp.arange(LANES) < LANES // 2, -1.0, 1.0).astype(F32).reshape(1, LANES)

    assert s % SUPER == 0
    outs = _proj_call(x2, positions.reshape(1, s), norm_w[0].reshape(1, d), scale, shift, w_all, conv_w[0],
                      pad_heads(a_log[0]), pad_heads(dt_bias[0]), invf, sgn, tm=256)
    qn, kn, v, zdn, bg, zat = outs[0:6]
    aq, ak, av = outs[6:9], outs[9:12], outs[12:15]

    pq, bm, op, gl = _dn_prep_call(qn, kn, v, bg, rows=256)
    odn = _dn_scan_call(pq, bm, op, gl, zdn, dn_norm_w[0].reshape(1, DN_HD), ncb=8)
    oat = _attn_call(aq, ak, av, zat, jnp.tile(at_norm_w[0], AT_HEADS).reshape(1, AT_W))

    out = _out_call(x2, odn, oat, gate, w_out[0].astype(BF16), final_norm_w.reshape(1, d), tm=512)
    return out.reshape(b, s, d)
```

```python
import math

import numpy as np
import jax
import jax.numpy as jnp
from jax import lax
from jax.experimental import pallas as pl
from jax.experimental.pallas import tpu as pltpu

D_MODEL = 1024
DN_HEADS = 4
DN_HD = 128
DN_W = DN_HEADS * DN_HD
AT_HEADS = 8
AT_HD = 64
AT_W = AT_HEADS * AT_HD
CONV_K = 4
CHUNK = 64
DILATIONS = (1, 4, 16)
W_SUB = 128
Q_BLOCK = 128
SUPER = Q_BLOCK * DILATIONS[-1]
ATT_LAG = 2
ROPE_THETA = 10000.0
EPS = 1e-6
NEG = -1e30

LANES = 128
SUBLANES = 8
MXU_COLS = 256
VMEM_LIMIT = 48 * 1024 * 1024
PROJ_VMEM_LIMIT = 56 * 1024 * 1024

C_QKV, C_Z, C_AQ, C_AK, C_AV, C_AZ, C_BA = 0, 1536, 2048, 2560, 3072, 3584, 4096
W_COLS = C_BA + LANES

F32 = jnp.float32
BF16 = jnp.bfloat16
NT_DIMS = (((1,), (1,)), ((), ()))


def _sigmoid(v):
    return 1.0 / (1.0 + jnp.exp(-v))


def _iota(shape, dim):
    return lax.broadcasted_iota(jnp.int32, shape, dim)


def _params(sem):
    return pltpu.CompilerParams(dimension_semantics=sem, vmem_limit_bytes=VMEM_LIMIT)


def _mod_body(c_ref, w_ref, b_ref, o_ref):
    c = c_ref[...]
    sc = c * _sigmoid(c)
    o_ref[...] = jnp.sum(sc * w_ref[...], axis=0, keepdims=True) + b_ref[...]


def _mod_call(c_col, w_mod, b_mod):
    d, n = w_mod.shape
    tn = 512
    return pl.pallas_call(
        _mod_body,
        out_shape=jax.ShapeDtypeStruct((1, n), F32),
        grid=(n // tn,),
        in_specs=[pl.BlockSpec((d, 1), lambda j: (0, 0)),
                  pl.BlockSpec((d, tn), lambda j: (0, j)),
                  pl.BlockSpec((1, tn), lambda j: (0, j))],
        out_specs=pl.BlockSpec((1, tn), lambda j: (0, j)),
        compiler_params=_params(("arbitrary",)),
        name="mod",
    )(c_col, w_mod, b_mod)


def _proj_body(x_ref, pos_ref, nw_ref, sc_ref, sh_ref, w_ref, cw_ref, alog_ref, dtb_ref, invf_ref,
               sgn_ref, qn_ref, kn_ref, v_ref, zdn_ref, bg_ref, zat_ref,
               aq1_ref, aq4_ref, aq16_ref, ak1_ref, ak4_ref, ak16_ref, av1_ref, av4_ref, av16_ref,
               cbuf_ref, hb_ref, pbuf_ref, p4buf_ref):
    i = pl.program_id(0)
    tm = x_ref.shape[0]
    x = x_ref[...]
    ms = jnp.mean(x * x, axis=-1, keepdims=True)
    hb_ref[...] = ((x * lax.rsqrt(ms + EPS)) * (nw_ref[...] * (1.0 + sc_ref[...])) + sh_ref[...]).astype(BF16)

    def proj(c0, width=MXU_COLS):
        return jnp.dot(hb_ref[...], w_ref[:, c0:c0 + width], preferred_element_type=F32)

    def silu(v):
        return v * _sigmoid(v)

    @pl.when(i == 0)
    def _():
        cbuf_ref[...] = jnp.zeros((SUBLANES, 3 * DN_W), F32)

    sub = _iota((SUBLANES, MXU_COLS), 0)
    for c0 in range(0, 3 * DN_W, MXU_COLS):
        cs = slice(c0, c0 + MXU_COLS)
        xg = proj(C_QKV + c0)
        prev = cbuf_ref[:, cs]
        conv = xg * cw_ref[CONV_K - 1:CONV_K, cs]
        for j in range(1, CONV_K):
            sh = pltpu.roll(xg, j, axis=0)
            head = jnp.where(sub < j, pltpu.roll(prev, j, axis=0), sh[0:SUBLANES])
            conv = conv + jnp.concatenate([head, sh[SUBLANES:tm]], axis=0) * cw_ref[CONV_K - 1 - j:CONV_K - j, cs]
        cbuf_ref[:, cs] = xg[tm - SUBLANES:tm]
        y = silu(conv)
        kind, off = c0 // DN_W, c0 % DN_W
        for hh in range(MXU_COLS // DN_HD):
            yh = y[:, hh * DN_HD:(hh + 1) * DN_HD]
            dst = slice(off + hh * DN_HD, off + (hh + 1) * DN_HD)
            if kind == 2:
                v_ref[:, dst] = yh.astype(BF16)
            else:
                inv = lax.rsqrt(jnp.sum(yh * yh, axis=-1, keepdims=True) + EPS)
                if kind == 0:
                    qn_ref[:, dst] = (yh * (inv * (DN_HD ** -0.5))).astype(BF16)
                else:
                    kn_ref[:, dst] = (yh * inv).astype(BF16)

    for c0 in range(0, DN_W, MXU_COLS):
        zdn_ref[:, c0:c0 + MXU_COLS] = silu(proj(C_Z + c0)).astype(BF16)
        zat_ref[:, c0:c0 + MXU_COLS] = silu(proj(C_AZ + c0)).astype(BF16)

    ba = proj(C_BA, LANES)
    lane = _iota(ba.shape, 1)
    a = ba + dtb_ref[...]
    softplus = jnp.maximum(a, 0.0) + jnp.log(1.0 + jnp.exp(-jnp.abs(a)))
    g = -jnp.exp(alog_ref[...]) * softplus
    bg_ref[...] = jnp.where(lane < DN_HEADS, _sigmoid(ba), jnp.where(lane < 2 * DN_HEADS, g, 0.0))

    ang = invf_ref[...] * pos_ref[...].astype(F32)
    reps = LANES // (AT_HD // 2)
    cos = jnp.concatenate([jnp.cos(ang)] * reps, axis=0).T
    sin = jnp.concatenate([jnp.sin(ang)] * reps, axis=0).T * sgn_ref[...]
    qscale = (AT_HD ** -0.5) * math.log2(math.e)
    step_in_super = i % (SUPER // tm)
    d4, d16 = DILATIONS[1], DILATIONS[2]
    n4, n16 = tm // d4, tm // d16

    def emit(val, gp, nat_ref, p4_ref, p16_ref):
        sl = slice(gp * LANES, (gp + 1) * LANES)
        nat_ref[:, sl] = val.astype(BF16)
        pbuf_ref[gp] = val
        for c in range(d4):
            seg = pbuf_ref[gp, pl.ds(c, n4, stride=d4), :]
            dst = pl.multiple_of(c * (SUPER // d4) + step_in_super * n4, n4)
            p4_ref[pl.ds(dst, n4), sl] = seg.astype(BF16)
            p4buf_ref[gp, c * n4:(c + 1) * n4, :] = seg
        for r in range(d16):
            seg = p4buf_ref[gp, pl.ds((r % d4) * n4 + r // d4, n16, stride=d4), :]
            dst = pl.multiple_of(r * (SUPER // d16) + step_in_super * n16, n16)
            p16_ref[pl.ds(dst, n16), sl] = seg.astype(BF16)

    def rope(v):
        return v * cos + pltpu.roll(v, LANES // 2, axis=1) * sin

    for c0 in range(0, AT_W, MXU_COLS):
        aq = proj(C_AQ + c0)
        ak = proj(C_AK + c0)
        av = proj(C_AV + c0)
        for hh in range(MXU_COLS // LANES):
            gp = c0 // LANES + hh
            ls = slice(hh * LANES, (hh + 1) * LANES)
            emit(rope(aq[:, ls]) * qscale, gp, aq1_ref, aq4_ref, aq16_ref)
            emit(rope(ak[:, ls]), gp, ak1_ref, ak4_ref, ak16_ref)
            emit(av[:, ls], gp, av1_ref, av4_ref, av16_ref)


def _proj_call(x2, pos_row, nw, scale, shift, w_all, cw, alog_row, dtb_row, invf, sgn, tm):
    s, d = x2.shape
    row = lambda n: pl.BlockSpec((1, n), lambda i: (0, 0))
    full = lambda a: pl.BlockSpec(a.shape, lambda i: (0, 0))
    tile = lambda n: pl.BlockSpec((tm, n), lambda i: (i, 0))
    sup = pl.BlockSpec((SUPER, AT_W), lambda i: (i // (SUPER // tm), 0))
    bf = lambda n: jax.ShapeDtypeStruct((s, n), BF16)
    outs = [bf(DN_W)] * 4 + [jax.ShapeDtypeStruct((s, LANES), F32), bf(AT_W)] + [bf(AT_W)] * 9
    out_specs = [tile(DN_W)] * 4 + [tile(LANES), tile(AT_W)] + [tile(AT_W), sup, sup] * 3
    return pl.pallas_call(
        _proj_body,
        out_shape=outs,
        grid=(s // tm,),
        in_specs=[tile(d), pl.BlockSpec((1, tm), lambda i: (0, i)), row(d), row(d), row(d),
                  pl.BlockSpec(w_all.shape, lambda i: (0, 0), pipeline_mode=pl.Buffered(1)),
                  full(cw), row(LANES), row(LANES), full(invf), row(LANES)],
        out_specs=out_specs,
        scratch_shapes=[pltpu.VMEM((SUBLANES, 3 * DN_W), F32), pltpu.VMEM((tm, d), BF16),
                        pltpu.VMEM((AT_W // LANES, tm, LANES), F32),
                        pltpu.VMEM((AT_W // LANES, tm, LANES), F32)],
        compiler_params=pltpu.CompilerParams(dimension_semantics=("arbitrary",),
                                             vmem_limit_bytes=PROJ_VMEM_LIMIT),
        name="proj",
    )(x2, pos_row, nw, scale, shift, w_all, cw, alog_row, dtb_row, invf, sgn)


def _dn_prep_body(q_ref, k_ref, v_ref, bg_ref, w_ref, bt_ref, op_ref, gl_ref):
    rows = q_ref.shape[0]
    nchunk = rows // CHUNK
    wide = DN_HEADS * CHUNK
    bg = bg_ref[...]
    cs_rows = min(rows, MXU_COLS)
    ri = _iota((cs_rows, cs_rows), 0)
    ci = _iota((cs_rows, cs_rows), 1)
    tril_blk = jnp.where((ri // CHUNK == ci // CHUNK) & (ri >= ci), 1.0, 0.0).astype(BF16)
    g_hi = bg.astype(BF16)
    g_r1 = bg - g_hi.astype(F32)
    g_mid = g_r1.astype(BF16)
    g_lo = (g_r1 - g_mid.astype(F32)).astype(BF16)
    g3 = jnp.concatenate([g_hi, g_mid, g_lo], axis=1)
    csum = jnp.concatenate([jnp.dot(tril_blk, g3[r0:r0 + cs_rows], preferred_element_type=F32)
                            for r0 in range(0, rows, cs_rows)], axis=0)
    gcum = csum[:, 0:LANES] + csum[:, LANES:2 * LANES] + csum[:, 2 * LANES:3 * LANES]
    bfull = jnp.concatenate([jnp.broadcast_to(bg[:, hd:hd + 1], (rows, DN_HD)) for hd in range(DN_HEADS)], axis=1)
    gfull = jnp.concatenate([jnp.broadcast_to(gcum[:, DN_HEADS + hd:DN_HEADS + hd + 1], (rows, DN_HD))
                             for hd in range(DN_HEADS)], axis=1)
    eg = jnp.exp(gfull)
    q = q_ref[...].astype(F32)
    k = k_ref[...].astype(F32)
    kb = k * bfull
    vb = v_ref[...].astype(F32) * bfull
    q_dec = q * eg
    kbg = kb * eg

    r64 = _iota((CHUNK, wide), 0)
    j64 = _iota((CHUNK, wide), 1) % CHUNK
    tril = r64 >= j64
    strict = r64 > j64
    eye = jnp.where(r64 == j64, 1.0, 0.0).astype(F32)
    low_half = _iota((CHUNK, LANES), 1) < CHUNK
    bd64 = _iota((wide, wide), 0) // CHUNK == _iota((wide, wide), 1) // CHUNK
    bd_k = _iota((wide, DN_W), 0) // CHUNK == _iota((wide, DN_W), 1) // DN_HD

    def block_diag(m):
        return jnp.where(bd64, jnp.concatenate([m.astype(BF16)] * DN_HEADS, axis=0), jnp.zeros((wide, wide), BF16))

    rsl = lambda c: slice(c * CHUNK, (c + 1) * CHUNK)
    hsl = lambda hd: slice(hd * DN_HD, (hd + 1) * DN_HD)
    chunks = range(nchunk)
    g_last = [gfull[c * CHUNK + CHUNK - 1:(c + 1) * CHUNK, :] for c in chunks]
    k_dec = [k[rsl(c)] * jnp.exp(g_last[c] - gfull[rsl(c)]) for c in chunks]
    g_row = [jnp.concatenate([gcum[rsl(c)], gcum[rsl(c)]], axis=0).T for c in chunks]
    dec = []
    for c in chunks:
        halves = []
        for pr in range(DN_HEADS // 2):
            ha, hb = 2 * pr, 2 * pr + 1
            col = jnp.where(low_half, gfull[rsl(c), hsl(ha)], gfull[rsl(c), hsl(hb)])
            row = jnp.where(low_half[0:1], g_row[c][DN_HEADS + ha:DN_HEADS + ha + 1, :],
                            g_row[c][DN_HEADS + hb:DN_HEADS + hb + 1, :])
            halves.append(col - row)
        dec.append(jnp.exp(jnp.where(tril, jnp.concatenate(halves, axis=1), NEG)))
    kq = [lax.dot_general(jnp.concatenate([kb[rsl(c)].astype(BF16), q_ref[rsl(c), :]], axis=0),
                          jnp.where(bd_k, jnp.concatenate([k_ref[rsl(c), :]] * DN_HEADS, axis=0),
                                    jnp.zeros((wide, DN_W), BF16)),
                          NT_DIMS, preferred_element_type=F32) for c in chunks]
    a_low = [jnp.where(strict, m[0:CHUNK] * d, 0.0) for m, d in zip(kq, dec)]
    attn = [jnp.where(tril, m[CHUNK:2 * CHUNK] * d, 0.0) for m, d in zip(kq, dec)]
    bmat = [-a for a in a_low]
    ymat = [eye + b for b in bmat]
    bmat = [jnp.dot(b.astype(BF16), block_diag(b), preferred_element_type=F32) for b in bmat]
    for _ in range(4):
        prod = [jnp.dot(jnp.concatenate([y, b], axis=0).astype(BF16), block_diag(b), preferred_element_type=F32)
                for y, b in zip(ymat, bmat)]
        ymat = [y + p[0:CHUNK] for y, p in zip(ymat, prod)]
        bmat = [p[CHUNK:2 * CHUNK] for p in prod]
    ymat = [y + jnp.dot(y.astype(BF16), block_diag(b), preferred_element_type=F32) for y, b in zip(ymat, bmat)]
    rhs = [jnp.concatenate([jnp.concatenate([kbg[rsl(c), hsl(hd)], vb[rsl(c), hsl(hd)]], axis=1)
                            for hd in range(DN_HEADS)], axis=0).astype(BF16) for c in chunks]
    wu = [jnp.dot(block_diag(y), r, preferred_element_type=F32) for y, r in zip(ymat, rhs)]
    wut = [m.T.astype(BF16) for m in wu]
    kdbd = [jnp.where(bd_k, jnp.concatenate([k_dec[c].astype(BF16)] * DN_HEADS, axis=0),
                      jnp.zeros((wide, DN_W), BF16)) for c in chunks]
    pbt = [jnp.dot(a, b, preferred_element_type=F32) for a, b in zip(wut, kdbd)]
    aw = [jnp.dot(block_diag(a), m.astype(BF16), preferred_element_type=F32) for a, m in zip(attn, wu)]
    qpt = [(jnp.concatenate([q_dec[rsl(c), hsl(hd)] for hd in range(DN_HEADS)], axis=0) - a[:, 0:DN_HD]).T
           .astype(BF16) for c, a in zip(chunks, aw)]
    for c in chunks:
        for hd in range(DN_HEADS):
            pair = slice((hd // 2) * LANES, (hd // 2 + 1) * LANES)
            w_ref[c, hd] = jnp.concatenate([pbt[c][0:DN_HD, hsl(hd)].astype(BF16), qpt[c][:, pair]], axis=1)
            bt_ref[c, hd] = pbt[c][DN_HD:2 * DN_HD, hsl(hd)]
            op_ref[rsl(c), hsl(hd)] = aw[c][hd * CHUNK:(hd + 1) * CHUNK, DN_HD:2 * DN_HD]
        gl_rows = [jnp.exp(g_last[c][:, hsl(hd)]) for hd in range(DN_HEADS)]
        gl_rows.append(jnp.zeros((SUBLANES - DN_HEADS, DN_HD), F32))
        gl_ref[c] = jnp.concatenate(gl_rows, axis=0)


def _dn_prep_call(qn, kn, v, bg, rows):
    s = qn.shape[0]
    nct = s // CHUNK
    ncb = rows // CHUNK
    tile = lambda n: pl.BlockSpec((rows, n), lambda i: (i, 0))
    return pl.pallas_call(
        _dn_prep_body,
        out_shape=[jax.ShapeDtypeStruct((nct, DN_HEADS, DN_HD, 2 * DN_HD), BF16),
                   jax.ShapeDtypeStruct((nct, DN_HEADS, DN_HD, DN_HD), F32),
                   jax.ShapeDtypeStruct((s, DN_W), F32),
                   jax.ShapeDtypeStruct((nct, SUBLANES, DN_HD), F32)],
        grid=(s // rows,),
        in_specs=[tile(DN_W), tile(DN_W), tile(DN_W), tile(LANES)],
        out_specs=[pl.BlockSpec((ncb, DN_HEADS, DN_HD, 2 * DN_HD), lambda i: (i, 0, 0, 0)),
                   pl.BlockSpec((ncb, DN_HEADS, DN_HD, DN_HD), lambda i: (i, 0, 0, 0)),
                   tile(DN_W),
                   pl.BlockSpec((ncb, SUBLANES, DN_HD), lambda i: (i, 0, 0))],
        compiler_params=_params(("arbitrary",)),
        name="dn_prep",
    )(qn, kn, v, bg)


def _dn_scan_body(w_ref, bt_ref, op_ref, gl_ref, z_ref, nw_ref, o_ref, st_ref):
    ncb = w_ref.shape[0]

    @pl.when(pl.program_id(0) == 0)
    def _():
        st_ref[...] = jnp.zeros(st_ref.shape, F32)

    def chunk(c, carry):
        rs = pl.ds(pl.multiple_of(c * CHUNK, CHUNK), CHUNK)
        for hd in range(DN_HEADS):
            hs = slice(hd * DN_HD, (hd + 1) * DN_HD)
            st = st_ref[hd]
            r = jnp.dot(st.astype(BF16), w_ref[c, hd], preferred_element_type=F32)
            st_ref[hd] = gl_ref[c, hd:hd + 1, :] * st - r[:, 0:DN_HD] + bt_ref[c, hd]
            half = (hd % 2) * CHUNK
            o = r[:, DN_HD:2 * DN_HD].T[half:half + CHUNK] + op_ref[rs, hs]
            ms = jnp.mean(o * o, axis=-1, keepdims=True)
            o_ref[rs, hs] = (o * lax.rsqrt(ms + EPS) * nw_ref[...] * z_ref[rs, hs].astype(F32)).astype(BF16)
        return carry

    lax.fori_loop(0, ncb, chunk, 0, unroll=True)


def _dn_scan_call(wt, bt, op, gl, zdn, nw_row, ncb):
    nct = wt.shape[0]
    s = op.shape[0]
    rows = ncb * CHUNK
    return pl.pallas_call(
        _dn_scan_body,
        out_shape=jax.ShapeDtypeStruct((s, DN_W), BF16),
        grid=(nct // ncb,),
        in_specs=[pl.BlockSpec((ncb, DN_HEADS, DN_HD, 2 * DN_HD), lambda i: (i, 0, 0, 0)),
                  pl.BlockSpec((ncb, DN_HEADS, DN_HD, DN_HD), lambda i: (i, 0, 0, 0)),
                  pl.BlockSpec((rows, DN_W), lambda i: (i, 0)),
                  pl.BlockSpec((ncb, SUBLANES, DN_HD), lambda i: (i, 0, 0)),
                  pl.BlockSpec((rows, DN_W), lambda i: (i, 0)),
                  pl.BlockSpec((1, DN_HD), lambda i: (0, 0))],
        out_specs=pl.BlockSpec((rows, DN_W), lambda i: (i, 0)),
        scratch_shapes=[pltpu.VMEM((DN_HEADS, DN_HD, DN_HD), F32)],
        compiler_params=_params(("arbitrary",)),
        name="dn_scan",
    )(wt, bt, op, gl, zdn, nw_row)


def _attn_body(q1, q4, q16, k1, k4, k16, v1, v4, v16, kp1, vp1, kp4a, kp4b, kp4c, kp4d,
               vp4a, vp4b, vp4c, vp4d, kp16, vp16, z_ref, anw_ref, o_ref, m_s, l_s, a_s, m_n, l_n, a_n):
    d4, d16 = DILATIONS[1], DILATIONS[2]
    seg4 = SUPER // d4
    kp4 = (kp4a, kp4b, kp4c, kp4d)
    vp4 = (vp4a, vp4b, vp4c, vp4d)
    kj = _iota((2 * Q_BLOCK, Q_BLOCK), 0)
    qi = _iota((2 * Q_BLOCK, Q_BLOCK), 1)
    rel = Q_BLOCK + qi - kj
    band = (rel >= 0) & (rel <= W_SUB)
    bias_in = jnp.where(band, 0.0, NEG).astype(BF16)
    bias_first = jnp.where(band & (kj >= Q_BLOCK), 0.0, NEG).astype(BF16)
    bias_edge = jnp.where(pl.program_id(0) == 0, bias_first, bias_in)
    one_hot = jnp.where(kj % Q_BLOCK == qi, 1.0, 0.0).astype(BF16)
    lane = _iota((Q_BLOCK, LANES), 1)
    lane_k = _iota((2 * Q_BLOCK, LANES), 1)
    head_a_q = (lane % AT_HD) < (AT_HD // 2)
    head_a = lane < AT_HD
    head_a_k = lane_k < AT_HD
    ro = _iota((4 * Q_BLOCK, LANES), 0)
    ones_rhs = jnp.where((ro < 2 * Q_BLOCK) == (_iota((4 * Q_BLOCK, LANES), 1) < AT_HD), 1.0, 0.0).astype(BF16)
    sr = _iota((2 * LANES, LANES), 0)
    ssq_rhs = jnp.where((sr % LANES) // AT_HD == _iota((2 * LANES, LANES), 1) // AT_HD, 1.0, 0.0).astype(BF16)
    zq = jnp.zeros((Q_BLOCK, LANES), BF16)
    zv = jnp.zeros((2 * Q_BLOCK, LANES), BF16)

    def scores(load):
        q, kb, vb, bias = load()
        q2 = jnp.concatenate([jnp.where(head_a_q, q, zq), jnp.where(head_a_q, zq, q)], axis=0)
        s = lax.dot_general(jnp.concatenate([q2, one_hot], axis=1), jnp.concatenate([kb, bias], axis=1),
                            NT_DIMS, preferred_element_type=F32)
        m = jnp.max(s, axis=-1, keepdims=True)
        return jnp.exp2(s - m).astype(BF16), m, vb

    def values(p, m, vb):
        pc = jnp.concatenate([p[0:Q_BLOCK], p[Q_BLOCK:2 * Q_BLOCK]], axis=1)
        rhs = jnp.concatenate([jnp.concatenate([jnp.where(head_a_k, vb, zv), jnp.where(head_a_k, zv, vb)], axis=0),
                               ones_rhs], axis=1)
        r = jnp.dot(pc, rhs, preferred_element_type=F32)
        return r[:, 0:LANES], r[:, LANES:2 * LANES], jnp.where(head_a, m[0:Q_BLOCK], m[Q_BLOCK:2 * Q_BLOCK])

    def run(loads, finish, post=None):
        n = len(loads)
        pending, done = {}, {}
        for i in range(n + ATT_LAG + 1):
            if i < n:
                pending[i] = scores(loads[i])
            j = i - ATT_LAG
            if 0 <= j < n:
                done[j] = finish(j, *values(*pending.pop(j)))
            if post is not None and 0 <= j - 1 < n:
                post(j - 1, done.pop(j - 1))

    def merged(acc, l, m, m_old, l_old, a_old):
        m_new = jnp.maximum(m_old, m)
        w_old = jnp.exp2(m_old - m_new)
        w_new = jnp.exp2(m - m_new)
        return a_old * w_old + acc * w_new, l_old * w_old + l * w_new, m_new

    def load16(r):
        rs = slice(r * Q_BLOCK, (r + 1) * Q_BLOCK)
        return lambda: (q16[rs, :], jnp.concatenate([kp16[rs, :], k16[rs, :]], axis=0),
                        jnp.concatenate([vp16[rs, :], v16[rs, :]], axis=0), bias_edge)

    def finish16(r, acc, l, m):
        rows = pl.ds((r % d4) * seg4 + r // d4, Q_BLOCK, stride=d4)
        a_s[rows, :] = acc
        l_s[rows, :] = l
        m_s[rows, :] = m

    run([load16(r) for r in range(d16)], finish16)

    def load4(r, b):
        base = r * seg4 + b * Q_BLOCK
        if b == 0:
            return lambda: (q4[base:base + Q_BLOCK, :],
                            jnp.concatenate([kp4[r][...], k4[base:base + Q_BLOCK, :]], axis=0),
                            jnp.concatenate([vp4[r][...], v4[base:base + Q_BLOCK, :]], axis=0), bias_edge)
        return lambda: (q4[base:base + Q_BLOCK, :], k4[base - Q_BLOCK:base + Q_BLOCK, :],
                        v4[base - Q_BLOCK:base + Q_BLOCK, :], bias_in)

    blocks4 = [(r, b) for b in range(seg4 // Q_BLOCK) for r in range(d4)]

    def finish4(j, acc, l, m):
        r, b = blocks4[j]
        rows = slice(r * seg4 + b * Q_BLOCK, r * seg4 + (b + 1) * Q_BLOCK)
        acc, l, m = merged(acc, l, m, m_s[rows, :], l_s[rows, :], a_s[rows, :])
        a_s[rows, :] = acc
        l_s[rows, :] = l
        m_s[rows, :] = m

    run([load4(r, b) for r, b in blocks4], finish4)

    for c in range(d4):
        src = slice(c * seg4, (c + 1) * seg4)
        dst = pl.ds(c, seg4, stride=d4)
        m_n[dst, :] = m_s[src, :]
        l_n[dst, :] = l_s[src, :]
        a_n[dst, :] = a_s[src, :]

    def load1(b):
        rs = slice(b * Q_BLOCK, (b + 1) * Q_BLOCK)
        if b == 0:
            return lambda: (q1[rs, :], jnp.concatenate([kp1[...], k1[rs, :]], axis=0),
                            jnp.concatenate([vp1[...], v1[rs, :]], axis=0), bias_edge)
        ks = slice((b - 1) * Q_BLOCK, (b + 1) * Q_BLOCK)
        return lambda: (q1[rs, :], k1[ks, :], v1[ks, :], bias_in)

    def finish1(b, acc, l, m):
        rs = slice(b * Q_BLOCK, (b + 1) * Q_BLOCK)
        acc, l, _ = merged(acc, l, m, m_n[rs, :], l_n[rs, :], a_n[rs, :])
        o = acc / l
        sq = o * o
        hi = sq.astype(BF16)
        return o, jnp.concatenate([hi, (sq - hi.astype(F32)).astype(BF16)], axis=1)

    def post1(b, res):
        o, sq2 = res
        rs = slice(b * Q_BLOCK, (b + 1) * Q_BLOCK)
        ssq = jnp.dot(sq2, ssq_rhs, preferred_element_type=F32)
        o_ref[rs, :] = (o * lax.rsqrt(ssq * (1.0 / AT_HD) + EPS) * anw_ref[...]
                        * z_ref[rs, :].astype(F32)).astype(BF16)

    run([load1(b) for b in range(SUPER // Q_BLOCK)], finish1, post1)


def _attn_call(aq, ak, av, zat, anw_row):
    s = zat.shape[0]
    d4 = DILATIONS[1]
    per_super = SUPER // Q_BLOCK
    cur = pl.BlockSpec((SUPER, LANES), lambda m, g: (m, g))
    prev16 = pl.BlockSpec((SUPER, LANES), lambda m, g: (jnp.maximum(m - 1, 0), g))
    prev1 = pl.BlockSpec((Q_BLOCK, LANES), lambda m, g: (jnp.maximum(m * per_super - 1, 0), g))
    prev4 = [pl.BlockSpec((Q_BLOCK, LANES),
                          lambda m, g, r=r: (jnp.maximum((m - 1) * per_super + (r + 1) * (per_super // d4) - 1, 0), g))
             for r in range(d4)]
    state = pltpu.VMEM((SUPER, LANES), F32)
    return pl.pallas_call(
        _attn_body,
        out_shape=jax.ShapeDtypeStruct((s, AT_W), BF16),
        grid=(s // SUPER, AT_W // LANES),
        in_specs=[cur] * 9 + [prev1, prev1] + prev4 + prev4 + [prev16, prev16, cur,
                                                               pl.BlockSpec((1, LANES), lambda m, g: (0, g))],
        out_specs=cur,
        scratch_shapes=[state] * 6,
        compiler_params=_params(("arbitrary", "arbitrary")),
        name="attn",
    )(*aq, *ak, *av, ak[0], av[0], *([ak[1]] * d4), *([av[1]] * d4), ak[2], av[2], zat, anw_row)


def _out_body(x_ref, odn_ref, oat_ref, gate_ref, w_ref, fnw_ref, o_ref):
    mixin = jnp.concatenate([odn_ref[...], oat_ref[...]], axis=1)
    mix = jnp.dot(mixin, w_ref[...], preferred_element_type=F32)
    y = x_ref[...] + gate_ref[...] * mix
    ms = jnp.mean(y * y, axis=-1, keepdims=True)
    o_ref[...] = y * lax.rsqrt(ms + EPS) * fnw_ref[...]


def _out_call(x2, odn, oat, gate, w_out, fnw_row, tm):
    s, d = x2.shape
    tile = lambda n: pl.BlockSpec((tm, n), lambda i: (i, 0))
    row = lambda n: pl.BlockSpec((1, n), lambda i: (0, 0))
    return pl.pallas_call(
        _out_body,
        out_shape=jax.ShapeDtypeStruct((s, d), F32),
        grid=(s // tm,),
        in_specs=[tile(d), tile(DN_W), tile(AT_W), row(d), pl.BlockSpec(w_out.shape, lambda i: (0, 0)), row(d)],
        out_specs=tile(d),
        compiler_params=_params(("arbitrary",)),
        name="out",
    )(x2, odn, oat, gate, w_out, fnw_row)


def _rope_pair_perm():
    half = AT_HD // 2
    idx = []
    for gp in range(AT_W // LANES):
        for l in range(LANES):
            part, hh, j = l // AT_HD, (l % AT_HD) // half, l % half
            idx.append((2 * gp + hh) * AT_HD + part * half + j)
    return np.asarray(idx, np.int32)


def kernel(x, c, positions, w_mod, b_mod, norm_w, w_in, conv_w, a_log, dt_bias, dn_norm_w, at_norm_w,
           w_out, final_norm_w):
    b, s, d = x.shape
    assert b == 1 and d == D_MODEL and w_mod.shape[0] == 1
    x2 = x.reshape(s, d)

    mod = _mod_call(c.reshape(d, 1), w_mod[0], b_mod[0].reshape(1, 3 * d))
    shift, scale, gate = mod[:, 0:d], mod[:, d:2 * d], mod[:, 2 * d:3 * d]

    wi = w_in[0]
    o_z, o_b, o_a, o_aq = 3 * DN_W, 4 * DN_W, 4 * DN_W + DN_HEADS, 4 * DN_W + 2 * DN_HEADS
    perm = _rope_pair_perm()
    w_aq = wi[:, o_aq:o_aq + AT_W][:, perm]
    w_ak = wi[:, o_aq + AT_W:o_aq + 2 * AT_W][:, perm]
    w_ba = jnp.pad(wi[:, o_b:o_b + 2 * DN_HEADS], ((0, 0), (0, LANES - 2 * DN_HEADS)))
    w_all = jnp.concatenate([wi[:, 0:o_z], wi[:, o_z:o_b], w_aq, w_ak,
                             wi[:, o_aq + 2 * AT_W:o_aq + 4 * AT_W], w_ba], axis=1).astype(BF16)
    assert w_all.shape[1] == W_COLS
    pad_heads = lambda v: jnp.pad(v.reshape(1, DN_HEADS), ((0, 0), (DN_HEADS, LANES - 2 * DN_HEADS)))
    half = AT_HD // 2
    invf = (ROPE_THETA ** (-jnp.arange(half, dtype=F32) / half)).reshape(half, 1)
    sgn = jnp.where(jnp.arange(LANES) < LANES // 2, -1.0, 1.0).astype(F32).reshape(1, LANES)

    assert s % SUPER == 0
    outs = _proj_call(x2, positions.reshape(1, s), norm_w[0].reshape(1, d), scale, shift, w_all, conv_w[0],
                      pad_heads(a_log[0]), pad_heads(dt_bias[0]), invf, sgn, tm=256)
    qn, kn, v, zdn, bg, zat = outs[0:6]
    aq, ak, av = outs[6:9], outs[9:12], outs[12:15]

    wt, bt, op, gl = _dn_prep_call(qn, kn, v, bg, rows=512)
    odn = _dn_scan_call(wt, bt, op, gl, zdn, dn_norm_w[0].reshape(1, DN_HD), ncb=8)
    oat = _attn_call(aq, ak, av, zat, jnp.tile(at_norm_w[0], AT_HEADS).reshape(1, AT_W))

    out = _out_call(x2, odn, oat, gate, w_out[0].astype(BF16), final_norm_w.reshape(1, d), tm=512)
    return out.reshape(b, s, d)
```

```python
import math

import numpy as np
import jax
import jax.numpy as jnp
from jax import lax
from jax.experimental import pallas as pl
from jax.experimental.pallas import tpu as pltpu

D_MODEL = 1024
DN_HEADS = 4
DN_HD = 128
DN_W = DN_HEADS * DN_HD
AT_HEADS = 8
AT_HD = 64
AT_W = AT_HEADS * AT_HD
CONV_K = 4
CHUNK = 64
DILATIONS = (1, 4, 16)
W_SUB = 128
Q_BLOCK = 128
SUPER = Q_BLOCK * DILATIONS[-1]
ATT_LAG = 2
ROPE_THETA = 10000.0
EPS = 1e-6
NEG = -1e30

LANES = 128
SUBLANES = 8
MXU_COLS = 256
VMEM_LIMIT = 48 * 1024 * 1024
PROJ_VMEM_LIMIT = 56 * 1024 * 1024

C_QKV, C_Z, C_AQ, C_AK, C_AV, C_AZ, C_BA = 0, 1536, 2048, 2560, 3072, 3584, 4096
W_COLS = C_BA + LANES

F32 = jnp.float32
BF16 = jnp.bfloat16
NT_DIMS = (((1,), (1,)), ((), ()))


def _sigmoid(v):
    return 1.0 / (1.0 + jnp.exp2(v * (-math.log2(math.e))))


def _iota(shape, dim):
    return lax.broadcasted_iota(jnp.int32, shape, dim)


def _params(sem):
    return pltpu.CompilerParams(dimension_semantics=sem, vmem_limit_bytes=VMEM_LIMIT)


def _mod_body(c_ref, w_ref, b_ref, o_ref):
    c = c_ref[...]
    sc = c * _sigmoid(c)
    o_ref[...] = jnp.sum(sc * w_ref[...], axis=0, keepdims=True) + b_ref[...]


def _mod_call(c_col, w_mod, b_mod):
    d, n = w_mod.shape
    tn = 512
    return pl.pallas_call(
        _mod_body,
        out_shape=jax.ShapeDtypeStruct((1, n), F32),
        grid=(n // tn,),
        in_specs=[pl.BlockSpec((d, 1), lambda j: (0, 0)),
                  pl.BlockSpec((d, tn), lambda j: (0, j)),
                  pl.BlockSpec((1, tn), lambda j: (0, j))],
        out_specs=pl.BlockSpec((1, tn), lambda j: (0, j)),
        compiler_params=_params(("arbitrary",)),
        name="mod",
    )(c_col, w_mod, b_mod)


def _proj_body(x_ref, pos_ref, nw_ref, sc_ref, sh_ref, w_ref, cw_ref, alog_ref, dtb_ref, invf_ref,
               sgn_ref, qn_ref, kn_ref, v_ref, zdn_ref, bg_ref, zat_ref,
               aq1_ref, aq4_ref, aq16_ref, ak1_ref, ak4_ref, ak16_ref, av1_ref, av4_ref, av16_ref,
               cbuf_ref, hb_ref, pbuf_ref, p4buf_ref):
    i = pl.program_id(0)
    tm = x_ref.shape[0]
    x = x_ref[...]
    ms = jnp.mean(x * x, axis=-1, keepdims=True)
    hb_ref[...] = ((x * lax.rsqrt(ms + EPS)) * (nw_ref[...] * (1.0 + sc_ref[...])) + sh_ref[...]).astype(BF16)

    def proj(c0, width=MXU_COLS):
        return jnp.dot(hb_ref[...], w_ref[:, c0:c0 + width], preferred_element_type=F32)

    def silu(v):
        return v * _sigmoid(v)

    @pl.when(i == 0)
    def _():
        cbuf_ref[...] = jnp.zeros((SUBLANES, 3 * DN_W), F32)

    sub = _iota((SUBLANES, MXU_COLS), 0)
    for c0 in range(0, 3 * DN_W, MXU_COLS):
        cs = slice(c0, c0 + MXU_COLS)
        xg = proj(C_QKV + c0)
        prev = cbuf_ref[:, cs]
        conv = xg * cw_ref[CONV_K - 1:CONV_K, cs]
        for j in range(1, CONV_K):
            sh = pltpu.roll(xg, j, axis=0)
            head = jnp.where(sub < j, pltpu.roll(prev, j, axis=0), sh[0:SUBLANES])
            conv = conv + jnp.concatenate([head, sh[SUBLANES:tm]], axis=0) * cw_ref[CONV_K - 1 - j:CONV_K - j, cs]
        cbuf_ref[:, cs] = xg[tm - SUBLANES:tm]
        y = silu(conv)
        kind, off = c0 // DN_W, c0 % DN_W
        for hh in range(MXU_COLS // DN_HD):
            yh = y[:, hh * DN_HD:(hh + 1) * DN_HD]
            dst = slice(off + hh * DN_HD, off + (hh + 1) * DN_HD)
            if kind == 2:
                v_ref[:, dst] = yh.astype(BF16)
            else:
                inv = lax.rsqrt(jnp.sum(yh * yh, axis=-1, keepdims=True) + EPS)
                if kind == 0:
                    qn_ref[:, dst] = (yh * (inv * (DN_HD ** -0.5))).astype(BF16)
                else:
                    kn_ref[:, dst] = (yh * inv).astype(BF16)

    for c0 in range(0, DN_W, MXU_COLS):
        zdn_ref[:, c0:c0 + MXU_COLS] = silu(proj(C_Z + c0)).astype(BF16)
        zat_ref[:, c0:c0 + MXU_COLS] = silu(proj(C_AZ + c0)).astype(BF16)

    ba = proj(C_BA, LANES)
    lane = _iota(ba.shape, 1)
    a = ba + dtb_ref[...]
    softplus = jnp.maximum(a, 0.0) + jnp.log(1.0 + jnp.exp(-jnp.abs(a)))
    g = -jnp.exp(alog_ref[...]) * softplus
    bg_ref[...] = jnp.where(lane < DN_HEADS, _sigmoid(ba), jnp.where(lane < 2 * DN_HEADS, g, 0.0))

    ang = invf_ref[...] * pos_ref[...].astype(F32)
    reps = LANES // (AT_HD // 2)
    cos = jnp.concatenate([jnp.cos(ang)] * reps, axis=0).T
    sin = jnp.concatenate([jnp.sin(ang)] * reps, axis=0).T * sgn_ref[...]
    qscale = (AT_HD ** -0.5) * math.log2(math.e)
    step_in_super = i % (SUPER // tm)
    d4, d16 = DILATIONS[1], DILATIONS[2]
    n4, n16 = tm // d4, tm // d16

    def emit(val, gp, nat_ref, p4_ref, p16_ref):
        sl = slice(gp * LANES, (gp + 1) * LANES)
        nat_ref[:, sl] = val.astype(BF16)
        pbuf_ref[gp] = val
        for c in range(d4):
            seg = pbuf_ref[gp, pl.ds(c, n4, stride=d4), :]
            dst = pl.multiple_of(c * (SUPER // d4) + step_in_super * n4, n4)
            p4_ref[pl.ds(dst, n4), sl] = seg.astype(BF16)
            p4buf_ref[gp, c * n4:(c + 1) * n4, :] = seg
        for r in range(d16):
            seg = p4buf_ref[gp, pl.ds((r % d4) * n4 + r // d4, n16, stride=d4), :]
            dst = pl.multiple_of(r * (SUPER // d16) + step_in_super * n16, n16)
            p16_ref[pl.ds(dst, n16), sl] = seg.astype(BF16)

    def rope(v):
        return v * cos + pltpu.roll(v, LANES // 2, axis=1) * sin

    for c0 in range(0, AT_W, MXU_COLS):
        aq = proj(C_AQ + c0)
        ak = proj(C_AK + c0)
        av = proj(C_AV + c0)
        for hh in range(MXU_COLS // LANES):
            gp = c0 // LANES + hh
            ls = slice(hh * LANES, (hh + 1) * LANES)
            emit(rope(aq[:, ls]) * qscale, gp, aq1_ref, aq4_ref, aq16_ref)
            emit(rope(ak[:, ls]), gp, ak1_ref, ak4_ref, ak16_ref)
            emit(av[:, ls], gp, av1_ref, av4_ref, av16_ref)


def _proj_call(x2, pos_row, nw, scale, shift, w_all, cw, alog_row, dtb_row, invf, sgn, tm):
    s, d = x2.shape
    row = lambda n: pl.BlockSpec((1, n), lambda i: (0, 0))
    full = lambda a: pl.BlockSpec(a.shape, lambda i: (0, 0))
    tile = lambda n: pl.BlockSpec((tm, n), lambda i: (i, 0))
    sup = pl.BlockSpec((SUPER, AT_W), lambda i: (i // (SUPER // tm), 0))
    bf = lambda n: jax.ShapeDtypeStruct((s, n), BF16)
    outs = [bf(DN_W)] * 4 + [jax.ShapeDtypeStruct((s, LANES), F32), bf(AT_W)] + [bf(AT_W)] * 9
    out_specs = [tile(DN_W)] * 4 + [tile(LANES), tile(AT_W)] + [tile(AT_W), sup, sup] * 3
    return pl.pallas_call(
        _proj_body,
        out_shape=outs,
        grid=(s // tm,),
        in_specs=[tile(d), pl.BlockSpec((1, tm), lambda i: (0, i)), row(d), row(d), row(d),
                  pl.BlockSpec(w_all.shape, lambda i: (0, 0), pipeline_mode=pl.Buffered(1)),
                  full(cw), row(LANES), row(LANES), full(invf), row(LANES)],
        out_specs=out_specs,
        scratch_shapes=[pltpu.VMEM((SUBLANES, 3 * DN_W), F32), pltpu.VMEM((tm, d), BF16),
                        pltpu.VMEM((AT_W // LANES, tm, LANES), F32),
                        pltpu.VMEM((AT_W // LANES, tm, LANES), F32)],
        compiler_params=pltpu.CompilerParams(dimension_semantics=("arbitrary",),
                                             vmem_limit_bytes=PROJ_VMEM_LIMIT),
        name="proj",
    )(x2, pos_row, nw, scale, shift, w_all, cw, alog_row, dtb_row, invf, sgn)


def _dn_prep_body(q_ref, k_ref, v_ref, bg_ref, w_ref, bt_ref, op_ref, gl_ref):
    rows = q_ref.shape[0]
    nchunk = rows // CHUNK
    wide = DN_HEADS * CHUNK
    bg = bg_ref[...]
    cs_rows = min(rows, MXU_COLS)
    ri = _iota((cs_rows, cs_rows), 0)
    ci = _iota((cs_rows, cs_rows), 1)
    tril_blk = jnp.where((ri // CHUNK == ci // CHUNK) & (ri >= ci), 1.0, 0.0).astype(BF16)
    g_hi = bg.astype(BF16)
    g_r1 = bg - g_hi.astype(F32)
    g_mid = g_r1.astype(BF16)
    g_lo = (g_r1 - g_mid.astype(F32)).astype(BF16)
    g3 = jnp.concatenate([g_hi, g_mid, g_lo], axis=1)
    csum = jnp.concatenate([jnp.dot(tril_blk, g3[r0:r0 + cs_rows], preferred_element_type=F32)
                            for r0 in range(0, rows, cs_rows)], axis=0)
    gcum = csum[:, 0:LANES] + csum[:, LANES:2 * LANES] + csum[:, 2 * LANES:3 * LANES]
    bfull = jnp.concatenate([jnp.broadcast_to(bg[:, hd:hd + 1], (rows, DN_HD)) for hd in range(DN_HEADS)], axis=1)
    gfull = jnp.concatenate([jnp.broadcast_to(gcum[:, DN_HEADS + hd:DN_HEADS + hd + 1], (rows, DN_HD))
                             for hd in range(DN_HEADS)], axis=1)
    eg = jnp.exp(gfull)
    q = q_ref[...].astype(F32)
    k = k_ref[...].astype(F32)
    kb = k * bfull
    vb = v_ref[...].astype(F32) * bfull
    q_dec = q * eg
    kbg = kb * eg

    r64 = _iota((CHUNK, wide), 0)
    j64 = _iota((CHUNK, wide), 1) % CHUNK
    tril = r64 >= j64
    strict = r64 > j64
    eye = jnp.where(r64 == j64, 1.0, 0.0).astype(F32)
    low_half = _iota((CHUNK, LANES), 1) < CHUNK
    bd64 = _iota((wide, wide), 0) // CHUNK == _iota((wide, wide), 1) // CHUNK
    bd_k = _iota((wide, DN_W), 0) // CHUNK == _iota((wide, DN_W), 1) // DN_HD

    def block_diag(m):
        return jnp.where(bd64, jnp.concatenate([m.astype(BF16)] * DN_HEADS, axis=0), jnp.zeros((wide, wide), BF16))

    rsl = lambda c: slice(c * CHUNK, (c + 1) * CHUNK)
    hsl = lambda hd: slice(hd * DN_HD, (hd + 1) * DN_HD)
    chunks = range(nchunk)
    g_last = [gfull[c * CHUNK + CHUNK - 1:(c + 1) * CHUNK, :] for c in chunks]
    k_dec = [k[rsl(c)] * jnp.exp(g_last[c] - gfull[rsl(c)]) for c in chunks]
    g_row = [jnp.concatenate([gcum[rsl(c)], gcum[rsl(c)]], axis=0).T for c in chunks]
    dec = []
    for c in chunks:
        halves = []
        for pr in range(DN_HEADS // 2):
            ha, hb = 2 * pr, 2 * pr + 1
            col = jnp.where(low_half, gfull[rsl(c), hsl(ha)], gfull[rsl(c), hsl(hb)])
            row = jnp.where(low_half[0:1], g_row[c][DN_HEADS + ha:DN_HEADS + ha + 1, :],
                            g_row[c][DN_HEADS + hb:DN_HEADS + hb + 1, :])
            halves.append(col - row)
        dec.append(jnp.exp(jnp.where(tril, jnp.concatenate(halves, axis=1), NEG)))
    kq = [lax.dot_general(jnp.concatenate([kb[rsl(c)].astype(BF16), q_ref[rsl(c), :]], axis=0),
                          jnp.where(bd_k, jnp.concatenate([k_ref[rsl(c), :]] * DN_HEADS, axis=0),
                                    jnp.zeros((wide, DN_W), BF16)),
                          NT_DIMS, preferred_element_type=F32) for c in chunks]
    a_low = [jnp.where(strict, m[0:CHUNK] * d, 0.0) for m, d in zip(kq, dec)]
    attn = [jnp.where(tril, m[CHUNK:2 * CHUNK] * d, 0.0) for m, d in zip(kq, dec)]
    bmat = [-a for a in a_low]
    ymat = [eye + b for b in bmat]
    bmat = [jnp.dot(b.astype(BF16), block_diag(b), preferred_element_type=F32) for b in bmat]
    for _ in range(4):
        prod = [jnp.dot(jnp.concatenate([y, b], axis=0).astype(BF16), block_diag(b), preferred_element_type=F32)
                for y, b in zip(ymat, bmat)]
        ymat = [y + p[0:CHUNK] for y, p in zip(ymat, prod)]
        bmat = [p[CHUNK:2 * CHUNK] for p in prod]
    ymat = [y + jnp.dot(y.astype(BF16), block_diag(b), preferred_element_type=F32) for y, b in zip(ymat, bmat)]
    rhs = [jnp.concatenate([jnp.concatenate([kbg[rsl(c), hsl(hd)], vb[rsl(c), hsl(hd)]], axis=1)
                            for hd in range(DN_HEADS)], axis=0).astype(BF16) for c in chunks]
    wu = [jnp.dot(block_diag(y), r, preferred_element_type=F32) for y, r in zip(ymat, rhs)]
    wut = [m.T.astype(BF16) for m in wu]
    kdbd = [jnp.where(bd_k, jnp.concatenate([k_dec[c].astype(BF16)] * DN_HEADS, axis=0),
                      jnp.zeros((wide, DN_W), BF16)) for c in chunks]
    pbt = [jnp.dot(a, b, preferred_element_type=F32) for a, b in zip(wut, kdbd)]
    aw = [jnp.dot(block_diag(a), m.astype(BF16), preferred_element_type=F32) for a, m in zip(attn, wu)]
    qpt = [(jnp.concatenate([q_dec[rsl(c), hsl(hd)] for hd in range(DN_HEADS)], axis=0) - a[:, 0:DN_HD]).T
           .astype(BF16) for c, a in zip(chunks, aw)]
    for c in chunks:
        for hd in range(DN_HEADS):
            pair = slice((hd // 2) * LANES, (hd // 2 + 1) * LANES)
            w_ref[c, hd] = jnp.concatenate([pbt[c][0:DN_HD, hsl(hd)].astype(BF16), qpt[c][:, pair]], axis=1)
            bt_ref[c, hd] = pbt[c][DN_HD:2 * DN_HD, hsl(hd)].astype(BF16)
            op_ref[rsl(c), hsl(hd)] = aw[c][hd * CHUNK:(hd + 1) * CHUNK, DN_HD:2 * DN_HD].astype(BF16)
        gl_rows = [jnp.exp(g_last[c][:, hsl(hd)]) for hd in range(DN_HEADS)]
        gl_rows.append(jnp.zeros((SUBLANES - DN_HEADS, DN_HD), F32))
        gl_ref[c] = jnp.concatenate(gl_rows, axis=0)


def _dn_prep_call(qn, kn, v, bg, rows):
    s = qn.shape[0]
    nct = s // CHUNK
    ncb = rows // CHUNK
    tile = lambda n: pl.BlockSpec((rows, n), lambda i: (i, 0))
    return pl.pallas_call(
        _dn_prep_body,
        out_shape=[jax.ShapeDtypeStruct((nct, DN_HEADS, DN_HD, 2 * DN_HD), BF16),
                   jax.ShapeDtypeStruct((nct, DN_HEADS, DN_HD, DN_HD), BF16),
                   jax.ShapeDtypeStruct((s, DN_W), BF16),
                   jax.ShapeDtypeStruct((nct, SUBLANES, DN_HD), F32)],
        grid=(s // rows,),
        in_specs=[tile(DN_W), tile(DN_W), tile(DN_W), tile(LANES)],
        out_specs=[pl.BlockSpec((ncb, DN_HEADS, DN_HD, 2 * DN_HD), lambda i: (i, 0, 0, 0)),
                   pl.BlockSpec((ncb, DN_HEADS, DN_HD, DN_HD), lambda i: (i, 0, 0, 0)),
                   tile(DN_W),
                   pl.BlockSpec((ncb, SUBLANES, DN_HD), lambda i: (i, 0, 0))],
        compiler_params=_params(("arbitrary",)),
        name="dn_prep",
    )(qn, kn, v, bg)


def _dn_scan_body(w_ref, bt_ref, op_ref, gl_ref, z_ref, nw_ref, o_ref, st_ref):
    ncb = w_ref.shape[0]

    @pl.when(pl.program_id(0) == 0)
    def _():
        st_ref[...] = jnp.zeros(st_ref.shape, F32)

    def chunk(c, carry):
        rs = pl.ds(pl.multiple_of(c * CHUNK, CHUNK), CHUNK)
        for hd in range(DN_HEADS):
            hs = slice(hd * DN_HD, (hd + 1) * DN_HD)
            st = st_ref[hd]
            r = jnp.dot(st.astype(BF16), w_ref[c, hd], preferred_element_type=F32)
            st_ref[hd] = gl_ref[c, hd:hd + 1, :] * st - r[:, 0:DN_HD] + bt_ref[c, hd].astype(F32)
            half = (hd % 2) * CHUNK
            o = r[:, DN_HD:2 * DN_HD].T[half:half + CHUNK] + op_ref[rs, hs].astype(F32)
            ms = jnp.mean(o * o, axis=-1, keepdims=True)
            o_ref[rs, hs] = (o * lax.rsqrt(ms + EPS) * nw_ref[...] * z_ref[rs, hs].astype(F32)).astype(BF16)
        return carry

    lax.fori_loop(0, ncb, chunk, 0, unroll=True)


def _dn_scan_call(wt, bt, op, gl, zdn, nw_row, ncb):
    nct = wt.shape[0]
    s = op.shape[0]
    rows = ncb * CHUNK
    return pl.pallas_call(
        _dn_scan_body,
        out_shape=jax.ShapeDtypeStruct((s, DN_W), BF16),
        grid=(nct // ncb,),
        in_specs=[pl.BlockSpec((ncb, DN_HEADS, DN_HD, 2 * DN_HD), lambda i: (i, 0, 0, 0)),
                  pl.BlockSpec((ncb, DN_HEADS, DN_HD, DN_HD), lambda i: (i, 0, 0, 0)),
                  pl.BlockSpec((rows, DN_W), lambda i: (i, 0)),
                  pl.BlockSpec((ncb, SUBLANES, DN_HD), lambda i: (i, 0, 0)),
                  pl.BlockSpec((rows, DN_W), lambda i: (i, 0)),
                  pl.BlockSpec((1, DN_HD), lambda i: (0, 0))],
        out_specs=pl.BlockSpec((rows, DN_W), lambda i: (i, 0)),
        scratch_shapes=[pltpu.VMEM((DN_HEADS, DN_HD, DN_HD), F32)],
        compiler_params=_params(("arbitrary",)),
        name="dn_scan",
    )(wt, bt, op, gl, zdn, nw_row)


def _attn_body(q1, q4, q16, k1, k4, k16, v1, v4, v16, kp1, vp1, kp4a, kp4b, kp4c, kp4d,
               vp4a, vp4b, vp4c, vp4d, kp16, vp16, z_ref, anw_ref, o_ref, m_s, l_s, a_s, m_n, l_n, a_n):
    d4, d16 = DILATIONS[1], DILATIONS[2]
    seg4 = SUPER // d4
    kp4 = (kp4a, kp4b, kp4c, kp4d)
    vp4 = (vp4a, vp4b, vp4c, vp4d)
    kj = _iota((2 * Q_BLOCK, Q_BLOCK), 0)
    qi = _iota((2 * Q_BLOCK, Q_BLOCK), 1)
    rel = Q_BLOCK + qi - kj
    band = (rel >= 0) & (rel <= W_SUB)
    bias_in = jnp.where(band, 0.0, NEG).astype(BF16)
    bias_first = jnp.where(band & (kj >= Q_BLOCK), 0.0, NEG).astype(BF16)
    bias_edge = jnp.where(pl.program_id(0) == 0, bias_first, bias_in)
    one_hot = jnp.where(kj % Q_BLOCK == qi, 1.0, 0.0).astype(BF16)
    lane = _iota((Q_BLOCK, LANES), 1)
    lane_k = _iota((2 * Q_BLOCK, LANES), 1)
    head_a_q = (lane % AT_HD) < (AT_HD // 2)
    head_a = lane < AT_HD
    head_a_k = lane_k < AT_HD
    ro = _iota((4 * Q_BLOCK, LANES), 0)
    ones_rhs = jnp.where((ro < 2 * Q_BLOCK) == (_iota((4 * Q_BLOCK, LANES), 1) < AT_HD), 1.0, 0.0).astype(BF16)
    sr = _iota((2 * LANES, LANES), 0)
    ssq_rhs = jnp.where((sr % LANES) // AT_HD == _iota((2 * LANES, LANES), 1) // AT_HD, 1.0, 0.0).astype(BF16)
    zq = jnp.zeros((Q_BLOCK, LANES), BF16)
    zv = jnp.zeros((2 * Q_BLOCK, LANES), BF16)

    def scores(load):
        q, kb, vb, bias = load()
        q2 = jnp.concatenate([jnp.where(head_a_q, q, zq), jnp.where(head_a_q, zq, q)], axis=0)
        s = lax.dot_general(jnp.concatenate([q2, one_hot], axis=1), jnp.concatenate([kb, bias], axis=1),
                            NT_DIMS, preferred_element_type=F32)
        m = jnp.max(s, axis=-1, keepdims=True)
        return jnp.exp2(s - m).astype(BF16), m, vb

    def values(p, m, vb):
        pc = jnp.concatenate([p[0:Q_BLOCK], p[Q_BLOCK:2 * Q_BLOCK]], axis=1)
        rhs = jnp.concatenate([jnp.concatenate([jnp.where(head_a_k, vb, zv), jnp.where(head_a_k, zv, vb)], axis=0),
                               ones_rhs], axis=1)
        r = jnp.dot(pc, rhs, preferred_element_type=F32)
        return r[:, 0:LANES], r[:, LANES:2 * LANES], jnp.where(head_a, m[0:Q_BLOCK], m[Q_BLOCK:2 * Q_BLOCK])

    def run(loads, finish, post=None):
        n = len(loads)
        pending, done = {}, {}
        for i in range(n + ATT_LAG + 1):
            if i < n:
                pending[i] = scores(loads[i])
            j = i - ATT_LAG
            if 0 <= j < n:
                done[j] = finish(j, *values(*pending.pop(j)))
            if post is not None and 0 <= j - 1 < n:
                post(j - 1, done.pop(j - 1))

    def merged(acc, l, m, m_old, l_old, a_old):
        m_new = jnp.maximum(m_old, m)
        w_old = jnp.exp2(m_old - m_new)
        w_new = jnp.exp2(m - m_new)
        return a_old * w_old + acc * w_new, l_old * w_old + l * w_new, m_new

    def load16(r):
        rs = slice(r * Q_BLOCK, (r + 1) * Q_BLOCK)
        return lambda: (q16[rs, :], jnp.concatenate([kp16[rs, :], k16[rs, :]], axis=0),
                        jnp.concatenate([vp16[rs, :], v16[rs, :]], axis=0), bias_edge)

    def finish16(r, acc, l, m):
        rows = pl.ds((r % d4) * seg4 + r // d4, Q_BLOCK, stride=d4)
        a_s[rows, :] = acc
        l_s[rows, :] = l
        m_s[rows, :] = m

    run([load16(r) for r in range(d16)], finish16)

    def load4(r, b):
        base = r * seg4 + b * Q_BLOCK
        if b == 0:
            return lambda: (q4[base:base + Q_BLOCK, :],
                            jnp.concatenate([kp4[r][...], k4[base:base + Q_BLOCK, :]], axis=0),
                            jnp.concatenate([vp4[r][...], v4[base:base + Q_BLOCK, :]], axis=0), bias_edge)
        return lambda: (q4[base:base + Q_BLOCK, :], k4[base - Q_BLOCK:base + Q_BLOCK, :],
                        v4[base - Q_BLOCK:base + Q_BLOCK, :], bias_in)

    blocks4 = [(r, b) for b in range(seg4 // Q_BLOCK) for r in range(d4)]

    def finish4(j, acc, l, m):
        r, b = blocks4[j]
        rows = slice(r * seg4 + b * Q_BLOCK, r * seg4 + (b + 1) * Q_BLOCK)
        acc, l, m = merged(acc, l, m, m_s[rows, :], l_s[rows, :], a_s[rows, :])
        a_s[rows, :] = acc
        l_s[rows, :] = l
        m_s[rows, :] = m

    run([load4(r, b) for r, b in blocks4], finish4)

    for c in range(d4):
        src = slice(c * seg4, (c + 1) * seg4)
        dst = pl.ds(c, seg4, stride=d4)
        m_n[dst, :] = m_s[src, :]
        l_n[dst, :] = l_s[src, :]
        a_n[dst, :] = a_s[src, :]

    def load1(b):
        rs = slice(b * Q_BLOCK, (b + 1) * Q_BLOCK)
        if b == 0:
            return lambda: (q1[rs, :], jnp.concatenate([kp1[...], k1[rs, :]], axis=0),
                            jnp.concatenate([vp1[...], v1[rs, :]], axis=0), bias_edge)
        ks = slice((b - 1) * Q_BLOCK, (b + 1) * Q_BLOCK)
        return lambda: (q1[rs, :], k1[ks, :], v1[ks, :], bias_in)

    def finish1(b, acc, l, m):
        rs = slice(b * Q_BLOCK, (b + 1) * Q_BLOCK)
        acc, l, _ = merged(acc, l, m, m_n[rs, :], l_n[rs, :], a_n[rs, :])
        o = acc / l
        sq = o * o
        hi = sq.astype(BF16)
        return o, jnp.concatenate([hi, (sq - hi.astype(F32)).astype(BF16)], axis=1)

    def post1(b, res):
        o, sq2 = res
        rs = slice(b * Q_BLOCK, (b + 1) * Q_BLOCK)
        ssq = jnp.dot(sq2, ssq_rhs, preferred_element_type=F32)
        o_ref[rs, :] = (o * lax.rsqrt(ssq * (1.0 / AT_HD) + EPS) * anw_ref[...]
                        * z_ref[rs, :].astype(F32)).astype(BF16)

    run([load1(b) for b in range(SUPER // Q_BLOCK)], finish1, post1)


def _attn_call(aq, ak, av, zat, anw_row):
    s = zat.shape[0]
    d4 = DILATIONS[1]
    per_super = SUPER // Q_BLOCK
    cur = pl.BlockSpec((SUPER, LANES), lambda m, g: (m, g))
    prev16 = pl.BlockSpec((SUPER, LANES), lambda m, g: (jnp.maximum(m - 1, 0), g))
    prev1 = pl.BlockSpec((Q_BLOCK, LANES), lambda m, g: (jnp.maximum(m * per_super - 1, 0), g))
    prev4 = [pl.BlockSpec((Q_BLOCK, LANES),
                          lambda m, g, r=r: (jnp.maximum((m - 1) * per_super + (r + 1) * (per_super // d4) - 1, 0), g))
             for r in range(d4)]
    state = pltpu.VMEM((SUPER, LANES), F32)
    return pl.pallas_call(
        _attn_body,
        out_shape=jax.ShapeDtypeStruct((s, AT_W), BF16),
        grid=(s // SUPER, AT_W // LANES),
        in_specs=[cur] * 9 + [prev1, prev1] + prev4 + prev4 + [prev16, prev16, cur,
                                                               pl.BlockSpec((1, LANES), lambda m, g: (0, g))],
        out_specs=cur,
        scratch_shapes=[state] * 6,
        compiler_params=_params(("arbitrary", "arbitrary")),
        name="attn",
    )(*aq, *ak, *av, ak[0], av[0], *([ak[1]] * d4), *([av[1]] * d4), ak[2], av[2], zat, anw_row)


def _out_body(x_ref, odn_ref, oat_ref, gate_ref, w_ref, fnw_ref, o_ref):
    mixin = jnp.concatenate([odn_ref[...], oat_ref[...]], axis=1)
    mix = jnp.dot(mixin, w_ref[...], preferred_element_type=F32)
    y = x_ref[...] + gate_ref[...] * mix
    ms = jnp.mean(y * y, axis=-1, keepdims=True)
    o_ref[...] = y * lax.rsqrt(ms + EPS) * fnw_ref[...]


def _out_call(x2, odn, oat, gate, w_out, fnw_row, tm):
    s, d = x2.shape
    tile = lambda n: pl.BlockSpec((tm, n), lambda i: (i, 0))
    row = lambda n: pl.BlockSpec((1, n), lambda i: (0, 0))
    return pl.pallas_call(
        _out_body,
        out_shape=jax.ShapeDtypeStruct((s, d), F32),
        grid=(s // tm,),
        in_specs=[tile(d), tile(DN_W), tile(AT_W), row(d), pl.BlockSpec(w_out.shape, lambda i: (0, 0)), row(d)],
        out_specs=tile(d),
        compiler_params=_params(("arbitrary",)),
        name="out",
    )(x2, odn, oat, gate, w_out, fnw_row)


def _rope_pair_perm():
    half = AT_HD // 2
    idx = []
    for gp in range(AT_W // LANES):
        for l in range(LANES):
            part, hh, j = l // AT_HD, (l % AT_HD) // half, l % half
            idx.append((2 * gp + hh) * AT_HD + part * half + j)
    return np.asarray(idx, np.int32)


def kernel(x, c, positions, w_mod, b_mod, norm_w, w_in, conv_w, a_log, dt_bias, dn_norm_w, at_norm_w,
           w_out, final_norm_w):
    b, s, d = x.shape
    assert b == 1 and d == D_MODEL and w_mod.shape[0] == 1
    x2 = x.reshape(s, d)

    mod = _mod_call(c.reshape(d, 1), w_mod[0], b_mod[0].reshape(1, 3 * d))
    shift, scale, gate = mod[:, 0:d], mod[:, d:2 * d], mod[:, 2 * d:3 * d]

    wi = w_in[0]
    o_z, o_b, o_a, o_aq = 3 * DN_W, 4 * DN_W, 4 * DN_W + DN_HEADS, 4 * DN_W + 2 * DN_HEADS
    perm = _rope_pair_perm()
    w_aq = wi[:, o_aq:o_aq + AT_W][:, perm]
    w_ak = wi[:, o_aq + AT_W:o_aq + 2 * AT_W][:, perm]
    w_ba = jnp.pad(wi[:, o_b:o_b + 2 * DN_HEADS], ((0, 0), (0, LANES - 2 * DN_HEADS)))
    w_all = jnp.concatenate([wi[:, 0:o_z], wi[:, o_z:o_b], w_aq, w_ak,
                             wi[:, o_aq + 2 * AT_W:o_aq + 4 * AT_W], w_ba], axis=1).astype(BF16)
    assert w_all.shape[1] == W_COLS
    pad_heads = lambda v: jnp.pad(v.reshape(1, DN_HEADS), ((0, 0), (DN_HEADS, LANES - 2 * DN_HEADS)))
    half = AT_HD // 2
    invf = (ROPE_THETA ** (-jnp.arange(half, dtype=F32) / half)).reshape(half, 1)
    sgn = jnp.where(jnp.arange(LANES) < LANES // 2, -1.0, 1.0).astype(F32).reshape(1, LANES)

    assert s % SUPER == 0
    outs = _proj_call(x2, positions.reshape(1, s), norm_w[0].reshape(1, d), scale, shift, w_all, conv_w[0],
                      pad_heads(a_log[0]), pad_heads(dt_bias[0]), invf, sgn, tm=256)
    qn, kn, v, zdn, bg, zat = outs[0:6]
    aq, ak, av = outs[6:9], outs[9:12], outs[12:15]

    wt, bt, op, gl = _dn_prep_call(qn, kn, v, bg, rows=512)
    odn = _dn_scan_call(wt, bt, op, gl, zdn, dn_norm_w[0].reshape(1, DN_HD), ncb=8)
    oat = _attn_call(aq, ak, av, zat, jnp.tile(at_norm_w[0], AT_HEADS).reshape(1, AT_W))

    out = _out_call(x2, odn, oat, gate, w_out[0].astype(BF16), final_norm_w.reshape(1, d), tm=1024)
    return out.reshape(b, s, d)
```

```python
import math

import numpy as np
import jax
import jax.numpy as jnp
from jax import lax
from jax.experimental import pallas as pl
from jax.experimental.pallas import tpu as pltpu

D_MODEL = 1024
DN_HEADS = 4
DN_HD = 128
DN_W = DN_HEADS * DN_HD
AT_HEADS = 8
AT_HD = 64
AT_W = AT_HEADS * AT_HD
CONV_K = 4
CHUNK = 64
DILATIONS = (1, 4, 16)
W_SUB = 128
Q_BLOCK = 128
SUPER = Q_BLOCK * DILATIONS[-1]
ATT_LAG = 2
ROPE_THETA = 10000.0
EPS = 1e-6
NEG = -1e30

LANES = 128
SUBLANES = 8
MXU_COLS = 256
VMEM_LIMIT = 48 * 1024 * 1024
PROJ_VMEM_LIMIT = 56 * 1024 * 1024

C_QKV, C_Z, C_AQ, C_AK, C_AV, C_AZ, C_BA = 0, 1536, 2048, 2560, 3072, 3584, 4096
W_COLS = C_BA + LANES

F32 = jnp.float32
BF16 = jnp.bfloat16
NT_DIMS = (((1,), (1,)), ((), ()))


def _sigmoid(v):
    return 1.0 / (1.0 + jnp.exp2(v * (-math.log2(math.e))))


def _iota(shape, dim):
    return lax.broadcasted_iota(jnp.int32, shape, dim)


def _params(sem):
    return pltpu.CompilerParams(dimension_semantics=sem, vmem_limit_bytes=VMEM_LIMIT)


def _mod_body(c_ref, w_ref, b_ref, o_ref):
    c = c_ref[...]
    sc = c * _sigmoid(c)
    o_ref[...] = jnp.sum(sc * w_ref[...], axis=0, keepdims=True) + b_ref[...]


def _mod_call(c_col, w_mod, b_mod):
    d, n = w_mod.shape
    tn = 512
    return pl.pallas_call(
        _mod_body,
        out_shape=jax.ShapeDtypeStruct((1, n), F32),
        grid=(n // tn,),
        in_specs=[pl.BlockSpec((d, 1), lambda j: (0, 0)),
                  pl.BlockSpec((d, tn), lambda j: (0, j)),
                  pl.BlockSpec((1, tn), lambda j: (0, j))],
        out_specs=pl.BlockSpec((1, tn), lambda j: (0, j)),
        compiler_params=_params(("arbitrary",)),
        name="mod",
    )(c_col, w_mod, b_mod)


def _proj_body(x_ref, pos_ref, nw_ref, sc_ref, sh_ref, w_ref, cw_ref, alog_ref, dtb_ref, invf_ref,
               sgn_ref, qn_ref, kn_ref, v_ref, zdn_ref, bg_ref, zat_ref,
               aq1_ref, aq4_ref, aq16_ref, ak1_ref, ak4_ref, ak16_ref, av1_ref, av4_ref, av16_ref,
               cbuf_ref, hb_ref, pbuf_ref, p4buf_ref):
    i = pl.program_id(0)
    tm = x_ref.shape[0]
    x = x_ref[...]
    ms = jnp.mean(x * x, axis=-1, keepdims=True)
    hb_ref[...] = ((x * lax.rsqrt(ms + EPS)) * (nw_ref[...] * (1.0 + sc_ref[...])) + sh_ref[...]).astype(BF16)

    def proj(c0, width=MXU_COLS):
        return jnp.dot(hb_ref[...], w_ref[:, c0:c0 + width], preferred_element_type=F32)

    def silu(v):
        return v * _sigmoid(v)

    @pl.when(i == 0)
    def _():
        cbuf_ref[...] = jnp.zeros((SUBLANES, 3 * DN_W), F32)

    sub = _iota((SUBLANES, MXU_COLS), 0)
    for c0 in range(0, 3 * DN_W, MXU_COLS):
        cs = slice(c0, c0 + MXU_COLS)
        xg = proj(C_QKV + c0)
        prev = cbuf_ref[:, cs]
        conv = xg * cw_ref[CONV_K - 1:CONV_K, cs]
        for j in range(1, CONV_K):
            sh = pltpu.roll(xg, j, axis=0)
            head = jnp.where(sub < j, pltpu.roll(prev, j, axis=0), sh[0:SUBLANES])
            conv = conv + jnp.concatenate([head, sh[SUBLANES:tm]], axis=0) * cw_ref[CONV_K - 1 - j:CONV_K - j, cs]
        cbuf_ref[:, cs] = xg[tm - SUBLANES:tm]
        y = silu(conv)
        kind, off = c0 // DN_W, c0 % DN_W
        for hh in range(MXU_COLS // DN_HD):
            yh = y[:, hh * DN_HD:(hh + 1) * DN_HD]
            dst = slice(off + hh * DN_HD, off + (hh + 1) * DN_HD)
            if kind == 2:
                v_ref[:, dst] = yh.astype(BF16)
            else:
                inv = lax.rsqrt(jnp.sum(yh * yh, axis=-1, keepdims=True) + EPS)
                if kind == 0:
                    qn_ref[:, dst] = (yh * (inv * (DN_HD ** -0.5))).astype(BF16)
                else:
                    kn_ref[:, dst] = (yh * inv).astype(BF16)

    for c0 in range(0, DN_W, MXU_COLS):
        zdn_ref[:, c0:c0 + MXU_COLS] = silu(proj(C_Z + c0)).astype(BF16)
        zat_ref[:, c0:c0 + MXU_COLS] = silu(proj(C_AZ + c0)).astype(BF16)

    ba = proj(C_BA, LANES)
    lane = _iota(ba.shape, 1)
    a = ba + dtb_ref[...]
    softplus = jnp.maximum(a, 0.0) + jnp.log(1.0 + jnp.exp(-jnp.abs(a)))
    g = -jnp.exp(alog_ref[...]) * softplus
    bg_ref[...] = jnp.where(lane < DN_HEADS, _sigmoid(ba), jnp.where(lane < 2 * DN_HEADS, g, 0.0))

    ang = invf_ref[...] * pos_ref[...].astype(F32)
    reps = LANES // (AT_HD // 2)
    cos = jnp.concatenate([jnp.cos(ang)] * reps, axis=0).T
    sin = jnp.concatenate([jnp.sin(ang)] * reps, axis=0).T * sgn_ref[...]
    qscale = (AT_HD ** -0.5) * math.log2(math.e)
    step_in_super = i % (SUPER // tm)
    d4, d16 = DILATIONS[1], DILATIONS[2]
    n4, n16 = tm // d4, tm // d16

    def emit(val, gp, nat_ref, p4_ref, p16_ref):
        sl = slice(gp * LANES, (gp + 1) * LANES)
        nat_ref[:, sl] = val.astype(BF16)
        pbuf_ref[gp] = val
        for c in range(d4):
            seg = pbuf_ref[gp, pl.ds(c, n4, stride=d4), :]
            dst = pl.multiple_of(c * (SUPER // d4) + step_in_super * n4, n4)
            p4_ref[pl.ds(dst, n4), sl] = seg.astype(BF16)
            p4buf_ref[gp, c * n4:(c + 1) * n4, :] = seg
        for r in range(d16):
            seg = p4buf_ref[gp, pl.ds((r % d4) * n4 + r // d4, n16, stride=d4), :]
            dst = pl.multiple_of(r * (SUPER // d16) + step_in_super * n16, n16)
            p16_ref[pl.ds(dst, n16), sl] = seg.astype(BF16)

    def rope(v):
        return v * cos + pltpu.roll(v, LANES // 2, axis=1) * sin

    for c0 in range(0, AT_W, MXU_COLS):
        aq = proj(C_AQ + c0)
        ak = proj(C_AK + c0)
        av = proj(C_AV + c0)
        for hh in range(MXU_COLS // LANES):
            gp = c0 // LANES + hh
            ls = slice(hh * LANES, (hh + 1) * LANES)
            emit(rope(aq[:, ls]) * qscale, gp, aq1_ref, aq4_ref, aq16_ref)
            emit(rope(ak[:, ls]), gp, ak1_ref, ak4_ref, ak16_ref)
            emit(av[:, ls], gp, av1_ref, av4_ref, av16_ref)


def _proj_call(x2, pos_row, nw, scale, shift, w_all, cw, alog_row, dtb_row, invf, sgn, tm):
    s, d = x2.shape
    row = lambda n: pl.BlockSpec((1, n), lambda i: (0, 0))
    full = lambda a: pl.BlockSpec(a.shape, lambda i: (0, 0))
    tile = lambda n: pl.BlockSpec((tm, n), lambda i: (i, 0))
    sup = pl.BlockSpec((SUPER, AT_W), lambda i: (i // (SUPER // tm), 0))
    bf = lambda n: jax.ShapeDtypeStruct((s, n), BF16)
    outs = [bf(DN_W)] * 4 + [jax.ShapeDtypeStruct((s, LANES), F32), bf(AT_W)] + [bf(AT_W)] * 9
    out_specs = [tile(DN_W)] * 4 + [tile(LANES), tile(AT_W)] + [tile(AT_W), sup, sup] * 3
    return pl.pallas_call(
        _proj_body,
        out_shape=outs,
        grid=(s // tm,),
        in_specs=[tile(d), pl.BlockSpec((1, tm), lambda i: (0, i)), row(d), row(d), row(d),
                  pl.BlockSpec(w_all.shape, lambda i: (0, 0), pipeline_mode=pl.Buffered(1)),
                  full(cw), row(LANES), row(LANES), full(invf), row(LANES)],
        out_specs=out_specs,
        scratch_shapes=[pltpu.VMEM((SUBLANES, 3 * DN_W), F32), pltpu.VMEM((tm, d), BF16),
                        pltpu.VMEM((AT_W // LANES, tm, LANES), F32),
                        pltpu.VMEM((AT_W // LANES, tm, LANES), F32)],
        compiler_params=pltpu.CompilerParams(dimension_semantics=("arbitrary",),
                                             vmem_limit_bytes=PROJ_VMEM_LIMIT),
        name="proj",
    )(x2, pos_row, nw, scale, shift, w_all, cw, alog_row, dtb_row, invf, sgn)


def _dn_body(q_ref, k_ref, v_ref, bg_ref, z_ref, nw_ref, o_ref, w_s, bt_s, op_s, gl_s, st_ref):
    step = pl.program_id(0)
    cur = step % 2
    prv = 1 - cur

    @pl.when(step == 0)
    def _():
        st_ref[...] = jnp.zeros(st_ref.shape, F32)
        w_s[1] = jnp.zeros(w_s.shape[1:], BF16)
        bt_s[1] = jnp.zeros(bt_s.shape[1:], BF16)
        op_s[1] = jnp.zeros(op_s.shape[1:], BF16)
        gl_s[1] = jnp.zeros(gl_s.shape[1:], F32)

    rows = q_ref.shape[0]
    nchunk = rows // CHUNK
    hsl = lambda hd: slice(hd * DN_HD, (hd + 1) * DN_HD)

    def scan_chunk(c):
        rs = slice(c * CHUNK, (c + 1) * CHUNK)
        for hd in range(DN_HEADS):
            st = st_ref[hd]
            r = jnp.dot(st.astype(BF16), w_s[prv, c, hd], preferred_element_type=F32)
            st_ref[hd] = gl_s[prv, c, hd:hd + 1, :] * st - r[:, 0:DN_HD] + bt_s[prv, c, hd].astype(F32)
            half = (hd % 2) * CHUNK
            o = r[:, DN_HD:2 * DN_HD].T[half:half + CHUNK] + op_s[prv, rs, hsl(hd)].astype(F32)
            ms = jnp.mean(o * o, axis=-1, keepdims=True)
            o_ref[rs, hsl(hd)] = (o * lax.rsqrt(ms + EPS) * nw_ref[...] * z_ref[rs, hsl(hd)].astype(F32)).astype(BF16)

    pending = list(range(nchunk))

    def tick():
        if pending:
            scan_chunk(pending.pop(0))

    wide = DN_HEADS * CHUNK
    bg = bg_ref[...]
    cs_rows = min(rows, MXU_COLS)
    ri = _iota((cs_rows, cs_rows), 0)
    ci = _iota((cs_rows, cs_rows), 1)
    tril_blk = jnp.where((ri // CHUNK == ci // CHUNK) & (ri >= ci), 1.0, 0.0).astype(BF16)
    g_hi = bg.astype(BF16)
    g_r1 = bg - g_hi.astype(F32)
    g_mid = g_r1.astype(BF16)
    g_lo = (g_r1 - g_mid.astype(F32)).astype(BF16)
    g3 = jnp.concatenate([g_hi, g_mid, g_lo], axis=1)
    csum = jnp.concatenate([jnp.dot(tril_blk, g3[r0:r0 + cs_rows], preferred_element_type=F32)
                            for r0 in range(0, rows, cs_rows)], axis=0)
    gcum = csum[:, 0:LANES] + csum[:, LANES:2 * LANES] + csum[:, 2 * LANES:3 * LANES]
    bfull = jnp.concatenate([jnp.broadcast_to(bg[:, hd:hd + 1], (rows, DN_HD)) for hd in range(DN_HEADS)], axis=1)
    gfull = jnp.concatenate([jnp.broadcast_to(gcum[:, DN_HEADS + hd:DN_HEADS + hd + 1], (rows, DN_HD))
                             for hd in range(DN_HEADS)], axis=1)
    eg = jnp.exp(gfull)
    q = q_ref[...].astype(F32)
    k = k_ref[...].astype(F32)
    kb = k * bfull
    vb = v_ref[...].astype(F32) * bfull
    q_dec = q * eg
    kbg = kb * eg

    r64 = _iota((CHUNK, wide), 0)
    j64 = _iota((CHUNK, wide), 1) % CHUNK
    tril = r64 >= j64
    strict = r64 > j64
    eye = jnp.where(r64 == j64, 1.0, 0.0).astype(F32)
    low_half = _iota((CHUNK, LANES), 1) < CHUNK
    bd64 = _iota((wide, wide), 0) // CHUNK == _iota((wide, wide), 1) // CHUNK
    bd_k = _iota((wide, DN_W), 0) // CHUNK == _iota((wide, DN_W), 1) // DN_HD

    def block_diag(m):
        return jnp.where(bd64, jnp.concatenate([m.astype(BF16)] * DN_HEADS, axis=0), jnp.zeros((wide, wide), BF16))

    rsl = lambda c: slice(c * CHUNK, (c + 1) * CHUNK)
    chunks = range(nchunk)
    g_last = [gfull[c * CHUNK + CHUNK - 1:(c + 1) * CHUNK, :] for c in chunks]
    k_dec = [k[rsl(c)] * jnp.exp(g_last[c] - gfull[rsl(c)]) for c in chunks]
    g_row = [jnp.concatenate([gcum[rsl(c)], gcum[rsl(c)]], axis=0).T for c in chunks]
    dec = []
    for c in chunks:
        halves = []
        for pr in range(DN_HEADS // 2):
            ha, hb = 2 * pr, 2 * pr + 1
            col = jnp.where(low_half, gfull[rsl(c), hsl(ha)], gfull[rsl(c), hsl(hb)])
            row = jnp.where(low_half[0:1], g_row[c][DN_HEADS + ha:DN_HEADS + ha + 1, :],
                            g_row[c][DN_HEADS + hb:DN_HEADS + hb + 1, :])
            halves.append(col - row)
        dec.append(jnp.exp(jnp.where(tril, jnp.concatenate(halves, axis=1), NEG)))
    kq = [lax.dot_general(jnp.concatenate([kb[rsl(c)].astype(BF16), q_ref[rsl(c), :]], axis=0),
                          jnp.where(bd_k, jnp.concatenate([k_ref[rsl(c), :]] * DN_HEADS, axis=0),
                                    jnp.zeros((wide, DN_W), BF16)),
                          NT_DIMS, preferred_element_type=F32) for c in chunks]
    tick()
    a_low = [jnp.where(strict, m[0:CHUNK] * d, 0.0) for m, d in zip(kq, dec)]
    attn = [jnp.where(tril, m[CHUNK:2 * CHUNK] * d, 0.0) for m, d in zip(kq, dec)]
    bmat = [-a for a in a_low]
    ymat = [eye + b for b in bmat]
    bmat = [jnp.dot(b.astype(BF16), block_diag(b), preferred_element_type=F32) for b in bmat]
    tick()
    for _ in range(4):
        prod = [jnp.dot(jnp.concatenate([y, b], axis=0).astype(BF16), block_diag(b), preferred_element_type=F32)
                for y, b in zip(ymat, bmat)]
        tick()
        ymat = [y + p[0:CHUNK] for y, p in zip(ymat, prod)]
        bmat = [p[CHUNK:2 * CHUNK] for p in prod]
    ymat = [y + jnp.dot(y.astype(BF16), block_diag(b), preferred_element_type=F32) for y, b in zip(ymat, bmat)]
    tick()
    rhs = [jnp.concatenate([jnp.concatenate([kbg[rsl(c), hsl(hd)], vb[rsl(c), hsl(hd)]], axis=1)
                            for hd in range(DN_HEADS)], axis=0).astype(BF16) for c in chunks]
    wu = [jnp.dot(block_diag(y), r, preferred_element_type=F32) for y, r in zip(ymat, rhs)]
    tick()
    wut = [m.T.astype(BF16) for m in wu]
    kdbd = [jnp.where(bd_k, jnp.concatenate([k_dec[c].astype(BF16)] * DN_HEADS, axis=0),
                      jnp.zeros((wide, DN_W), BF16)) for c in chunks]
    pbt = [jnp.dot(a, b, preferred_element_type=F32) for a, b in zip(wut, kdbd)]
    tick()
    aw = [jnp.dot(block_diag(a), m.astype(BF16), preferred_element_type=F32) for a, m in zip(attn, wu)]
    qpt = [(jnp.concatenate([q_dec[rsl(c), hsl(hd)] for hd in range(DN_HEADS)], axis=0) - a[:, 0:DN_HD]).T
           .astype(BF16) for c, a in zip(chunks, aw)]
    while pending:
        tick()
    for c in chunks:
        for hd in range(DN_HEADS):
            pair = slice((hd // 2) * LANES, (hd // 2 + 1) * LANES)
            w_s[cur, c, hd] = jnp.concatenate([pbt[c][0:DN_HD, hsl(hd)].astype(BF16), qpt[c][:, pair]], axis=1)
            bt_s[cur, c, hd] = pbt[c][DN_HD:2 * DN_HD, hsl(hd)].astype(BF16)
            op_s[cur, rsl(c), hsl(hd)] = aw[c][hd * CHUNK:(hd + 1) * CHUNK, DN_HD:2 * DN_HD].astype(BF16)
        gl_rows = [jnp.exp(g_last[c][:, hsl(hd)]) for hd in range(DN_HEADS)]
        gl_rows.append(jnp.zeros((SUBLANES - DN_HEADS, DN_HD), F32))
        gl_s[cur, c] = jnp.concatenate(gl_rows, axis=0)


def _dn_call(qn, kn, v, bg, zdn, nw_row, rows):
    s = qn.shape[0]
    nblk = s // rows
    ncb = rows // CHUNK
    cur = lambda n: pl.BlockSpec((rows, n), lambda i: (jnp.minimum(i, nblk - 1), 0))
    prv = lambda n: pl.BlockSpec((rows, n), lambda i: (jnp.maximum(i - 1, 0), 0))
    return pl.pallas_call(
        _dn_body,
        out_shape=jax.ShapeDtypeStruct((s, DN_W), BF16),
        grid=(nblk + 1,),
        in_specs=[cur(DN_W), cur(DN_W), cur(DN_W), cur(LANES), prv(DN_W), pl.BlockSpec((1, DN_HD), lambda i: (0, 0))],
        out_specs=prv(DN_W),
        scratch_shapes=[pltpu.VMEM((2, ncb, DN_HEADS, DN_HD, 2 * DN_HD), BF16),
                        pltpu.VMEM((2, ncb, DN_HEADS, DN_HD, DN_HD), BF16),
                        pltpu.VMEM((2, rows, DN_W), BF16),
                        pltpu.VMEM((2, ncb, SUBLANES, DN_HD), F32),
                        pltpu.VMEM((DN_HEADS, DN_HD, DN_HD), F32)],
        compiler_params=_params(("arbitrary",)),
        name="deltanet",
    )(qn, kn, v, bg, zdn, nw_row)


def _attn_body(q1, q4, q16, k1, k4, k16, v1, v4, v16, kp1, vp1, kp4a, kp4b, kp4c, kp4d,
               vp4a, vp4b, vp4c, vp4d, kp16, vp16, z_ref, anw_ref, o_ref, m_s, l_s, a_s, m_n, l_n, a_n):
    d4, d16 = DILATIONS[1], DILATIONS[2]
    seg4 = SUPER // d4
    kp4 = (kp4a, kp4b, kp4c, kp4d)
    vp4 = (vp4a, vp4b, vp4c, vp4d)
    kj = _iota((2 * Q_BLOCK, Q_BLOCK), 0)
    qi = _iota((2 * Q_BLOCK, Q_BLOCK), 1)
    rel = Q_BLOCK + qi - kj
    band = (rel >= 0) & (rel <= W_SUB)
    bias_in = jnp.where(band, 0.0, NEG).astype(BF16)
    bias_first = jnp.where(band & (kj >= Q_BLOCK), 0.0, NEG).astype(BF16)
    bias_edge = jnp.where(pl.program_id(0) == 0, bias_first, bias_in)
    one_hot = jnp.where(kj % Q_BLOCK == qi, 1.0, 0.0).astype(BF16)
    lane = _iota((Q_BLOCK, LANES), 1)
    lane_k = _iota((2 * Q_BLOCK, LANES), 1)
    head_a_q = (lane % AT_HD) < (AT_HD // 2)
    head_a = lane < AT_HD
    head_a_k = lane_k < AT_HD
    ro = _iota((4 * Q_BLOCK, LANES), 0)
    ones_rhs = jnp.where((ro < 2 * Q_BLOCK) == (_iota((4 * Q_BLOCK, LANES), 1) < AT_HD), 1.0, 0.0).astype(BF16)
    sr = _iota((2 * LANES, LANES), 0)
    ssq_rhs = jnp.where((sr % LANES) // AT_HD == _iota((2 * LANES, LANES), 1) // AT_HD, 1.0, 0.0).astype(BF16)
    zq = jnp.zeros((Q_BLOCK, LANES), BF16)
    zv = jnp.zeros((2 * Q_BLOCK, LANES), BF16)

    def scores(load):
        q, kb, vb, bias = load()
        q2 = jnp.concatenate([jnp.where(head_a_q, q, zq), jnp.where(head_a_q, zq, q)], axis=0)
        s = lax.dot_general(jnp.concatenate([q2, one_hot], axis=1), jnp.concatenate([kb, bias], axis=1),
                            NT_DIMS, preferred_element_type=F32)
        m = jnp.max(s, axis=-1, keepdims=True)
        return jnp.exp2(s - m).astype(BF16), m, vb

    def values(p, m, vb):
        pc = jnp.concatenate([p[0:Q_BLOCK], p[Q_BLOCK:2 * Q_BLOCK]], axis=1)
        rhs = jnp.concatenate([jnp.concatenate([jnp.where(head_a_k, vb, zv), jnp.where(head_a_k, zv, vb)], axis=0),
                               ones_rhs], axis=1)
        r = jnp.dot(pc, rhs, preferred_element_type=F32)
        return r[:, 0:LANES], r[:, LANES:2 * LANES], jnp.where(head_a, m[0:Q_BLOCK], m[Q_BLOCK:2 * Q_BLOCK])

    def run(loads, finish, post=None):
        n = len(loads)
        pending, done = {}, {}
        for i in range(n + ATT_LAG + 1):
            if i < n:
                pending[i] = scores(loads[i])
            j = i - ATT_LAG
            if 0 <= j < n:
                done[j] = finish(j, *values(*pending.pop(j)))
            if post is not None and 0 <= j - 1 < n:
                post(j - 1, done.pop(j - 1))

    def merged(acc, l, m, m_old, l_old, a_old):
        m_new = jnp.maximum(m_old, m)
        w_old = jnp.exp2(m_old - m_new)
        w_new = jnp.exp2(m - m_new)
        return a_old * w_old + acc * w_new, l_old * w_old + l * w_new, m_new

    def load16(r):
        rs = slice(r * Q_BLOCK, (r + 1) * Q_BLOCK)
        return lambda: (q16[rs, :], jnp.concatenate([kp16[rs, :], k16[rs, :]], axis=0),
                        jnp.concatenate([vp16[rs, :], v16[rs, :]], axis=0), bias_edge)

    def finish16(r, acc, l, m):
        rows = pl.ds((r % d4) * seg4 + r // d4, Q_BLOCK, stride=d4)
        a_s[rows, :] = acc
        l_s[rows, :] = l
        m_s[rows, :] = m

    run([load16(r) for r in range(d16)], finish16)

    def load4(r, b):
        base = r * seg4 + b * Q_BLOCK
        if b == 0:
            return lambda: (q4[base:base + Q_BLOCK, :],
                            jnp.concatenate([kp4[r][...], k4[base:base + Q_BLOCK, :]], axis=0),
                            jnp.concatenate([vp4[r][...], v4[base:base + Q_BLOCK, :]], axis=0), bias_edge)
        return lambda: (q4[base:base + Q_BLOCK, :], k4[base - Q_BLOCK:base + Q_BLOCK, :],
                        v4[base - Q_BLOCK:base + Q_BLOCK, :], bias_in)

    blocks4 = [(r, b) for b in range(seg4 // Q_BLOCK) for r in range(d4)]

    def finish4(j, acc, l, m):
        r, b = blocks4[j]
        rows = slice(r * seg4 + b * Q_BLOCK, r * seg4 + (b + 1) * Q_BLOCK)
        acc, l, m = merged(acc, l, m, m_s[rows, :], l_s[rows, :], a_s[rows, :])
        a_s[rows, :] = acc
        l_s[rows, :] = l
        m_s[rows, :] = m

    run([load4(r, b) for r, b in blocks4], finish4)

    for c in range(d4):
        src = slice(c * seg4, (c + 1) * seg4)
        dst = pl.ds(c, seg4, stride=d4)
        m_n[dst, :] = m_s[src, :]
        l_n[dst, :] = l_s[src, :]
        a_n[dst, :] = a_s[src, :]

    def load1(b):
        rs = slice(b * Q_BLOCK, (b + 1) * Q_BLOCK)
        if b == 0:
            return lambda: (q1[rs, :], jnp.concatenate([kp1[...], k1[rs, :]], axis=0),
                            jnp.concatenate([vp1[...], v1[rs, :]], axis=0), bias_edge)
        ks = slice((b - 1) * Q_BLOCK, (b + 1) * Q_BLOCK)
        return lambda: (q1[rs, :], k1[ks, :], v1[ks, :], bias_in)

    def finish1(b, acc, l, m):
        rs = slice(b * Q_BLOCK, (b + 1) * Q_BLOCK)
        acc, l, _ = merged(acc, l, m, m_n[rs, :], l_n[rs, :], a_n[rs, :])
        o = acc / l
        sq = o * o
        hi = sq.astype(BF16)
        return o, jnp.concatenate([hi, (sq - hi.astype(F32)).astype(BF16)], axis=1)

    def post1(b, res):
        o, sq2 = res
        rs = slice(b * Q_BLOCK, (b + 1) * Q_BLOCK)
        ssq = jnp.dot(sq2, ssq_rhs, preferred_element_type=F32)
        o_ref[rs, :] = (o * lax.rsqrt(ssq * (1.0 / AT_HD) + EPS) * anw_ref[...]
                        * z_ref[rs, :].astype(F32)).astype(BF16)

    run([load1(b) for b in range(SUPER // Q_BLOCK)], finish1, post1)


def _attn_call(aq, ak, av, zat, anw_row):
    s = zat.shape[0]
    d4 = DILATIONS[1]
    per_super = SUPER // Q_BLOCK
    cur = pl.BlockSpec((SUPER, LANES), lambda m, g: (m, g))
    prev16 = pl.BlockSpec((SUPER, LANES), lambda m, g: (jnp.maximum(m - 1, 0), g))
    prev1 = pl.BlockSpec((Q_BLOCK, LANES), lambda m, g: (jnp.maximum(m * per_super - 1, 0), g))
    prev4 = [pl.BlockSpec((Q_BLOCK, LANES),
                          lambda m, g, r=r: (jnp.maximum((m - 1) * per_super + (r + 1) * (per_super // d4) - 1, 0), g))
             for r in range(d4)]
    state = pltpu.VMEM((SUPER, LANES), F32)
    return pl.pallas_call(
        _attn_body,
        out_shape=jax.ShapeDtypeStruct((s, AT_W), BF16),
        grid=(s // SUPER, AT_W // LANES),
        in_specs=[cur] * 9 + [prev1, prev1] + prev4 + prev4 + [prev16, prev16, cur,
                                                               pl.BlockSpec((1, LANES), lambda m, g: (0, g))],
        out_specs=cur,
        scratch_shapes=[state] * 6,
        compiler_params=_params(("arbitrary", "arbitrary")),
        name="attn",
    )(*aq, *ak, *av, ak[0], av[0], *([ak[1]] * d4), *([av[1]] * d4), ak[2], av[2], zat, anw_row)


def _out_body(x_ref, odn_ref, oat_ref, gate_ref, w_ref, fnw_ref, o_ref):
    mixin = jnp.concatenate([odn_ref[...], oat_ref[...]], axis=1)
    mix = jnp.dot(mixin, w_ref[...], preferred_element_type=F32)
    y = x_ref[...] + gate_ref[...] * mix
    ms = jnp.mean(y * y, axis=-1, keepdims=True)
    o_ref[...] = y * lax.rsqrt(ms + EPS) * fnw_ref[...]


def _out_call(x2, odn, oat, gate, w_out, fnw_row, tm):
    s, d = x2.shape
    tile = lambda n: pl.BlockSpec((tm, n), lambda i: (i, 0))
    row = lambda n: pl.BlockSpec((1, n), lambda i: (0, 0))
    return pl.pallas_call(
        _out_body,
        out_shape=jax.ShapeDtypeStruct((s, d), F32),
        grid=(s // tm,),
        in_specs=[tile(d), tile(DN_W), tile(AT_W), row(d), pl.BlockSpec(w_out.shape, lambda i: (0, 0)), row(d)],
        out_specs=tile(d),
        compiler_params=_params(("arbitrary",)),
        name="out",
    )(x2, odn, oat, gate, w_out, fnw_row)


def _rope_pair_perm():
    half = AT_HD // 2
    idx = []
    for gp in range(AT_W // LANES):
        for l in range(LANES):
            part, hh, j = l // AT_HD, (l % AT_HD) // half, l % half
            idx.append((2 * gp + hh) * AT_HD + part * half + j)
    return np.asarray(idx, np.int32)


def kernel(x, c, positions, w_mod, b_mod, norm_w, w_in, conv_w, a_log, dt_bias, dn_norm_w, at_norm_w,
           w_out, final_norm_w):
    b, s, d = x.shape
    assert b == 1 and d == D_MODEL and w_mod.shape[0] == 1
    x2 = x.reshape(s, d)

    mod = _mod_call(c.reshape(d, 1), w_mod[0], b_mod[0].reshape(1, 3 * d))
    shift, scale, gate = mod[:, 0:d], mod[:, d:2 * d], mod[:, 2 * d:3 * d]

    wi = w_in[0]
    o_z, o_b, o_a, o_aq = 3 * DN_W, 4 * DN_W, 4 * DN_W + DN_HEADS, 4 * DN_W + 2 * DN_HEADS
    perm = _rope_pair_perm()
    w_aq = wi[:, o_aq:o_aq + AT_W][:, perm]
    w_ak = wi[:, o_aq + AT_W:o_aq + 2 * AT_W][:, perm]
    w_ba = jnp.pad(wi[:, o_b:o_b + 2 * DN_HEADS], ((0, 0), (0, LANES - 2 * DN_HEADS)))
    w_all = jnp.concatenate([wi[:, 0:o_z], wi[:, o_z:o_b], w_aq, w_ak,
                             wi[:, o_aq + 2 * AT_W:o_aq + 4 * AT_W], w_ba], axis=1).astype(BF16)
    assert w_all.shape[1] == W_COLS
    pad_heads = lambda v: jnp.pad(v.reshape(1, DN_HEADS), ((0, 0), (DN_HEADS, LANES - 2 * DN_HEADS)))
    half = AT_HD // 2
    invf = (ROPE_THETA ** (-jnp.arange(half, dtype=F32) / half)).reshape(half, 1)
    sgn = jnp.where(jnp.arange(LANES) < LANES // 2, -1.0, 1.0).astype(F32).reshape(1, LANES)

    assert s % SUPER == 0
    outs = _proj_call(x2, positions.reshape(1, s), norm_w[0].reshape(1, d), scale, shift, w_all, conv_w[0],
                      pad_heads(a_log[0]), pad_heads(dt_bias[0]), invf, sgn, tm=256)
    qn, kn, v, zdn, bg, zat = outs[0:6]
    aq, ak, av = outs[6:9], outs[9:12], outs[12:15]

    odn = _dn_call(qn, kn, v, bg, zdn, dn_norm_w[0].reshape(1, DN_HD), rows=512)
    oat = _attn_call(aq, ak, av, zat, jnp.tile(at_norm_w[0], AT_HEADS).reshape(1, AT_W))

    out = _out_call(x2, odn, oat, gate, w_out[0].astype(BF16), final_norm_w.reshape(1, d), tm=1024)
    return out.reshape(b, s, d)
```

```python
import math

import numpy as np
import jax
import jax.numpy as jnp
from jax import lax
from jax.experimental import pallas as pl
from jax.experimental.pallas import tpu as pltpu

D_MODEL = 1024
DN_HEADS = 4
DN_HD = 128
DN_W = DN_HEADS * DN_HD
AT_HEADS = 8
AT_HD = 64
AT_W = AT_HEADS * AT_HD
CONV_K = 4
CHUNK = 64
DILATIONS = (1, 4, 16)
W_SUB = 128
Q_BLOCK = 128
SUPER = Q_BLOCK * DILATIONS[-1]
ATT_LAG = 2
ROPE_THETA = 10000.0
EPS = 1e-6
NEG = -1e30

LANES = 128
SUBLANES = 8
MXU_COLS = 256
PROJ_COLS = 512
VMEM_LIMIT = 48 * 1024 * 1024
PROJ_VMEM_LIMIT = 56 * 1024 * 1024

C_QKV, C_Z, C_AQ, C_AK, C_AV, C_AZ, C_BA = 0, 1536, 2048, 2560, 3072, 3584, 4096
W_COLS = C_BA + LANES

F32 = jnp.float32
BF16 = jnp.bfloat16
NT_DIMS = (((1,), (1,)), ((), ()))


def _sigmoid(v):
    return 1.0 / (1.0 + jnp.exp2(v * (-math.log2(math.e))))


def _iota(shape, dim):
    return lax.broadcasted_iota(jnp.int32, shape, dim)


def _params(sem):
    return pltpu.CompilerParams(dimension_semantics=sem, vmem_limit_bytes=VMEM_LIMIT)


def _mod_body(c_ref, w_ref, b_ref, o_ref):
    c = c_ref[...]
    sc = c * _sigmoid(c)
    o_ref[...] = jnp.sum(sc * w_ref[...], axis=0, keepdims=True) + b_ref[...]


def _mod_call(c_col, w_mod, b_mod):
    d, n = w_mod.shape
    tn = 512
    return pl.pallas_call(
        _mod_body,
        out_shape=jax.ShapeDtypeStruct((1, n), F32),
        grid=(n // tn,),
        in_specs=[pl.BlockSpec((d, 1), lambda j: (0, 0)),
                  pl.BlockSpec((d, tn), lambda j: (0, j)),
                  pl.BlockSpec((1, tn), lambda j: (0, j))],
        out_specs=pl.BlockSpec((1, tn), lambda j: (0, j)),
        compiler_params=_params(("arbitrary",)),
        name="mod",
    )(c_col, w_mod, b_mod)


def _proj_body(x_ref, pos_ref, nw_ref, sc_ref, sh_ref, w_ref, cw_ref, alog_ref, dtb_ref, invf_ref,
               sgn_ref, qn_ref, kn_ref, v_ref, zdn_ref, bg_ref, zat_ref,
               aq1_ref, aq4_ref, aq16_ref, ak1_ref, ak4_ref, ak16_ref, av1_ref, av4_ref, av16_ref,
               cbuf_ref, hb_ref, pbuf_ref, p4buf_ref):
    i = pl.program_id(0)
    tm = x_ref.shape[0]
    x = x_ref[...]
    ms = jnp.mean(x * x, axis=-1, keepdims=True)
    hb_ref[...] = ((x * lax.rsqrt(ms + EPS)) * (nw_ref[...] * (1.0 + sc_ref[...])) + sh_ref[...]).astype(BF16)

    def proj(c0, width=PROJ_COLS):
        return jnp.dot(hb_ref[...], w_ref[:, c0:c0 + width], preferred_element_type=F32)

    def silu(v):
        return v * _sigmoid(v)

    @pl.when(i == 0)
    def _():
        cbuf_ref[:, 0:SUBLANES, :] = jnp.zeros((cbuf_ref.shape[0], SUBLANES, LANES), F32)

    dyn0 = jnp.minimum(i, 0)
    for c0 in range(0, 3 * DN_W, PROJ_COLS):
        xg = proj(C_QKV + c0)
        halves = []
        for hh in range(PROJ_COLS // LANES):
            grp = c0 // LANES + hh
            ls = slice(c0 + hh * LANES, c0 + (hh + 1) * LANES)
            xh = xg[:, hh * LANES:(hh + 1) * LANES]
            cbuf_ref[grp, SUBLANES:SUBLANES + tm, :] = xh
            acc = xh * cw_ref[CONV_K - 1:CONV_K, ls]
            for j in range(1, CONV_K):
                acc = acc + cbuf_ref[grp, pl.ds(dyn0 + SUBLANES - j, tm), :] * cw_ref[CONV_K - 1 - j:CONV_K - j, ls]
            cbuf_ref[grp, 0:SUBLANES, :] = xh[tm - SUBLANES:tm]
            halves.append(acc)
        conv = jnp.concatenate(halves, axis=1)
        y = silu(conv)
        kind, off = c0 // DN_W, c0 % DN_W
        for hh in range(PROJ_COLS // DN_HD):
            yh = y[:, hh * DN_HD:(hh + 1) * DN_HD]
            dst = slice(off + hh * DN_HD, off + (hh + 1) * DN_HD)
            if kind == 2:
                v_ref[:, dst] = yh.astype(BF16)
            else:
                inv = lax.rsqrt(jnp.sum(yh * yh, axis=-1, keepdims=True) + EPS)
                if kind == 0:
                    qn_ref[:, dst] = (yh * (inv * (DN_HD ** -0.5))).astype(BF16)
                else:
                    kn_ref[:, dst] = (yh * inv).astype(BF16)

    for c0 in range(0, DN_W, PROJ_COLS):
        zdn_ref[:, c0:c0 + PROJ_COLS] = silu(proj(C_Z + c0)).astype(BF16)
        zat_ref[:, c0:c0 + PROJ_COLS] = silu(proj(C_AZ + c0)).astype(BF16)

    ba = proj(C_BA, LANES)
    lane = _iota(ba.shape, 1)
    a = ba + dtb_ref[...]
    softplus = jnp.maximum(a, 0.0) + jnp.log(1.0 + jnp.exp(-jnp.abs(a)))
    g = -jnp.exp(alog_ref[...]) * softplus
    bg_ref[...] = jnp.where(lane < DN_HEADS, _sigmoid(ba), jnp.where(lane < 2 * DN_HEADS, g, 0.0))

    ang = invf_ref[...] * pos_ref[...].astype(F32)
    reps = LANES // (AT_HD // 2)
    cos = jnp.concatenate([jnp.cos(ang)] * reps, axis=0).T
    sin = jnp.concatenate([jnp.sin(ang)] * reps, axis=0).T * sgn_ref[...]
    qscale = (AT_HD ** -0.5) * math.log2(math.e)
    step_in_super = i % (SUPER // tm)
    d4, d16 = DILATIONS[1], DILATIONS[2]
    n4, n16 = tm // d4, tm // d16

    def emit(val, gp, nat_ref, p4_ref, p16_ref):
        sl = slice(gp * LANES, (gp + 1) * LANES)
        nat_ref[:, sl] = val.astype(BF16)
        pbuf_ref[gp] = val
        for c in range(d4):
            seg = pbuf_ref[gp, pl.ds(c, n4, stride=d4), :]
            dst = pl.multiple_of(c * (SUPER // d4) + step_in_super * n4, n4)
            p4_ref[pl.ds(dst, n4), sl] = seg.astype(BF16)
            p4buf_ref[gp, c * n4:(c + 1) * n4, :] = seg
        for r in range(d16):
            seg = p4buf_ref[gp, pl.ds((r % d4) * n4 + r // d4, n16, stride=d4), :]
            dst = pl.multiple_of(r * (SUPER // d16) + step_in_super * n16, n16)
            p16_ref[pl.ds(dst, n16), sl] = seg.astype(BF16)

    def rope(v):
        return v * cos + pltpu.roll(v, LANES // 2, axis=1) * sin

    for c0 in range(0, AT_W, PROJ_COLS):
        aq = proj(C_AQ + c0)
        ak = proj(C_AK + c0)
        av = proj(C_AV + c0)
        for hh in range(PROJ_COLS // LANES):
            gp = c0 // LANES + hh
            ls = slice(hh * LANES, (hh + 1) * LANES)
            emit(rope(aq[:, ls]) * qscale, gp, aq1_ref, aq4_ref, aq16_ref)
            emit(rope(ak[:, ls]), gp, ak1_ref, ak4_ref, ak16_ref)
            emit(av[:, ls], gp, av1_ref, av4_ref, av16_ref)


def _proj_call(x2, pos_row, nw, scale, shift, w_all, cw, alog_row, dtb_row, invf, sgn, tm):
    s, d = x2.shape
    row = lambda n: pl.BlockSpec((1, n), lambda i: (0, 0))
    full = lambda a: pl.BlockSpec(a.shape, lambda i: (0, 0))
    tile = lambda n: pl.BlockSpec((tm, n), lambda i: (i, 0))
    sup = pl.BlockSpec((SUPER, AT_W), lambda i: (i // (SUPER // tm), 0))
    bf = lambda n: jax.ShapeDtypeStruct((s, n), BF16)
    outs = [bf(DN_W)] * 4 + [jax.ShapeDtypeStruct((s, LANES), F32), bf(AT_W)] + [bf(AT_W)] * 9
    out_specs = [tile(DN_W)] * 4 + [tile(LANES), tile(AT_W)] + [tile(AT_W), sup, sup] * 3
    return pl.pallas_call(
        _proj_body,
        out_shape=outs,
        grid=(s // tm,),
        in_specs=[tile(d), pl.BlockSpec((1, tm), lambda i: (0, i)), row(d), row(d), row(d),
                  pl.BlockSpec(w_all.shape, lambda i: (0, 0), pipeline_mode=pl.Buffered(1)),
                  full(cw), row(LANES), row(LANES), full(invf), row(LANES)],
        out_specs=out_specs,
        scratch_shapes=[pltpu.VMEM((3 * DN_W // LANES, SUBLANES + tm, LANES), F32), pltpu.VMEM((tm, d), BF16),
                        pltpu.VMEM((AT_W // LANES, tm, LANES), F32),
                        pltpu.VMEM((AT_W // LANES, tm, LANES), F32)],
        compiler_params=pltpu.CompilerParams(dimension_semantics=("arbitrary",),
                                             vmem_limit_bytes=PROJ_VMEM_LIMIT),
        name="proj",
    )(x2, pos_row, nw, scale, shift, w_all, cw, alog_row, dtb_row, invf, sgn)


def _dn_body(q_ref, k_ref, v_ref, bg_ref, z_ref, nw_ref, o_ref, w_s, bt_s, op_s, gl_s, st_ref):
    step = pl.program_id(0)
    cur = step % 2
    prv = 1 - cur

    @pl.when(step == 0)
    def _():
        st_ref[...] = jnp.zeros(st_ref.shape, F32)
        w_s[1] = jnp.zeros(w_s.shape[1:], BF16)
        bt_s[1] = jnp.zeros(bt_s.shape[1:], BF16)
        op_s[1] = jnp.zeros(op_s.shape[1:], BF16)
        gl_s[1] = jnp.zeros(gl_s.shape[1:], F32)

    rows = q_ref.shape[0]
    nchunk = rows // CHUNK
    hsl = lambda hd: slice(hd * DN_HD, (hd + 1) * DN_HD)

    def scan_chunk(c):
        rs = slice(c * CHUNK, (c + 1) * CHUNK)
        for hd in range(DN_HEADS):
            st = st_ref[hd]
            r = jnp.dot(st.astype(BF16), w_s[prv, c, hd], preferred_element_type=F32)
            st_ref[hd] = gl_s[prv, c, hd:hd + 1, :] * st - r[:, 0:DN_HD] + bt_s[prv, c, hd].astype(F32)
            half = (hd % 2) * CHUNK
            o = r[:, DN_HD:2 * DN_HD].T[half:half + CHUNK] + op_s[prv, rs, hsl(hd)].astype(F32)
            ms = jnp.mean(o * o, axis=-1, keepdims=True)
            o_ref[rs, hsl(hd)] = (o * lax.rsqrt(ms + EPS) * nw_ref[...] * z_ref[rs, hsl(hd)].astype(F32)).astype(BF16)

    pending = list(range(nchunk))

    def tick():
        if pending:
            scan_chunk(pending.pop(0))

    wide = DN_HEADS * CHUNK
    bg = bg_ref[...]
    cs_rows = min(rows, MXU_COLS)
    ri = _iota((cs_rows, cs_rows), 0)
    ci = _iota((cs_rows, cs_rows), 1)
    tril_blk = jnp.where((ri // CHUNK == ci // CHUNK) & (ri >= ci), 1.0, 0.0).astype(BF16)
    g_hi = bg.astype(BF16)
    g_r1 = bg - g_hi.astype(F32)
    g_mid = g_r1.astype(BF16)
    g_lo = (g_r1 - g_mid.astype(F32)).astype(BF16)
    g3 = jnp.concatenate([g_hi, g_mid, g_lo], axis=1)
    csum = jnp.concatenate([jnp.dot(tril_blk, g3[r0:r0 + cs_rows], preferred_element_type=F32)
                            for r0 in range(0, rows, cs_rows)], axis=0)
    gcum = csum[:, 0:LANES] + csum[:, LANES:2 * LANES] + csum[:, 2 * LANES:3 * LANES]
    bfull = jnp.concatenate([jnp.broadcast_to(bg[:, hd:hd + 1], (rows, DN_HD)) for hd in range(DN_HEADS)], axis=1)
    gfull = jnp.concatenate([jnp.broadcast_to(gcum[:, DN_HEADS + hd:DN_HEADS + hd + 1], (rows, DN_HD))
                             for hd in range(DN_HEADS)], axis=1)
    eg = jnp.exp(gfull)
    q = q_ref[...].astype(F32)
    k = k_ref[...].astype(F32)
    kb = k * bfull
    vb = v_ref[...].astype(F32) * bfull
    q_dec = q * eg
    kbg = kb * eg

    r64 = _iota((CHUNK, wide), 0)
    j64 = _iota((CHUNK, wide), 1) % CHUNK
    tril = r64 >= j64
    strict = r64 > j64
    eye = jnp.where(r64 == j64, 1.0, 0.0).astype(F32)
    low_half = _iota((CHUNK, LANES), 1) < CHUNK
    bd64 = _iota((wide, wide), 0) // CHUNK == _iota((wide, wide), 1) // CHUNK
    bd_k = _iota((wide, DN_W), 0) // CHUNK == _iota((wide, DN_W), 1) // DN_HD

    def block_diag(m):
        return jnp.where(bd64, jnp.concatenate([m.astype(BF16)] * DN_HEADS, axis=0), jnp.zeros((wide, wide), BF16))

    rsl = lambda c: slice(c * CHUNK, (c + 1) * CHUNK)
    chunks = range(nchunk)
    g_last = [gfull[c * CHUNK + CHUNK - 1:(c + 1) * CHUNK, :] for c in chunks]
    k_dec = [k[rsl(c)] * jnp.exp(g_last[c] - gfull[rsl(c)]) for c in chunks]
    g_row = [jnp.concatenate([gcum[rsl(c)], gcum[rsl(c)]], axis=0).T for c in chunks]
    dec = []
    for c in chunks:
        halves = []
        for pr in range(DN_HEADS // 2):
            ha, hb = 2 * pr, 2 * pr + 1
            col = jnp.where(low_half, gfull[rsl(c), hsl(ha)], gfull[rsl(c), hsl(hb)])
            row = jnp.where(low_half[0:1], g_row[c][DN_HEADS + ha:DN_HEADS + ha + 1, :],
                            g_row[c][DN_HEADS + hb:DN_HEADS + hb + 1, :])
            halves.append(col - row)
        dec.append(jnp.exp(jnp.where(tril, jnp.concatenate(halves, axis=1), NEG)))
    kq = [lax.dot_general(jnp.concatenate([kb[rsl(c)].astype(BF16), q_ref[rsl(c), :]], axis=0),
                          jnp.where(bd_k, jnp.concatenate([k_ref[rsl(c), :]] * DN_HEADS, axis=0),
                                    jnp.zeros((wide, DN_W), BF16)),
                          NT_DIMS, preferred_element_type=F32) for c in chunks]
    tick()
    a_low = [jnp.where(strict, m[0:CHUNK] * d, 0.0) for m, d in zip(kq, dec)]
    attn = [jnp.where(tril, m[CHUNK:2 * CHUNK] * d, 0.0) for m, d in zip(kq, dec)]
    bmat = [-a for a in a_low]
    ymat = [eye + b for b in bmat]
    bmat = [jnp.dot(b.astype(BF16), block_diag(b), preferred_element_type=F32) for b in bmat]
    tick()
    for _ in range(4):
        prod = [jnp.dot(jnp.concatenate([y, b], axis=0).astype(BF16), block_diag(b), preferred_element_type=F32)
                for y, b in zip(ymat, bmat)]
        tick()
        ymat = [y + p[0:CHUNK] for y, p in zip(ymat, prod)]
        bmat = [p[CHUNK:2 * CHUNK] for p in prod]
    ymat = [y + jnp.dot(y.astype(BF16), block_diag(b), preferred_element_type=F32) for y, b in zip(ymat, bmat)]
    tick()
    rhs = [jnp.concatenate([jnp.concatenate([kbg[rsl(c), hsl(hd)], vb[rsl(c), hsl(hd)]], axis=1)
                            for hd in range(DN_HEADS)], axis=0).astype(BF16) for c in chunks]
    wu = [jnp.dot(block_diag(y), r, preferred_element_type=F32) for y, r in zip(ymat, rhs)]
    tick()
    wut = [m.T.astype(BF16) for m in wu]
    kdbd = [jnp.where(bd_k, jnp.concatenate([k_dec[c].astype(BF16)] * DN_HEADS, axis=0),
                      jnp.zeros((wide, DN_W), BF16)) for c in chunks]
    pbt = [jnp.dot(a, b, preferred_element_type=F32) for a, b in zip(wut, kdbd)]
    tick()
    aw = [jnp.dot(block_diag(a), m.astype(BF16), preferred_element_type=F32) for a, m in zip(attn, wu)]
    qpt = [(jnp.concatenate([q_dec[rsl(c), hsl(hd)] for hd in range(DN_HEADS)], axis=0) - a[:, 0:DN_HD]).T
           .astype(BF16) for c, a in zip(chunks, aw)]
    while pending:
        tick()
    for c in chunks:
        for hd in range(DN_HEADS):
            pair = slice((hd // 2) * LANES, (hd // 2 + 1) * LANES)
            w_s[cur, c, hd] = jnp.concatenate([pbt[c][0:DN_HD, hsl(hd)].astype(BF16), qpt[c][:, pair]], axis=1)
            bt_s[cur, c, hd] = pbt[c][DN_HD:2 * DN_HD, hsl(hd)].astype(BF16)
            op_s[cur, rsl(c), hsl(hd)] = aw[c][hd * CHUNK:(hd + 1) * CHUNK, DN_HD:2 * DN_HD].astype(BF16)
        gl_rows = [jnp.exp(g_last[c][:, hsl(hd)]) for hd in range(DN_HEADS)]
        gl_rows.append(jnp.zeros((SUBLANES - DN_HEADS, DN_HD), F32))
        gl_s[cur, c] = jnp.concatenate(gl_rows, axis=0)


def _dn_call(qn, kn, v, bg, zdn, nw_row, rows):
    s = qn.shape[0]
    nblk = s // rows
    ncb = rows // CHUNK
    cur = lambda n: pl.BlockSpec((rows, n), lambda i: (jnp.minimum(i, nblk - 1), 0))
    prv = lambda n: pl.BlockSpec((rows, n), lambda i: (jnp.maximum(i - 1, 0), 0))
    return pl.pallas_call(
        _dn_body,
        out_shape=jax.ShapeDtypeStruct((s, DN_W), BF16),
        grid=(nblk + 1,),
        in_specs=[cur(DN_W), cur(DN_W), cur(DN_W), cur(LANES), prv(DN_W), pl.BlockSpec((1, DN_HD), lambda i: (0, 0))],
        out_specs=prv(DN_W),
        scratch_shapes=[pltpu.VMEM((2, ncb, DN_HEADS, DN_HD, 2 * DN_HD), BF16),
                        pltpu.VMEM((2, ncb, DN_HEADS, DN_HD, DN_HD), BF16),
                        pltpu.VMEM((2, rows, DN_W), BF16),
                        pltpu.VMEM((2, ncb, SUBLANES, DN_HD), F32),
                        pltpu.VMEM((DN_HEADS, DN_HD, DN_HD), F32)],
        compiler_params=_params(("arbitrary",)),
        name="deltanet",
    )(qn, kn, v, bg, zdn, nw_row)


def _attn_body(q1, q4, q16, k1, k4, k16, v1, v4, v16, kp1, vp1, kp4a, kp4b, kp4c, kp4d,
               vp4a, vp4b, vp4c, vp4d, kp16, vp16, z_ref, anw_ref, o_ref, m_s, l_s, a_s, m_n, l_n, a_n):
    d4, d16 = DILATIONS[1], DILATIONS[2]
    seg4 = SUPER // d4
    kp4 = (kp4a, kp4b, kp4c, kp4d)
    vp4 = (vp4a, vp4b, vp4c, vp4d)
    kj = _iota((2 * Q_BLOCK, Q_BLOCK), 0)
    qi = _iota((2 * Q_BLOCK, Q_BLOCK), 1)
    rel = Q_BLOCK + qi - kj
    band = (rel >= 0) & (rel <= W_SUB)
    bias_in = jnp.where(band, 0.0, NEG).astype(BF16)
    bias_first = jnp.where(band & (kj >= Q_BLOCK), 0.0, NEG).astype(BF16)
    bias_edge = jnp.where(pl.program_id(0) == 0, bias_first, bias_in)
    one_hot = jnp.where(kj % Q_BLOCK == qi, 1.0, 0.0).astype(BF16)
    lane = _iota((Q_BLOCK, LANES), 1)
    lane_k = _iota((2 * Q_BLOCK, LANES), 1)
    head_a_q = (lane % AT_HD) < (AT_HD // 2)
    head_a = lane < AT_HD
    head_a_k = lane_k < AT_HD
    ro = _iota((4 * Q_BLOCK, LANES), 0)
    ones_rhs = jnp.where((ro < 2 * Q_BLOCK) == (_iota((4 * Q_BLOCK, LANES), 1) < AT_HD), 1.0, 0.0).astype(BF16)
    sr = _iota((2 * LANES, LANES), 0)
    ssq_rhs = jnp.where((sr % LANES) // AT_HD == _iota((2 * LANES, LANES), 1) // AT_HD, 1.0, 0.0).astype(BF16)
    zq = jnp.zeros((Q_BLOCK, LANES), BF16)
    zv = jnp.zeros((2 * Q_BLOCK, LANES), BF16)

    def scores(load):
        q, kb, vb, bias = load()
        q2 = jnp.concatenate([jnp.where(head_a_q, q, zq), jnp.where(head_a_q, zq, q)], axis=0)
        s = lax.dot_general(jnp.concatenate([q2, one_hot], axis=1), jnp.concatenate([kb, bias], axis=1),
                            NT_DIMS, preferred_element_type=F32)
        m = jnp.max(s, axis=-1, keepdims=True)
        return jnp.exp2(s - m).astype(BF16), m, vb

    def values(p, m, vb):
        pc = jnp.concatenate([p[0:Q_BLOCK], p[Q_BLOCK:2 * Q_BLOCK]], axis=1)
        rhs = jnp.concatenate([jnp.concatenate([jnp.where(head_a_k, vb, zv), jnp.where(head_a_k, zv, vb)], axis=0),
                               ones_rhs], axis=1)
        r = jnp.dot(pc, rhs, preferred_element_type=F32)
        return r[:, 0:LANES], r[:, LANES:2 * LANES], jnp.where(head_a, m[0:Q_BLOCK], m[Q_BLOCK:2 * Q_BLOCK])

    def run(loads, finish, post=None):
        n = len(loads)
        pending, done = {}, {}
        for i in range(n + ATT_LAG + 1):
            if i < n:
                pending[i] = scores(loads[i])
            j = i - ATT_LAG
            if 0 <= j < n:
                done[j] = finish(j, *values(*pending.pop(j)))
            if post is not None and 0 <= j - 1 < n:
                post(j - 1, done.pop(j - 1))

    def merged(acc, l, m, m_old, l_old, a_old):
        m_new = jnp.maximum(m_old, m)
        w_old = jnp.exp2(m_old - m_new)
        w_new = jnp.exp2(m - m_new)
        return a_old * w_old + acc * w_new, l_old * w_old + l * w_new, m_new

    def load16(r):
        rs = slice(r * Q_BLOCK, (r + 1) * Q_BLOCK)
        return lambda: (q16[rs, :], jnp.concatenate([kp16[rs, :], k16[rs, :]], axis=0),
                        jnp.concatenate([vp16[rs, :], v16[rs, :]], axis=0), bias_edge)

    def finish16(r, acc, l, m):
        rows = pl.ds((r % d4) * seg4 + r // d4, Q_BLOCK, stride=d4)
        a_s[rows, :] = acc
        l_s[rows, :] = l
        m_s[rows, :] = m

    run([load16(r) for r in range(d16)], finish16)

    def load4(r, b):
        base = r * seg4 + b * Q_BLOCK
        if b == 0:
            return lambda: (q4[base:base + Q_BLOCK, :],
                            jnp.concatenate([kp4[r][...], k4[base:base + Q_BLOCK, :]], axis=0),
                            jnp.concatenate([vp4[r][...], v4[base:base + Q_BLOCK, :]], axis=0), bias_edge)
        return lambda: (q4[base:base + Q_BLOCK, :], k4[base - Q_BLOCK:base + Q_BLOCK, :],
                        v4[base - Q_BLOCK:base + Q_BLOCK, :], bias_in)

    blocks4 = [(r, b) for b in range(seg4 // Q_BLOCK) for r in range(d4)]

    def finish4(j, acc, l, m):
        r, b = blocks4[j]
        rows = slice(r * seg4 + b * Q_BLOCK, r * seg4 + (b + 1) * Q_BLOCK)
        acc, l, m = merged(acc, l, m, m_s[rows, :], l_s[rows, :], a_s[rows, :])
        a_s[rows, :] = acc
        l_s[rows, :] = l
        m_s[rows, :] = m

    run([load4(r, b) for r, b in blocks4], finish4)

    for c in range(d4):
        src = slice(c * seg4, (c + 1) * seg4)
        dst = pl.ds(c, seg4, stride=d4)
        m_n[dst, :] = m_s[src, :]
        l_n[dst, :] = l_s[src, :]
        a_n[dst, :] = a_s[src, :]

    def load1(b):
        rs = slice(b * Q_BLOCK, (b + 1) * Q_BLOCK)
        if b == 0:
            return lambda: (q1[rs, :], jnp.concatenate([kp1[...], k1[rs, :]], axis=0),
                            jnp.concatenate([vp1[...], v1[rs, :]], axis=0), bias_edge)
        ks = slice((b - 1) * Q_BLOCK, (b + 1) * Q_BLOCK)
        return lambda: (q1[rs, :], k1[ks, :], v1[ks, :], bias_in)

    def finish1(b, acc, l, m):
        rs = slice(b * Q_BLOCK, (b + 1) * Q_BLOCK)
        acc, l, _ = merged(acc, l, m, m_n[rs, :], l_n[rs, :], a_n[rs, :])
        o = acc / l
        sq = o * o
        hi = sq.astype(BF16)
        return o, jnp.concatenate([hi, (sq - hi.astype(F32)).astype(BF16)], axis=1)

    def post1(b, res):
        o, sq2 = res
        rs = slice(b * Q_BLOCK, (b + 1) * Q_BLOCK)
        ssq = jnp.dot(sq2, ssq_rhs, preferred_element_type=F32)
        o_ref[rs, :] = (o * lax.rsqrt(ssq * (1.0 / AT_HD) + EPS) * anw_ref[...]
                        * z_ref[rs, :].astype(F32)).astype(BF16)

    run([load1(b) for b in range(SUPER // Q_BLOCK)], finish1, post1)


def _attn_call(aq, ak, av, zat, anw_row):
    s = zat.shape[0]
    d4 = DILATIONS[1]
    per_super = SUPER // Q_BLOCK
    cur = pl.BlockSpec((SUPER, LANES), lambda m, g: (m, g))
    prev16 = pl.BlockSpec((SUPER, LANES), lambda m, g: (jnp.maximum(m - 1, 0), g))
    prev1 = pl.BlockSpec((Q_BLOCK, LANES), lambda m, g: (jnp.maximum(m * per_super - 1, 0), g))
    prev4 = [pl.BlockSpec((Q_BLOCK, LANES),
                          lambda m, g, r=r: (jnp.maximum((m - 1) * per_super + (r + 1) * (per_super // d4) - 1, 0), g))
             for r in range(d4)]
    state = pltpu.VMEM((SUPER, LANES), F32)
    return pl.pallas_call(
        _attn_body,
        out_shape=jax.ShapeDtypeStruct((s, AT_W), BF16),
        grid=(s // SUPER, AT_W // LANES),
        in_specs=[cur] * 9 + [prev1, prev1] + prev4 + prev4 + [prev16, prev16, cur,
                                                               pl.BlockSpec((1, LANES), lambda m, g: (0, g))],
        out_specs=cur,
        scratch_shapes=[state] * 6,
        compiler_params=_params(("arbitrary", "arbitrary")),
        name="attn",
    )(*aq, *ak, *av, ak[0], av[0], *([ak[1]] * d4), *([av[1]] * d4), ak[2], av[2], zat, anw_row)


def _out_body(x_ref, odn_ref, oat_ref, gate_ref, w_ref, fnw_ref, o_ref):
    mixin = jnp.concatenate([odn_ref[...], oat_ref[...]], axis=1)
    mix = jnp.dot(mixin, w_ref[...], preferred_element_type=F32)
    y = x_ref[...] + gate_ref[...] * mix
    ms = jnp.mean(y * y, axis=-1, keepdims=True)
    o_ref[...] = y * lax.rsqrt(ms + EPS) * fnw_ref[...]


def _out_call(x2, odn, oat, gate, w_out, fnw_row, tm):
    s, d = x2.shape
    tile = lambda n: pl.BlockSpec((tm, n), lambda i: (i, 0))
    row = lambda n: pl.BlockSpec((1, n), lambda i: (0, 0))
    return pl.pallas_call(
        _out_body,
        out_shape=jax.ShapeDtypeStruct((s, d), F32),
        grid=(s // tm,),
        in_specs=[tile(d), tile(DN_W), tile(AT_W), row(d), pl.BlockSpec(w_out.shape, lambda i: (0, 0)), row(d)],
        out_specs=tile(d),
        compiler_params=_params(("arbitrary",)),
        name="out",
    )(x2, odn, oat, gate, w_out, fnw_row)


def _rope_pair_perm():
    half = AT_HD // 2
    idx = []
    for gp in range(AT_W // LANES):
        for l in range(LANES):
            part, hh, j = l // AT_HD, (l % AT_HD) // half, l % half
            idx.append((2 * gp + hh) * AT_HD + part * half + j)
    return np.asarray(idx, np.int32)


def kernel(x, c, positions, w_mod, b_mod, norm_w, w_in, conv_w, a_log, dt_bias, dn_norm_w, at_norm_w,
           w_out, final_norm_w):
    b, s, d = x.shape
    assert b == 1 and d == D_MODEL and w_mod.shape[0] == 1
    x2 = x.reshape(s, d)

    mod = _mod_call(c.reshape(d, 1), w_mod[0], b_mod[0].reshape(1, 3 * d))
    shift, scale, gate = mod[:, 0:d], mod[:, d:2 * d], mod[:, 2 * d:3 * d]

    wi = w_in[0]
    o_z, o_b, o_a, o_aq = 3 * DN_W, 4 * DN_W, 4 * DN_W + DN_HEADS, 4 * DN_W + 2 * DN_HEADS
    perm = _rope_pair_perm()
    w_aq = wi[:, o_aq:o_aq + AT_W][:, perm]
    w_ak = wi[:, o_aq + AT_W:o_aq + 2 * AT_W][:, perm]
    w_ba = jnp.pad(wi[:, o_b:o_b + 2 * DN_HEADS], ((0, 0), (0, LANES - 2 * DN_HEADS)))
    w_all = jnp.concatenate([wi[:, 0:o_z], wi[:, o_z:o_b], w_aq, w_ak,
                             wi[:, o_aq + 2 * AT_W:o_aq + 4 * AT_W], w_ba], axis=1).astype(BF16)
    assert w_all.shape[1] == W_COLS
    pad_heads = lambda v: jnp.pad(v.reshape(1, DN_HEADS), ((0, 0), (DN_HEADS, LANES - 2 * DN_HEADS)))
    half = AT_HD // 2
    invf = (ROPE_THETA ** (-jnp.arange(half, dtype=F32) / half)).reshape(half, 1)
    sgn = jnp.where(jnp.arange(LANES) < LANES // 2, -1.0, 1.0).astype(F32).reshape(1, LANES)

    assert s % SUPER == 0
    outs = _proj_call(x2, positions.reshape(1, s), norm_w[0].reshape(1, d), scale, shift, w_all, conv_w[0],
                      pad_heads(a_log[0]), pad_heads(dt_bias[0]), invf, sgn, tm=256)
    qn, kn, v, zdn, bg, zat = outs[0:6]
    aq, ak, av = outs[6:9], outs[9:12], outs[12:15]

    odn = _dn_call(qn, kn, v, bg, zdn, dn_norm_w[0].reshape(1, DN_HD), rows=512)
    oat = _attn_call(aq, ak, av, zat, jnp.tile(at_norm_w[0], AT_HEADS).reshape(1, AT_W))

    out = _out_call(x2, odn, oat, gate, w_out[0].astype(BF16), final_norm_w.reshape(1, d), tm=1024)
    return out.reshape(b, s, d)
```

```python
import math

import numpy as np
import jax
import jax.numpy as jnp
from jax import lax
from jax.experimental import pallas as pl
from jax.experimental.pallas import tpu as pltpu

D_MODEL = 1024
DN_HEADS = 4
DN_HD = 128
DN_W = DN_HEADS * DN_HD
AT_HEADS = 8
AT_HD = 64
AT_W = AT_HEADS * AT_HD
CONV_K = 4
CHUNK = 64
DILATIONS = (1, 4, 16)
W_SUB = 128
Q_BLOCK = 128
SUPER = Q_BLOCK * DILATIONS[-1]
ATT_LAG = 2
ROPE_THETA = 10000.0
EPS = 1e-6
NEG = -1e30

LANES = 128
SUBLANES = 8
MXU_COLS = 256
PROJ_COLS = 512
VMEM_LIMIT = 48 * 1024 * 1024

C_QKV, C_Z, C_AQ, C_AK, C_AV, C_AZ, C_BA = 0, 1536, 2048, 2560, 3072, 3584, 4096
W_COLS = C_BA + LANES

F32 = jnp.float32
BF16 = jnp.bfloat16
NT_DIMS = (((1,), (1,)), ((), ()))


def _sigmoid(v):
    return 1.0 / (1.0 + jnp.exp2(v * (-math.log2(math.e))))


def _iota(shape, dim):
    return lax.broadcasted_iota(jnp.int32, shape, dim)


def _params(sem):
    return pltpu.CompilerParams(dimension_semantics=sem, vmem_limit_bytes=VMEM_LIMIT)


def _mod_body(c_ref, w_ref, b_ref, o_ref):
    c = c_ref[...]
    sc = c * _sigmoid(c)
    o_ref[...] = jnp.sum(sc * w_ref[...], axis=0, keepdims=True) + b_ref[...]


def _mod_call(c_col, w_mod, b_mod):
    d, n = w_mod.shape
    tn = 512
    return pl.pallas_call(
        _mod_body,
        out_shape=jax.ShapeDtypeStruct((1, n), F32),
        grid=(n // tn,),
        in_specs=[pl.BlockSpec((d, 1), lambda j: (0, 0)),
                  pl.BlockSpec((d, tn), lambda j: (0, j)),
                  pl.BlockSpec((1, tn), lambda j: (0, j))],
        out_specs=pl.BlockSpec((1, tn), lambda j: (0, j)),
        compiler_params=_params(("arbitrary",)),
        name="mod",
    )(c_col, w_mod, b_mod)


def _proj_body(x_ref, pos_ref, nw_ref, sc_ref, sh_ref, w_ref, cw_ref, alog_ref, dtb_ref, invf_ref,
               sgn_ref, qn_ref, kn_ref, v_ref, zdn_ref, bg_ref, zat_ref,
               aq1_ref, aq4_ref, aq16_ref, ak1_ref, ak4_ref, ak16_ref, av1_ref, av4_ref, av16_ref,
               cbuf_ref, hb_ref, pbuf_ref, p4buf_ref):
    i = pl.program_id(0)
    tm = x_ref.shape[0]
    x = x_ref[...]
    ms = jnp.mean(x * x, axis=-1, keepdims=True)
    hb_ref[...] = ((x * lax.rsqrt(ms + EPS)) * (nw_ref[...] * (1.0 + sc_ref[...])) + sh_ref[...]).astype(BF16)

    def proj(c0, width=PROJ_COLS):
        return jnp.dot(hb_ref[...], w_ref[:, c0:c0 + width], preferred_element_type=F32)

    def silu(v):
        return v * _sigmoid(v)

    @pl.when(i == 0)
    def _():
        cbuf_ref[:, 0:SUBLANES, :] = jnp.zeros((cbuf_ref.shape[0], SUBLANES, LANES), F32)

    dyn0 = jnp.minimum(i, 0)
    for c0 in range(0, 3 * DN_W, PROJ_COLS):
        xg = proj(C_QKV + c0)
        halves = []
        for hh in range(PROJ_COLS // LANES):
            grp = c0 // LANES + hh
            ls = slice(c0 + hh * LANES, c0 + (hh + 1) * LANES)
            xh = xg[:, hh * LANES:(hh + 1) * LANES]
            cbuf_ref[grp, SUBLANES:SUBLANES + tm, :] = xh
            acc = xh * cw_ref[CONV_K - 1:CONV_K, ls]
            for j in range(1, CONV_K):
                acc = acc + cbuf_ref[grp, pl.ds(dyn0 + SUBLANES - j, tm), :] * cw_ref[CONV_K - 1 - j:CONV_K - j, ls]
            cbuf_ref[grp, 0:SUBLANES, :] = xh[tm - SUBLANES:tm]
            halves.append(acc)
        conv = jnp.concatenate(halves, axis=1)
        y = silu(conv)
        kind, off = c0 // DN_W, c0 % DN_W
        for hh in range(PROJ_COLS // DN_HD):
            yh = y[:, hh * DN_HD:(hh + 1) * DN_HD]
            dst = slice(off + hh * DN_HD, off + (hh + 1) * DN_HD)
            if kind == 2:
                v_ref[:, dst] = yh.astype(BF16)
            else:
                inv = lax.rsqrt(jnp.sum(yh * yh, axis=-1, keepdims=True) + EPS)
                if kind == 0:
                    qn_ref[:, dst] = (yh * (inv * (DN_HD ** -0.5))).astype(BF16)
                else:
                    kn_ref[:, dst] = (yh * inv).astype(BF16)

    for c0 in range(0, DN_W, PROJ_COLS):
        zdn_ref[:, c0:c0 + PROJ_COLS] = silu(proj(C_Z + c0)).astype(BF16)
        zat_ref[:, c0:c0 + PROJ_COLS] = silu(proj(C_AZ + c0)).astype(BF16)

    ba = proj(C_BA, LANES)
    lane = _iota(ba.shape, 1)
    a = ba + dtb_ref[...]
    softplus = jnp.maximum(a, 0.0) + jnp.log(1.0 + jnp.exp(-jnp.abs(a)))
    g = -jnp.exp(alog_ref[...]) * softplus
    bg_ref[...] = jnp.where(lane < DN_HEADS, _sigmoid(ba), jnp.where(lane < 2 * DN_HEADS, g, 0.0))

    ang = invf_ref[...] * pos_ref[...].astype(F32)
    reps = LANES // (AT_HD // 2)
    cos = jnp.concatenate([jnp.cos(ang)] * reps, axis=0).T
    sin = jnp.concatenate([jnp.sin(ang)] * reps, axis=0).T * sgn_ref[...]
    qscale = (AT_HD ** -0.5) * math.log2(math.e)
    d4, d16 = DILATIONS[1], DILATIONS[2]
    n4, n16 = tm // d4, tm // d16

    def emit(val, gp, nat_ref, p4_ref, p16_ref):
        sl = slice(gp * LANES, (gp + 1) * LANES)
        nat_ref[:, sl] = val.astype(BF16)
        pbuf_ref[gp] = val
        for c in range(d4):
            seg = pbuf_ref[gp, pl.ds(c, n4, stride=d4), :]
            p4_ref[0, c, :, sl] = seg.astype(BF16)
            p4buf_ref[gp, c * n4:(c + 1) * n4, :] = seg
        for r in range(d16):
            seg = p4buf_ref[gp, pl.ds((r % d4) * n4 + r // d4, n16, stride=d4), :]
            p16_ref[0, r, :, sl] = seg.astype(BF16)

    def rope(v):
        return v * cos + pltpu.roll(v, LANES // 2, axis=1) * sin

    for c0 in range(0, AT_W, PROJ_COLS):
        aq = proj(C_AQ + c0)
        ak = proj(C_AK + c0)
        av = proj(C_AV + c0)
        for hh in range(PROJ_COLS // LANES):
            gp = c0 // LANES + hh
            ls = slice(hh * LANES, (hh + 1) * LANES)
            emit(rope(aq[:, ls]) * qscale, gp, aq1_ref, aq4_ref, aq16_ref)
            emit(rope(ak[:, ls]), gp, ak1_ref, ak4_ref, ak16_ref)
            emit(av[:, ls], gp, av1_ref, av4_ref, av16_ref)


def _proj_call(x2, pos_row, nw, scale, shift, w_all, cw, alog_row, dtb_row, invf, sgn, tm):
    s, d = x2.shape
    row = lambda n: pl.BlockSpec((1, n), lambda i: (0, 0))
    full = lambda a: pl.BlockSpec(a.shape, lambda i: (0, 0))
    tile = lambda n: pl.BlockSpec((tm, n), lambda i: (i, 0))
    spb = SUPER // tm
    res_shape = lambda dil: jax.ShapeDtypeStruct((s // SUPER, dil, SUPER // dil, AT_W), BF16)
    res_spec = lambda dil: pl.BlockSpec((1, dil, tm // dil, AT_W), lambda i: (i // spb, 0, i % spb, 0))
    bf = lambda n: jax.ShapeDtypeStruct((s, n), BF16)
    outs = [bf(DN_W)] * 4 + [jax.ShapeDtypeStruct((s, LANES), F32), bf(AT_W)] + \
           [bf(AT_W), res_shape(DILATIONS[1]), res_shape(DILATIONS[2])] * 3
    out_specs = [tile(DN_W)] * 4 + [tile(LANES), tile(AT_W)] + \
                [tile(AT_W), res_spec(DILATIONS[1]), res_spec(DILATIONS[2])] * 3
    return pl.pallas_call(
        _proj_body,
        out_shape=outs,
        grid=(s // tm,),
        in_specs=[tile(d), pl.BlockSpec((1, tm), lambda i: (0, i)), row(d), row(d), row(d),
                  pl.BlockSpec(w_all.shape, lambda i: (0, 0), pipeline_mode=pl.Buffered(1)),
                  full(cw), row(LANES), row(LANES), full(invf), row(LANES)],
        out_specs=out_specs,
        scratch_shapes=[pltpu.VMEM((3 * DN_W // LANES, SUBLANES + tm, LANES), F32), pltpu.VMEM((tm, d), BF16),
                        pltpu.VMEM((AT_W // LANES, tm, LANES), F32),
                        pltpu.VMEM((AT_W // LANES, tm, LANES), F32)],
        compiler_params=_params(("arbitrary",)),
        name="proj",
    )(x2, pos_row, nw, scale, shift, w_all, cw, alog_row, dtb_row, invf, sgn)


def _dn_body(q_ref, k_ref, v_ref, bg_ref, z_ref, nw_ref, o_ref, w_s, bt_s, op_s, gl_s, st_ref):
    step = pl.program_id(0)
    cur = step % 2
    prv = 1 - cur

    @pl.when(step == 0)
    def _():
        st_ref[...] = jnp.zeros(st_ref.shape, F32)
        w_s[1] = jnp.zeros(w_s.shape[1:], BF16)
        bt_s[1] = jnp.zeros(bt_s.shape[1:], BF16)
        op_s[1] = jnp.zeros(op_s.shape[1:], BF16)
        gl_s[1] = jnp.zeros(gl_s.shape[1:], F32)

    rows = q_ref.shape[0]
    nchunk = rows // CHUNK
    hsl = lambda hd: slice(hd * DN_HD, (hd + 1) * DN_HD)

    def scan_chunk(c):
        rs = slice(c * CHUNK, (c + 1) * CHUNK)
        for hd in range(DN_HEADS):
            st = st_ref[hd]
            r = jnp.dot(st.astype(BF16), w_s[prv, c, hd], preferred_element_type=F32)
            st_ref[hd] = gl_s[prv, c, hd:hd + 1, :] * st - r[:, 0:DN_HD] + bt_s[prv, c, hd].astype(F32)
            half = (hd % 2) * CHUNK
            o = r[:, DN_HD:2 * DN_HD].T[half:half + CHUNK] + op_s[prv, rs, hsl(hd)].astype(F32)
            ms = jnp.mean(o * o, axis=-1, keepdims=True)
            o_ref[rs, hsl(hd)] = (o * lax.rsqrt(ms + EPS) * nw_ref[...] * z_ref[rs, hsl(hd)].astype(F32)).astype(BF16)

    pending = list(range(nchunk))

    def tick():
        if pending:
            scan_chunk(pending.pop(0))

    wide = DN_HEADS * CHUNK
    bg = bg_ref[...]
    cs_rows = min(rows, MXU_COLS)
    ri = _iota((cs_rows, cs_rows), 0)
    ci = _iota((cs_rows, cs_rows), 1)
    tril_blk = jnp.where((ri // CHUNK == ci // CHUNK) & (ri >= ci), 1.0, 0.0).astype(BF16)
    g_hi = bg.astype(BF16)
    g_r1 = bg - g_hi.astype(F32)
    g_mid = g_r1.astype(BF16)
    g_lo = (g_r1 - g_mid.astype(F32)).astype(BF16)
    g3 = jnp.concatenate([g_hi, g_mid, g_lo], axis=1)
    csum = jnp.concatenate([jnp.dot(tril_blk, g3[r0:r0 + cs_rows], preferred_element_type=F32)
                            for r0 in range(0, rows, cs_rows)], axis=0)
    gcum = csum[:, 0:LANES] + csum[:, LANES:2 * LANES] + csum[:, 2 * LANES:3 * LANES]
    bfull = jnp.concatenate([jnp.broadcast_to(bg[:, hd:hd + 1], (rows, DN_HD)) for hd in range(DN_HEADS)], axis=1)
    gfull = jnp.concatenate([jnp.broadcast_to(gcum[:, DN_HEADS + hd:DN_HEADS + hd + 1], (rows, DN_HD))
                             for hd in range(DN_HEADS)], axis=1)
    eg = jnp.exp(gfull)
    q = q_ref[...].astype(F32)
    k = k_ref[...].astype(F32)
    kb = k * bfull
    vb = v_ref[...].astype(F32) * bfull
    q_dec = q * eg
    kbg = kb * eg

    r64 = _iota((CHUNK, wide), 0)
    j64 = _iota((CHUNK, wide), 1) % CHUNK
    tril = r64 >= j64
    strict = r64 > j64
    eye = jnp.where(r64 == j64, 1.0, 0.0).astype(F32)
    low_half = _iota((CHUNK, LANES), 1) < CHUNK
    bd64 = _iota((wide, wide), 0) // CHUNK == _iota((wide, wide), 1) // CHUNK
    bd_k = _iota((wide, DN_W), 0) // CHUNK == _iota((wide, DN_W), 1) // DN_HD

    def block_diag(m):
        return jnp.where(bd64, jnp.concatenate([m.astype(BF16)] * DN_HEADS, axis=0), jnp.zeros((wide, wide), BF16))

    rsl = lambda c: slice(c * CHUNK, (c + 1) * CHUNK)
    chunks = range(nchunk)
    g_last = [gfull[c * CHUNK + CHUNK - 1:(c + 1) * CHUNK, :] for c in chunks]
    k_dec = [k[rsl(c)] * jnp.exp(g_last[c] - gfull[rsl(c)]) for c in chunks]
    g_row = [jnp.concatenate([gcum[rsl(c)], gcum[rsl(c)]], axis=0).T for c in chunks]
    dec = []
    for c in chunks:
        halves = []
        for pr in range(DN_HEADS // 2):
            ha, hb = 2 * pr, 2 * pr + 1
            col = jnp.where(low_half, gfull[rsl(c), hsl(ha)], gfull[rsl(c), hsl(hb)])
            row = jnp.where(low_half[0:1], g_row[c][DN_HEADS + ha:DN_HEADS + ha + 1, :],
                            g_row[c][DN_HEADS + hb:DN_HEADS + hb + 1, :])
            halves.append(col - row)
        dec.append(jnp.exp(jnp.where(tril, jnp.concatenate(halves, axis=1), NEG)))
    kq = [lax.dot_general(jnp.concatenate([kb[rsl(c)].astype(BF16), q_ref[rsl(c), :]], axis=0),
                          jnp.where(bd_k, jnp.concatenate([k_ref[rsl(c), :]] * DN_HEADS, axis=0),
                                    jnp.zeros((wide, DN_W), BF16)),
                          NT_DIMS, preferred_element_type=F32) for c in chunks]
    tick()
    a_low = [jnp.where(strict, m[0:CHUNK] * d, 0.0) for m, d in zip(kq, dec)]
    attn = [jnp.where(tril, m[CHUNK:2 * CHUNK] * d, 0.0) for m, d in zip(kq, dec)]
    bmat = [-a for a in a_low]
    ymat = [eye + b for b in bmat]
    bmat = [jnp.dot(b.astype(BF16), block_diag(b), preferred_element_type=F32) for b in bmat]
    tick()
    for _ in range(4):
        prod = [jnp.dot(jnp.concatenate([y, b], axis=0).astype(BF16), block_diag(b), preferred_element_type=F32)
                for y, b in zip(ymat, bmat)]
        tick()
        ymat = [y + p[0:CHUNK] for y, p in zip(ymat, prod)]
        bmat = [p[CHUNK:2 * CHUNK] for p in prod]
    ymat = [y + jnp.dot(y.astype(BF16), block_diag(b), preferred_element_type=F32) for y, b in zip(ymat, bmat)]
    tick()
    rhs = [jnp.concatenate([jnp.concatenate([kbg[rsl(c), hsl(hd)], vb[rsl(c), hsl(hd)]], axis=1)
                            for hd in range(DN_HEADS)], axis=0).astype(BF16) for c in chunks]
    wu = [jnp.dot(block_diag(y), r, preferred_element_type=F32) for y, r in zip(ymat, rhs)]
    tick()
    wut = [m.T.astype(BF16) for m in wu]
    kdbd = [jnp.where(bd_k, jnp.concatenate([k_dec[c].astype(BF16)] * DN_HEADS, axis=0),
                      jnp.zeros((wide, DN_W), BF16)) for c in chunks]
    pbt = [jnp.dot(a, b, preferred_element_type=F32) for a, b in zip(wut, kdbd)]
    tick()
    aw = [jnp.dot(block_diag(a), m.astype(BF16), preferred_element_type=F32) for a, m in zip(attn, wu)]
    qpt = [(jnp.concatenate([q_dec[rsl(c), hsl(hd)] for hd in range(DN_HEADS)], axis=0) - a[:, 0:DN_HD]).T
           .astype(BF16) for c, a in zip(chunks, aw)]
    while pending:
        tick()
    for c in chunks:
        for hd in range(DN_HEADS):
            pair = slice((hd // 2) * LANES, (hd // 2 + 1) * LANES)
            w_s[cur, c, hd] = jnp.concatenate([pbt[c][0:DN_HD, hsl(hd)].astype(BF16), qpt[c][:, pair]], axis=1)
            bt_s[cur, c, hd] = pbt[c][DN_HD:2 * DN_HD, hsl(hd)].astype(BF16)
            op_s[cur, rsl(c), hsl(hd)] = aw[c][hd * CHUNK:(hd + 1) * CHUNK, DN_HD:2 * DN_HD].astype(BF16)
        gl_rows = [jnp.exp(g_last[c][:, hsl(hd)]) for hd in range(DN_HEADS)]
        gl_rows.append(jnp.zeros((SUBLANES - DN_HEADS, DN_HD), F32))
        gl_s[cur, c] = jnp.concatenate(gl_rows, axis=0)


def _dn_call(qn, kn, v, bg, zdn, nw_row, rows):
    s = qn.shape[0]
    nblk = s // rows
    ncb = rows // CHUNK
    cur = lambda n: pl.BlockSpec((rows, n), lambda i: (jnp.minimum(i, nblk - 1), 0))
    prv = lambda n: pl.BlockSpec((rows, n), lambda i: (jnp.maximum(i - 1, 0), 0))
    return pl.pallas_call(
        _dn_body,
        out_shape=jax.ShapeDtypeStruct((s, DN_W), BF16),
        grid=(nblk + 1,),
        in_specs=[cur(DN_W), cur(DN_W), cur(DN_W), cur(LANES), prv(DN_W), pl.BlockSpec((1, DN_HD), lambda i: (0, 0))],
        out_specs=prv(DN_W),
        scratch_shapes=[pltpu.VMEM((2, ncb, DN_HEADS, DN_HD, 2 * DN_HD), BF16),
                        pltpu.VMEM((2, ncb, DN_HEADS, DN_HD, DN_HD), BF16),
                        pltpu.VMEM((2, rows, DN_W), BF16),
                        pltpu.VMEM((2, ncb, SUBLANES, DN_HD), F32),
                        pltpu.VMEM((DN_HEADS, DN_HD, DN_HD), F32)],
        compiler_params=_params(("arbitrary",)),
        name="deltanet",
    )(qn, kn, v, bg, zdn, nw_row)


def _attn_body(q1, q4, q16, k1, k4, k16, v1, v4, v16, kp1, vp1, kp4a, kp4b, kp4c, kp4d,
               vp4a, vp4b, vp4c, vp4d, kp16, vp16, z_ref, anw_ref, o_ref, m_s, l_s, a_s, m_n, l_n, a_n):
    d4, d16 = DILATIONS[1], DILATIONS[2]
    seg4 = SUPER // d4
    kp4 = (kp4a, kp4b, kp4c, kp4d)
    vp4 = (vp4a, vp4b, vp4c, vp4d)
    kj = _iota((2 * Q_BLOCK, Q_BLOCK), 0)
    qi = _iota((2 * Q_BLOCK, Q_BLOCK), 1)
    rel = Q_BLOCK + qi - kj
    band = (rel >= 0) & (rel <= W_SUB)
    bias_in = jnp.where(band, 0.0, NEG).astype(BF16)
    bias_first = jnp.where(band & (kj >= Q_BLOCK), 0.0, NEG).astype(BF16)
    bias_edge = jnp.where(pl.program_id(0) == 0, bias_first, bias_in)
    one_hot = jnp.where(kj % Q_BLOCK == qi, 1.0, 0.0).astype(BF16)
    lane = _iota((Q_BLOCK, LANES), 1)
    lane_k = _iota((2 * Q_BLOCK, LANES), 1)
    head_a_q = (lane % AT_HD) < (AT_HD // 2)
    head_a = lane < AT_HD
    head_a_k = lane_k < AT_HD
    ro = _iota((4 * Q_BLOCK, LANES), 0)
    ones_rhs = jnp.where((ro < 2 * Q_BLOCK) == (_iota((4 * Q_BLOCK, LANES), 1) < AT_HD), 1.0, 0.0).astype(BF16)
    sr = _iota((2 * LANES, LANES), 0)
    ssq_rhs = jnp.where((sr % LANES) // AT_HD == _iota((2 * LANES, LANES), 1) // AT_HD, 1.0, 0.0).astype(BF16)
    zq = jnp.zeros((Q_BLOCK, LANES), BF16)
    zv = jnp.zeros((2 * Q_BLOCK, LANES), BF16)

    def scores(load):
        q, kb, vb, bias = load()
        q2 = jnp.concatenate([jnp.where(head_a_q, q, zq), jnp.where(head_a_q, zq, q)], axis=0)
        s = lax.dot_general(jnp.concatenate([q2, one_hot], axis=1), jnp.concatenate([kb, bias], axis=1),
                            NT_DIMS, preferred_element_type=F32)
        m = jnp.max(s, axis=-1, keepdims=True)
        return jnp.exp2(s - m).astype(BF16), m, vb

    def values(p, m, vb):
        pc = jnp.concatenate([p[0:Q_BLOCK], p[Q_BLOCK:2 * Q_BLOCK]], axis=1)
        rhs = jnp.concatenate([jnp.concatenate([jnp.where(head_a_k, vb, zv), jnp.where(head_a_k, zv, vb)], axis=0),
                               ones_rhs], axis=1)
        r = jnp.dot(pc, rhs, preferred_element_type=F32)
        return r[:, 0:LANES], r[:, LANES:2 * LANES], jnp.where(head_a, m[0:Q_BLOCK], m[Q_BLOCK:2 * Q_BLOCK])

    def run(loads, finish, post=None):
        n = len(loads)
        pending, done = {}, {}
        for i in range(n + ATT_LAG + 1):
            if i < n:
                pending[i] = scores(loads[i])
            j = i - ATT_LAG
            if 0 <= j < n:
                done[j] = finish(j, *values(*pending.pop(j)))
            if post is not None and 0 <= j - 1 < n:
                post(j - 1, done.pop(j - 1))

    def merged(acc, l, m, m_old, l_old, a_old):
        m_new = jnp.maximum(m_old, m)
        w_old = jnp.exp2(m_old - m_new)
        w_new = jnp.exp2(m - m_new)
        return a_old * w_old + acc * w_new, l_old * w_old + l * w_new, m_new

    def load16(r):
        rs = slice(r * Q_BLOCK, (r + 1) * Q_BLOCK)
        return lambda: (q16[rs, :], jnp.concatenate([kp16[rs, :], k16[rs, :]], axis=0),
                        jnp.concatenate([vp16[rs, :], v16[rs, :]], axis=0), bias_edge)

    def finish16(r, acc, l, m):
        rows = pl.ds((r % d4) * seg4 + r // d4, Q_BLOCK, stride=d4)
        a_s[rows, :] = acc
        l_s[rows, :] = l
        m_s[rows, :] = m

    run([load16(r) for r in range(d16)], finish16)

    def load4(r, b):
        base = r * seg4 + b * Q_BLOCK
        if b == 0:
            return lambda: (q4[base:base + Q_BLOCK, :],
                            jnp.concatenate([kp4[r][...], k4[base:base + Q_BLOCK, :]], axis=0),
                            jnp.concatenate([vp4[r][...], v4[base:base + Q_BLOCK, :]], axis=0), bias_edge)
        return lambda: (q4[base:base + Q_BLOCK, :], k4[base - Q_BLOCK:base + Q_BLOCK, :],
                        v4[base - Q_BLOCK:base + Q_BLOCK, :], bias_in)

    blocks4 = [(r, b) for b in range(seg4 // Q_BLOCK) for r in range(d4)]

    def finish4(j, acc, l, m):
        r, b = blocks4[j]
        rows = slice(r * seg4 + b * Q_BLOCK, r * seg4 + (b + 1) * Q_BLOCK)
        acc, l, m = merged(acc, l, m, m_s[rows, :], l_s[rows, :], a_s[rows, :])
        a_s[rows, :] = acc
        l_s[rows, :] = l
        m_s[rows, :] = m

    run([load4(r, b) for r, b in blocks4], finish4)

    for c in range(d4):
        src = slice(c * seg4, (c + 1) * seg4)
        dst = pl.ds(c, seg4, stride=d4)
        m_n[dst, :] = m_s[src, :]
        l_n[dst, :] = l_s[src, :]
        a_n[dst, :] = a_s[src, :]

    def load1(b):
        rs = slice(b * Q_BLOCK, (b + 1) * Q_BLOCK)
        if b == 0:
            return lambda: (q1[rs, :], jnp.concatenate([kp1[...], k1[rs, :]], axis=0),
                            jnp.concatenate([vp1[...], v1[rs, :]], axis=0), bias_edge)
        ks = slice((b - 1) * Q_BLOCK, (b + 1) * Q_BLOCK)
        return lambda: (q1[rs, :], k1[ks, :], v1[ks, :], bias_in)

    def finish1(b, acc, l, m):
        rs = slice(b * Q_BLOCK, (b + 1) * Q_BLOCK)
        acc, l, _ = merged(acc, l, m, m_n[rs, :], l_n[rs, :], a_n[rs, :])
        o = acc / l
        sq = o * o
        hi = sq.astype(BF16)
        return o, jnp.concatenate([hi, (sq - hi.astype(F32)).astype(BF16)], axis=1)

    def post1(b, res):
        o, sq2 = res
        rs = slice(b * Q_BLOCK, (b + 1) * Q_BLOCK)
        ssq = jnp.dot(sq2, ssq_rhs, preferred_element_type=F32)
        o_ref[rs, :] = (o * lax.rsqrt(ssq * (1.0 / AT_HD) + EPS) * anw_ref[...]
                        * z_ref[rs, :].astype(F32)).astype(BF16)

    run([load1(b) for b in range(SUPER // Q_BLOCK)], finish1, post1)


def _attn_call(aq, ak, av, zat, anw_row):
    s = zat.shape[0]
    d4 = DILATIONS[1]
    per_super = SUPER // Q_BLOCK
    cur = pl.BlockSpec((SUPER, LANES), lambda m, g: (m, g))
    prev16 = pl.BlockSpec((SUPER, LANES), lambda m, g: (jnp.maximum(m - 1, 0), g))
    prev1 = pl.BlockSpec((Q_BLOCK, LANES), lambda m, g: (jnp.maximum(m * per_super - 1, 0), g))
    prev4 = [pl.BlockSpec((Q_BLOCK, LANES),
                          lambda m, g, r=r: (jnp.maximum((m - 1) * per_super + (r + 1) * (per_super // d4) - 1, 0), g))
             for r in range(d4)]
    state = pltpu.VMEM((SUPER, LANES), F32)
    return pl.pallas_call(
        _attn_body,
        out_shape=jax.ShapeDtypeStruct((s, AT_W), BF16),
        grid=(s // SUPER, AT_W // LANES),
        in_specs=[cur] * 9 + [prev1, prev1] + prev4 + prev4 + [prev16, prev16, cur,
                                                               pl.BlockSpec((1, LANES), lambda m, g: (0, g))],
        out_specs=cur,
        scratch_shapes=[state] * 6,
        compiler_params=_params(("arbitrary", "arbitrary")),
        name="attn",
    )(*aq, *ak, *av, ak[0], av[0], *([ak[1]] * d4), *([av[1]] * d4), ak[2], av[2], zat, anw_row)


def _out_body(x_ref, odn_ref, oat_ref, gate_ref, w_ref, fnw_ref, o_ref):
    mixin = jnp.concatenate([odn_ref[...], oat_ref[...]], axis=1)
    mix = jnp.dot(mixin, w_ref[...], preferred_element_type=F32)
    y = x_ref[...] + gate_ref[...] * mix
    ms = jnp.mean(y * y, axis=-1, keepdims=True)
    o_ref[...] = y * lax.rsqrt(ms + EPS) * fnw_ref[...]


def _out_call(x2, odn, oat, gate, w_out, fnw_row, tm):
    s, d = x2.shape
    tile = lambda n: pl.BlockSpec((tm, n), lambda i: (i, 0))
    row = lambda n: pl.BlockSpec((1, n), lambda i: (0, 0))
    return pl.pallas_call(
        _out_body,
        out_shape=jax.ShapeDtypeStruct((s, d), F32),
        grid=(s // tm,),
        in_specs=[tile(d), tile(DN_W), tile(AT_W), row(d), pl.BlockSpec(w_out.shape, lambda i: (0, 0)), row(d)],
        out_specs=tile(d),
        compiler_params=_params(("arbitrary",)),
        name="out",
    )(x2, odn, oat, gate, w_out, fnw_row)


def _rope_pair_perm():
    half = AT_HD // 2
    idx = []
    for gp in range(AT_W // LANES):
        for l in range(LANES):
            part, hh, j = l // AT_HD, (l % AT_HD) // half, l % half
            idx.append((2 * gp + hh) * AT_HD + part * half + j)
    return np.asarray(idx, np.int32)


def kernel(x, c, positions, w_mod, b_mod, norm_w, w_in, conv_w, a_log, dt_bias, dn_norm_w, at_norm_w,
           w_out, final_norm_w):
    b, s, d = x.shape
    assert b == 1 and d == D_MODEL and w_mod.shape[0] == 1
    x2 = x.reshape(s, d)

    mod = _mod_call(c.reshape(d, 1), w_mod[0], b_mod[0].reshape(1, 3 * d))
    shift, scale, gate = mod[:, 0:d], mod[:, d:2 * d], mod[:, 2 * d:3 * d]

    wi = w_in[0]
    o_z, o_b, o_a, o_aq = 3 * DN_W, 4 * DN_W, 4 * DN_W + DN_HEADS, 4 * DN_W + 2 * DN_HEADS
    perm = _rope_pair_perm()
    w_aq = wi[:, o_aq:o_aq + AT_W][:, perm]
    w_ak = wi[:, o_aq + AT_W:o_aq + 2 * AT_W][:, perm]
    w_ba = jnp.pad(wi[:, o_b:o_b + 2 * DN_HEADS], ((0, 0), (0, LANES - 2 * DN_HEADS)))
    w_all = jnp.concatenate([wi[:, 0:o_z], wi[:, o_z:o_b], w_aq, w_ak,
                             wi[:, o_aq + 2 * AT_W:o_aq + 4 * AT_W], w_ba], axis=1).astype(BF16)
    assert w_all.shape[1] == W_COLS
    pad_heads = lambda v: jnp.pad(v.reshape(1, DN_HEADS), ((0, 0), (DN_HEADS, LANES - 2 * DN_HEADS)))
    half = AT_HD // 2
    invf = (ROPE_THETA ** (-jnp.arange(half, dtype=F32) / half)).reshape(half, 1)
    sgn = jnp.where(jnp.arange(LANES) < LANES // 2, -1.0, 1.0).astype(F32).reshape(1, LANES)

    assert s % SUPER == 0
    outs = _proj_call(x2, positions.reshape(1, s), norm_w[0].reshape(1, d), scale, shift, w_all, conv_w[0],
                      pad_heads(a_log[0]), pad_heads(dt_bias[0]), invf, sgn, tm=256)
    qn, kn, v, zdn, bg, zat = outs[0:6]
    flat = lambda a: a.reshape(s, AT_W)
    aq, ak, av = [[flat(a) for a in outs[j:j + 3]] for j in (6, 9, 12)]

    odn = _dn_call(qn, kn, v, bg, zdn, dn_norm_w[0].reshape(1, DN_HD), rows=512)
    oat = _attn_call(aq, ak, av, zat, jnp.tile(at_norm_w[0], AT_HEADS).reshape(1, AT_W))

    out = _out_call(x2, odn, oat, gate, w_out[0].astype(BF16), final_norm_w.reshape(1, d), tm=1024)
    return out.reshape(b, s, d)
```

```python
import math

import numpy as np
import jax
import jax.numpy as jnp
from jax import lax
from jax.experimental import pallas as pl
from jax.experimental.pallas import tpu as pltpu

D_MODEL = 1024
DN_HEADS = 4
DN_HD = 128
DN_W = DN_HEADS * DN_HD
AT_HEADS = 8
AT_HD = 64
AT_W = AT_HEADS * AT_HD
CONV_K = 4
CHUNK = 64
DILATIONS = (1, 4, 16)
W_SUB = 128
Q_BLOCK = 128
SUPER = Q_BLOCK * DILATIONS[-1]
ATT_LAG = 2
ROPE_THETA = 10000.0
EPS = 1e-6
NEG = -1e30

LANES = 128
SUBLANES = 8
MXU_COLS = 256
PROJ_COLS = 512
VMEM_LIMIT = 48 * 1024 * 1024

C_QKV, C_Z, C_AQ, C_AK, C_AV, C_AZ, C_BA = 0, 1536, 2048, 2560, 3072, 3584, 4096
W_COLS = C_BA + LANES

F32 = jnp.float32
BF16 = jnp.bfloat16
NT_DIMS = (((1,), (1,)), ((), ()))


def _sigmoid(v):
    return 1.0 / (1.0 + jnp.exp2(v * (-math.log2(math.e))))


def _iota(shape, dim):
    return lax.broadcasted_iota(jnp.int32, shape, dim)


def _params(sem):
    return pltpu.CompilerParams(dimension_semantics=sem, vmem_limit_bytes=VMEM_LIMIT)


def _mod_body(c_ref, w_ref, b_ref, o_ref):
    c = c_ref[...]
    sc = c * _sigmoid(c)
    o_ref[...] = jnp.sum(sc * w_ref[...], axis=0, keepdims=True) + b_ref[...]


def _mod_call(c_col, w_mod, b_mod):
    d, n = w_mod.shape
    tn = 512
    return pl.pallas_call(
        _mod_body,
        out_shape=jax.ShapeDtypeStruct((1, n), F32),
        grid=(n // tn,),
        in_specs=[pl.BlockSpec((d, 1), lambda j: (0, 0)),
                  pl.BlockSpec((d, tn), lambda j: (0, j)),
                  pl.BlockSpec((1, tn), lambda j: (0, j))],
        out_specs=pl.BlockSpec((1, tn), lambda j: (0, j)),
        compiler_params=_params(("arbitrary",)),
        name="mod",
    )(c_col, w_mod, b_mod)


def _proj_body(x_ref, pos_ref, nw_ref, sc_ref, sh_ref, w_ref, cw_ref, alog_ref, dtb_ref, invf_ref,
               sgn_ref, qn_ref, kn_ref, v_ref, zdn_ref, bg_ref, zat_ref,
               aq1_ref, aq4_ref, aq16_ref, ak1_ref, ak4_ref, ak16_ref, av1_ref, av4_ref, av16_ref,
               cbuf_ref, hb_ref, pbuf_ref, p4buf_ref):
    i = pl.program_id(0)
    tm = x_ref.shape[0]
    x = x_ref[...]
    ms = jnp.mean(x * x, axis=-1, keepdims=True)
    hb_ref[...] = ((x * lax.rsqrt(ms + EPS)) * (nw_ref[...] * (1.0 + sc_ref[...])) + sh_ref[...]).astype(BF16)

    def proj(c0, width=PROJ_COLS):
        return jnp.dot(hb_ref[...], w_ref[:, c0:c0 + width], preferred_element_type=F32)

    def silu(v):
        return v * _sigmoid(v)

    @pl.when(i == 0)
    def _():
        cbuf_ref[:, 0:SUBLANES, :] = jnp.zeros((cbuf_ref.shape[0], SUBLANES, LANES), F32)

    dyn0 = jnp.minimum(i, 0)
    for c0 in range(0, 3 * DN_W, PROJ_COLS):
        xg = proj(C_QKV + c0)
        halves = []
        for hh in range(PROJ_COLS // LANES):
            grp = c0 // LANES + hh
            ls = slice(c0 + hh * LANES, c0 + (hh + 1) * LANES)
            xh = xg[:, hh * LANES:(hh + 1) * LANES]
            cbuf_ref[grp, SUBLANES:SUBLANES + tm, :] = xh
            acc = xh * cw_ref[CONV_K - 1:CONV_K, ls]
            for j in range(1, CONV_K):
                acc = acc + cbuf_ref[grp, pl.ds(dyn0 + SUBLANES - j, tm), :] * cw_ref[CONV_K - 1 - j:CONV_K - j, ls]
            cbuf_ref[grp, 0:SUBLANES, :] = xh[tm - SUBLANES:tm]
            halves.append(acc)
        conv = jnp.concatenate(halves, axis=1)
        y = silu(conv)
        kind, off = c0 // DN_W, c0 % DN_W
        for hh in range(PROJ_COLS // DN_HD):
            yh = y[:, hh * DN_HD:(hh + 1) * DN_HD]
            dst = slice(off + hh * DN_HD, off + (hh + 1) * DN_HD)
            if kind == 2:
                v_ref[:, dst] = yh.astype(BF16)
            else:
                inv = lax.rsqrt(jnp.sum(yh * yh, axis=-1, keepdims=True) + EPS)
                if kind == 0:
                    qn_ref[:, dst] = (yh * (inv * (DN_HD ** -0.5))).astype(BF16)
                else:
                    kn_ref[:, dst] = (yh * inv).astype(BF16)

    for c0 in range(0, DN_W, PROJ_COLS):
        zdn_ref[:, c0:c0 + PROJ_COLS] = silu(proj(C_Z + c0)).astype(BF16)
        zat_ref[:, c0:c0 + PROJ_COLS] = silu(proj(C_AZ + c0)).astype(BF16)

    ba = proj(C_BA, LANES)
    lane = _iota(ba.shape, 1)
    a = ba + dtb_ref[...]
    softplus = jnp.maximum(a, 0.0) + jnp.log(1.0 + jnp.exp(-jnp.abs(a)))
    g = -jnp.exp(alog_ref[...]) * softplus
    bg_ref[...] = jnp.where(lane < DN_HEADS, _sigmoid(ba), jnp.where(lane < 2 * DN_HEADS, g, 0.0))

    ang = invf_ref[...] * pos_ref[...].astype(F32)
    reps = LANES // (AT_HD // 2)
    cos = jnp.concatenate([jnp.cos(ang)] * reps, axis=0).T
    sin = jnp.concatenate([jnp.sin(ang)] * reps, axis=0).T * sgn_ref[...]
    qscale = (AT_HD ** -0.5) * math.log2(math.e)
    d4, d16 = DILATIONS[1], DILATIONS[2]
    n4, n16 = tm // d4, tm // d16

    def emit(val, gp, nat_ref, p4_ref, p16_ref):
        sl = slice(gp * LANES, (gp + 1) * LANES)
        nat_ref[:, sl] = val.astype(BF16)
        pbuf_ref[gp] = val
        for c in range(d4):
            seg = pbuf_ref[gp, pl.ds(c, n4, stride=d4), :]
            p4_ref[0, c, :, sl] = seg.astype(BF16)
            p4buf_ref[gp, c * n4:(c + 1) * n4, :] = seg
        for r in range(d16):
            seg = p4buf_ref[gp, pl.ds((r % d4) * n4 + r // d4, n16, stride=d4), :]
            p16_ref[0, r, :, sl] = seg.astype(BF16)

    def rope(v):
        return v * cos + pltpu.roll(v, LANES // 2, axis=1) * sin

    for c0 in range(0, AT_W, PROJ_COLS):
        aq = proj(C_AQ + c0)
        ak = proj(C_AK + c0)
        av = proj(C_AV + c0)
        for hh in range(PROJ_COLS // LANES):
            gp = c0 // LANES + hh
            ls = slice(hh * LANES, (hh + 1) * LANES)
            emit(rope(aq[:, ls]) * qscale, gp, aq1_ref, aq4_ref, aq16_ref)
            emit(rope(ak[:, ls]), gp, ak1_ref, ak4_ref, ak16_ref)
            emit(av[:, ls], gp, av1_ref, av4_ref, av16_ref)


def _proj_call(x2, pos_row, nw, scale, shift, w_all, cw, alog_row, dtb_row, invf, sgn, tm):
    s, d = x2.shape
    row = lambda n: pl.BlockSpec((1, n), lambda i: (0, 0))
    full = lambda a: pl.BlockSpec(a.shape, lambda i: (0, 0))
    tile = lambda n: pl.BlockSpec((tm, n), lambda i: (i, 0))
    spb = SUPER // tm
    res_shape = lambda dil: jax.ShapeDtypeStruct((s // SUPER, dil, SUPER // dil, AT_W), BF16)
    res_spec = lambda dil: pl.BlockSpec((1, dil, tm // dil, AT_W), lambda i: (i // spb, 0, i % spb, 0))
    bf = lambda n: jax.ShapeDtypeStruct((s, n), BF16)
    outs = [bf(DN_W)] * 4 + [jax.ShapeDtypeStruct((s, LANES), F32), bf(AT_W)] + \
           [bf(AT_W), res_shape(DILATIONS[1]), res_shape(DILATIONS[2])] * 3
    out_specs = [tile(DN_W)] * 4 + [tile(LANES), tile(AT_W)] + \
                [tile(AT_W), res_spec(DILATIONS[1]), res_spec(DILATIONS[2])] * 3
    return pl.pallas_call(
        _proj_body,
        out_shape=outs,
        grid=(s // tm,),
        in_specs=[tile(d), pl.BlockSpec((1, tm), lambda i: (0, i)), row(d), row(d), row(d),
                  pl.BlockSpec(w_all.shape, lambda i: (0, 0), pipeline_mode=pl.Buffered(1)),
                  full(cw), row(LANES), row(LANES), full(invf), row(LANES)],
        out_specs=out_specs,
        scratch_shapes=[pltpu.VMEM((3 * DN_W // LANES, SUBLANES + tm, LANES), F32), pltpu.VMEM((tm, d), BF16),
                        pltpu.VMEM((AT_W // LANES, tm, LANES), F32),
                        pltpu.VMEM((AT_W // LANES, tm, LANES), F32)],
        compiler_params=_params(("arbitrary",)),
        name="proj",
    )(x2, pos_row, nw, scale, shift, w_all, cw, alog_row, dtb_row, invf, sgn)


def _dn_body(q_ref, k_ref, v_ref, bg_ref, z_ref, nw_ref, o_ref, w_s, bt_s, op_s, gl_s, st_ref):
    step = pl.program_id(0)
    cur = step % 2
    prv = 1 - cur

    @pl.when(step == 0)
    def _():
        st_ref[...] = jnp.zeros(st_ref.shape, F32)
        w_s[1] = jnp.zeros(w_s.shape[1:], BF16)
        bt_s[1] = jnp.zeros(bt_s.shape[1:], BF16)
        op_s[1] = jnp.zeros(op_s.shape[1:], BF16)
        gl_s[1] = jnp.zeros(gl_s.shape[1:], F32)

    rows = q_ref.shape[0]
    nchunk = rows // CHUNK
    hsl = lambda hd: slice(hd * DN_HD, (hd + 1) * DN_HD)

    def scan_chunk(c):
        rs = slice(c * CHUNK, (c + 1) * CHUNK)
        for hd in range(DN_HEADS):
            st = st_ref[hd]
            r = jnp.dot(st.astype(BF16), w_s[prv, c, hd], preferred_element_type=F32)
            st_ref[hd] = gl_s[prv, c, hd:hd + 1, :] * st - r[:, 0:DN_HD] + bt_s[prv, c, hd].astype(F32)
            half = (hd % 2) * CHUNK
            o = r[:, DN_HD:2 * DN_HD].T[half:half + CHUNK] + op_s[prv, rs, hsl(hd)].astype(F32)
            ms = jnp.mean(o * o, axis=-1, keepdims=True)
            o_ref[rs, hsl(hd)] = (o * lax.rsqrt(ms + EPS) * nw_ref[...] * z_ref[rs, hsl(hd)].astype(F32)).astype(BF16)

    pending = list(range(nchunk))

    def tick():
        if pending:
            scan_chunk(pending.pop(0))

    wide = DN_HEADS * CHUNK
    bg = bg_ref[...]
    cs_rows = min(rows, MXU_COLS)
    ri = _iota((cs_rows, cs_rows), 0)
    ci = _iota((cs_rows, cs_rows), 1)
    tril_blk = jnp.where((ri // CHUNK == ci // CHUNK) & (ri >= ci), 1.0, 0.0).astype(BF16)
    g_hi = bg.astype(BF16)
    g_r1 = bg - g_hi.astype(F32)
    g_mid = g_r1.astype(BF16)
    g_lo = (g_r1 - g_mid.astype(F32)).astype(BF16)
    g3 = jnp.concatenate([g_hi, g_mid, g_lo], axis=1)
    csum = jnp.concatenate([jnp.dot(tril_blk, g3[r0:r0 + cs_rows], preferred_element_type=F32)
                            for r0 in range(0, rows, cs_rows)], axis=0)
    gcum = csum[:, 0:LANES] + csum[:, LANES:2 * LANES] + csum[:, 2 * LANES:3 * LANES]
    bfull = jnp.concatenate([jnp.broadcast_to(bg[:, hd:hd + 1], (rows, DN_HD)) for hd in range(DN_HEADS)], axis=1)
    gfull = jnp.concatenate([jnp.broadcast_to(gcum[:, DN_HEADS + hd:DN_HEADS + hd + 1], (rows, DN_HD))
                             for hd in range(DN_HEADS)], axis=1)
    eg = jnp.exp(gfull)
    q = q_ref[...].astype(F32)
    k = k_ref[...].astype(F32)
    kb = k * bfull
    vb = v_ref[...].astype(F32) * bfull
    q_dec = q * eg
    kbg = kb * eg

    r64 = _iota((CHUNK, wide), 0)
    j64 = _iota((CHUNK, wide), 1) % CHUNK
    tril = r64 >= j64
    strict = r64 > j64
    eye = jnp.where(r64 == j64, 1.0, 0.0).astype(F32)
    low_half = _iota((CHUNK, LANES), 1) < CHUNK
    bd64 = _iota((wide, wide), 0) // CHUNK == _iota((wide, wide), 1) // CHUNK
    bd_k = _iota((wide, DN_W), 0) // CHUNK == _iota((wide, DN_W), 1) // DN_HD

    def block_diag(m):
        return jnp.where(bd64, jnp.concatenate([m.astype(BF16)] * DN_HEADS, axis=0), jnp.zeros((wide, wide), BF16))

    rsl = lambda c: slice(c * CHUNK, (c + 1) * CHUNK)
    chunks = range(nchunk)
    g_last = [gfull[c * CHUNK + CHUNK - 1:(c + 1) * CHUNK, :] for c in chunks]
    k_dec = [k[rsl(c)] * jnp.exp(g_last[c] - gfull[rsl(c)]) for c in chunks]
    g_row = [jnp.concatenate([gcum[rsl(c)], gcum[rsl(c)]], axis=0).T for c in chunks]
    dec = []
    for c in chunks:
        halves = []
        for pr in range(DN_HEADS // 2):
            ha, hb = 2 * pr, 2 * pr + 1
            col = jnp.where(low_half, gfull[rsl(c), hsl(ha)], gfull[rsl(c), hsl(hb)])
            row = jnp.where(low_half[0:1], g_row[c][DN_HEADS + ha:DN_HEADS + ha + 1, :],
                            g_row[c][DN_HEADS + hb:DN_HEADS + hb + 1, :])
            halves.append(col - row)
        dec.append(jnp.exp(jnp.where(tril, jnp.concatenate(halves, axis=1), NEG)))
    kq = [lax.dot_general(jnp.concatenate([kb[rsl(c)].astype(BF16), q_ref[rsl(c), :]], axis=0),
                          jnp.where(bd_k, jnp.concatenate([k_ref[rsl(c), :]] * DN_HEADS, axis=0),
                                    jnp.zeros((wide, DN_W), BF16)),
                          NT_DIMS, preferred_element_type=F32) for c in chunks]
    tick()
    a_low = [jnp.where(strict, m[0:CHUNK] * d, 0.0) for m, d in zip(kq, dec)]
    attn = [jnp.where(tril, m[CHUNK:2 * CHUNK] * d, 0.0) for m, d in zip(kq, dec)]
    bmat = [-a for a in a_low]
    ymat = [eye + b for b in bmat]
    bmat = [jnp.dot(b.astype(BF16), block_diag(b), preferred_element_type=F32) for b in bmat]
    tick()
    for _ in range(4):
        prod = [jnp.dot(jnp.concatenate([y, b], axis=0).astype(BF16), block_diag(b), preferred_element_type=F32)
                for y, b in zip(ymat, bmat)]
        tick()
        ymat = [y + p[0:CHUNK] for y, p in zip(ymat, prod)]
        bmat = [p[CHUNK:2 * CHUNK] for p in prod]
    ymat = [y + jnp.dot(y.astype(BF16), block_diag(b), preferred_element_type=F32) for y, b in zip(ymat, bmat)]
    tick()
    rhs = [jnp.concatenate([jnp.concatenate([kbg[rsl(c), hsl(hd)], vb[rsl(c), hsl(hd)]], axis=1)
                            for hd in range(DN_HEADS)], axis=0).astype(BF16) for c in chunks]
    wu = [jnp.dot(block_diag(y), r, preferred_element_type=F32) for y, r in zip(ymat, rhs)]
    tick()
    wut = [m.T.astype(BF16) for m in wu]
    kdbd = [jnp.where(bd_k, jnp.concatenate([k_dec[c].astype(BF16)] * DN_HEADS, axis=0),
                      jnp.zeros((wide, DN_W), BF16)) for c in chunks]
    pbt = [jnp.dot(a, b, preferred_element_type=F32) for a, b in zip(wut, kdbd)]
    tick()
    aw = [jnp.dot(block_diag(a), m.astype(BF16), preferred_element_type=F32) for a, m in zip(attn, wu)]
    qpt = [(jnp.concatenate([q_dec[rsl(c), hsl(hd)] for hd in range(DN_HEADS)], axis=0) - a[:, 0:DN_HD]).T
           .astype(BF16) for c, a in zip(chunks, aw)]
    while pending:
        tick()
    for c in chunks:
        for hd in range(DN_HEADS):
            pair = slice((hd // 2) * LANES, (hd // 2 + 1) * LANES)
            w_s[cur, c, hd] = jnp.concatenate([pbt[c][0:DN_HD, hsl(hd)].astype(BF16), qpt[c][:, pair]], axis=1)
            bt_s[cur, c, hd] = pbt[c][DN_HD:2 * DN_HD, hsl(hd)].astype(BF16)
            op_s[cur, rsl(c), hsl(hd)] = aw[c][hd * CHUNK:(hd + 1) * CHUNK, DN_HD:2 * DN_HD].astype(BF16)
        gl_rows = [jnp.exp(g_last[c][:, hsl(hd)]) for hd in range(DN_HEADS)]
        gl_rows.append(jnp.zeros((SUBLANES - DN_HEADS, DN_HD), F32))
        gl_s[cur, c] = jnp.concatenate(gl_rows, axis=0)


def _dn_call(qn, kn, v, bg, zdn, nw_row, rows):
    s = qn.shape[0]
    nblk = s // rows
    ncb = rows // CHUNK
    cur = lambda n: pl.BlockSpec((rows, n), lambda i: (jnp.minimum(i, nblk - 1), 0))
    prv = lambda n: pl.BlockSpec((rows, n), lambda i: (jnp.maximum(i - 1, 0), 0))
    return pl.pallas_call(
        _dn_body,
        out_shape=jax.ShapeDtypeStruct((s, DN_W), BF16),
        grid=(nblk + 1,),
        in_specs=[cur(DN_W), cur(DN_W), cur(DN_W), cur(LANES), prv(DN_W), pl.BlockSpec((1, DN_HD), lambda i: (0, 0))],
        out_specs=prv(DN_W),
        scratch_shapes=[pltpu.VMEM((2, ncb, DN_HEADS, DN_HD, 2 * DN_HD), BF16),
                        pltpu.VMEM((2, ncb, DN_HEADS, DN_HD, DN_HD), BF16),
                        pltpu.VMEM((2, rows, DN_W), BF16),
                        pltpu.VMEM((2, ncb, SUBLANES, DN_HD), F32),
                        pltpu.VMEM((DN_HEADS, DN_HD, DN_HD), F32)],
        compiler_params=_params(("arbitrary",)),
        name="deltanet",
    )(qn, kn, v, bg, zdn, nw_row)


def _attn_body(q1, q4, q16, k1, k4, k16, v1, v4, v16, kp1, vp1, kp4a, kp4b, kp4c, kp4d,
               vp4a, vp4b, vp4c, vp4d, kp16, vp16, z_ref, anw_ref, o_ref, m_s, l_s, a_s, m_n, l_n, a_n):
    d4, d16 = DILATIONS[1], DILATIONS[2]
    seg4 = SUPER // d4
    kp4 = (kp4a, kp4b, kp4c, kp4d)
    vp4 = (vp4a, vp4b, vp4c, vp4d)
    kj = _iota((2 * Q_BLOCK, Q_BLOCK), 0)
    qi = _iota((2 * Q_BLOCK, Q_BLOCK), 1)
    rel = Q_BLOCK + qi - kj
    band = (rel >= 0) & (rel <= W_SUB)
    bias_in = jnp.where(band, 0.0, NEG).astype(BF16)
    bias_first = jnp.where(band & (kj >= Q_BLOCK), 0.0, NEG).astype(BF16)
    bias_edge = jnp.where(pl.program_id(0) == 0, bias_first, bias_in)
    one_hot = jnp.where(kj % Q_BLOCK == qi, 1.0, 0.0).astype(BF16)
    lane = _iota((Q_BLOCK, LANES), 1)
    lane_k = _iota((2 * Q_BLOCK, LANES), 1)
    head_a_q = (lane % AT_HD) < (AT_HD // 2)
    head_a = lane < AT_HD
    head_a_k = lane_k < AT_HD
    ro = _iota((4 * Q_BLOCK, LANES), 0)
    ones_rhs = jnp.where((ro < 2 * Q_BLOCK) == (_iota((4 * Q_BLOCK, LANES), 1) < AT_HD), 1.0, 0.0).astype(BF16)
    sr = _iota((2 * LANES, LANES), 0)
    ssq_rhs = jnp.where((sr % LANES) // AT_HD == _iota((2 * LANES, LANES), 1) // AT_HD, 1.0, 0.0).astype(BF16)
    zq = jnp.zeros((Q_BLOCK, LANES), BF16)
    zv = jnp.zeros((2 * Q_BLOCK, LANES), BF16)

    def scores(load):
        q, kb, vb, bias = load()
        q2 = jnp.concatenate([jnp.where(head_a_q, q, zq), jnp.where(head_a_q, zq, q)], axis=0)
        s = lax.dot_general(jnp.concatenate([q2, one_hot], axis=1), jnp.concatenate([kb, bias], axis=1),
                            NT_DIMS, preferred_element_type=F32)
        m = jnp.max(s, axis=-1, keepdims=True)
        return jnp.exp2(s - m).astype(BF16), m, vb

    def values(p, m, vb):
        pc = jnp.concatenate([p[0:Q_BLOCK], p[Q_BLOCK:2 * Q_BLOCK]], axis=1)
        rhs = jnp.concatenate([jnp.concatenate([jnp.where(head_a_k, vb, zv), jnp.where(head_a_k, zv, vb)], axis=0),
                               ones_rhs], axis=1)
        r = jnp.dot(pc, rhs, preferred_element_type=F32)
        return r[:, 0:LANES], r[:, LANES:2 * LANES], jnp.where(head_a, m[0:Q_BLOCK], m[Q_BLOCK:2 * Q_BLOCK])

    def run(loads, finish, post=None):
        n = len(loads)
        pending, done = {}, {}
        for i in range(n + ATT_LAG + 1):
            if i < n:
                pending[i] = scores(loads[i])
            j = i - ATT_LAG
            if 0 <= j < n:
                done[j] = finish(j, *values(*pending.pop(j)))
            if post is not None and 0 <= j - 1 < n:
                post(j - 1, done.pop(j - 1))

    def merged(acc, l, m, m_old, l_old, a_old):
        m_new = jnp.maximum(m_old, m)
        w_old = jnp.exp2(m_old - m_new)
        w_new = jnp.exp2(m - m_new)
        return a_old * w_old + acc * w_new, l_old * w_old + l * w_new, m_new

    def load16(r):
        rs = slice(r * Q_BLOCK, (r + 1) * Q_BLOCK)
        return lambda: (q16[rs, :], jnp.concatenate([kp16[rs, :], k16[rs, :]], axis=0),
                        jnp.concatenate([vp16[rs, :], v16[rs, :]], axis=0), bias_edge)

    def finish16(r, acc, l, m):
        rows = pl.ds((r % d4) * seg4 + r // d4, Q_BLOCK, stride=d4)
        a_s[rows, :] = acc
        l_s[rows, :] = l
        m_s[rows, :] = m

    run([load16(r) for r in range(d16)], finish16)

    def load4(r, b):
        base = r * seg4 + b * Q_BLOCK
        if b == 0:
            return lambda: (q4[base:base + Q_BLOCK, :],
                            jnp.concatenate([kp4[r][...], k4[base:base + Q_BLOCK, :]], axis=0),
                            jnp.concatenate([vp4[r][...], v4[base:base + Q_BLOCK, :]], axis=0), bias_edge)
        return lambda: (q4[base:base + Q_BLOCK, :], k4[base - Q_BLOCK:base + Q_BLOCK, :],
                        v4[base - Q_BLOCK:base + Q_BLOCK, :], bias_in)

    blocks4 = [(r, b) for b in range(seg4 // Q_BLOCK) for r in range(d4)]

    def finish4(j, acc, l, m):
        r, b = blocks4[j]
        rows = slice(r * seg4 + b * Q_BLOCK, r * seg4 + (b + 1) * Q_BLOCK)
        acc, l, m = merged(acc, l, m, m_s[rows, :], l_s[rows, :], a_s[rows, :])
        a_s[rows, :] = acc
        l_s[rows, :] = l
        m_s[rows, :] = m

    run([load4(r, b) for r, b in blocks4], finish4)

    for c in range(d4):
        src = slice(c * seg4, (c + 1) * seg4)
        dst = pl.ds(c, seg4, stride=d4)
        m_n[dst, :] = m_s[src, :]
        l_n[dst, :] = l_s[src, :]
        a_n[dst, :] = a_s[src, :]

    def load1(b):
        rs = slice(b * Q_BLOCK, (b + 1) * Q_BLOCK)
        if b == 0:
            return lambda: (q1[rs, :], jnp.concatenate([kp1[...], k1[rs, :]], axis=0),
                            jnp.concatenate([vp1[...], v1[rs, :]], axis=0), bias_edge)
        ks = slice((b - 1) * Q_BLOCK, (b + 1) * Q_BLOCK)
        return lambda: (q1[rs, :], k1[ks, :], v1[ks, :], bias_in)

    def finish1(b, acc, l, m):
        rs = slice(b * Q_BLOCK, (b + 1) * Q_BLOCK)
        acc, l, _ = merged(acc, l, m, m_n[rs, :], l_n[rs, :], a_n[rs, :])
        o = acc / l
        sq = o * o
        hi = sq.astype(BF16)
        return o, jnp.concatenate([hi, (sq - hi.astype(F32)).astype(BF16)], axis=1)

    def post1(b, res):
        o, sq2 = res
        rs = slice(b * Q_BLOCK, (b + 1) * Q_BLOCK)
        ssq = jnp.dot(sq2, ssq_rhs, preferred_element_type=F32)
        o_ref[rs, :] = (o * lax.rsqrt(ssq * (1.0 / AT_HD) + EPS) * anw_ref[...]
                        * z_ref[rs, :].astype(F32)).astype(BF16)

    run([load1(b) for b in range(SUPER // Q_BLOCK)], finish1, post1)


def _attn_call(aq, ak, av, zat, anw_row):
    s = zat.shape[0]
    d4 = DILATIONS[1]
    per_super = SUPER // Q_BLOCK
    cur = pl.BlockSpec((SUPER, LANES), lambda m, g: (m, g))
    prev16 = pl.BlockSpec((SUPER, LANES), lambda m, g: (jnp.maximum(m - 1, 0), g))
    prev1 = pl.BlockSpec((Q_BLOCK, LANES), lambda m, g: (jnp.maximum(m * per_super - 1, 0), g))
    prev4 = [pl.BlockSpec((Q_BLOCK, LANES),
                          lambda m, g, r=r: (jnp.maximum((m - 1) * per_super + (r + 1) * (per_super // d4) - 1, 0), g))
             for r in range(d4)]
    state = pltpu.VMEM((SUPER, LANES), F32)
    return pl.pallas_call(
        _attn_body,
        out_shape=jax.ShapeDtypeStruct((s, AT_W), BF16),
        grid=(s // SUPER, AT_W // LANES),
        in_specs=[cur] * 9 + [prev1, prev1] + prev4 + prev4 + [prev16, prev16, cur,
                                                               pl.BlockSpec((1, LANES), lambda m, g: (0, g))],
        out_specs=cur,
        scratch_shapes=[state] * 6,
        compiler_params=_params(("arbitrary", "arbitrary")),
        name="attn",
    )(*aq, *ak, *av, ak[0], av[0], *([ak[1]] * d4), *([av[1]] * d4), ak[2], av[2], zat, anw_row)


def _out_body(x_ref, odn_ref, oat_ref, gate_ref, w_ref, fnw_ref, o_ref):
    mixin = jnp.concatenate([odn_ref[...], oat_ref[...]], axis=1)
    mix = jnp.dot(mixin, w_ref[...], preferred_element_type=F32)
    y = x_ref[...] + gate_ref[...] * mix
    ms = jnp.mean(y * y, axis=-1, keepdims=True)
    o_ref[...] = y * lax.rsqrt(ms + EPS) * fnw_ref[...]


def _out_call(x2, odn, oat, gate, w_out, fnw_row, tm):
    s, d = x2.shape
    tile = lambda n: pl.BlockSpec((tm, n), lambda i: (i, 0))
    row = lambda n: pl.BlockSpec((1, n), lambda i: (0, 0))
    return pl.pallas_call(
        _out_body,
        out_shape=jax.ShapeDtypeStruct((s, d), F32),
        grid=(s // tm,),
        in_specs=[tile(d), tile(DN_W), tile(AT_W), row(d), pl.BlockSpec(w_out.shape, lambda i: (0, 0)), row(d)],
        out_specs=tile(d),
        compiler_params=_params(("arbitrary",)),
        name="out",
    )(x2, odn, oat, gate, w_out, fnw_row)


def _rope_pair_perm():
    half = AT_HD // 2
    idx = []
    for gp in range(AT_W // LANES):
        for l in range(LANES):
            part, hh, j = l // AT_HD, (l % AT_HD) // half, l % half
            idx.append((2 * gp + hh) * AT_HD + part * half + j)
    return np.asarray(idx, np.int32)


def kernel(x, c, positions, w_mod, b_mod, norm_w, w_in, conv_w, a_log, dt_bias, dn_norm_w, at_norm_w,
           w_out, final_norm_w):
    b, s, d = x.shape
    assert b == 1 and d == D_MODEL and w_mod.shape[0] == 1
    x2 = x.reshape(s, d)

    mod = _mod_call(c.reshape(d, 1), w_mod[0], b_mod[0].reshape(1, 3 * d))
    shift, scale, gate = mod[:, 0:d], mod[:, d:2 * d], mod[:, 2 * d:3 * d]

    wi = w_in[0]
    o_z, o_b, o_a, o_aq = 3 * DN_W, 4 * DN_W, 4 * DN_W + DN_HEADS, 4 * DN_W + 2 * DN_HEADS
    perm = _rope_pair_perm()
    w_aq = wi[:, o_aq:o_aq + AT_W][:, perm]
    w_ak = wi[:, o_aq + AT_W:o_aq + 2 * AT_W][:, perm]
    w_ba = jnp.pad(wi[:, o_b:o_b + 2 * DN_HEADS], ((0, 0), (0, LANES - 2 * DN_HEADS)))
    w_all = jnp.concatenate([wi[:, 0:o_z], wi[:, o_z:o_b], w_aq, w_ak,
                             wi[:, o_aq + 2 * AT_W:o_aq + 4 * AT_W], w_ba], axis=1).astype(BF16)
    assert w_all.shape[1] == W_COLS
    pad_heads = lambda v: jnp.pad(v.reshape(1, DN_HEADS), ((0, 0), (DN_HEADS, LANES - 2 * DN_HEADS)))
    half = AT_HD // 2
    invf = (ROPE_THETA ** (-jnp.arange(half, dtype=F32) / half)).reshape(half, 1)
    sgn = jnp.where(jnp.arange(LANES) < LANES // 2, -1.0, 1.0).astype(F32).reshape(1, LANES)

    assert s % SUPER == 0
    outs = _proj_call(x2, positions.reshape(1, s), norm_w[0].reshape(1, d), scale, shift, w_all, conv_w[0],
                      pad_heads(a_log[0]), pad_heads(dt_bias[0]), invf, sgn, tm=512)
    qn, kn, v, zdn, bg, zat = outs[0:6]
    flat = lambda a: a.reshape(s, AT_W)
    aq, ak, av = [[flat(a) for a in outs[j:j + 3]] for j in (6, 9, 12)]

    odn = _dn_call(qn, kn, v, bg, zdn, dn_norm_w[0].reshape(1, DN_HD), rows=512)
    oat = _attn_call(aq, ak, av, zat, jnp.tile(at_norm_w[0], AT_HEADS).reshape(1, AT_W))

    out = _out_call(x2, odn, oat, gate, w_out[0].astype(BF16), final_norm_w.reshape(1, d), tm=1024)
    return out.reshape(b, s, d)
```

```python
import math

import jax
import jax.numpy as jnp
from jax import lax
from jax.experimental import pallas as pl
from jax.experimental.pallas import tpu as pltpu

D_MODEL = 1024
DN_HEADS = 4
DN_HD = 128
DN_W = DN_HEADS * DN_HD
AT_HEADS = 8
AT_HD = 64
AT_W = AT_HEADS * AT_HD
CONV_K = 4
CHUNK = 64
DILATIONS = (1, 4, 16)
W_SUB = 128
Q_BLOCK = 128
SUPER = Q_BLOCK * DILATIONS[-1]
ATT_LAG = 2
ROPE_THETA = 10000.0
EPS = 1e-6
NEG = -1e30

LANES = 128
SUBLANES = 8
MXU_COLS = 256
PROJ_COLS = 512
VMEM_LIMIT = 48 * 1024 * 1024

C_QKV, C_Z, C_AQ, C_AK, C_AV, C_AZ, C_BA = 0, 1536, 2048, 2560, 3072, 3584, 4096
W_COLS = C_BA + LANES

F32 = jnp.float32
BF16 = jnp.bfloat16
NT_DIMS = (((1,), (1,)), ((), ()))


def _sigmoid(v):
    return 1.0 / (1.0 + jnp.exp2(v * (-math.log2(math.e))))


def _iota(shape, dim):
    return lax.broadcasted_iota(jnp.int32, shape, dim)


def _params(sem):
    return pltpu.CompilerParams(dimension_semantics=sem, vmem_limit_bytes=VMEM_LIMIT)


def _mod_body(c_ref, w_ref, b_ref, o_ref):
    c = c_ref[...]
    sc = c * _sigmoid(c)
    o_ref[...] = jnp.sum(sc * w_ref[...], axis=0, keepdims=True) + b_ref[...]


def _mod_call(c_col, w_mod, b_mod):
    d, n = w_mod.shape
    tn = 512
    return pl.pallas_call(
        _mod_body,
        out_shape=jax.ShapeDtypeStruct((1, n), F32),
        grid=(n // tn,),
        in_specs=[pl.BlockSpec((d, 1), lambda j: (0, 0)),
                  pl.BlockSpec((d, tn), lambda j: (0, j)),
                  pl.BlockSpec((1, tn), lambda j: (0, j))],
        out_specs=pl.BlockSpec((1, tn), lambda j: (0, j)),
        compiler_params=_params(("arbitrary",)),
        name="mod",
    )(c_col, w_mod, b_mod)


def _wpack_body(w_ref, wo_ref, o_ref, oo_ref):
    rows = o_ref.shape[0]
    o_b = 4 * DN_W
    o_q = o_b + 2 * DN_HEADS
    half = AT_HD // 2
    o_ref[:, C_QKV:C_AQ] = w_ref[0, :, 0:o_b].astype(BF16)
    lane = _iota((rows, LANES), 1)
    o_ref[:, C_BA:C_BA + LANES] = jnp.where(lane < 2 * DN_HEADS, w_ref[0, :, o_b:o_b + LANES], 0.0).astype(BF16)
    qk = w_ref[0, :, o_q:o_q + 2 * AT_W]
    lq = _iota(qk.shape, 1) % LANES
    qk = jnp.where((lq >= half) & (lq < 2 * half), pltpu.roll(qk, 2 * AT_W - half, axis=1),
                   jnp.where((lq >= 2 * half) & (lq < 3 * half), pltpu.roll(qk, half, axis=1), qk))
    o_ref[:, C_AQ:C_AV] = qk.astype(BF16)
    o_ref[:, C_AV:C_BA] = w_ref[0, :, o_q + 2 * AT_W:o_q + 4 * AT_W].astype(BF16)
    oo_ref[...] = wo_ref[0].astype(BF16)


def _wpack_call(w_in, w_out):
    _, d, n_in = w_in.shape
    n_out = w_out.shape[2]
    rows = 256
    return pl.pallas_call(
        _wpack_body,
        out_shape=[jax.ShapeDtypeStruct((d, W_COLS), BF16), jax.ShapeDtypeStruct((w_out.shape[1], n_out), BF16)],
        grid=(d // rows,),
        in_specs=[pl.BlockSpec((1, rows, n_in), lambda i: (0, i, 0)),
                  pl.BlockSpec((1, rows, n_out), lambda i: (0, i, 0))],
        out_specs=[pl.BlockSpec((rows, W_COLS), lambda i: (i, 0)), pl.BlockSpec((rows, n_out), lambda i: (i, 0))],
        compiler_params=_params(("arbitrary",)),
        name="wpack",
    )(w_in, w_out)


def _proj_body(x_ref, pos_ref, nw_ref, sc_ref, sh_ref, w_ref, cw_ref, alog_ref, dtb_ref, invf_ref,
               sgn_ref, qn_ref, kn_ref, v_ref, zdn_ref, bg_ref, zat_ref,
               aq1_ref, aq4_ref, aq16_ref, ak1_ref, ak4_ref, ak16_ref, av1_ref, av4_ref, av16_ref,
               cbuf_ref, hb_ref, pbuf_ref, p4buf_ref):
    i = pl.program_id(0)
    tm = x_ref.shape[0]
    x = x_ref[...]
    ms = jnp.mean(x * x, axis=-1, keepdims=True)
    hb_ref[...] = ((x * lax.rsqrt(ms + EPS)) * (nw_ref[...] * (1.0 + sc_ref[...])) + sh_ref[...]).astype(BF16)

    def proj(c0, width=PROJ_COLS):
        return jnp.dot(hb_ref[...], w_ref[:, c0:c0 + width], preferred_element_type=F32)

    def silu(v):
        return v * _sigmoid(v)

    @pl.when(i == 0)
    def _():
        cbuf_ref[:, 0:SUBLANES, :] = jnp.zeros((cbuf_ref.shape[0], SUBLANES, LANES), F32)

    dyn0 = jnp.minimum(i, 0)
    for c0 in range(0, 3 * DN_W, PROJ_COLS):
        xg = proj(C_QKV + c0)
        halves = []
        for hh in range(PROJ_COLS // LANES):
            grp = c0 // LANES + hh
            ls = slice(c0 + hh * LANES, c0 + (hh + 1) * LANES)
            xh = xg[:, hh * LANES:(hh + 1) * LANES]
            cbuf_ref[grp, SUBLANES:SUBLANES + tm, :] = xh
            acc = xh * cw_ref[CONV_K - 1:CONV_K, ls]
            for j in range(1, CONV_K):
                acc = acc + cbuf_ref[grp, pl.ds(dyn0 + SUBLANES - j, tm), :] * cw_ref[CONV_K - 1 - j:CONV_K - j, ls]
            cbuf_ref[grp, 0:SUBLANES, :] = xh[tm - SUBLANES:tm]
            halves.append(acc)
        conv = jnp.concatenate(halves, axis=1)
        y = silu(conv)
        kind, off = c0 // DN_W, c0 % DN_W
        for hh in range(PROJ_COLS // DN_HD):
            yh = y[:, hh * DN_HD:(hh + 1) * DN_HD]
            dst = slice(off + hh * DN_HD, off + (hh + 1) * DN_HD)
            if kind == 2:
                v_ref[:, dst] = yh.astype(BF16)
            else:
                inv = lax.rsqrt(jnp.sum(yh * yh, axis=-1, keepdims=True) + EPS)
                if kind == 0:
                    qn_ref[:, dst] = (yh * (inv * (DN_HD ** -0.5))).astype(BF16)
                else:
                    kn_ref[:, dst] = (yh * inv).astype(BF16)

    for c0 in range(0, DN_W, PROJ_COLS):
        zdn_ref[:, c0:c0 + PROJ_COLS] = silu(proj(C_Z + c0)).astype(BF16)
        zat_ref[:, c0:c0 + PROJ_COLS] = silu(proj(C_AZ + c0)).astype(BF16)

    ba = proj(C_BA, LANES)
    lane = _iota(ba.shape, 1)
    a = ba + dtb_ref[...]
    softplus = jnp.maximum(a, 0.0) + jnp.log(1.0 + jnp.exp(-jnp.abs(a)))
    g = -jnp.exp(alog_ref[...]) * softplus
    bg_ref[...] = jnp.where(lane < DN_HEADS, _sigmoid(ba), jnp.where(lane < 2 * DN_HEADS, g, 0.0))

    ang = invf_ref[...] * pos_ref[...].astype(F32)
    reps = LANES // (AT_HD // 2)
    cos = jnp.concatenate([jnp.cos(ang)] * reps, axis=0).T
    sin = jnp.concatenate([jnp.sin(ang)] * reps, axis=0).T * sgn_ref[...]
    qscale = (AT_HD ** -0.5) * math.log2(math.e)
    d4, d16 = DILATIONS[1], DILATIONS[2]
    n4, n16 = tm // d4, tm // d16

    def emit(val, gp, nat_ref, p4_ref, p16_ref):
        sl = slice(gp * LANES, (gp + 1) * LANES)
        nat_ref[:, sl] = val.astype(BF16)
        pbuf_ref[gp] = val
        for c in range(d4):
            seg = pbuf_ref[gp, pl.ds(c, n4, stride=d4), :]
            p4_ref[0, c, :, sl] = seg.astype(BF16)
            p4buf_ref[gp, c * n4:(c + 1) * n4, :] = seg
        for r in range(d16):
            seg = p4buf_ref[gp, pl.ds((r % d4) * n4 + r // d4, n16, stride=d4), :]
            p16_ref[0, r, :, sl] = seg.astype(BF16)

    def rope(v):
        return v * cos + pltpu.roll(v, LANES // 2, axis=1) * sin

    for c0 in range(0, AT_W, PROJ_COLS):
        aq = proj(C_AQ + c0)
        ak = proj(C_AK + c0)
        av = proj(C_AV + c0)
        for hh in range(PROJ_COLS // LANES):
            gp = c0 // LANES + hh
            ls = slice(hh * LANES, (hh + 1) * LANES)
            emit(rope(aq[:, ls]) * qscale, gp, aq1_ref, aq4_ref, aq16_ref)
            emit(rope(ak[:, ls]), gp, ak1_ref, ak4_ref, ak16_ref)
            emit(av[:, ls], gp, av1_ref, av4_ref, av16_ref)


def _proj_call(x2, pos_row, nw, scale, shift, w_all, cw, alog_row, dtb_row, invf, sgn, tm):
    s, d = x2.shape
    row = lambda n: pl.BlockSpec((1, n), lambda i: (0, 0))
    full = lambda a: pl.BlockSpec(a.shape, lambda i: (0, 0))
    tile = lambda n: pl.BlockSpec((tm, n), lambda i: (i, 0))
    spb = SUPER // tm
    res_shape = lambda dil: jax.ShapeDtypeStruct((s // SUPER, dil, SUPER // dil, AT_W), BF16)
    res_spec = lambda dil: pl.BlockSpec((1, dil, tm // dil, AT_W), lambda i: (i // spb, 0, i % spb, 0))
    bf = lambda n: jax.ShapeDtypeStruct((s, n), BF16)
    outs = [bf(DN_W)] * 4 + [jax.ShapeDtypeStruct((s, LANES), F32), bf(AT_W)] + \
           [bf(AT_W), res_shape(DILATIONS[1]), res_shape(DILATIONS[2])] * 3
    out_specs = [tile(DN_W)] * 4 + [tile(LANES), tile(AT_W)] + \
                [tile(AT_W), res_spec(DILATIONS[1]), res_spec(DILATIONS[2])] * 3
    return pl.pallas_call(
        _proj_body,
        out_shape=outs,
        grid=(s // tm,),
        in_specs=[tile(d), pl.BlockSpec((1, tm), lambda i: (0, i)), row(d), row(d), row(d),
                  pl.BlockSpec(w_all.shape, lambda i: (0, 0), pipeline_mode=pl.Buffered(1)),
                  full(cw), row(LANES), row(LANES), full(invf), row(LANES)],
        out_specs=out_specs,
        scratch_shapes=[pltpu.VMEM((3 * DN_W // LANES, SUBLANES + tm, LANES), F32), pltpu.VMEM((tm, d), BF16),
                        pltpu.VMEM((AT_W // LANES, tm, LANES), F32),
                        pltpu.VMEM((AT_W // LANES, tm, LANES), F32)],
        compiler_params=_params(("arbitrary",)),
        name="proj",
    )(x2, pos_row, nw, scale, shift, w_all, cw, alog_row, dtb_row, invf, sgn)


def _dn_body(q_ref, k_ref, v_ref, bg_ref, z_ref, nw_ref, o_ref, w_s, bt_s, op_s, gl_s, st_ref):
    step = pl.program_id(0)
    cur = step % 2
    prv = 1 - cur

    @pl.when(step == 0)
    def _():
        st_ref[...] = jnp.zeros(st_ref.shape, F32)
        w_s[1] = jnp.zeros(w_s.shape[1:], BF16)
        bt_s[1] = jnp.zeros(bt_s.shape[1:], BF16)
        op_s[1] = jnp.zeros(op_s.shape[1:], BF16)
        gl_s[1] = jnp.zeros(gl_s.shape[1:], F32)

    rows = q_ref.shape[0]
    nchunk = rows // CHUNK
    hsl = lambda hd: slice(hd * DN_HD, (hd + 1) * DN_HD)

    def scan_chunk(c):
        rs = slice(c * CHUNK, (c + 1) * CHUNK)
        for hd in range(DN_HEADS):
            st = st_ref[hd]
            r = jnp.dot(st.astype(BF16), w_s[prv, c, hd], preferred_element_type=F32)
            st_ref[hd] = gl_s[prv, c, hd:hd + 1, :] * st - r[:, 0:DN_HD] + bt_s[prv, c, hd].astype(F32)
            half = (hd % 2) * CHUNK
            o = r[:, DN_HD:2 * DN_HD].T[half:half + CHUNK] + op_s[prv, rs, hsl(hd)].astype(F32)
            ms = jnp.mean(o * o, axis=-1, keepdims=True)
            o_ref[rs, hsl(hd)] = (o * lax.rsqrt(ms + EPS) * nw_ref[...] * z_ref[rs, hsl(hd)].astype(F32)).astype(BF16)

    pending = list(range(nchunk))

    def tick():
        if pending:
            scan_chunk(pending.pop(0))

    wide = DN_HEADS * CHUNK
    bg = bg_ref[...]
    cs_rows = min(rows, MXU_COLS)
    ri = _iota((cs_rows, cs_rows), 0)
    ci = _iota((cs_rows, cs_rows), 1)
    tril_blk = jnp.where((ri // CHUNK == ci // CHUNK) & (ri >= ci), 1.0, 0.0).astype(BF16)
    g_hi = bg.astype(BF16)
    g_r1 = bg - g_hi.astype(F32)
    g_mid = g_r1.astype(BF16)
    g_lo = (g_r1 - g_mid.astype(F32)).astype(BF16)
    g3 = jnp.concatenate([g_hi, g_mid, g_lo], axis=1)
    csum = jnp.concatenate([jnp.dot(tril_blk, g3[r0:r0 + cs_rows], preferred_element_type=F32)
                            for r0 in range(0, rows, cs_rows)], axis=0)
    gcum = csum[:, 0:LANES] + csum[:, LANES:2 * LANES] + csum[:, 2 * LANES:3 * LANES]
    bfull = jnp.concatenate([jnp.broadcast_to(bg[:, hd:hd + 1], (rows, DN_HD)) for hd in range(DN_HEADS)], axis=1)
    gfull = jnp.concatenate([jnp.broadcast_to(gcum[:, DN_HEADS + hd:DN_HEADS + hd + 1], (rows, DN_HD))
                             for hd in range(DN_HEADS)], axis=1)
    eg = jnp.exp(gfull)
    q = q_ref[...].astype(F32)
    k = k_ref[...].astype(F32)
    kb = k * bfull
    vb = v_ref[...].astype(F32) * bfull
    q_dec = q * eg
    kbg = kb * eg

    r64 = _iota((CHUNK, wide), 0)
    j64 = _iota((CHUNK, wide), 1) % CHUNK
    tril = r64 >= j64
    strict = r64 > j64
    eye = jnp.where(r64 == j64, 1.0, 0.0).astype(F32)
    low_half = _iota((CHUNK, LANES), 1) < CHUNK
    bd64 = _iota((wide, wide), 0) // CHUNK == _iota((wide, wide), 1) // CHUNK
    bd_k = _iota((wide, DN_W), 0) // CHUNK == _iota((wide, DN_W), 1) // DN_HD

    def block_diag(m):
        return jnp.where(bd64, jnp.concatenate([m.astype(BF16)] * DN_HEADS, axis=0), jnp.zeros((wide, wide), BF16))

    rsl = lambda c: slice(c * CHUNK, (c + 1) * CHUNK)
    chunks = range(nchunk)
    g_last = [gfull[c * CHUNK + CHUNK - 1:(c + 1) * CHUNK, :] for c in chunks]
    k_dec = [k[rsl(c)] * jnp.exp(g_last[c] - gfull[rsl(c)]) for c in chunks]
    g_row = [jnp.concatenate([gcum[rsl(c)], gcum[rsl(c)]], axis=0).T for c in chunks]
    dec = []
    for c in chunks:
        halves = []
        for pr in range(DN_HEADS // 2):
            ha, hb = 2 * pr, 2 * pr + 1
            col = jnp.where(low_half, gfull[rsl(c), hsl(ha)], gfull[rsl(c), hsl(hb)])
            row = jnp.where(low_half[0:1], g_row[c][DN_HEADS + ha:DN_HEADS + ha + 1, :],
                            g_row[c][DN_HEADS + hb:DN_HEADS + hb + 1, :])
            halves.append(col - row)
        dec.append(jnp.exp(jnp.where(tril, jnp.concatenate(halves, axis=1), NEG)))
    kq = [lax.dot_general(jnp.concatenate([kb[rsl(c)].astype(BF16), q_ref[rsl(c), :]], axis=0),
                          jnp.where(bd_k, jnp.concatenate([k_ref[rsl(c), :]] * DN_HEADS, axis=0),
                                    jnp.zeros((wide, DN_W), BF16)),
                          NT_DIMS, preferred_element_type=F32) for c in chunks]
    tick()
    a_low = [jnp.where(strict, m[0:CHUNK] * d, 0.0) for m, d in zip(kq, dec)]
    attn = [jnp.where(tril, m[CHUNK:2 * CHUNK] * d, 0.0) for m, d in zip(kq, dec)]
    bmat = [-a for a in a_low]
    ymat = [eye + b for b in bmat]
    bmat = [jnp.dot(b.astype(BF16), block_diag(b), preferred_element_type=F32) for b in bmat]
    tick()
    for _ in range(4):
        prod = [jnp.dot(jnp.concatenate([y, b], axis=0).astype(BF16), block_diag(b), preferred_element_type=F32)
                for y, b in zip(ymat, bmat)]
        tick()
        ymat = [y + p[0:CHUNK] for y, p in zip(ymat, prod)]
        bmat = [p[CHUNK:2 * CHUNK] for p in prod]
    ymat = [y + jnp.dot(y.astype(BF16), block_diag(b), preferred_element_type=F32) for y, b in zip(ymat, bmat)]
    tick()
    rhs = [jnp.concatenate([jnp.concatenate([kbg[rsl(c), hsl(hd)], vb[rsl(c), hsl(hd)]], axis=1)
                            for hd in range(DN_HEADS)], axis=0).astype(BF16) for c in chunks]
    wu = [jnp.dot(block_diag(y), r, preferred_element_type=F32) for y, r in zip(ymat, rhs)]
    tick()
    wut = [m.T.astype(BF16) for m in wu]
    kdbd = [jnp.where(bd_k, jnp.concatenate([k_dec[c].astype(BF16)] * DN_HEADS, axis=0),
                      jnp.zeros((wide, DN_W), BF16)) for c in chunks]
    pbt = [jnp.dot(a, b, preferred_element_type=F32) for a, b in zip(wut, kdbd)]
    tick()
    aw = [jnp.dot(block_diag(a), m.astype(BF16), preferred_element_type=F32) for a, m in zip(attn, wu)]
    qpt = [(jnp.concatenate([q_dec[rsl(c), hsl(hd)] for hd in range(DN_HEADS)], axis=0) - a[:, 0:DN_HD]).T
           .astype(BF16) for c, a in zip(chunks, aw)]
    while pending:
        tick()
    for c in chunks:
        for hd in range(DN_HEADS):
            pair = slice((hd // 2) * LANES, (hd // 2 + 1) * LANES)
            w_s[cur, c, hd] = jnp.concatenate([pbt[c][0:DN_HD, hsl(hd)].astype(BF16), qpt[c][:, pair]], axis=1)
            bt_s[cur, c, hd] = pbt[c][DN_HD:2 * DN_HD, hsl(hd)].astype(BF16)
            op_s[cur, rsl(c), hsl(hd)] = aw[c][hd * CHUNK:(hd + 1) * CHUNK, DN_HD:2 * DN_HD].astype(BF16)
        gl_rows = [jnp.exp(g_last[c][:, hsl(hd)]) for hd in range(DN_HEADS)]
        gl_rows.append(jnp.zeros((SUBLANES - DN_HEADS, DN_HD), F32))
        gl_s[cur, c] = jnp.concatenate(gl_rows, axis=0)


def _dn_call(qn, kn, v, bg, zdn, nw_row, rows):
    s = qn.shape[0]
    nblk = s // rows
    ncb = rows // CHUNK
    cur = lambda n: pl.BlockSpec((rows, n), lambda i: (jnp.minimum(i, nblk - 1), 0))
    prv = lambda n: pl.BlockSpec((rows, n), lambda i: (jnp.maximum(i - 1, 0), 0))
    return pl.pallas_call(
        _dn_body,
        out_shape=jax.ShapeDtypeStruct((s, DN_W), BF16),
        grid=(nblk + 1,),
        in_specs=[cur(DN_W), cur(DN_W), cur(DN_W), cur(LANES), prv(DN_W), pl.BlockSpec((1, DN_HD), lambda i: (0, 0))],
        out_specs=prv(DN_W),
        scratch_shapes=[pltpu.VMEM((2, ncb, DN_HEADS, DN_HD, 2 * DN_HD), BF16),
                        pltpu.VMEM((2, ncb, DN_HEADS, DN_HD, DN_HD), BF16),
                        pltpu.VMEM((2, rows, DN_W), BF16),
                        pltpu.VMEM((2, ncb, SUBLANES, DN_HD), F32),
                        pltpu.VMEM((DN_HEADS, DN_HD, DN_HD), F32)],
        compiler_params=_params(("arbitrary",)),
        name="deltanet",
    )(qn, kn, v, bg, zdn, nw_row)


def _attn_body(q1, q4, q16, k1, k4, k16, v1, v4, v16, kp1, vp1, kp4a, kp4b, kp4c, kp4d,
               vp4a, vp4b, vp4c, vp4d, kp16, vp16, z_ref, anw_ref, o_ref, m_s, l_s, a_s, m_n, l_n, a_n):
    d4, d16 = DILATIONS[1], DILATIONS[2]
    seg4 = SUPER // d4
    kp4 = (kp4a, kp4b, kp4c, kp4d)
    vp4 = (vp4a, vp4b, vp4c, vp4d)
    kj = _iota((2 * Q_BLOCK, Q_BLOCK), 0)
    qi = _iota((2 * Q_BLOCK, Q_BLOCK), 1)
    rel = Q_BLOCK + qi - kj
    band = (rel >= 0) & (rel <= W_SUB)
    bias_in = jnp.where(band, 0.0, NEG).astype(BF16)
    bias_first = jnp.where(band & (kj >= Q_BLOCK), 0.0, NEG).astype(BF16)
    bias_edge = jnp.where(pl.program_id(0) == 0, bias_first, bias_in)
    one_hot = jnp.where(kj % Q_BLOCK == qi, 1.0, 0.0).astype(BF16)
    lane = _iota((Q_BLOCK, LANES), 1)
    lane_k = _iota((2 * Q_BLOCK, LANES), 1)
    head_a_q = (lane % AT_HD) < (AT_HD // 2)
    head_a = lane < AT_HD
    head_a_k = lane_k < AT_HD
    ro = _iota((4 * Q_BLOCK, LANES), 0)
    ones_rhs = jnp.where((ro < 2 * Q_BLOCK) == (_iota((4 * Q_BLOCK, LANES), 1) < AT_HD), 1.0, 0.0).astype(BF16)
    sr = _iota((2 * LANES, LANES), 0)
    ssq_rhs = jnp.where((sr % LANES) // AT_HD == _iota((2 * LANES, LANES), 1) // AT_HD, 1.0, 0.0).astype(BF16)
    zq = jnp.zeros((Q_BLOCK, LANES), BF16)
    zv = jnp.zeros((2 * Q_BLOCK, LANES), BF16)

    def scores(load):
        q, kb, vb, bias = load()
        q2 = jnp.concatenate([jnp.where(head_a_q, q, zq), jnp.where(head_a_q, zq, q)], axis=0)
        s = lax.dot_general(jnp.concatenate([q2, one_hot], axis=1), jnp.concatenate([kb, bias], axis=1),
                            NT_DIMS, preferred_element_type=F32)
        m = jnp.max(s, axis=-1, keepdims=True)
        return jnp.exp2(s - m).astype(BF16), m, vb

    def values(p, m, vb):
        pc = jnp.concatenate([p[0:Q_BLOCK], p[Q_BLOCK:2 * Q_BLOCK]], axis=1)
        rhs = jnp.concatenate([jnp.concatenate([jnp.where(head_a_k, vb, zv), jnp.where(head_a_k, zv, vb)], axis=0),
                               ones_rhs], axis=1)
        r = jnp.dot(pc, rhs, preferred_element_type=F32)
        return r[:, 0:LANES], r[:, LANES:2 * LANES], jnp.where(head_a, m[0:Q_BLOCK], m[Q_BLOCK:2 * Q_BLOCK])

    def run(loads, finish, post=None):
        n = len(loads)
        pending, done = {}, {}
        for i in range(n + ATT_LAG + 1):
            if i < n:
                pending[i] = scores(loads[i])
            j = i - ATT_LAG
            if 0 <= j < n:
                done[j] = finish(j, *values(*pending.pop(j)))
            if post is not None and 0 <= j - 1 < n:
                post(j - 1, done.pop(j - 1))

    def merged(acc, l, m, m_old, l_old, a_old):
        m_new = jnp.maximum(m_old, m)
        w_old = jnp.exp2(m_old - m_new)
        w_new = jnp.exp2(m - m_new)
        return a_old * w_old + acc * w_new, l_old * w_old + l * w_new, m_new

    def load16(r):
        rs = slice(r * Q_BLOCK, (r + 1) * Q_BLOCK)
        return lambda: (q16[rs, :], jnp.concatenate([kp16[rs, :], k16[rs, :]], axis=0),
                        jnp.concatenate([vp16[rs, :], v16[rs, :]], axis=0), bias_edge)

    def finish16(r, acc, l, m):
        rows = pl.ds((r % d4) * seg4 + r // d4, Q_BLOCK, stride=d4)
        a_s[rows, :] = acc
        l_s[rows, :] = l
        m_s[rows, :] = m

    run([load16(r) for r in range(d16)], finish16)

    def load4(r, b):
        base = r * seg4 + b * Q_BLOCK
        if b == 0:
            return lambda: (q4[base:base + Q_BLOCK, :],
                            jnp.concatenate([kp4[r][...], k4[base:base + Q_BLOCK, :]], axis=0),
                            jnp.concatenate([vp4[r][...], v4[base:base + Q_BLOCK, :]], axis=0), bias_edge)
        return lambda: (q4[base:base + Q_BLOCK, :], k4[base - Q_BLOCK:base + Q_BLOCK, :],
                        v4[base - Q_BLOCK:base + Q_BLOCK, :], bias_in)

    blocks4 = [(r, b) for b in range(seg4 // Q_BLOCK) for r in range(d4)]

    def finish4(j, acc, l, m):
        r, b = blocks4[j]
        rows = slice(r * seg4 + b * Q_BLOCK, r * seg4 + (b + 1) * Q_BLOCK)
        acc, l, m = merged(acc, l, m, m_s[rows, :], l_s[rows, :], a_s[rows, :])
        a_s[rows, :] = acc
        l_s[rows, :] = l
        m_s[rows, :] = m

    run([load4(r, b) for r, b in blocks4], finish4)

    for c in range(d4):
        src = slice(c * seg4, (c + 1) * seg4)
        dst = pl.ds(c, seg4, stride=d4)
        m_n[dst, :] = m_s[src, :]
        l_n[dst, :] = l_s[src, :]
        a_n[dst, :] = a_s[src, :]

    def load1(b):
        rs = slice(b * Q_BLOCK, (b + 1) * Q_BLOCK)
        if b == 0:
            return lambda: (q1[rs, :], jnp.concatenate([kp1[...], k1[rs, :]], axis=0),
                            jnp.concatenate([vp1[...], v1[rs, :]], axis=0), bias_edge)
        ks = slice((b - 1) * Q_BLOCK, (b + 1) * Q_BLOCK)
        return lambda: (q1[rs, :], k1[ks, :], v1[ks, :], bias_in)

    def finish1(b, acc, l, m):
        rs = slice(b * Q_BLOCK, (b + 1) * Q_BLOCK)
        acc, l, _ = merged(acc, l, m, m_n[rs, :], l_n[rs, :], a_n[rs, :])
        o = acc / l
        sq = o * o
        hi = sq.astype(BF16)
        return o, jnp.concatenate([hi, (sq - hi.astype(F32)).astype(BF16)], axis=1)

    def post1(b, res):
        o, sq2 = res
        rs = slice(b * Q_BLOCK, (b + 1) * Q_BLOCK)
        ssq = jnp.dot(sq2, ssq_rhs, preferred_element_type=F32)
        o_ref[rs, :] = (o * lax.rsqrt(ssq * (1.0 / AT_HD) + EPS) * anw_ref[...]
                        * z_ref[rs, :].astype(F32)).astype(BF16)

    run([load1(b) for b in range(SUPER // Q_BLOCK)], finish1, post1)


def _attn_call(aq, ak, av, zat, anw_row):
    s = zat.shape[0]
    d4 = DILATIONS[1]
    per_super = SUPER // Q_BLOCK
    cur = pl.BlockSpec((SUPER, LANES), lambda m, g: (m, g))
    prev16 = pl.BlockSpec((SUPER, LANES), lambda m, g: (jnp.maximum(m - 1, 0), g))
    prev1 = pl.BlockSpec((Q_BLOCK, LANES), lambda m, g: (jnp.maximum(m * per_super - 1, 0), g))
    prev4 = [pl.BlockSpec((Q_BLOCK, LANES),
                          lambda m, g, r=r: (jnp.maximum((m - 1) * per_super + (r + 1) * (per_super // d4) - 1, 0), g))
             for r in range(d4)]
    state = pltpu.VMEM((SUPER, LANES), F32)
    return pl.pallas_call(
        _attn_body,
        out_shape=jax.ShapeDtypeStruct((s, AT_W), BF16),
        grid=(s // SUPER, AT_W // LANES),
        in_specs=[cur] * 9 + [prev1, prev1] + prev4 + prev4 + [prev16, prev16, cur,
                                                               pl.BlockSpec((1, LANES), lambda m, g: (0, g))],
        out_specs=cur,
        scratch_shapes=[state] * 6,
        compiler_params=_params(("arbitrary", "arbitrary")),
        name="attn",
    )(*aq, *ak, *av, ak[0], av[0], *([ak[1]] * d4), *([av[1]] * d4), ak[2], av[2], zat, anw_row)


def _out_body(x_ref, odn_ref, oat_ref, gate_ref, w_ref, fnw_ref, o_ref):
    mixin = jnp.concatenate([odn_ref[...], oat_ref[...]], axis=1)
    mix = jnp.dot(mixin, w_ref[...], preferred_element_type=F32)
    y = x_ref[...] + gate_ref[...] * mix
    ms = jnp.mean(y * y, axis=-1, keepdims=True)
    o_ref[...] = y * lax.rsqrt(ms + EPS) * fnw_ref[...]


def _out_call(x2, odn, oat, gate, w_out, fnw_row, tm):
    s, d = x2.shape
    tile = lambda n: pl.BlockSpec((tm, n), lambda i: (i, 0))
    row = lambda n: pl.BlockSpec((1, n), lambda i: (0, 0))
    return pl.pallas_call(
        _out_body,
        out_shape=jax.ShapeDtypeStruct((s, d), F32),
        grid=(s // tm,),
        in_specs=[tile(d), tile(DN_W), tile(AT_W), row(d), pl.BlockSpec(w_out.shape, lambda i: (0, 0)), row(d)],
        out_specs=tile(d),
        compiler_params=_params(("arbitrary",)),
        name="out",
    )(x2, odn, oat, gate, w_out, fnw_row)


def kernel(x, c, positions, w_mod, b_mod, norm_w, w_in, conv_w, a_log, dt_bias, dn_norm_w, at_norm_w,
           w_out, final_norm_w):
    b, s, d = x.shape
    assert b == 1 and d == D_MODEL and w_mod.shape[0] == 1
    x2 = x.reshape(s, d)

    mod = _mod_call(c.reshape(d, 1), w_mod[0], b_mod[0].reshape(1, 3 * d))
    shift, scale, gate = mod[:, 0:d], mod[:, d:2 * d], mod[:, 2 * d:3 * d]

    w_all, w_out_b = _wpack_call(w_in, w_out)
    pad_heads = lambda v: jnp.pad(v.reshape(1, DN_HEADS), ((0, 0), (DN_HEADS, LANES - 2 * DN_HEADS)))
    half = AT_HD // 2
    invf = (ROPE_THETA ** (-jnp.arange(half, dtype=F32) / half)).reshape(half, 1)
    sgn = jnp.where(jnp.arange(LANES) < LANES // 2, -1.0, 1.0).astype(F32).reshape(1, LANES)

    assert s % SUPER == 0
    outs = _proj_call(x2, positions.reshape(1, s), norm_w[0].reshape(1, d), scale, shift, w_all, conv_w[0],
                      pad_heads(a_log[0]), pad_heads(dt_bias[0]), invf, sgn, tm=512)
    qn, kn, v, zdn, bg, zat = outs[0:6]
    flat = lambda a: a.reshape(s, AT_W)
    aq, ak, av = [[flat(a) for a in outs[j:j + 3]] for j in (6, 9, 12)]

    odn = _dn_call(qn, kn, v, bg, zdn, dn_norm_w[0].reshape(1, DN_HD), rows=512)
    oat = _attn_call(aq, ak, av, zat, jnp.tile(at_norm_w[0], AT_HEADS).reshape(1, AT_W))

    out = _out_call(x2, odn, oat, gate, w_out_b, final_norm_w.reshape(1, d), tm=1024)
    return out.reshape(b, s, d)
```

```python
import math

import jax
import jax.numpy as jnp
from jax import lax
from jax.experimental import pallas as pl
from jax.experimental.pallas import tpu as pltpu

D_MODEL = 1024
DN_HEADS = 4
DN_HD = 128
DN_W = DN_HEADS * DN_HD
AT_HEADS = 8
AT_HD = 64
AT_W = AT_HEADS * AT_HD
CONV_K = 4
CHUNK = 64
DILATIONS = (1, 4, 16)
W_SUB = 128
Q_BLOCK = 128
SUPER = Q_BLOCK * DILATIONS[-1]
ATT_LAG = 2
ROPE_THETA = 10000.0
EPS = 1e-6
NEG = -1e30

LANES = 128
SUBLANES = 8
MXU_COLS = 256
PROJ_COLS = 512
VMEM_LIMIT = 48 * 1024 * 1024

C_QKV, C_Z, C_AQ, C_AK, C_AV, C_AZ, C_BA = 0, 1536, 2048, 2560, 3072, 3584, 4096
W_COLS = C_BA + LANES

F32 = jnp.float32
BF16 = jnp.bfloat16
NT_DIMS = (((1,), (1,)), ((), ()))


def _sigmoid(v):
    return 1.0 / (1.0 + jnp.exp2(v * (-math.log2(math.e))))


def _iota(shape, dim):
    return lax.broadcasted_iota(jnp.int32, shape, dim)


def _params(sem):
    return pltpu.CompilerParams(dimension_semantics=sem, vmem_limit_bytes=VMEM_LIMIT)


def _mod_body(c_ref, w_ref, b_ref, o_ref):
    c = c_ref[...]
    sc = c * _sigmoid(c)
    o_ref[...] = jnp.sum(sc * w_ref[...], axis=0, keepdims=True) + b_ref[...]


def _mod_call(c_col, w_mod, b_mod):
    d, n = w_mod.shape
    tn = 512
    return pl.pallas_call(
        _mod_body,
        out_shape=jax.ShapeDtypeStruct((1, n), F32),
        grid=(n // tn,),
        in_specs=[pl.BlockSpec((d, 1), lambda j: (0, 0)),
                  pl.BlockSpec((d, tn), lambda j: (0, j)),
                  pl.BlockSpec((1, tn), lambda j: (0, j))],
        out_specs=pl.BlockSpec((1, tn), lambda j: (0, j)),
        compiler_params=_params(("arbitrary",)),
        name="mod",
    )(c_col, w_mod, b_mod)


def _wpack_body(wt_ref, wo_ref, o_ref, oo_ref):
    kb = wt_ref.shape[2]
    o_b = 4 * DN_W
    o_q = o_b + 2 * DN_HEADS
    half = AT_HD // 2

    def put(col0, src):
        o_ref[:, col0:col0 + LANES] = src.T.astype(BF16)

    for g in range(o_b // LANES):
        put(C_QKV + g * LANES, wt_ref[0, g * LANES:(g + 1) * LANES, :])
    for g in range(2 * AT_W // LANES):
        base = o_q + g * LANES
        put(C_AQ + g * LANES, jnp.concatenate([wt_ref[0, base:base + half, :],
                                               wt_ref[0, base + 2 * half:base + 3 * half, :],
                                               wt_ref[0, base + half:base + 2 * half, :],
                                               wt_ref[0, base + 3 * half:base + 4 * half, :]], axis=0))
    for g in range(2 * AT_W // LANES):
        base = o_q + 2 * AT_W + g * LANES
        put(C_AV + g * LANES, wt_ref[0, base:base + LANES, :])
    put(C_BA, jnp.concatenate([wt_ref[0, o_b:o_b + 2 * DN_HEADS, :],
                               jnp.zeros((LANES - 2 * DN_HEADS, kb), F32)], axis=0))
    oo_ref[...] = wo_ref[0].astype(BF16)


def _wpack_call(w_in, w_out):
    _, d, n_in = w_in.shape
    n_out = w_out.shape[2]
    kb = 256
    w_t = jnp.swapaxes(w_in, 1, 2)
    return pl.pallas_call(
        _wpack_body,
        out_shape=[jax.ShapeDtypeStruct((d, W_COLS), BF16), jax.ShapeDtypeStruct((w_out.shape[1], n_out), BF16)],
        grid=(d // kb,),
        in_specs=[pl.BlockSpec((1, n_in, kb), lambda i: (0, 0, i)),
                  pl.BlockSpec((1, kb, n_out), lambda i: (0, i, 0))],
        out_specs=[pl.BlockSpec((kb, W_COLS), lambda i: (i, 0)), pl.BlockSpec((kb, n_out), lambda i: (i, 0))],
        compiler_params=_params(("arbitrary",)),
        name="wpack",
    )(w_t, w_out)


def _proj_body(x_ref, pos_ref, nw_ref, sc_ref, sh_ref, w_ref, cw_ref, alog_ref, dtb_ref, invf_ref,
               sgn_ref, qn_ref, kn_ref, v_ref, zdn_ref, bg_ref, zat_ref,
               aq1_ref, aq4_ref, aq16_ref, ak1_ref, ak4_ref, ak16_ref, av1_ref, av4_ref, av16_ref,
               cbuf_ref, hb_ref, pbuf_ref, p4buf_ref):
    i = pl.program_id(0)
    tm = x_ref.shape[0]
    x = x_ref[...]
    ms = jnp.mean(x * x, axis=-1, keepdims=True)
    hb_ref[...] = ((x * lax.rsqrt(ms + EPS)) * (nw_ref[...] * (1.0 + sc_ref[...])) + sh_ref[...]).astype(BF16)

    def proj(c0, width=PROJ_COLS):
        return jnp.dot(hb_ref[...], w_ref[:, c0:c0 + width], preferred_element_type=F32)

    def silu(v):
        return v * _sigmoid(v)

    @pl.when(i == 0)
    def _():
        cbuf_ref[:, 0:SUBLANES, :] = jnp.zeros((cbuf_ref.shape[0], SUBLANES, LANES), F32)

    dyn0 = jnp.minimum(i, 0)
    for c0 in range(0, 3 * DN_W, PROJ_COLS):
        xg = proj(C_QKV + c0)
        halves = []
        for hh in range(PROJ_COLS // LANES):
            grp = c0 // LANES + hh
            ls = slice(c0 + hh * LANES, c0 + (hh + 1) * LANES)
            xh = xg[:, hh * LANES:(hh + 1) * LANES]
            cbuf_ref[grp, SUBLANES:SUBLANES + tm, :] = xh
            acc = xh * cw_ref[CONV_K - 1:CONV_K, ls]
            for j in range(1, CONV_K):
                acc = acc + cbuf_ref[grp, pl.ds(dyn0 + SUBLANES - j, tm), :] * cw_ref[CONV_K - 1 - j:CONV_K - j, ls]
            cbuf_ref[grp, 0:SUBLANES, :] = xh[tm - SUBLANES:tm]
            halves.append(acc)
        conv = jnp.concatenate(halves, axis=1)
        y = silu(conv)
        kind, off = c0 // DN_W, c0 % DN_W
        for hh in range(PROJ_COLS // DN_HD):
            yh = y[:, hh * DN_HD:(hh + 1) * DN_HD]
            dst = slice(off + hh * DN_HD, off + (hh + 1) * DN_HD)
            if kind == 2:
                v_ref[:, dst] = yh.astype(BF16)
            else:
                inv = lax.rsqrt(jnp.sum(yh * yh, axis=-1, keepdims=True) + EPS)
                if kind == 0:
                    qn_ref[:, dst] = (yh * (inv * (DN_HD ** -0.5))).astype(BF16)
                else:
                    kn_ref[:, dst] = (yh * inv).astype(BF16)

    for c0 in range(0, DN_W, PROJ_COLS):
        zdn_ref[:, c0:c0 + PROJ_COLS] = silu(proj(C_Z + c0)).astype(BF16)
        zat_ref[:, c0:c0 + PROJ_COLS] = silu(proj(C_AZ + c0)).astype(BF16)

    ba = proj(C_BA, LANES)
    lane = _iota(ba.shape, 1)
    a = ba + dtb_ref[...]
    softplus = jnp.maximum(a, 0.0) + jnp.log(1.0 + jnp.exp(-jnp.abs(a)))
    g = -jnp.exp(alog_ref[...]) * softplus
    bg_ref[...] = jnp.where(lane < DN_HEADS, _sigmoid(ba), jnp.where(lane < 2 * DN_HEADS, g, 0.0))

    ang = invf_ref[...] * pos_ref[...].astype(F32)
    reps = LANES // (AT_HD // 2)
    cos = jnp.concatenate([jnp.cos(ang)] * reps, axis=0).T
    sin = jnp.concatenate([jnp.sin(ang)] * reps, axis=0).T * sgn_ref[...]
    qscale = (AT_HD ** -0.5) * math.log2(math.e)
    d4, d16 = DILATIONS[1], DILATIONS[2]
    n4, n16 = tm // d4, tm // d16

    def emit(val, gp, nat_ref, p4_ref, p16_ref):
        sl = slice(gp * LANES, (gp + 1) * LANES)
        nat_ref[:, sl] = val.astype(BF16)
        pbuf_ref[gp] = val
        for c in range(d4):
            seg = pbuf_ref[gp, pl.ds(c, n4, stride=d4), :]
            p4_ref[0, c, :, sl] = seg.astype(BF16)
            p4buf_ref[gp, c * n4:(c + 1) * n4, :] = seg
        for r in range(d16):
            seg = p4buf_ref[gp, pl.ds((r % d4) * n4 + r // d4, n16, stride=d4), :]
            p16_ref[0, r, :, sl] = seg.astype(BF16)

    def rope(v):
        return v * cos + pltpu.roll(v, LANES // 2, axis=1) * sin

    for c0 in range(0, AT_W, PROJ_COLS):
        aq = proj(C_AQ + c0)
        ak = proj(C_AK + c0)
        av = proj(C_AV + c0)
        for hh in range(PROJ_COLS // LANES):
            gp = c0 // LANES + hh
            ls = slice(hh * LANES, (hh + 1) * LANES)
            emit(rope(aq[:, ls]) * qscale, gp, aq1_ref, aq4_ref, aq16_ref)
            emit(rope(ak[:, ls]), gp, ak1_ref, ak4_ref, ak16_ref)
            emit(av[:, ls], gp, av1_ref, av4_ref, av16_ref)


def _proj_call(x2, pos_row, nw, scale, shift, w_all, cw, alog_row, dtb_row, invf, sgn, tm):
    s, d = x2.shape
    row = lambda n: pl.BlockSpec((1, n), lambda i: (0, 0))
    full = lambda a: pl.BlockSpec(a.shape, lambda i: (0, 0))
    tile = lambda n: pl.BlockSpec((tm, n), lambda i: (i, 0))
    spb = SUPER // tm
    res_shape = lambda dil: jax.ShapeDtypeStruct((s // SUPER, dil, SUPER // dil, AT_W), BF16)
    res_spec = lambda dil: pl.BlockSpec((1, dil, tm // dil, AT_W), lambda i: (i // spb, 0, i % spb, 0))
    bf = lambda n: jax.ShapeDtypeStruct((s, n), BF16)
    outs = [bf(DN_W)] * 4 + [jax.ShapeDtypeStruct((s, LANES), F32), bf(AT_W)] + \
           [bf(AT_W), res_shape(DILATIONS[1]), res_shape(DILATIONS[2])] * 3
    out_specs = [tile(DN_W)] * 4 + [tile(LANES), tile(AT_W)] + \
                [tile(AT_W), res_spec(DILATIONS[1]), res_spec(DILATIONS[2])] * 3
    return pl.pallas_call(
        _proj_body,
        out_shape=outs,
        grid=(s // tm,),
        in_specs=[tile(d), pl.BlockSpec((1, tm), lambda i: (0, i)), row(d), row(d), row(d),
                  pl.BlockSpec(w_all.shape, lambda i: (0, 0), pipeline_mode=pl.Buffered(1)),
                  full(cw), row(LANES), row(LANES), full(invf), row(LANES)],
        out_specs=out_specs,
        scratch_shapes=[pltpu.VMEM((3 * DN_W // LANES, SUBLANES + tm, LANES), F32), pltpu.VMEM((tm, d), BF16),
                        pltpu.VMEM((AT_W // LANES, tm, LANES), F32),
                        pltpu.VMEM((AT_W // LANES, tm, LANES), F32)],
        compiler_params=_params(("arbitrary",)),
        name="proj",
    )(x2, pos_row, nw, scale, shift, w_all, cw, alog_row, dtb_row, invf, sgn)


def _dn_body(q_ref, k_ref, v_ref, bg_ref, z_ref, nw_ref, o_ref, w_s, bt_s, op_s, gl_s, st_ref):
    step = pl.program_id(0)
    cur = step % 2
    prv = 1 - cur

    @pl.when(step == 0)
    def _():
        st_ref[...] = jnp.zeros(st_ref.shape, F32)
        w_s[1] = jnp.zeros(w_s.shape[1:], BF16)
        bt_s[1] = jnp.zeros(bt_s.shape[1:], BF16)
        op_s[1] = jnp.zeros(op_s.shape[1:], BF16)
        gl_s[1] = jnp.zeros(gl_s.shape[1:], F32)

    rows = q_ref.shape[0]
    nchunk = rows // CHUNK
    hsl = lambda hd: slice(hd * DN_HD, (hd + 1) * DN_HD)

    def scan_chunk(c):
        rs = slice(c * CHUNK, (c + 1) * CHUNK)
        for hd in range(DN_HEADS):
            st = st_ref[hd]
            r = jnp.dot(st.astype(BF16), w_s[prv, c, hd], preferred_element_type=F32)
            st_ref[hd] = gl_s[prv, c, hd:hd + 1, :] * st - r[:, 0:DN_HD] + bt_s[prv, c, hd].astype(F32)
            half = (hd % 2) * CHUNK
            o = r[:, DN_HD:2 * DN_HD].T[half:half + CHUNK] + op_s[prv, rs, hsl(hd)].astype(F32)
            ms = jnp.mean(o * o, axis=-1, keepdims=True)
            o_ref[rs, hsl(hd)] = (o * lax.rsqrt(ms + EPS) * nw_ref[...] * z_ref[rs, hsl(hd)].astype(F32)).astype(BF16)

    pending = list(range(nchunk))

    def tick():
        if pending:
            scan_chunk(pending.pop(0))

    wide = DN_HEADS * CHUNK
    bg = bg_ref[...]
    cs_rows = min(rows, MXU_COLS)
    ri = _iota((cs_rows, cs_rows), 0)
    ci = _iota((cs_rows, cs_rows), 1)
    tril_blk = jnp.where((ri // CHUNK == ci // CHUNK) & (ri >= ci), 1.0, 0.0).astype(BF16)
    g_hi = bg.astype(BF16)
    g_r1 = bg - g_hi.astype(F32)
    g_mid = g_r1.astype(BF16)
    g_lo = (g_r1 - g_mid.astype(F32)).astype(BF16)
    g3 = jnp.concatenate([g_hi, g_mid, g_lo], axis=1)
    csum = jnp.concatenate([jnp.dot(tril_blk, g3[r0:r0 + cs_rows], preferred_element_type=F32)
                            for r0 in range(0, rows, cs_rows)], axis=0)
    gcum = csum[:, 0:LANES] + csum[:, LANES:2 * LANES] + csum[:, 2 * LANES:3 * LANES]
    bfull = jnp.concatenate([jnp.broadcast_to(bg[:, hd:hd + 1], (rows, DN_HD)) for hd in range(DN_HEADS)], axis=1)
    gfull = jnp.concatenate([jnp.broadcast_to(gcum[:, DN_HEADS + hd:DN_HEADS + hd + 1], (rows, DN_HD))
                             for hd in range(DN_HEADS)], axis=1)
    eg = jnp.exp(gfull)
    q = q_ref[...].astype(F32)
    k = k_ref[...].astype(F32)
    kb = k * bfull
    vb = v_ref[...].astype(F32) * bfull
    q_dec = q * eg
    kbg = kb * eg

    r64 = _iota((CHUNK, wide), 0)
    j64 = _iota((CHUNK, wide), 1) % CHUNK
    tril = r64 >= j64
    strict = r64 > j64
    eye = jnp.where(r64 == j64, 1.0, 0.0).astype(F32)
    low_half = _iota((CHUNK, LANES), 1) < CHUNK
    bd64 = _iota((wide, wide), 0) // CHUNK == _iota((wide, wide), 1) // CHUNK
    bd_k = _iota((wide, DN_W), 0) // CHUNK == _iota((wide, DN_W), 1) // DN_HD

    def block_diag(m):
        return jnp.where(bd64, jnp.concatenate([m.astype(BF16)] * DN_HEADS, axis=0), jnp.zeros((wide, wide), BF16))

    rsl = lambda c: slice(c * CHUNK, (c + 1) * CHUNK)
    chunks = range(nchunk)
    g_last = [gfull[c * CHUNK + CHUNK - 1:(c + 1) * CHUNK, :] for c in chunks]
    k_dec = [k[rsl(c)] * jnp.exp(g_last[c] - gfull[rsl(c)]) for c in chunks]
    g_row = [jnp.concatenate([gcum[rsl(c)], gcum[rsl(c)]], axis=0).T for c in chunks]
    dec = []
    for c in chunks:
        halves = []
        for pr in range(DN_HEADS // 2):
            ha, hb = 2 * pr, 2 * pr + 1
            col = jnp.where(low_half, gfull[rsl(c), hsl(ha)], gfull[rsl(c), hsl(hb)])
            row = jnp.where(low_half[0:1], g_row[c][DN_HEADS + ha:DN_HEADS + ha + 1, :],
                            g_row[c][DN_HEADS + hb:DN_HEADS + hb + 1, :])
            halves.append(col - row)
        dec.append(jnp.exp(jnp.where(tril, jnp.concatenate(halves, axis=1), NEG)))
    kq = [lax.dot_general(jnp.concatenate([kb[rsl(c)].astype(BF16), q_ref[rsl(c), :]], axis=0),
                          jnp.where(bd_k, jnp.concatenate([k_ref[rsl(c), :]] * DN_HEADS, axis=0),
                                    jnp.zeros((wide, DN_W), BF16)),
                          NT_DIMS, preferred_element_type=F32) for c in chunks]
    tick()
    a_low = [jnp.where(strict, m[0:CHUNK] * d, 0.0) for m, d in zip(kq, dec)]
    attn = [jnp.where(tril, m[CHUNK:2 * CHUNK] * d, 0.0) for m, d in zip(kq, dec)]
    bmat = [-a for a in a_low]
    ymat = [eye + b for b in bmat]
    bmat = [jnp.dot(b.astype(BF16), block_diag(b), preferred_element_type=F32) for b in bmat]
    tick()
    for _ in range(4):
        prod = [jnp.dot(jnp.concatenate([y, b], axis=0).astype(BF16), block_diag(b), preferred_element_type=F32)
                for y, b in zip(ymat, bmat)]
        tick()
        ymat = [y + p[0:CHUNK] for y, p in zip(ymat, prod)]
        bmat = [p[CHUNK:2 * CHUNK] for p in prod]
    ymat = [y + jnp.dot(y.astype(BF16), block_diag(b), preferred_element_type=F32) for y, b in zip(ymat, bmat)]
    tick()
    rhs = [jnp.concatenate([jnp.concatenate([kbg[rsl(c), hsl(hd)], vb[rsl(c), hsl(hd)]], axis=1)
                            for hd in range(DN_HEADS)], axis=0).astype(BF16) for c in chunks]
    wu = [jnp.dot(block_diag(y), r, preferred_element_type=F32) for y, r in zip(ymat, rhs)]
    tick()
    wut = [m.T.astype(BF16) for m in wu]
    kdbd = [jnp.where(bd_k, jnp.concatenate([k_dec[c].astype(BF16)] * DN_HEADS, axis=0),
                      jnp.zeros((wide, DN_W), BF16)) for c in chunks]
    pbt = [jnp.dot(a, b, preferred_element_type=F32) for a, b in zip(wut, kdbd)]
    tick()
    aw = [jnp.dot(block_diag(a), m.astype(BF16), preferred_element_type=F32) for a, m in zip(attn, wu)]
    qpt = [(jnp.concatenate([q_dec[rsl(c), hsl(hd)] for hd in range(DN_HEADS)], axis=0) - a[:, 0:DN_HD]).T
           .astype(BF16) for c, a in zip(chunks, aw)]
    while pending:
        tick()
    for c in chunks:
        for hd in range(DN_HEADS):
            pair = slice((hd // 2) * LANES, (hd // 2 + 1) * LANES)
            w_s[cur, c, hd] = jnp.concatenate([pbt[c][0:DN_HD, hsl(hd)].astype(BF16), qpt[c][:, pair]], axis=1)
            bt_s[cur, c, hd] = pbt[c][DN_HD:2 * DN_HD, hsl(hd)].astype(BF16)
            op_s[cur, rsl(c), hsl(hd)] = aw[c][hd * CHUNK:(hd + 1) * CHUNK, DN_HD:2 * DN_HD].astype(BF16)
        gl_rows = [jnp.exp(g_last[c][:, hsl(hd)]) for hd in range(DN_HEADS)]
        gl_rows.append(jnp.zeros((SUBLANES - DN_HEADS, DN_HD), F32))
        gl_s[cur, c] = jnp.concatenate(gl_rows, axis=0)


def _dn_call(qn, kn, v, bg, zdn, nw_row, rows):
    s = qn.shape[0]
    nblk = s // rows
    ncb = rows // CHUNK
    cur = lambda n: pl.BlockSpec((rows, n), lambda i: (jnp.minimum(i, nblk - 1), 0))
    prv = lambda n: pl.BlockSpec((rows, n), lambda i: (jnp.maximum(i - 1, 0), 0))
    return pl.pallas_call(
        _dn_body,
        out_shape=jax.ShapeDtypeStruct((s, DN_W), BF16),
        grid=(nblk + 1,),
        in_specs=[cur(DN_W), cur(DN_W), cur(DN_W), cur(LANES), prv(DN_W), pl.BlockSpec((1, DN_HD), lambda i: (0, 0))],
        out_specs=prv(DN_W),
        scratch_shapes=[pltpu.VMEM((2, ncb, DN_HEADS, DN_HD, 2 * DN_HD), BF16),
                        pltpu.VMEM((2, ncb, DN_HEADS, DN_HD, DN_HD), BF16),
                        pltpu.VMEM((2, rows, DN_W), BF16),
                        pltpu.VMEM((2, ncb, SUBLANES, DN_HD), F32),
                        pltpu.VMEM((DN_HEADS, DN_HD, DN_HD), F32)],
        compiler_params=_params(("arbitrary",)),
        name="deltanet",
    )(qn, kn, v, bg, zdn, nw_row)


def _attn_body(q1, q4, q16, k1, k4, k16, v1, v4, v16, kp1, vp1, kp4a, kp4b, kp4c, kp4d,
               vp4a, vp4b, vp4c, vp4d, kp16, vp16, z_ref, anw_ref, o_ref, m_s, l_s, a_s, m_n, l_n, a_n):
    d4, d16 = DILATIONS[1], DILATIONS[2]
    seg4 = SUPER // d4
    kp4 = (kp4a, kp4b, kp4c, kp4d)
    vp4 = (vp4a, vp4b, vp4c, vp4d)
    kj = _iota((2 * Q_BLOCK, Q_BLOCK), 0)
    qi = _iota((2 * Q_BLOCK, Q_BLOCK), 1)
    rel = Q_BLOCK + qi - kj
    band = (rel >= 0) & (rel <= W_SUB)
    bias_in = jnp.where(band, 0.0, NEG).astype(BF16)
    bias_first = jnp.where(band & (kj >= Q_BLOCK), 0.0, NEG).astype(BF16)
    bias_edge = jnp.where(pl.program_id(0) == 0, bias_first, bias_in)
    one_hot = jnp.where(kj % Q_BLOCK == qi, 1.0, 0.0).astype(BF16)
    lane = _iota((Q_BLOCK, LANES), 1)
    lane_k = _iota((2 * Q_BLOCK, LANES), 1)
    head_a_q = (lane % AT_HD) < (AT_HD // 2)
    head_a = lane < AT_HD
    head_a_k = lane_k < AT_HD
    ro = _iota((4 * Q_BLOCK, LANES), 0)
    ones_rhs = jnp.where((ro < 2 * Q_BLOCK) == (_iota((4 * Q_BLOCK, LANES), 1) < AT_HD), 1.0, 0.0).astype(BF16)
    sr = _iota((2 * LANES, LANES), 0)
    ssq_rhs = jnp.where((sr % LANES) // AT_HD == _iota((2 * LANES, LANES), 1) // AT_HD, 1.0, 0.0).astype(BF16)
    zq = jnp.zeros((Q_BLOCK, LANES), BF16)
    zv = jnp.zeros((2 * Q_BLOCK, LANES), BF16)

    def scores(load):
        q, kb, vb, bias = load()
        q2 = jnp.concatenate([jnp.where(head_a_q, q, zq), jnp.where(head_a_q, zq, q)], axis=0)
        s = lax.dot_general(jnp.concatenate([q2, one_hot], axis=1), jnp.concatenate([kb, bias], axis=1),
                            NT_DIMS, preferred_element_type=F32)
        m = jnp.max(s, axis=-1, keepdims=True)
        return jnp.exp2(s - m).astype(BF16), m, vb

    def values(p, m, vb):
        pc = jnp.concatenate([p[0:Q_BLOCK], p[Q_BLOCK:2 * Q_BLOCK]], axis=1)
        rhs = jnp.concatenate([jnp.concatenate([jnp.where(head_a_k, vb, zv), jnp.where(head_a_k, zv, vb)], axis=0),
                               ones_rhs], axis=1)
        r = jnp.dot(pc, rhs, preferred_element_type=F32)
        return r[:, 0:LANES], r[:, LANES:2 * LANES], jnp.where(head_a, m[0:Q_BLOCK], m[Q_BLOCK:2 * Q_BLOCK])

    def run(loads, finish, post=None):
        n = len(loads)
        pending, done = {}, {}
        for i in range(n + ATT_LAG + 1):
            if i < n:
                pending[i] = scores(loads[i])
            j = i - ATT_LAG
            if 0 <= j < n:
                done[j] = finish(j, *values(*pending.pop(j)))
            if post is not None and 0 <= j - 1 < n:
                post(j - 1, done.pop(j - 1))

    def merged(acc, l, m, m_old, l_old, a_old):
        m_new = jnp.maximum(m_old, m)
        w_old = jnp.exp2(m_old - m_new)
        w_new = jnp.exp2(m - m_new)
        return a_old * w_old + acc * w_new, l_old * w_old + l * w_new, m_new

    def load16(r):
        rs = slice(r * Q_BLOCK, (r + 1) * Q_BLOCK)
        return lambda: (q16[rs, :], jnp.concatenate([kp16[rs, :], k16[rs, :]], axis=0),
                        jnp.concatenate([vp16[rs, :], v16[rs, :]], axis=0), bias_edge)

    def finish16(r, acc, l, m):
        rows = pl.ds((r % d4) * seg4 + r // d4, Q_BLOCK, stride=d4)
        a_s[rows, :] = acc
        l_s[rows, :] = l
        m_s[rows, :] = m

    run([load16(r) for r in range(d16)], finish16)

    def load4(r, b):
        base = r * seg4 + b * Q_BLOCK
        if b == 0:
            return lambda: (q4[base:base + Q_BLOCK, :],
                            jnp.concatenate([kp4[r][...], k4[base:base + Q_BLOCK, :]], axis=0),
                            jnp.concatenate([vp4[r][...], v4[base:base + Q_BLOCK, :]], axis=0), bias_edge)
        return lambda: (q4[base:base + Q_BLOCK, :], k4[base - Q_BLOCK:base + Q_BLOCK, :],
                        v4[base - Q_BLOCK:base + Q_BLOCK, :], bias_in)

    blocks4 = [(r, b) for b in range(seg4 // Q_BLOCK) for r in range(d4)]

    def finish4(j, acc, l, m):
        r, b = blocks4[j]
        rows = slice(r * seg4 + b * Q_BLOCK, r * seg4 + (b + 1) * Q_BLOCK)
        acc, l, m = merged(acc, l, m, m_s[rows, :], l_s[rows, :], a_s[rows, :])
        a_s[rows, :] = acc
        l_s[rows, :] = l
        m_s[rows, :] = m

    run([load4(r, b) for r, b in blocks4], finish4)

    for c in range(d4):
        src = slice(c * seg4, (c + 1) * seg4)
        dst = pl.ds(c, seg4, stride=d4)
        m_n[dst, :] = m_s[src, :]
        l_n[dst, :] = l_s[src, :]
        a_n[dst, :] = a_s[src, :]

    def load1(b):
        rs = slice(b * Q_BLOCK, (b + 1) * Q_BLOCK)
        if b == 0:
            return lambda: (q1[rs, :], jnp.concatenate([kp1[...], k1[rs, :]], axis=0),
                            jnp.concatenate([vp1[...], v1[rs, :]], axis=0), bias_edge)
        ks = slice((b - 1) * Q_BLOCK, (b + 1) * Q_BLOCK)
        return lambda: (q1[rs, :], k1[ks, :], v1[ks, :], bias_in)

    def finish1(b, acc, l, m):
        rs = slice(b * Q_BLOCK, (b + 1) * Q_BLOCK)
        acc, l, _ = merged(acc, l, m, m_n[rs, :], l_n[rs, :], a_n[rs, :])
        o = acc / l
        sq = o * o
        hi = sq.astype(BF16)
        return o, jnp.concatenate([hi, (sq - hi.astype(F32)).astype(BF16)], axis=1)

    def post1(b, res):
        o, sq2 = res
        rs = slice(b * Q_BLOCK, (b + 1) * Q_BLOCK)
        ssq = jnp.dot(sq2, ssq_rhs, preferred_element_type=F32)
        o_ref[rs, :] = (o * lax.rsqrt(ssq * (1.0 / AT_HD) + EPS) * anw_ref[...]
                        * z_ref[rs, :].astype(F32)).astype(BF16)

    run([load1(b) for b in range(SUPER // Q_BLOCK)], finish1, post1)


def _attn_call(aq, ak, av, zat, anw_row):
    s = zat.shape[0]
    d4 = DILATIONS[1]
    per_super = SUPER // Q_BLOCK
    cur = pl.BlockSpec((SUPER, LANES), lambda m, g: (m, g))
    prev16 = pl.BlockSpec((SUPER, LANES), lambda m, g: (jnp.maximum(m - 1, 0), g))
    prev1 = pl.BlockSpec((Q_BLOCK, LANES), lambda m, g: (jnp.maximum(m * per_super - 1, 0), g))
    prev4 = [pl.BlockSpec((Q_BLOCK, LANES),
                          lambda m, g, r=r: (jnp.maximum((m - 1) * per_super + (r + 1) * (per_super // d4) - 1, 0), g))
             for r in range(d4)]
    state = pltpu.VMEM((SUPER, LANES), F32)
    return pl.pallas_call(
        _attn_body,
        out_shape=jax.ShapeDtypeStruct((s, AT_W), BF16),
        grid=(s // SUPER, AT_W // LANES),
        in_specs=[cur] * 9 + [prev1, prev1] + prev4 + prev4 + [prev16, prev16, cur,
                                                               pl.BlockSpec((1, LANES), lambda m, g: (0, g))],
        out_specs=cur,
        scratch_shapes=[state] * 6,
        compiler_params=_params(("arbitrary", "arbitrary")),
        name="attn",
    )(*aq, *ak, *av, ak[0], av[0], *([ak[1]] * d4), *([av[1]] * d4), ak[2], av[2], zat, anw_row)


def _out_body(x_ref, odn_ref, oat_ref, gate_ref, w_ref, fnw_ref, o_ref):
    mixin = jnp.concatenate([odn_ref[...], oat_ref[...]], axis=1)
    mix = jnp.dot(mixin, w_ref[...], preferred_element_type=F32)
    y = x_ref[...] + gate_ref[...] * mix
    ms = jnp.mean(y * y, axis=-1, keepdims=True)
    o_ref[...] = y * lax.rsqrt(ms + EPS) * fnw_ref[...]


def _out_call(x2, odn, oat, gate, w_out, fnw_row, tm):
    s, d = x2.shape
    tile = lambda n: pl.BlockSpec((tm, n), lambda i: (i, 0))
    row = lambda n: pl.BlockSpec((1, n), lambda i: (0, 0))
    return pl.pallas_call(
        _out_body,
        out_shape=jax.ShapeDtypeStruct((s, d), F32),
        grid=(s // tm,),
        in_specs=[tile(d), tile(DN_W), tile(AT_W), row(d), pl.BlockSpec(w_out.shape, lambda i: (0, 0)), row(d)],
        out_specs=tile(d),
        compiler_params=_params(("arbitrary",)),
        name="out",
    )(x2, odn, oat, gate, w_out, fnw_row)


def kernel(x, c, positions, w_mod, b_mod, norm_w, w_in, conv_w, a_log, dt_bias, dn_norm_w, at_norm_w,
           w_out, final_norm_w):
    b, s, d = x.shape
    assert b == 1 and d == D_MODEL and w_mod.shape[0] == 1
    x2 = x.reshape(s, d)

    mod = _mod_call(c.reshape(d, 1), w_mod[0], b_mod[0].reshape(1, 3 * d))
    shift, scale, gate = mod[:, 0:d], mod[:, d:2 * d], mod[:, 2 * d:3 * d]

    w_all, w_out_b = _wpack_call(w_in, w_out)
    pad_heads = lambda v: jnp.pad(v.reshape(1, DN_HEADS), ((0, 0), (DN_HEADS, LANES - 2 * DN_HEADS)))
    half = AT_HD // 2
    invf = (ROPE_THETA ** (-jnp.arange(half, dtype=F32) / half)).reshape(half, 1)
    sgn = jnp.where(jnp.arange(LANES) < LANES // 2, -1.0, 1.0).astype(F32).reshape(1, LANES)

    assert s % SUPER == 0
    outs = _proj_call(x2, positions.reshape(1, s), norm_w[0].reshape(1, d), scale, shift, w_all, conv_w[0],
                      pad_heads(a_log[0]), pad_heads(dt_bias[0]), invf, sgn, tm=512)
    qn, kn, v, zdn, bg, zat = outs[0:6]
    flat = lambda a: a.reshape(s, AT_W)
    aq, ak, av = [[flat(a) for a in outs[j:j + 3]] for j in (6, 9, 12)]

    odn = _dn_call(qn, kn, v, bg, zdn, dn_norm_w[0].reshape(1, DN_HD), rows=512)
    oat = _attn_call(aq, ak, av, zat, jnp.tile(at_norm_w[0], AT_HEADS).reshape(1, AT_W))

    out = _out_call(x2, odn, oat, gate, w_out_b, final_norm_w.reshape(1, d), tm=1024)
    return out.reshape(b, s, d)
```

```python
import math

import jax
import jax.numpy as jnp
from jax import lax
from jax.experimental import pallas as pl
from jax.experimental.pallas import tpu as pltpu

D_MODEL = 1024
DN_HEADS = 4
DN_HD = 128
DN_W = DN_HEADS * DN_HD
AT_HEADS = 8
AT_HD = 64
AT_W = AT_HEADS * AT_HD
CONV_K = 4
CHUNK = 64
DILATIONS = (1, 4, 16)
W_SUB = 128
Q_BLOCK = 128
SUPER = Q_BLOCK * DILATIONS[-1]
ATT_LAG = 2
ROPE_THETA = 10000.0
EPS = 1e-6
NEG = -1e30

LANES = 128
SUBLANES = 8
MXU_COLS = 256
PROJ_COLS = 512
VMEM_LIMIT = 48 * 1024 * 1024
OUT_VMEM_LIMIT = 56 * 1024 * 1024

C_QKV, C_Z, C_AQ, C_AK, C_AV, C_AZ, C_BA = 0, 1536, 2048, 2560, 3072, 3584, 4096
W_COLS = C_BA + LANES

F32 = jnp.float32
BF16 = jnp.bfloat16
NT_DIMS = (((1,), (1,)), ((), ()))


def _sigmoid(v):
    return 1.0 / (1.0 + jnp.exp2(v * (-math.log2(math.e))))


def _iota(shape, dim):
    return lax.broadcasted_iota(jnp.int32, shape, dim)


def _params(sem):
    return pltpu.CompilerParams(dimension_semantics=sem, vmem_limit_bytes=VMEM_LIMIT)


def _mod_body(c_ref, w_ref, b_ref, o_ref):
    c = c_ref[...]
    sc = c * _sigmoid(c)
    o_ref[...] = jnp.sum(sc * w_ref[...], axis=0, keepdims=True) + b_ref[...]


def _mod_call(c_col, w_mod, b_mod):
    d, n = w_mod.shape
    tn = 512
    return pl.pallas_call(
        _mod_body,
        out_shape=jax.ShapeDtypeStruct((1, n), F32),
        grid=(n // tn,),
        in_specs=[pl.BlockSpec((d, 1), lambda j: (0, 0)),
                  pl.BlockSpec((d, tn), lambda j: (0, j)),
                  pl.BlockSpec((1, tn), lambda j: (0, j))],
        out_specs=pl.BlockSpec((1, tn), lambda j: (0, j)),
        compiler_params=_params(("arbitrary",)),
        name="mod",
    )(c_col, w_mod, b_mod)


def _wpack_body(wt_ref, wo_ref, o_ref, oo_ref):
    kb = wt_ref.shape[2]
    o_b = 4 * DN_W
    o_q = o_b + 2 * DN_HEADS
    half = AT_HD // 2

    def put(col0, src):
        o_ref[:, col0:col0 + LANES] = src.T.astype(BF16)

    for g in range(o_b // LANES):
        put(C_QKV + g * LANES, wt_ref[0, g * LANES:(g + 1) * LANES, :])
    for g in range(2 * AT_W // LANES):
        base = o_q + g * LANES
        put(C_AQ + g * LANES, jnp.concatenate([wt_ref[0, base:base + half, :],
                                               wt_ref[0, base + 2 * half:base + 3 * half, :],
                                               wt_ref[0, base + half:base + 2 * half, :],
                                               wt_ref[0, base + 3 * half:base + 4 * half, :]], axis=0))
    for g in range(2 * AT_W // LANES):
        base = o_q + 2 * AT_W + g * LANES
        put(C_AV + g * LANES, wt_ref[0, base:base + LANES, :])
    put(C_BA, jnp.concatenate([wt_ref[0, o_b:o_b + 2 * DN_HEADS, :],
                               jnp.zeros((LANES - 2 * DN_HEADS, kb), F32)], axis=0))
    oo_ref[...] = wo_ref[0].astype(BF16)


def _wpack_call(w_in, w_out):
    _, d, n_in = w_in.shape
    n_out = w_out.shape[2]
    kb = 256
    w_t = jnp.swapaxes(w_in, 1, 2)
    return pl.pallas_call(
        _wpack_body,
        out_shape=[jax.ShapeDtypeStruct((d, W_COLS), BF16), jax.ShapeDtypeStruct((w_out.shape[1], n_out), BF16)],
        grid=(d // kb,),
        in_specs=[pl.BlockSpec((1, n_in, kb), lambda i: (0, 0, i)),
                  pl.BlockSpec((1, kb, n_out), lambda i: (0, i, 0))],
        out_specs=[pl.BlockSpec((kb, W_COLS), lambda i: (i, 0)), pl.BlockSpec((kb, n_out), lambda i: (i, 0))],
        compiler_params=_params(("arbitrary",)),
        name="wpack",
    )(w_t, w_out)


def _proj_body(x_ref, pos_ref, nw_ref, sc_ref, sh_ref, w_ref, cw_ref, alog_ref, dtb_ref, invf_ref,
               sgn_ref, qn_ref, kn_ref, v_ref, zdn_ref, bg_ref, zat_ref,
               aq1_ref, aq4_ref, aq16_ref, ak1_ref, ak4_ref, ak16_ref, av1_ref, av4_ref, av16_ref,
               cbuf_ref, hb_ref, pbuf_ref, p4buf_ref):
    i = pl.program_id(0)
    tm = x_ref.shape[0]
    x = x_ref[...]
    ms = jnp.mean(x * x, axis=-1, keepdims=True)
    hb_ref[...] = ((x * lax.rsqrt(ms + EPS)) * (nw_ref[...] * (1.0 + sc_ref[...])) + sh_ref[...]).astype(BF16)

    def proj(c0, width=PROJ_COLS):
        return jnp.dot(hb_ref[...], w_ref[:, c0:c0 + width], preferred_element_type=F32)

    def silu(v):
        return v * _sigmoid(v)

    @pl.when(i == 0)
    def _():
        cbuf_ref[:, 0:SUBLANES, :] = jnp.zeros((cbuf_ref.shape[0], SUBLANES, LANES), F32)

    dyn0 = jnp.minimum(i, 0)
    for c0 in range(0, 3 * DN_W, PROJ_COLS):
        xg = proj(C_QKV + c0)
        halves = []
        for hh in range(PROJ_COLS // LANES):
            grp = c0 // LANES + hh
            ls = slice(c0 + hh * LANES, c0 + (hh + 1) * LANES)
            xh = xg[:, hh * LANES:(hh + 1) * LANES]
            cbuf_ref[grp, SUBLANES:SUBLANES + tm, :] = xh
            acc = xh * cw_ref[CONV_K - 1:CONV_K, ls]
            for j in range(1, CONV_K):
                acc = acc + cbuf_ref[grp, pl.ds(dyn0 + SUBLANES - j, tm), :] * cw_ref[CONV_K - 1 - j:CONV_K - j, ls]
            cbuf_ref[grp, 0:SUBLANES, :] = xh[tm - SUBLANES:tm]
            halves.append(acc)
        conv = jnp.concatenate(halves, axis=1)
        y = silu(conv)
        kind, off = c0 // DN_W, c0 % DN_W
        for hh in range(PROJ_COLS // DN_HD):
            yh = y[:, hh * DN_HD:(hh + 1) * DN_HD]
            dst = slice(off + hh * DN_HD, off + (hh + 1) * DN_HD)
            if kind == 2:
                v_ref[:, dst] = yh.astype(BF16)
            else:
                inv = lax.rsqrt(jnp.sum(yh * yh, axis=-1, keepdims=True) + EPS)
                if kind == 0:
                    qn_ref[:, dst] = (yh * (inv * (DN_HD ** -0.5))).astype(BF16)
                else:
                    kn_ref[:, dst] = (yh * inv).astype(BF16)

    for c0 in range(0, DN_W, PROJ_COLS):
        zdn_ref[:, c0:c0 + PROJ_COLS] = silu(proj(C_Z + c0)).astype(BF16)
        zat_ref[:, c0:c0 + PROJ_COLS] = silu(proj(C_AZ + c0)).astype(BF16)

    ba = proj(C_BA, LANES)
    lane = _iota(ba.shape, 1)
    a = ba + dtb_ref[...]
    softplus = jnp.maximum(a, 0.0) + jnp.log(1.0 + jnp.exp(-jnp.abs(a)))
    g = -jnp.exp(alog_ref[...]) * softplus
    bg_ref[...] = jnp.where(lane < DN_HEADS, _sigmoid(ba), jnp.where(lane < 2 * DN_HEADS, g, 0.0))

    ang = invf_ref[...] * pos_ref[...].astype(F32)
    reps = LANES // (AT_HD // 2)
    cos = jnp.concatenate([jnp.cos(ang)] * reps, axis=0).T
    sin = jnp.concatenate([jnp.sin(ang)] * reps, axis=0).T * sgn_ref[...]
    qscale = (AT_HD ** -0.5) * math.log2(math.e)
    d4, d16 = DILATIONS[1], DILATIONS[2]
    n4, n16 = tm // d4, tm // d16

    def emit(val, gp, nat_ref, p4_ref, p16_ref):
        sl = slice(gp * LANES, (gp + 1) * LANES)
        nat_ref[:, sl] = val.astype(BF16)
        pbuf_ref[gp] = val
        for c in range(d4):
            seg = pbuf_ref[gp, pl.ds(c, n4, stride=d4), :]
            p4_ref[0, c, :, sl] = seg.astype(BF16)
            p4buf_ref[gp, c * n4:(c + 1) * n4, :] = seg
        for r in range(d16):
            seg = p4buf_ref[gp, pl.ds((r % d4) * n4 + r // d4, n16, stride=d4), :]
            p16_ref[0, r, :, sl] = seg.astype(BF16)

    def rope(v):
        return v * cos + pltpu.roll(v, LANES // 2, axis=1) * sin

    for c0 in range(0, AT_W, PROJ_COLS):
        aq = proj(C_AQ + c0)
        ak = proj(C_AK + c0)
        av = proj(C_AV + c0)
        for hh in range(PROJ_COLS // LANES):
            gp = c0 // LANES + hh
            ls = slice(hh * LANES, (hh + 1) * LANES)
            emit(rope(aq[:, ls]) * qscale, gp, aq1_ref, aq4_ref, aq16_ref)
            emit(rope(ak[:, ls]), gp, ak1_ref, ak4_ref, ak16_ref)
            emit(av[:, ls], gp, av1_ref, av4_ref, av16_ref)


def _proj_call(x2, pos_row, nw, scale, shift, w_all, cw, alog_row, dtb_row, invf, sgn, tm):
    s, d = x2.shape
    row = lambda n: pl.BlockSpec((1, n), lambda i: (0, 0))
    full = lambda a: pl.BlockSpec(a.shape, lambda i: (0, 0))
    tile = lambda n: pl.BlockSpec((tm, n), lambda i: (i, 0))
    spb = SUPER // tm
    res_shape = lambda dil: jax.ShapeDtypeStruct((s // SUPER, dil, SUPER // dil, AT_W), BF16)
    res_spec = lambda dil: pl.BlockSpec((1, dil, tm // dil, AT_W), lambda i: (i // spb, 0, i % spb, 0))
    bf = lambda n: jax.ShapeDtypeStruct((s, n), BF16)
    outs = [bf(DN_W)] * 4 + [jax.ShapeDtypeStruct((s, LANES), F32), bf(AT_W)] + \
           [bf(AT_W), res_shape(DILATIONS[1]), res_shape(DILATIONS[2])] * 3
    out_specs = [tile(DN_W)] * 4 + [tile(LANES), tile(AT_W)] + \
                [tile(AT_W), res_spec(DILATIONS[1]), res_spec(DILATIONS[2])] * 3
    return pl.pallas_call(
        _proj_body,
        out_shape=outs,
        grid=(s // tm,),
        in_specs=[tile(d), pl.BlockSpec((1, tm), lambda i: (0, i)), row(d), row(d), row(d),
                  pl.BlockSpec(w_all.shape, lambda i: (0, 0), pipeline_mode=pl.Buffered(1)),
                  full(cw), row(LANES), row(LANES), full(invf), row(LANES)],
        out_specs=out_specs,
        scratch_shapes=[pltpu.VMEM((3 * DN_W // LANES, SUBLANES + tm, LANES), F32), pltpu.VMEM((tm, d), BF16),
                        pltpu.VMEM((AT_W // LANES, tm, LANES), F32),
                        pltpu.VMEM((AT_W // LANES, tm, LANES), F32)],
        compiler_params=_params(("arbitrary",)),
        name="proj",
    )(x2, pos_row, nw, scale, shift, w_all, cw, alog_row, dtb_row, invf, sgn)


def _dn_body(q_ref, k_ref, v_ref, bg_ref, z_ref, nw_ref, o_ref, w_s, bt_s, op_s, gl_s, st_ref):
    step = pl.program_id(0)
    cur = step % 2
    prv = 1 - cur

    @pl.when(step == 0)
    def _():
        st_ref[...] = jnp.zeros(st_ref.shape, F32)
        w_s[1] = jnp.zeros(w_s.shape[1:], BF16)
        bt_s[1] = jnp.zeros(bt_s.shape[1:], BF16)
        op_s[1] = jnp.zeros(op_s.shape[1:], BF16)
        gl_s[1] = jnp.zeros(gl_s.shape[1:], F32)

    rows = q_ref.shape[0]
    nchunk = rows // CHUNK
    hsl = lambda hd: slice(hd * DN_HD, (hd + 1) * DN_HD)

    def scan_chunk(c):
        rs = slice(c * CHUNK, (c + 1) * CHUNK)
        for hd in range(DN_HEADS):
            st = st_ref[hd]
            r = jnp.dot(st.astype(BF16), w_s[prv, c, hd], preferred_element_type=F32)
            st_ref[hd] = gl_s[prv, c, hd:hd + 1, :] * st - r[:, 0:DN_HD] + bt_s[prv, c, hd].astype(F32)
            half = (hd % 2) * CHUNK
            o = r[:, DN_HD:2 * DN_HD].T[half:half + CHUNK] + op_s[prv, rs, hsl(hd)].astype(F32)
            ms = jnp.mean(o * o, axis=-1, keepdims=True)
            o_ref[rs, hsl(hd)] = (o * lax.rsqrt(ms + EPS) * nw_ref[...] * z_ref[rs, hsl(hd)].astype(F32)).astype(BF16)

    pending = list(range(nchunk))

    def tick():
        if pending:
            scan_chunk(pending.pop(0))

    wide = DN_HEADS * CHUNK
    bg = bg_ref[...]
    cs_rows = min(rows, MXU_COLS)
    ri = _iota((cs_rows, cs_rows), 0)
    ci = _iota((cs_rows, cs_rows), 1)
    tril_blk = jnp.where((ri // CHUNK == ci // CHUNK) & (ri >= ci), 1.0, 0.0).astype(BF16)
    g_hi = bg.astype(BF16)
    g_r1 = bg - g_hi.astype(F32)
    g_mid = g_r1.astype(BF16)
    g_lo = (g_r1 - g_mid.astype(F32)).astype(BF16)
    g3 = jnp.concatenate([g_hi, g_mid, g_lo], axis=1)
    csum = jnp.concatenate([jnp.dot(tril_blk, g3[r0:r0 + cs_rows], preferred_element_type=F32)
                            for r0 in range(0, rows, cs_rows)], axis=0)
    gcum = csum[:, 0:LANES] + csum[:, LANES:2 * LANES] + csum[:, 2 * LANES:3 * LANES]
    bfull = jnp.concatenate([jnp.broadcast_to(bg[:, hd:hd + 1], (rows, DN_HD)) for hd in range(DN_HEADS)], axis=1)
    gfull = jnp.concatenate([jnp.broadcast_to(gcum[:, DN_HEADS + hd:DN_HEADS + hd + 1], (rows, DN_HD))
                             for hd in range(DN_HEADS)], axis=1)
    eg = jnp.exp(gfull)
    q = q_ref[...].astype(F32)
    k = k_ref[...].astype(F32)
    kb = k * bfull
    vb = v_ref[...].astype(F32) * bfull
    q_dec = q * eg
    kbg = kb * eg

    r64 = _iota((CHUNK, wide), 0)
    j64 = _iota((CHUNK, wide), 1) % CHUNK
    tril = r64 >= j64
    strict = r64 > j64
    eye = jnp.where(r64 == j64, 1.0, 0.0).astype(F32)
    low_half = _iota((CHUNK, LANES), 1) < CHUNK
    bd64 = _iota((wide, wide), 0) // CHUNK == _iota((wide, wide), 1) // CHUNK
    bd_k = _iota((wide, DN_W), 0) // CHUNK == _iota((wide, DN_W), 1) // DN_HD

    def block_diag(m):
        return jnp.where(bd64, jnp.concatenate([m.astype(BF16)] * DN_HEADS, axis=0), jnp.zeros((wide, wide), BF16))

    rsl = lambda c: slice(c * CHUNK, (c + 1) * CHUNK)
    chunks = range(nchunk)
    g_last = [gfull[c * CHUNK + CHUNK - 1:(c + 1) * CHUNK, :] for c in chunks]
    k_dec = [k[rsl(c)] * jnp.exp(g_last[c] - gfull[rsl(c)]) for c in chunks]
    g_row = [jnp.concatenate([gcum[rsl(c)], gcum[rsl(c)]], axis=0).T for c in chunks]
    dec = []
    for c in chunks:
        halves = []
        for pr in range(DN_HEADS // 2):
            ha, hb = 2 * pr, 2 * pr + 1
            col = jnp.where(low_half, gfull[rsl(c), hsl(ha)], gfull[rsl(c), hsl(hb)])
            row = jnp.where(low_half[0:1], g_row[c][DN_HEADS + ha:DN_HEADS + ha + 1, :],
                            g_row[c][DN_HEADS + hb:DN_HEADS + hb + 1, :])
            halves.append(col - row)
        dec.append(jnp.exp(jnp.where(tril, jnp.concatenate(halves, axis=1), NEG)))
    kq = [lax.dot_general(jnp.concatenate([kb[rsl(c)].astype(BF16), q_ref[rsl(c), :]], axis=0),
                          jnp.where(bd_k, jnp.concatenate([k_ref[rsl(c), :]] * DN_HEADS, axis=0),
                                    jnp.zeros((wide, DN_W), BF16)),
                          NT_DIMS, preferred_element_type=F32) for c in chunks]
    tick()
    a_low = [jnp.where(strict, m[0:CHUNK] * d, 0.0) for m, d in zip(kq, dec)]
    attn = [jnp.where(tril, m[CHUNK:2 * CHUNK] * d, 0.0) for m, d in zip(kq, dec)]
    bmat = [-a for a in a_low]
    ymat = [eye + b for b in bmat]
    bmat = [jnp.dot(b.astype(BF16), block_diag(b), preferred_element_type=F32) for b in bmat]
    tick()
    for _ in range(4):
        prod = [jnp.dot(jnp.concatenate([y, b], axis=0).astype(BF16), block_diag(b), preferred_element_type=F32)
                for y, b in zip(ymat, bmat)]
        tick()
        ymat = [y + p[0:CHUNK] for y, p in zip(ymat, prod)]
        bmat = [p[CHUNK:2 * CHUNK] for p in prod]
    ymat = [y + jnp.dot(y.astype(BF16), block_diag(b), preferred_element_type=F32) for y, b in zip(ymat, bmat)]
    tick()
    rhs = [jnp.concatenate([jnp.concatenate([kbg[rsl(c), hsl(hd)], vb[rsl(c), hsl(hd)]], axis=1)
                            for hd in range(DN_HEADS)], axis=0).astype(BF16) for c in chunks]
    wu = [jnp.dot(block_diag(y), r, preferred_element_type=F32) for y, r in zip(ymat, rhs)]
    tick()
    wut = [m.T.astype(BF16) for m in wu]
    kdbd = [jnp.where(bd_k, jnp.concatenate([k_dec[c].astype(BF16)] * DN_HEADS, axis=0),
                      jnp.zeros((wide, DN_W), BF16)) for c in chunks]
    pbt = [jnp.dot(a, b, preferred_element_type=F32) for a, b in zip(wut, kdbd)]
    tick()
    aw = [jnp.dot(block_diag(a), m.astype(BF16), preferred_element_type=F32) for a, m in zip(attn, wu)]
    qpt = [(jnp.concatenate([q_dec[rsl(c), hsl(hd)] for hd in range(DN_HEADS)], axis=0) - a[:, 0:DN_HD]).T
           .astype(BF16) for c, a in zip(chunks, aw)]
    while pending:
        tick()
    for c in chunks:
        for hd in range(DN_HEADS):
            pair = slice((hd // 2) * LANES, (hd // 2 + 1) * LANES)
            w_s[cur, c, hd] = jnp.concatenate([pbt[c][0:DN_HD, hsl(hd)].astype(BF16), qpt[c][:, pair]], axis=1)
            bt_s[cur, c, hd] = pbt[c][DN_HD:2 * DN_HD, hsl(hd)].astype(BF16)
            op_s[cur, rsl(c), hsl(hd)] = aw[c][hd * CHUNK:(hd + 1) * CHUNK, DN_HD:2 * DN_HD].astype(BF16)
        gl_rows = [jnp.exp(g_last[c][:, hsl(hd)]) for hd in range(DN_HEADS)]
        gl_rows.append(jnp.zeros((SUBLANES - DN_HEADS, DN_HD), F32))
        gl_s[cur, c] = jnp.concatenate(gl_rows, axis=0)


def _dn_call(qn, kn, v, bg, zdn, nw_row, rows):
    s = qn.shape[0]
    nblk = s // rows
    ncb = rows // CHUNK
    cur = lambda n: pl.BlockSpec((rows, n), lambda i: (jnp.minimum(i, nblk - 1), 0))
    prv = lambda n: pl.BlockSpec((rows, n), lambda i: (jnp.maximum(i - 1, 0), 0))
    return pl.pallas_call(
        _dn_body,
        out_shape=jax.ShapeDtypeStruct((s, DN_W), BF16),
        grid=(nblk + 1,),
        in_specs=[cur(DN_W), cur(DN_W), cur(DN_W), cur(LANES), prv(DN_W), pl.BlockSpec((1, DN_HD), lambda i: (0, 0))],
        out_specs=prv(DN_W),
        scratch_shapes=[pltpu.VMEM((2, ncb, DN_HEADS, DN_HD, 2 * DN_HD), BF16),
                        pltpu.VMEM((2, ncb, DN_HEADS, DN_HD, DN_HD), BF16),
                        pltpu.VMEM((2, rows, DN_W), BF16),
                        pltpu.VMEM((2, ncb, SUBLANES, DN_HD), F32),
                        pltpu.VMEM((DN_HEADS, DN_HD, DN_HD), F32)],
        compiler_params=_params(("arbitrary",)),
        name="deltanet",
    )(qn, kn, v, bg, zdn, nw_row)


def _attn_body(q1, q4, q16, k1, k4, k16, v1, v4, v16, kp1, vp1, kp4a, kp4b, kp4c, kp4d,
               vp4a, vp4b, vp4c, vp4d, kp16, vp16, z_ref, anw_ref, o_ref, m_s, l_s, a_s, m_n, l_n, a_n):
    d4, d16 = DILATIONS[1], DILATIONS[2]
    seg4 = SUPER // d4
    kp4 = (kp4a, kp4b, kp4c, kp4d)
    vp4 = (vp4a, vp4b, vp4c, vp4d)
    kj = _iota((2 * Q_BLOCK, Q_BLOCK), 0)
    qi = _iota((2 * Q_BLOCK, Q_BLOCK), 1)
    rel = Q_BLOCK + qi - kj
    band = (rel >= 0) & (rel <= W_SUB)
    bias_in = jnp.where(band, 0.0, NEG).astype(BF16)
    bias_first = jnp.where(band & (kj >= Q_BLOCK), 0.0, NEG).astype(BF16)
    bias_edge = jnp.where(pl.program_id(0) == 0, bias_first, bias_in)
    one_hot = jnp.where(kj % Q_BLOCK == qi, 1.0, 0.0).astype(BF16)
    lane = _iota((Q_BLOCK, LANES), 1)
    lane_k = _iota((2 * Q_BLOCK, LANES), 1)
    head_a_q = (lane % AT_HD) < (AT_HD // 2)
    head_a = lane < AT_HD
    head_a_k = lane_k < AT_HD
    ro = _iota((4 * Q_BLOCK, LANES), 0)
    ones_rhs = jnp.where((ro < 2 * Q_BLOCK) == (_iota((4 * Q_BLOCK, LANES), 1) < AT_HD), 1.0, 0.0).astype(BF16)
    sr = _iota((2 * LANES, LANES), 0)
    ssq_rhs = jnp.where((sr % LANES) // AT_HD == _iota((2 * LANES, LANES), 1) // AT_HD, 1.0, 0.0).astype(BF16)
    zq = jnp.zeros((Q_BLOCK, LANES), BF16)
    zv = jnp.zeros((2 * Q_BLOCK, LANES), BF16)

    def scores(load):
        q, kb, vb, bias = load()
        q2 = jnp.concatenate([jnp.where(head_a_q, q, zq), jnp.where(head_a_q, zq, q)], axis=0)
        s = lax.dot_general(jnp.concatenate([q2, one_hot], axis=1), jnp.concatenate([kb, bias], axis=1),
                            NT_DIMS, preferred_element_type=F32)
        m = jnp.max(s, axis=-1, keepdims=True)
        return jnp.exp2(s - m).astype(BF16), m, vb

    def values(p, m, vb):
        pc = jnp.concatenate([p[0:Q_BLOCK], p[Q_BLOCK:2 * Q_BLOCK]], axis=1)
        rhs = jnp.concatenate([jnp.concatenate([jnp.where(head_a_k, vb, zv), jnp.where(head_a_k, zv, vb)], axis=0),
                               ones_rhs], axis=1)
        r = jnp.dot(pc, rhs, preferred_element_type=F32)
        return r[:, 0:LANES], r[:, LANES:2 * LANES], jnp.where(head_a, m[0:Q_BLOCK], m[Q_BLOCK:2 * Q_BLOCK])

    def run(loads, finish, post=None):
        n = len(loads)
        pending, done = {}, {}
        for i in range(n + ATT_LAG + 1):
            if i < n:
                pending[i] = scores(loads[i])
            j = i - ATT_LAG
            if 0 <= j < n:
                done[j] = finish(j, *values(*pending.pop(j)))
            if post is not None and 0 <= j - 1 < n:
                post(j - 1, done.pop(j - 1))

    def merged(acc, l, m, m_old, l_old, a_old):
        m_new = jnp.maximum(m_old, m)
        w_old = jnp.exp2(m_old - m_new)
        w_new = jnp.exp2(m - m_new)
        return a_old * w_old + acc * w_new, l_old * w_old + l * w_new, m_new

    def load16(r):
        rs = slice(r * Q_BLOCK, (r + 1) * Q_BLOCK)
        return lambda: (q16[rs, :], jnp.concatenate([kp16[rs, :], k16[rs, :]], axis=0),
                        jnp.concatenate([vp16[rs, :], v16[rs, :]], axis=0), bias_edge)

    def finish16(r, acc, l, m):
        rows = pl.ds((r % d4) * seg4 + r // d4, Q_BLOCK, stride=d4)
        a_s[rows, :] = acc
        l_s[rows, :] = l
        m_s[rows, :] = m

    run([load16(r) for r in range(d16)], finish16)

    def load4(r, b):
        base = r * seg4 + b * Q_BLOCK
        if b == 0:
            return lambda: (q4[base:base + Q_BLOCK, :],
                            jnp.concatenate([kp4[r][...], k4[base:base + Q_BLOCK, :]], axis=0),
                            jnp.concatenate([vp4[r][...], v4[base:base + Q_BLOCK, :]], axis=0), bias_edge)
        return lambda: (q4[base:base + Q_BLOCK, :], k4[base - Q_BLOCK:base + Q_BLOCK, :],
                        v4[base - Q_BLOCK:base + Q_BLOCK, :], bias_in)

    blocks4 = [(r, b) for b in range(seg4 // Q_BLOCK) for r in range(d4)]

    def finish4(j, acc, l, m):
        r, b = blocks4[j]
        rows = slice(r * seg4 + b * Q_BLOCK, r * seg4 + (b + 1) * Q_BLOCK)
        acc, l, m = merged(acc, l, m, m_s[rows, :], l_s[rows, :], a_s[rows, :])
        a_s[rows, :] = acc
        l_s[rows, :] = l
        m_s[rows, :] = m

    run([load4(r, b) for r, b in blocks4], finish4)

    for c in range(d4):
        src = slice(c * seg4, (c + 1) * seg4)
        dst = pl.ds(c, seg4, stride=d4)
        m_n[dst, :] = m_s[src, :]
        l_n[dst, :] = l_s[src, :]
        a_n[dst, :] = a_s[src, :]

    def load1(b):
        rs = slice(b * Q_BLOCK, (b + 1) * Q_BLOCK)
        if b == 0:
            return lambda: (q1[rs, :], jnp.concatenate([kp1[...], k1[rs, :]], axis=0),
                            jnp.concatenate([vp1[...], v1[rs, :]], axis=0), bias_edge)
        ks = slice((b - 1) * Q_BLOCK, (b + 1) * Q_BLOCK)
        return lambda: (q1[rs, :], k1[ks, :], v1[ks, :], bias_in)

    def finish1(b, acc, l, m):
        rs = slice(b * Q_BLOCK, (b + 1) * Q_BLOCK)
        acc, l, _ = merged(acc, l, m, m_n[rs, :], l_n[rs, :], a_n[rs, :])
        o = acc / l
        sq = o * o
        hi = sq.astype(BF16)
        return o, jnp.concatenate([hi, (sq - hi.astype(F32)).astype(BF16)], axis=1)

    def post1(b, res):
        o, sq2 = res
        rs = slice(b * Q_BLOCK, (b + 1) * Q_BLOCK)
        ssq = jnp.dot(sq2, ssq_rhs, preferred_element_type=F32)
        o_ref[rs, :] = (o * lax.rsqrt(ssq * (1.0 / AT_HD) + EPS) * anw_ref[...]
                        * z_ref[rs, :].astype(F32)).astype(BF16)

    run([load1(b) for b in range(SUPER // Q_BLOCK)], finish1, post1)


def _attn_call(aq, ak, av, zat, anw_row):
    s = zat.shape[0]
    d4 = DILATIONS[1]
    per_super = SUPER // Q_BLOCK
    cur = pl.BlockSpec((SUPER, LANES), lambda m, g: (m, g))
    prev16 = pl.BlockSpec((SUPER, LANES), lambda m, g: (jnp.maximum(m - 1, 0), g))
    prev1 = pl.BlockSpec((Q_BLOCK, LANES), lambda m, g: (jnp.maximum(m * per_super - 1, 0), g))
    prev4 = [pl.BlockSpec((Q_BLOCK, LANES),
                          lambda m, g, r=r: (jnp.maximum((m - 1) * per_super + (r + 1) * (per_super // d4) - 1, 0), g))
             for r in range(d4)]
    state = pltpu.VMEM((SUPER, LANES), F32)
    return pl.pallas_call(
        _attn_body,
        out_shape=jax.ShapeDtypeStruct((s, AT_W), BF16),
        grid=(s // SUPER, AT_W // LANES),
        in_specs=[cur] * 9 + [prev1, prev1] + prev4 + prev4 + [prev16, prev16, cur,
                                                               pl.BlockSpec((1, LANES), lambda m, g: (0, g))],
        out_specs=cur,
        scratch_shapes=[state] * 6,
        compiler_params=_params(("arbitrary", "arbitrary")),
        name="attn",
    )(*aq, *ak, *av, ak[0], av[0], *([ak[1]] * d4), *([av[1]] * d4), ak[2], av[2], zat, anw_row)


def _out_body(x_ref, odn_ref, oat_ref, gate_ref, w_ref, fnw_ref, o_ref):
    mixin = jnp.concatenate([odn_ref[...], oat_ref[...]], axis=1)
    mix = jnp.dot(mixin, w_ref[...], preferred_element_type=F32)
    y = x_ref[...] + gate_ref[...] * mix
    ms = jnp.mean(y * y, axis=-1, keepdims=True)
    o_ref[...] = y * lax.rsqrt(ms + EPS) * fnw_ref[...]


def _out_call(x2, odn, oat, gate, w_out, fnw_row, tm):
    s, d = x2.shape
    tile = lambda n: pl.BlockSpec((tm, n), lambda i: (i, 0))
    row = lambda n: pl.BlockSpec((1, n), lambda i: (0, 0))
    return pl.pallas_call(
        _out_body,
        out_shape=jax.ShapeDtypeStruct((s, d), F32),
        grid=(s // tm,),
        in_specs=[tile(d), tile(DN_W), tile(AT_W), row(d), pl.BlockSpec(w_out.shape, lambda i: (0, 0)), row(d)],
        out_specs=tile(d),
        compiler_params=pltpu.CompilerParams(dimension_semantics=("arbitrary",), vmem_limit_bytes=OUT_VMEM_LIMIT),
        name="out",
    )(x2, odn, oat, gate, w_out, fnw_row)


def kernel(x, c, positions, w_mod, b_mod, norm_w, w_in, conv_w, a_log, dt_bias, dn_norm_w, at_norm_w,
           w_out, final_norm_w):
    b, s, d = x.shape
    assert b == 1 and d == D_MODEL and w_mod.shape[0] == 1
    x2 = x.reshape(s, d)

    mod = _mod_call(c.reshape(d, 1), w_mod[0], b_mod[0].reshape(1, 3 * d))
    shift, scale, gate = mod[:, 0:d], mod[:, d:2 * d], mod[:, 2 * d:3 * d]

    w_all, w_out_b = _wpack_call(w_in, w_out)
    pad_heads = lambda v: jnp.pad(v.reshape(1, DN_HEADS), ((0, 0), (DN_HEADS, LANES - 2 * DN_HEADS)))
    half = AT_HD // 2
    invf = (ROPE_THETA ** (-jnp.arange(half, dtype=F32) / half)).reshape(half, 1)
    sgn = jnp.where(jnp.arange(LANES) < LANES // 2, -1.0, 1.0).astype(F32).reshape(1, LANES)

    assert s % SUPER == 0
    outs = _proj_call(x2, positions.reshape(1, s), norm_w[0].reshape(1, d), scale, shift, w_all, conv_w[0],
                      pad_heads(a_log[0]), pad_heads(dt_bias[0]), invf, sgn, tm=512)
    qn, kn, v, zdn, bg, zat = outs[0:6]
    flat = lambda a: a.reshape(s, AT_W)
    aq, ak, av = [[flat(a) for a in outs[j:j + 3]] for j in (6, 9, 12)]

    odn = _dn_call(qn, kn, v, bg, zdn, dn_norm_w[0].reshape(1, DN_HD), rows=512)
    oat = _attn_call(aq, ak, av, zat, jnp.tile(at_norm_w[0], AT_HEADS).reshape(1, AT_W))

    out = _out_call(x2, odn, oat, gate, w_out_b, final_norm_w.reshape(1, d), tm=2048)
    return out.reshape(b, s, d)
```

```python
import math

import jax
import jax.numpy as jnp
from jax import lax
from jax.experimental import pallas as pl
from jax.experimental.pallas import tpu as pltpu

D_MODEL = 1024
DN_HEADS = 4
DN_HD = 128
DN_W = DN_HEADS * DN_HD
AT_HEADS = 8
AT_HD = 64
AT_W = AT_HEADS * AT_HD
CONV_K = 4
CHUNK = 64
DILATIONS = (1, 4, 16)
W_SUB = 128
Q_BLOCK = 128
SUPER = Q_BLOCK * DILATIONS[-1]
ATT_LAG = 2
ROPE_THETA = 10000.0
EPS = 1e-6
NEG = -1e30

LANES = 128
SUBLANES = 8
MXU_COLS = 256
PROJ_COLS = 512
VMEM_LIMIT = 48 * 1024 * 1024

C_QKV, C_Z, C_AQ, C_AK, C_AV, C_AZ, C_BA = 0, 1536, 2048, 2560, 3072, 3584, 4096
W_COLS = C_BA + LANES

F32 = jnp.float32
BF16 = jnp.bfloat16
NT_DIMS = (((1,), (1,)), ((), ()))


def _sigmoid(v):
    return 1.0 / (1.0 + jnp.exp2(v * (-math.log2(math.e))))


def _iota(shape, dim):
    return lax.broadcasted_iota(jnp.int32, shape, dim)


def _params(sem):
    return pltpu.CompilerParams(dimension_semantics=sem, vmem_limit_bytes=VMEM_LIMIT)


def _mod_body(c_ref, w_ref, b_ref, o_ref):
    c = c_ref[...]
    sc = c * _sigmoid(c)
    o_ref[...] = jnp.sum(sc * w_ref[...], axis=0, keepdims=True) + b_ref[...]


def _mod_call(c_col, w_mod, b_mod):
    d, n = w_mod.shape
    tn = 512
    return pl.pallas_call(
        _mod_body,
        out_shape=jax.ShapeDtypeStruct((1, n), F32),
        grid=(n // tn,),
        in_specs=[pl.BlockSpec((d, 1), lambda j: (0, 0)),
                  pl.BlockSpec((d, tn), lambda j: (0, j)),
                  pl.BlockSpec((1, tn), lambda j: (0, j))],
        out_specs=pl.BlockSpec((1, tn), lambda j: (0, j)),
        compiler_params=_params(("arbitrary",)),
        name="mod",
    )(c_col, w_mod, b_mod)


def _wpack_body(wt_ref, wo_ref, o_ref, oo_ref):
    kb = wt_ref.shape[2]
    o_b = 4 * DN_W
    o_q = o_b + 2 * DN_HEADS
    half = AT_HD // 2

    def put(col0, src):
        o_ref[:, col0:col0 + LANES] = src.T.astype(BF16)

    for g in range(o_b // LANES):
        put(C_QKV + g * LANES, wt_ref[0, g * LANES:(g + 1) * LANES, :])
    for g in range(2 * AT_W // LANES):
        base = o_q + g * LANES
        put(C_AQ + g * LANES, jnp.concatenate([wt_ref[0, base:base + half, :],
                                               wt_ref[0, base + 2 * half:base + 3 * half, :],
                                               wt_ref[0, base + half:base + 2 * half, :],
                                               wt_ref[0, base + 3 * half:base + 4 * half, :]], axis=0))
    for g in range(2 * AT_W // LANES):
        base = o_q + 2 * AT_W + g * LANES
        put(C_AV + g * LANES, wt_ref[0, base:base + LANES, :])
    put(C_BA, jnp.concatenate([wt_ref[0, o_b:o_b + 2 * DN_HEADS, :],
                               jnp.zeros((LANES - 2 * DN_HEADS, kb), F32)], axis=0))
    oo_ref[...] = wo_ref[0].astype(BF16)


def _wpack_call(w_in, w_out):
    _, d, n_in = w_in.shape
    n_out = w_out.shape[2]
    kb = 256
    w_t = jnp.swapaxes(w_in, 1, 2)
    return pl.pallas_call(
        _wpack_body,
        out_shape=[jax.ShapeDtypeStruct((d, W_COLS), BF16), jax.ShapeDtypeStruct((w_out.shape[1], n_out), BF16)],
        grid=(d // kb,),
        in_specs=[pl.BlockSpec((1, n_in, kb), lambda i: (0, 0, i)),
                  pl.BlockSpec((1, kb, n_out), lambda i: (0, i, 0))],
        out_specs=[pl.BlockSpec((kb, W_COLS), lambda i: (i, 0)), pl.BlockSpec((kb, n_out), lambda i: (i, 0))],
        compiler_params=_params(("arbitrary",)),
        name="wpack",
    )(w_t, w_out)


def _proj_body(x_ref, pos_ref, nw_ref, sc_ref, sh_ref, w_ref, cw_ref, alog_ref, dtb_ref, invf_ref,
               sgn_ref, qn_ref, kn_ref, v_ref, zdn_ref, bg_ref, zat_ref,
               aq1_ref, aq4_ref, aq16_ref, ak1_ref, ak4_ref, ak16_ref, av1_ref, av4_ref, av16_ref,
               cbuf_ref, hb_ref, pbuf_ref, p4buf_ref):
    i = pl.program_id(0)
    tm = x_ref.shape[0]
    x = x_ref[...]
    ms = jnp.mean(x * x, axis=-1, keepdims=True)
    hb_ref[...] = ((x * lax.rsqrt(ms + EPS)) * (nw_ref[...] * (1.0 + sc_ref[...])) + sh_ref[...]).astype(BF16)

    def proj(c0, width=PROJ_COLS):
        return jnp.dot(hb_ref[...], w_ref[:, c0:c0 + width], preferred_element_type=F32)

    def silu(v):
        return v * _sigmoid(v)

    @pl.when(i == 0)
    def _():
        cbuf_ref[:, 0:SUBLANES, :] = jnp.zeros((cbuf_ref.shape[0], SUBLANES, LANES), F32)

    dyn0 = jnp.minimum(i, 0)
    for c0 in range(0, 3 * DN_W, PROJ_COLS):
        xg = proj(C_QKV + c0)
        halves = []
        for hh in range(PROJ_COLS // LANES):
            grp = c0 // LANES + hh
            ls = slice(c0 + hh * LANES, c0 + (hh + 1) * LANES)
            xh = xg[:, hh * LANES:(hh + 1) * LANES]
            cbuf_ref[grp, SUBLANES:SUBLANES + tm, :] = xh
            acc = xh * cw_ref[CONV_K - 1:CONV_K, ls]
            for j in range(1, CONV_K):
                acc = acc + cbuf_ref[grp, pl.ds(dyn0 + SUBLANES - j, tm), :] * cw_ref[CONV_K - 1 - j:CONV_K - j, ls]
            cbuf_ref[grp, 0:SUBLANES, :] = xh[tm - SUBLANES:tm]
            halves.append(acc)
        conv = jnp.concatenate(halves, axis=1)
        y = silu(conv)
        kind, off = c0 // DN_W, c0 % DN_W
        for hh in range(PROJ_COLS // DN_HD):
            yh = y[:, hh * DN_HD:(hh + 1) * DN_HD]
            dst = slice(off + hh * DN_HD, off + (hh + 1) * DN_HD)
            if kind == 2:
                v_ref[:, dst] = yh.astype(BF16)
            else:
                inv = lax.rsqrt(jnp.sum(yh * yh, axis=-1, keepdims=True) + EPS)
                if kind == 0:
                    qn_ref[:, dst] = (yh * (inv * (DN_HD ** -0.5))).astype(BF16)
                else:
                    kn_ref[:, dst] = (yh * inv).astype(BF16)

    for c0 in range(0, DN_W, PROJ_COLS):
        zdn_ref[:, c0:c0 + PROJ_COLS] = silu(proj(C_Z + c0)).astype(BF16)
        zat_ref[:, c0:c0 + PROJ_COLS] = silu(proj(C_AZ + c0)).astype(BF16)

    ba = proj(C_BA, LANES)
    lane = _iota(ba.shape, 1)
    a = ba + dtb_ref[...]
    softplus = jnp.maximum(a, 0.0) + jnp.log(1.0 + jnp.exp(-jnp.abs(a)))
    g = -jnp.exp(alog_ref[...]) * softplus
    bg_ref[...] = jnp.where(lane < DN_HEADS, _sigmoid(ba), jnp.where(lane < 2 * DN_HEADS, g, 0.0))

    ang = invf_ref[...] * pos_ref[...].astype(F32)
    reps = LANES // (AT_HD // 2)
    cos = jnp.concatenate([jnp.cos(ang)] * reps, axis=0).T
    sin = jnp.concatenate([jnp.sin(ang)] * reps, axis=0).T * sgn_ref[...]
    qscale = (AT_HD ** -0.5) * math.log2(math.e)
    d4, d16 = DILATIONS[1], DILATIONS[2]
    n4, n16 = tm // d4, tm // d16

    def emit(val, gp, nat_ref, p4_ref, p16_ref):
        sl = slice(gp * LANES, (gp + 1) * LANES)
        nat_ref[:, sl] = val.astype(BF16)
        pbuf_ref[gp] = val
        for c in range(d4):
            seg = pbuf_ref[gp, pl.ds(c, n4, stride=d4), :]
            p4_ref[0, c, :, sl] = seg.astype(BF16)
            p4buf_ref[gp, c * n4:(c + 1) * n4, :] = seg
        for r in range(d16):
            seg = p4buf_ref[gp, pl.ds((r % d4) * n4 + r // d4, n16, stride=d4), :]
            p16_ref[0, r, :, sl] = seg.astype(BF16)

    def rope(v):
        return v * cos + pltpu.roll(v, LANES // 2, axis=1) * sin

    for c0 in range(0, AT_W, PROJ_COLS):
        aq = proj(C_AQ + c0)
        ak = proj(C_AK + c0)
        av = proj(C_AV + c0)
        for hh in range(PROJ_COLS // LANES):
            gp = c0 // LANES + hh
            ls = slice(hh * LANES, (hh + 1) * LANES)
            emit(rope(aq[:, ls]) * qscale, gp, aq1_ref, aq4_ref, aq16_ref)
            emit(rope(ak[:, ls]), gp, ak1_ref, ak4_ref, ak16_ref)
            emit(av[:, ls], gp, av1_ref, av4_ref, av16_ref)


def _proj_call(x2, pos_row, nw, scale, shift, w_all, cw, alog_row, dtb_row, invf, sgn, tm):
    s, d = x2.shape
    row = lambda n: pl.BlockSpec((1, n), lambda i: (0, 0))
    full = lambda a: pl.BlockSpec(a.shape, lambda i: (0, 0))
    tile = lambda n: pl.BlockSpec((tm, n), lambda i: (i, 0))
    spb = SUPER // tm
    res_shape = lambda dil: jax.ShapeDtypeStruct((s // SUPER, dil, SUPER // dil, AT_W), BF16)
    res_spec = lambda dil: pl.BlockSpec((1, dil, tm // dil, AT_W), lambda i: (i // spb, 0, i % spb, 0))
    bf = lambda n: jax.ShapeDtypeStruct((s, n), BF16)
    outs = [bf(DN_W)] * 4 + [jax.ShapeDtypeStruct((s, LANES), F32), bf(AT_W)] + \
           [bf(AT_W), res_shape(DILATIONS[1]), res_shape(DILATIONS[2])] * 3
    out_specs = [tile(DN_W)] * 4 + [tile(LANES), tile(AT_W)] + \
                [tile(AT_W), res_spec(DILATIONS[1]), res_spec(DILATIONS[2])] * 3
    return pl.pallas_call(
        _proj_body,
        out_shape=outs,
        grid=(s // tm,),
        in_specs=[tile(d), pl.BlockSpec((1, tm), lambda i: (0, i)), row(d), row(d), row(d),
                  pl.BlockSpec(w_all.shape, lambda i: (0, 0), pipeline_mode=pl.Buffered(1)),
                  full(cw), row(LANES), row(LANES), full(invf), row(LANES)],
        out_specs=out_specs,
        scratch_shapes=[pltpu.VMEM((3 * DN_W // LANES, SUBLANES + tm, LANES), F32), pltpu.VMEM((tm, d), BF16),
                        pltpu.VMEM((AT_W // LANES, tm, LANES), F32),
                        pltpu.VMEM((AT_W // LANES, tm, LANES), F32)],
        compiler_params=_params(("arbitrary",)),
        name="proj",
    )(x2, pos_row, nw, scale, shift, w_all, cw, alog_row, dtb_row, invf, sgn)


def _dn_body(q_ref, k_ref, v_ref, bg_ref, z_ref, nw_ref, o_ref, w_s, bt_s, op_s, gl_s, st_ref):
    step = pl.program_id(0)
    cur = step % 2
    prv = 1 - cur

    @pl.when(step == 0)
    def _():
        st_ref[...] = jnp.zeros(st_ref.shape, F32)
        w_s[1] = jnp.zeros(w_s.shape[1:], BF16)
        bt_s[1] = jnp.zeros(bt_s.shape[1:], BF16)
        op_s[1] = jnp.zeros(op_s.shape[1:], BF16)
        gl_s[1] = jnp.zeros(gl_s.shape[1:], F32)

    rows = q_ref.shape[0]
    nchunk = rows // CHUNK
    hsl = lambda hd: slice(hd * DN_HD, (hd + 1) * DN_HD)

    def scan_chunk(c):
        rs = slice(c * CHUNK, (c + 1) * CHUNK)
        for hd in range(DN_HEADS):
            st = st_ref[hd]
            r = jnp.dot(st.astype(BF16), w_s[prv, c, hd], preferred_element_type=F32)
            st_ref[hd] = gl_s[prv, c, hd:hd + 1, :] * st - r[:, 0:DN_HD] + bt_s[prv, c, hd].astype(F32)
            half = (hd % 2) * CHUNK
            o = r[:, DN_HD:2 * DN_HD].T[half:half + CHUNK] + op_s[prv, rs, hsl(hd)].astype(F32)
            ms = jnp.mean(o * o, axis=-1, keepdims=True)
            o_ref[rs, hsl(hd)] = (o * lax.rsqrt(ms + EPS) * nw_ref[...] * z_ref[rs, hsl(hd)].astype(F32)).astype(BF16)

    pending = list(range(nchunk))

    def tick():
        if pending:
            scan_chunk(pending.pop(0))

    wide = DN_HEADS * CHUNK
    bg = bg_ref[...]
    cs_rows = min(rows, MXU_COLS)
    ri = _iota((cs_rows, cs_rows), 0)
    ci = _iota((cs_rows, cs_rows), 1)
    tril_blk = jnp.where((ri // CHUNK == ci // CHUNK) & (ri >= ci), 1.0, 0.0).astype(BF16)
    g_hi = bg.astype(BF16)
    g_r1 = bg - g_hi.astype(F32)
    g_mid = g_r1.astype(BF16)
    g_lo = (g_r1 - g_mid.astype(F32)).astype(BF16)
    g3 = jnp.concatenate([g_hi, g_mid, g_lo], axis=1)
    csum = jnp.concatenate([jnp.dot(tril_blk, g3[r0:r0 + cs_rows], preferred_element_type=F32)
                            for r0 in range(0, rows, cs_rows)], axis=0)
    gcum = csum[:, 0:LANES] + csum[:, LANES:2 * LANES] + csum[:, 2 * LANES:3 * LANES]
    bfull = jnp.concatenate([jnp.broadcast_to(bg[:, hd:hd + 1], (rows, DN_HD)) for hd in range(DN_HEADS)], axis=1)
    gfull = jnp.concatenate([jnp.broadcast_to(gcum[:, DN_HEADS + hd:DN_HEADS + hd + 1], (rows, DN_HD))
                             for hd in range(DN_HEADS)], axis=1)
    eg = jnp.exp(gfull)
    q = q_ref[...].astype(F32)
    k = k_ref[...].astype(F32)
    kb = k * bfull
    vb = v_ref[...].astype(F32) * bfull
    q_dec = q * eg
    kbg = kb * eg

    r64 = _iota((CHUNK, wide), 0)
    j64 = _iota((CHUNK, wide), 1) % CHUNK
    tril = r64 >= j64
    strict = r64 > j64
    eye = jnp.where(r64 == j64, 1.0, 0.0).astype(F32)
    low_half = _iota((CHUNK, LANES), 1) < CHUNK
    bd64 = _iota((wide, wide), 0) // CHUNK == _iota((wide, wide), 1) // CHUNK
    bd_k = _iota((wide, DN_W), 0) // CHUNK == _iota((wide, DN_W), 1) // DN_HD

    def block_diag(m):
        return jnp.where(bd64, jnp.concatenate([m.astype(BF16)] * DN_HEADS, axis=0), jnp.zeros((wide, wide), BF16))

    rsl = lambda c: slice(c * CHUNK, (c + 1) * CHUNK)
    chunks = range(nchunk)
    g_last = [gfull[c * CHUNK + CHUNK - 1:(c + 1) * CHUNK, :] for c in chunks]
    k_dec = [k[rsl(c)] * jnp.exp(g_last[c] - gfull[rsl(c)]) for c in chunks]
    g_row = [jnp.concatenate([gcum[rsl(c)], gcum[rsl(c)]], axis=0).T for c in chunks]
    dec = []
    for c in chunks:
        halves = []
        for pr in range(DN_HEADS // 2):
            ha, hb = 2 * pr, 2 * pr + 1
            col = jnp.where(low_half, gfull[rsl(c), hsl(ha)], gfull[rsl(c), hsl(hb)])
            row = jnp.where(low_half[0:1], g_row[c][DN_HEADS + ha:DN_HEADS + ha + 1, :],
                            g_row[c][DN_HEADS + hb:DN_HEADS + hb + 1, :])
            halves.append(col - row)
        dec.append(jnp.exp(jnp.where(tril, jnp.concatenate(halves, axis=1), NEG)))
    kq = [lax.dot_general(jnp.concatenate([kb[rsl(c)].astype(BF16), q_ref[rsl(c), :]], axis=0),
                          jnp.where(bd_k, jnp.concatenate([k_ref[rsl(c), :]] * DN_HEADS, axis=0),
                                    jnp.zeros((wide, DN_W), BF16)),
                          NT_DIMS, preferred_element_type=F32) for c in chunks]
    tick()
    a_low = [jnp.where(strict, m[0:CHUNK] * d, 0.0) for m, d in zip(kq, dec)]
    attn = [jnp.where(tril, m[CHUNK:2 * CHUNK] * d, 0.0) for m, d in zip(kq, dec)]
    bmat = [-a for a in a_low]
    ymat = [eye + b for b in bmat]
    bmat = [jnp.dot(b.astype(BF16), block_diag(b), preferred_element_type=F32) for b in bmat]
    tick()
    for _ in range(4):
        prod = [jnp.dot(jnp.concatenate([y, b], axis=0).astype(BF16), block_diag(b), preferred_element_type=F32)
                for y, b in zip(ymat, bmat)]
        tick()
        ymat = [y + p[0:CHUNK] for y, p in zip(ymat, prod)]
        bmat = [p[CHUNK:2 * CHUNK] for p in prod]
    ymat = [y + jnp.dot(y.astype(BF16), block_diag(b), preferred_element_type=F32) for y, b in zip(ymat, bmat)]
    tick()
    rhs = [jnp.concatenate([jnp.concatenate([kbg[rsl(c), hsl(hd)], vb[rsl(c), hsl(hd)]], axis=1)
                            for hd in range(DN_HEADS)], axis=0).astype(BF16) for c in chunks]
    wu = [jnp.dot(block_diag(y), r, preferred_element_type=F32) for y, r in zip(ymat, rhs)]
    tick()
    wut = [m.T.astype(BF16) for m in wu]
    kdbd = [jnp.where(bd_k, jnp.concatenate([k_dec[c].astype(BF16)] * DN_HEADS, axis=0),
                      jnp.zeros((wide, DN_W), BF16)) for c in chunks]
    pbt = [jnp.dot(a, b, preferred_element_type=F32) for a, b in zip(wut, kdbd)]
    tick()
    aw = [jnp.dot(block_diag(a), m.astype(BF16), preferred_element_type=F32) for a, m in zip(attn, wu)]
    qpt = [(jnp.concatenate([q_dec[rsl(c), hsl(hd)] for hd in range(DN_HEADS)], axis=0) - a[:, 0:DN_HD]).T
           .astype(BF16) for c, a in zip(chunks, aw)]
    while pending:
        tick()
    for c in chunks:
        for hd in range(DN_HEADS):
            pair = slice((hd // 2) * LANES, (hd // 2 + 1) * LANES)
            w_s[cur, c, hd] = jnp.concatenate([pbt[c][0:DN_HD, hsl(hd)].astype(BF16), qpt[c][:, pair]], axis=1)
            bt_s[cur, c, hd] = pbt[c][DN_HD:2 * DN_HD, hsl(hd)].astype(BF16)
            op_s[cur, rsl(c), hsl(hd)] = aw[c][hd * CHUNK:(hd + 1) * CHUNK, DN_HD:2 * DN_HD].astype(BF16)
        gl_rows = [jnp.exp(g_last[c][:, hsl(hd)]) for hd in range(DN_HEADS)]
        gl_rows.append(jnp.zeros((SUBLANES - DN_HEADS, DN_HD), F32))
        gl_s[cur, c] = jnp.concatenate(gl_rows, axis=0)


def _dn_call(qn, kn, v, bg, zdn, nw_row, rows):
    s = qn.shape[0]
    nblk = s // rows
    ncb = rows // CHUNK
    cur = lambda n: pl.BlockSpec((rows, n), lambda i: (jnp.minimum(i, nblk - 1), 0))
    prv = lambda n: pl.BlockSpec((rows, n), lambda i: (jnp.maximum(i - 1, 0), 0))
    return pl.pallas_call(
        _dn_body,
        out_shape=jax.ShapeDtypeStruct((s, DN_W), BF16),
        grid=(nblk + 1,),
        in_specs=[cur(DN_W), cur(DN_W), cur(DN_W), cur(LANES), prv(DN_W), pl.BlockSpec((1, DN_HD), lambda i: (0, 0))],
        out_specs=prv(DN_W),
        scratch_shapes=[pltpu.VMEM((2, ncb, DN_HEADS, DN_HD, 2 * DN_HD), BF16),
                        pltpu.VMEM((2, ncb, DN_HEADS, DN_HD, DN_HD), BF16),
                        pltpu.VMEM((2, rows, DN_W), BF16),
                        pltpu.VMEM((2, ncb, SUBLANES, DN_HD), F32),
                        pltpu.VMEM((DN_HEADS, DN_HD, DN_HD), F32)],
        compiler_params=_params(("arbitrary",)),
        name="deltanet",
    )(qn, kn, v, bg, zdn, nw_row)


def _attn_body(q1, q4, q16, k1, k4, k16, v1, v4, v16, kp1, vp1, kp4a, kp4b, kp4c, kp4d,
               vp4a, vp4b, vp4c, vp4d, kp16, vp16, z_ref, anw_ref, o_ref, m_s, l_s, a_s, m_n, l_n, a_n):
    d4, d16 = DILATIONS[1], DILATIONS[2]
    seg4 = SUPER // d4
    kp4 = (kp4a, kp4b, kp4c, kp4d)
    vp4 = (vp4a, vp4b, vp4c, vp4d)
    kj = _iota((2 * Q_BLOCK, Q_BLOCK), 0)
    qi = _iota((2 * Q_BLOCK, Q_BLOCK), 1)
    rel = Q_BLOCK + qi - kj
    band = (rel >= 0) & (rel <= W_SUB)
    bias_in = jnp.where(band, 0.0, NEG).astype(BF16)
    bias_first = jnp.where(band & (kj >= Q_BLOCK), 0.0, NEG).astype(BF16)
    bias_edge = jnp.where(pl.program_id(0) == 0, bias_first, bias_in)
    one_hot = jnp.where(kj % Q_BLOCK == qi, 1.0, 0.0).astype(BF16)
    lane = _iota((Q_BLOCK, LANES), 1)
    lane_k = _iota((2 * Q_BLOCK, LANES), 1)
    head_a_q = (lane % AT_HD) < (AT_HD // 2)
    head_a = lane < AT_HD
    head_a_k = lane_k < AT_HD
    ro = _iota((4 * Q_BLOCK, LANES), 0)
    ones_rhs = jnp.where((ro < 2 * Q_BLOCK) == (_iota((4 * Q_BLOCK, LANES), 1) < AT_HD), 1.0, 0.0).astype(BF16)
    sr = _iota((2 * LANES, LANES), 0)
    ssq_rhs = jnp.where((sr % LANES) // AT_HD == _iota((2 * LANES, LANES), 1) // AT_HD, 1.0, 0.0).astype(BF16)
    zq = jnp.zeros((Q_BLOCK, LANES), BF16)
    zv = jnp.zeros((2 * Q_BLOCK, LANES), BF16)

    def scores(load):
        q, kb, vb, bias = load()
        q2 = jnp.concatenate([jnp.where(head_a_q, q, zq), jnp.where(head_a_q, zq, q)], axis=0)
        s = lax.dot_general(jnp.concatenate([q2, one_hot], axis=1), jnp.concatenate([kb, bias], axis=1),
                            NT_DIMS, preferred_element_type=F32)
        m = jnp.max(s, axis=-1, keepdims=True)
        return jnp.exp2(s - m).astype(BF16), m, vb

    def values(p, m, vb):
        pc = jnp.concatenate([p[0:Q_BLOCK], p[Q_BLOCK:2 * Q_BLOCK]], axis=1)
        rhs = jnp.concatenate([jnp.concatenate([jnp.where(head_a_k, vb, zv), jnp.where(head_a_k, zv, vb)], axis=0),
                               ones_rhs], axis=1)
        r = jnp.dot(pc, rhs, preferred_element_type=F32)
        return r[:, 0:LANES], r[:, LANES:2 * LANES], jnp.where(head_a, m[0:Q_BLOCK], m[Q_BLOCK:2 * Q_BLOCK])

    def run(loads, finish, post=None):
        n = len(loads)
        pending, done = {}, {}
        for i in range(n + ATT_LAG + 1):
            if i < n:
                pending[i] = scores(loads[i])
            j = i - ATT_LAG
            if 0 <= j < n:
                done[j] = finish(j, *values(*pending.pop(j)))
            if post is not None and 0 <= j - 1 < n:
                post(j - 1, done.pop(j - 1))

    def merged(acc, l, m, m_old, l_old, a_old):
        m_new = jnp.maximum(m_old, m)
        w_old = jnp.exp2(m_old - m_new)
        w_new = jnp.exp2(m - m_new)
        return a_old * w_old + acc * w_new, l_old * w_old + l * w_new, m_new

    def load16(r):
        rs = slice(r * Q_BLOCK, (r + 1) * Q_BLOCK)
        return lambda: (q16[rs, :], jnp.concatenate([kp16[rs, :], k16[rs, :]], axis=0),
                        jnp.concatenate([vp16[rs, :], v16[rs, :]], axis=0), bias_edge)

    def finish16(r, acc, l, m):
        rows = pl.ds((r % d4) * seg4 + r // d4, Q_BLOCK, stride=d4)
        a_s[rows, :] = acc
        l_s[rows, :] = l
        m_s[rows, :] = m

    run([load16(r) for r in range(d16)], finish16)

    def load4(r, b):
        base = r * seg4 + b * Q_BLOCK
        if b == 0:
            return lambda: (q4[base:base + Q_BLOCK, :],
                            jnp.concatenate([kp4[r][...], k4[base:base + Q_BLOCK, :]], axis=0),
                            jnp.concatenate([vp4[r][...], v4[base:base + Q_BLOCK, :]], axis=0), bias_edge)
        return lambda: (q4[base:base + Q_BLOCK, :], k4[base - Q_BLOCK:base + Q_BLOCK, :],
                        v4[base - Q_BLOCK:base + Q_BLOCK, :], bias_in)

    blocks4 = [(r, b) for b in range(seg4 // Q_BLOCK) for r in range(d4)]

    def finish4(j, acc, l, m):
        r, b = blocks4[j]
        rows = slice(r * seg4 + b * Q_BLOCK, r * seg4 + (b + 1) * Q_BLOCK)
        acc, l, m = merged(acc, l, m, m_s[rows, :], l_s[rows, :], a_s[rows, :])
        a_s[rows, :] = acc
        l_s[rows, :] = l
        m_s[rows, :] = m

    run([load4(r, b) for r, b in blocks4], finish4)

    for c in range(d4):
        src = slice(c * seg4, (c + 1) * seg4)
        dst = pl.ds(c, seg4, stride=d4)
        m_n[dst, :] = m_s[src, :]
        l_n[dst, :] = l_s[src, :]
        a_n[dst, :] = a_s[src, :]

    def load1(b):
        rs = slice(b * Q_BLOCK, (b + 1) * Q_BLOCK)
        if b == 0:
            return lambda: (q1[rs, :], jnp.concatenate([kp1[...], k1[rs, :]], axis=0),
                            jnp.concatenate([vp1[...], v1[rs, :]], axis=0), bias_edge)
        ks = slice((b - 1) * Q_BLOCK, (b + 1) * Q_BLOCK)
        return lambda: (q1[rs, :], k1[ks, :], v1[ks, :], bias_in)

    def finish1(b, acc, l, m):
        rs = slice(b * Q_BLOCK, (b + 1) * Q_BLOCK)
        acc, l, _ = merged(acc, l, m, m_n[rs, :], l_n[rs, :], a_n[rs, :])
        o = acc / l
        sq = o * o
        hi = sq.astype(BF16)
        return o, jnp.concatenate([hi, (sq - hi.astype(F32)).astype(BF16)], axis=1)

    def post1(b, res):
        o, sq2 = res
        rs = slice(b * Q_BLOCK, (b + 1) * Q_BLOCK)
        ssq = jnp.dot(sq2, ssq_rhs, preferred_element_type=F32)
        o_ref[rs, :] = (o * lax.rsqrt(ssq * (1.0 / AT_HD) + EPS) * anw_ref[...]
                        * z_ref[rs, :].astype(F32)).astype(BF16)

    run([load1(b) for b in range(SUPER // Q_BLOCK)], finish1, post1)


def _attn_call(aq, ak, av, zat, anw_row):
    s = zat.shape[0]
    d4 = DILATIONS[1]
    per_super = SUPER // Q_BLOCK
    cur = pl.BlockSpec((SUPER, LANES), lambda m, g: (m, g))
    prev16 = pl.BlockSpec((SUPER, LANES), lambda m, g: (jnp.maximum(m - 1, 0), g))
    prev1 = pl.BlockSpec((Q_BLOCK, LANES), lambda m, g: (jnp.maximum(m * per_super - 1, 0), g))
    prev4 = [pl.BlockSpec((Q_BLOCK, LANES),
                          lambda m, g, r=r: (jnp.maximum((m - 1) * per_super + (r + 1) * (per_super // d4) - 1, 0), g))
             for r in range(d4)]
    state = pltpu.VMEM((SUPER, LANES), F32)
    return pl.pallas_call(
        _attn_body,
        out_shape=jax.ShapeDtypeStruct((s, AT_W), BF16),
        grid=(s // SUPER, AT_W // LANES),
        in_specs=[cur] * 9 + [prev1, prev1] + prev4 + prev4 + [prev16, prev16, cur,
                                                               pl.BlockSpec((1, LANES), lambda m, g: (0, g))],
        out_specs=cur,
        scratch_shapes=[state] * 6,
        compiler_params=_params(("arbitrary", "arbitrary")),
        name="attn",
    )(*aq, *ak, *av, ak[0], av[0], *([ak[1]] * d4), *([av[1]] * d4), ak[2], av[2], zat, anw_row)


def _out_body(x_ref, odn_ref, oat_ref, gate_ref, w_ref, fnw_ref, o_ref):
    mixin = jnp.concatenate([odn_ref[...], oat_ref[...]], axis=1)
    mix = jnp.dot(mixin, w_ref[...], preferred_element_type=F32)
    y = x_ref[...] + gate_ref[...] * mix
    ms = jnp.mean(y * y, axis=-1, keepdims=True)
    o_ref[...] = y * lax.rsqrt(ms + EPS) * fnw_ref[...]


def _out_call(x2, odn, oat, gate, w_out, fnw_row, tm):
    s, d = x2.shape
    tile = lambda n: pl.BlockSpec((tm, n), lambda i: (i, 0))
    row = lambda n: pl.BlockSpec((1, n), lambda i: (0, 0))
    return pl.pallas_call(
        _out_body,
        out_shape=jax.ShapeDtypeStruct((s, d), F32),
        grid=(s // tm,),
        in_specs=[tile(d), tile(DN_W), tile(AT_W), row(d), pl.BlockSpec(w_out.shape, lambda i: (0, 0)), row(d)],
        out_specs=tile(d),
        compiler_params=_params(("arbitrary",)),
        name="out",
    )(x2, odn, oat, gate, w_out, fnw_row)


def kernel(x, c, positions, w_mod, b_mod, norm_w, w_in, conv_w, a_log, dt_bias, dn_norm_w, at_norm_w,
           w_out, final_norm_w):
    b, s, d = x.shape
    assert b == 1 and d == D_MODEL and w_mod.shape[0] == 1
    x2 = x.reshape(s, d)

    mod = _mod_call(c.reshape(d, 1), w_mod[0], b_mod[0].reshape(1, 3 * d))
    shift, scale, gate = mod[:, 0:d], mod[:, d:2 * d], mod[:, 2 * d:3 * d]

    w_all, w_out_b = _wpack_call(w_in, w_out)
    pad_heads = lambda v: jnp.pad(v.reshape(1, DN_HEADS), ((0, 0), (DN_HEADS, LANES - 2 * DN_HEADS)))
    half = AT_HD // 2
    invf = (ROPE_THETA ** (-jnp.arange(half, dtype=F32) / half)).reshape(half, 1)
    sgn = jnp.where(jnp.arange(LANES) < LANES // 2, -1.0, 1.0).astype(F32).reshape(1, LANES)

    assert s % SUPER == 0
    outs = _proj_call(x2, positions.reshape(1, s), norm_w[0].reshape(1, d), scale, shift, w_all, conv_w[0],
                      pad_heads(a_log[0]), pad_heads(dt_bias[0]), invf, sgn, tm=512)
    qn, kn, v, zdn, bg, zat = outs[0:6]
    flat = lambda a: a.reshape(s, AT_W)
    aq, ak, av = [[flat(a) for a in outs[j:j + 3]] for j in (6, 9, 12)]

    odn = _dn_call(qn, kn, v, bg, zdn, dn_norm_w[0].reshape(1, DN_HD), rows=512)
    oat = _attn_call(aq, ak, av, zat, jnp.tile(at_norm_w[0], AT_HEADS).reshape(1, AT_W))

    out = _out_call(x2, odn, oat, gate, w_out_b, final_norm_w.reshape(1, d), tm=1024)
    return out.reshape(b, s, d)
```

```python
import math

import jax
import jax.numpy as jnp
from jax import lax
from jax.experimental import pallas as pl
from jax.experimental.pallas import tpu as pltpu

D_MODEL = 1024
DN_HEADS = 4
DN_HD = 128
DN_W = DN_HEADS * DN_HD
AT_HEADS = 8
AT_HD = 64
AT_W = AT_HEADS * AT_HD
CONV_K = 4
CHUNK = 64
DILATIONS = (1, 4, 16)
W_SUB = 128
Q_BLOCK = 128
SUPER = Q_BLOCK * DILATIONS[-1]
ATT_PAIRS = 2
ATT_LAG = 2
ROPE_THETA = 10000.0
EPS = 1e-6
NEG = -1e30

LANES = 128
SUBLANES = 8
MXU_COLS = 256
PROJ_COLS = 512
VMEM_LIMIT = 48 * 1024 * 1024

C_QKV, C_Z, C_AQ, C_AK, C_AV, C_AZ, C_BA = 0, 1536, 2048, 2560, 3072, 3584, 4096
W_COLS = C_BA + LANES

F32 = jnp.float32
BF16 = jnp.bfloat16
NT_DIMS = (((1,), (1,)), ((), ()))


def _sigmoid(v):
    return 1.0 / (1.0 + jnp.exp2(v * (-math.log2(math.e))))


def _iota(shape, dim):
    return lax.broadcasted_iota(jnp.int32, shape, dim)


def _params(sem):
    return pltpu.CompilerParams(dimension_semantics=sem, vmem_limit_bytes=VMEM_LIMIT)


def _mod_body(c_ref, w_ref, b_ref, o_ref):
    c = c_ref[...]
    sc = c * _sigmoid(c)
    o_ref[...] = jnp.sum(sc * w_ref[...], axis=0, keepdims=True) + b_ref[...]


def _mod_call(c_col, w_mod, b_mod):
    d, n = w_mod.shape
    tn = 512
    return pl.pallas_call(
        _mod_body,
        out_shape=jax.ShapeDtypeStruct((1, n), F32),
        grid=(n // tn,),
        in_specs=[pl.BlockSpec((d, 1), lambda j: (0, 0)),
                  pl.BlockSpec((d, tn), lambda j: (0, j)),
                  pl.BlockSpec((1, tn), lambda j: (0, j))],
        out_specs=pl.BlockSpec((1, tn), lambda j: (0, j)),
        compiler_params=_params(("arbitrary",)),
        name="mod",
    )(c_col, w_mod, b_mod)


def _wpack_body(wt_ref, wo_ref, o_ref, oo_ref):
    kb = wt_ref.shape[2]
    o_b = 4 * DN_W
    o_q = o_b + 2 * DN_HEADS
    half = AT_HD // 2

    def put(col0, src):
        o_ref[:, col0:col0 + LANES] = src.T.astype(BF16)

    for g in range(o_b // LANES):
        put(C_QKV + g * LANES, wt_ref[0, g * LANES:(g + 1) * LANES, :])
    for g in range(2 * AT_W // LANES):
        base = o_q + g * LANES
        put(C_AQ + g * LANES, jnp.concatenate([wt_ref[0, base:base + half, :],
                                               wt_ref[0, base + 2 * half:base + 3 * half, :],
                                               wt_ref[0, base + half:base + 2 * half, :],
                                               wt_ref[0, base + 3 * half:base + 4 * half, :]], axis=0))
    for g in range(2 * AT_W // LANES):
        base = o_q + 2 * AT_W + g * LANES
        put(C_AV + g * LANES, wt_ref[0, base:base + LANES, :])
    put(C_BA, jnp.concatenate([wt_ref[0, o_b:o_b + 2 * DN_HEADS, :],
                               jnp.zeros((LANES - 2 * DN_HEADS, kb), F32)], axis=0))
    oo_ref[...] = wo_ref[0].astype(BF16)


def _wpack_call(w_in, w_out):
    _, d, n_in = w_in.shape
    n_out = w_out.shape[2]
    kb = 256
    w_t = jnp.swapaxes(w_in, 1, 2)
    return pl.pallas_call(
        _wpack_body,
        out_shape=[jax.ShapeDtypeStruct((d, W_COLS), BF16), jax.ShapeDtypeStruct((w_out.shape[1], n_out), BF16)],
        grid=(d // kb,),
        in_specs=[pl.BlockSpec((1, n_in, kb), lambda i: (0, 0, i)),
                  pl.BlockSpec((1, kb, n_out), lambda i: (0, i, 0))],
        out_specs=[pl.BlockSpec((kb, W_COLS), lambda i: (i, 0)), pl.BlockSpec((kb, n_out), lambda i: (i, 0))],
        compiler_params=_params(("arbitrary",)),
        name="wpack",
    )(w_t, w_out)


def _proj_body(x_ref, pos_ref, nw_ref, sc_ref, sh_ref, w_ref, cw_ref, alog_ref, dtb_ref, invf_ref,
               sgn_ref, qn_ref, kn_ref, v_ref, zdn_ref, bg_ref, zat_ref,
               aq1_ref, aq4_ref, aq16_ref, ak1_ref, ak4_ref, ak16_ref, av1_ref, av4_ref, av16_ref,
               cbuf_ref, hb_ref, pbuf_ref, p4buf_ref):
    i = pl.program_id(0)
    tm = x_ref.shape[0]
    x = x_ref[...]
    ms = jnp.mean(x * x, axis=-1, keepdims=True)
    hb_ref[...] = ((x * lax.rsqrt(ms + EPS)) * (nw_ref[...] * (1.0 + sc_ref[...])) + sh_ref[...]).astype(BF16)

    def proj(c0, width=PROJ_COLS):
        return jnp.dot(hb_ref[...], w_ref[:, c0:c0 + width], preferred_element_type=F32)

    def silu(v):
        return v * _sigmoid(v)

    @pl.when(i == 0)
    def _():
        cbuf_ref[:, 0:SUBLANES, :] = jnp.zeros((cbuf_ref.shape[0], SUBLANES, LANES), F32)

    dyn0 = jnp.minimum(i, 0)
    for c0 in range(0, 3 * DN_W, PROJ_COLS):
        xg = proj(C_QKV + c0)
        halves = []
        for hh in range(PROJ_COLS // LANES):
            grp = c0 // LANES + hh
            ls = slice(c0 + hh * LANES, c0 + (hh + 1) * LANES)
            xh = xg[:, hh * LANES:(hh + 1) * LANES]
            cbuf_ref[grp, SUBLANES:SUBLANES + tm, :] = xh
            acc = xh * cw_ref[CONV_K - 1:CONV_K, ls]
            for j in range(1, CONV_K):
                acc = acc + cbuf_ref[grp, pl.ds(dyn0 + SUBLANES - j, tm), :] * cw_ref[CONV_K - 1 - j:CONV_K - j, ls]
            cbuf_ref[grp, 0:SUBLANES, :] = xh[tm - SUBLANES:tm]
            halves.append(acc)
        conv = jnp.concatenate(halves, axis=1)
        y = silu(conv)
        kind, off = c0 // DN_W, c0 % DN_W
        for hh in range(PROJ_COLS // DN_HD):
            yh = y[:, hh * DN_HD:(hh + 1) * DN_HD]
            dst = slice(off + hh * DN_HD, off + (hh + 1) * DN_HD)
            if kind == 2:
                v_ref[:, dst] = yh.astype(BF16)
            else:
                inv = lax.rsqrt(jnp.sum(yh * yh, axis=-1, keepdims=True) + EPS)
                if kind == 0:
                    qn_ref[:, dst] = (yh * (inv * (DN_HD ** -0.5))).astype(BF16)
                else:
                    kn_ref[:, dst] = (yh * inv).astype(BF16)

    for c0 in range(0, DN_W, PROJ_COLS):
        zdn_ref[:, c0:c0 + PROJ_COLS] = silu(proj(C_Z + c0)).astype(BF16)
        zat_ref[:, c0:c0 + PROJ_COLS] = silu(proj(C_AZ + c0)).astype(BF16)

    ba = proj(C_BA, LANES)
    lane = _iota(ba.shape, 1)
    a = ba + dtb_ref[...]
    softplus = jnp.maximum(a, 0.0) + jnp.log(1.0 + jnp.exp(-jnp.abs(a)))
    g = -jnp.exp(alog_ref[...]) * softplus
    bg_ref[...] = jnp.where(lane < DN_HEADS, _sigmoid(ba), jnp.where(lane < 2 * DN_HEADS, g, 0.0))

    ang = invf_ref[...] * pos_ref[...].astype(F32)
    reps = LANES // (AT_HD // 2)
    cos = jnp.concatenate([jnp.cos(ang)] * reps, axis=0).T
    sin = jnp.concatenate([jnp.sin(ang)] * reps, axis=0).T * sgn_ref[...]
    qscale = (AT_HD ** -0.5) * math.log2(math.e)
    d4, d16 = DILATIONS[1], DILATIONS[2]
    n4, n16 = tm // d4, tm // d16

    def emit(val, gp, nat_ref, p4_ref, p16_ref):
        sl = slice(gp * LANES, (gp + 1) * LANES)
        nat_ref[:, sl] = val.astype(BF16)
        pbuf_ref[gp] = val
        for c in range(d4):
            seg = pbuf_ref[gp, pl.ds(c, n4, stride=d4), :]
            p4_ref[0, c, :, sl] = seg.astype(BF16)
            p4buf_ref[gp, c * n4:(c + 1) * n4, :] = seg
        for r in range(d16):
            seg = p4buf_ref[gp, pl.ds((r % d4) * n4 + r // d4, n16, stride=d4), :]
            p16_ref[0, r, :, sl] = seg.astype(BF16)

    def rope(v):
        return v * cos + pltpu.roll(v, LANES // 2, axis=1) * sin

    for c0 in range(0, AT_W, PROJ_COLS):
        aq = proj(C_AQ + c0)
        ak = proj(C_AK + c0)
        av = proj(C_AV + c0)
        for hh in range(PROJ_COLS // LANES):
            gp = c0 // LANES + hh
            ls = slice(hh * LANES, (hh + 1) * LANES)
            emit(rope(aq[:, ls]) * qscale, gp, aq1_ref, aq4_ref, aq16_ref)
            emit(rope(ak[:, ls]), gp, ak1_ref, ak4_ref, ak16_ref)
            emit(av[:, ls], gp, av1_ref, av4_ref, av16_ref)


def _proj_call(x2, pos_row, nw, scale, shift, w_all, cw, alog_row, dtb_row, invf, sgn, tm):
    s, d = x2.shape
    row = lambda n: pl.BlockSpec((1, n), lambda i: (0, 0))
    full = lambda a: pl.BlockSpec(a.shape, lambda i: (0, 0))
    tile = lambda n: pl.BlockSpec((tm, n), lambda i: (i, 0))
    spb = SUPER // tm
    res_shape = lambda dil: jax.ShapeDtypeStruct((s // SUPER, dil, SUPER // dil, AT_W), BF16)
    res_spec = lambda dil: pl.BlockSpec((1, dil, tm // dil, AT_W), lambda i: (i // spb, 0, i % spb, 0))
    bf = lambda n: jax.ShapeDtypeStruct((s, n), BF16)
    outs = [bf(DN_W)] * 4 + [jax.ShapeDtypeStruct((s, LANES), F32), bf(AT_W)] + \
           [bf(AT_W), res_shape(DILATIONS[1]), res_shape(DILATIONS[2])] * 3
    out_specs = [tile(DN_W)] * 4 + [tile(LANES), tile(AT_W)] + \
                [tile(AT_W), res_spec(DILATIONS[1]), res_spec(DILATIONS[2])] * 3
    return pl.pallas_call(
        _proj_body,
        out_shape=outs,
        grid=(s // tm,),
        in_specs=[tile(d), pl.BlockSpec((1, tm), lambda i: (0, i)), row(d), row(d), row(d),
                  pl.BlockSpec(w_all.shape, lambda i: (0, 0), pipeline_mode=pl.Buffered(1)),
                  full(cw), row(LANES), row(LANES), full(invf), row(LANES)],
        out_specs=out_specs,
        scratch_shapes=[pltpu.VMEM((3 * DN_W // LANES, SUBLANES + tm, LANES), F32), pltpu.VMEM((tm, d), BF16),
                        pltpu.VMEM((AT_W // LANES, tm, LANES), F32),
                        pltpu.VMEM((AT_W // LANES, tm, LANES), F32)],
        compiler_params=_params(("arbitrary",)),
        name="proj",
    )(x2, pos_row, nw, scale, shift, w_all, cw, alog_row, dtb_row, invf, sgn)


def _dn_body(q_ref, k_ref, v_ref, bg_ref, z_ref, nw_ref, o_ref, w_s, bt_s, op_s, gl_s, st_ref):
    step = pl.program_id(0)
    cur = step % 2
    prv = 1 - cur

    @pl.when(step == 0)
    def _():
        st_ref[...] = jnp.zeros(st_ref.shape, F32)
        w_s[1] = jnp.zeros(w_s.shape[1:], BF16)
        bt_s[1] = jnp.zeros(bt_s.shape[1:], BF16)
        op_s[1] = jnp.zeros(op_s.shape[1:], BF16)
        gl_s[1] = jnp.zeros(gl_s.shape[1:], F32)

    rows = q_ref.shape[0]
    nchunk = rows // CHUNK
    hsl = lambda hd: slice(hd * DN_HD, (hd + 1) * DN_HD)

    def scan_chunk(c):
        rs = slice(c * CHUNK, (c + 1) * CHUNK)
        for hd in range(DN_HEADS):
            st = st_ref[hd]
            r = jnp.dot(st.astype(BF16), w_s[prv, c, hd], preferred_element_type=F32)
            st_ref[hd] = gl_s[prv, c, hd:hd + 1, :] * st - r[:, 0:DN_HD] + bt_s[prv, c, hd].astype(F32)
            half = (hd % 2) * CHUNK
            o = r[:, DN_HD:2 * DN_HD].T[half:half + CHUNK] + op_s[prv, rs, hsl(hd)].astype(F32)
            ms = jnp.mean(o * o, axis=-1, keepdims=True)
            o_ref[rs, hsl(hd)] = (o * lax.rsqrt(ms + EPS) * nw_ref[...] * z_ref[rs, hsl(hd)].astype(F32)).astype(BF16)

    pending = list(range(nchunk))

    def tick():
        if pending:
            scan_chunk(pending.pop(0))

    wide = DN_HEADS * CHUNK
    bg = bg_ref[...]
    cs_rows = min(rows, MXU_COLS)
    ri = _iota((cs_rows, cs_rows), 0)
    ci = _iota((cs_rows, cs_rows), 1)
    tril_blk = jnp.where((ri // CHUNK == ci // CHUNK) & (ri >= ci), 1.0, 0.0).astype(BF16)
    g_hi = bg.astype(BF16)
    g_r1 = bg - g_hi.astype(F32)
    g_mid = g_r1.astype(BF16)
    g_lo = (g_r1 - g_mid.astype(F32)).astype(BF16)
    g3 = jnp.concatenate([g_hi, g_mid, g_lo], axis=1)
    csum = jnp.concatenate([jnp.dot(tril_blk, g3[r0:r0 + cs_rows], preferred_element_type=F32)
                            for r0 in range(0, rows, cs_rows)], axis=0)
    gcum = csum[:, 0:LANES] + csum[:, LANES:2 * LANES] + csum[:, 2 * LANES:3 * LANES]
    bfull = jnp.concatenate([jnp.broadcast_to(bg[:, hd:hd + 1], (rows, DN_HD)) for hd in range(DN_HEADS)], axis=1)
    gfull = jnp.concatenate([jnp.broadcast_to(gcum[:, DN_HEADS + hd:DN_HEADS + hd + 1], (rows, DN_HD))
                             for hd in range(DN_HEADS)], axis=1)
    eg = jnp.exp(gfull)
    q = q_ref[...].astype(F32)
    k = k_ref[...].astype(F32)
    kb = k * bfull
    vb = v_ref[...].astype(F32) * bfull
    q_dec = q * eg
    kbg = kb * eg

    r64 = _iota((CHUNK, wide), 0)
    j64 = _iota((CHUNK, wide), 1) % CHUNK
    tril = r64 >= j64
    strict = r64 > j64
    eye = jnp.where(r64 == j64, 1.0, 0.0).astype(F32)
    low_half = _iota((CHUNK, LANES), 1) < CHUNK
    bd64 = _iota((wide, wide), 0) // CHUNK == _iota((wide, wide), 1) // CHUNK
    bd_k = _iota((wide, DN_W), 0) // CHUNK == _iota((wide, DN_W), 1) // DN_HD

    def block_diag(m):
        return jnp.where(bd64, jnp.concatenate([m.astype(BF16)] * DN_HEADS, axis=0), jnp.zeros((wide, wide), BF16))

    rsl = lambda c: slice(c * CHUNK, (c + 1) * CHUNK)
    chunks = range(nchunk)
    g_last = [gfull[c * CHUNK + CHUNK - 1:(c + 1) * CHUNK, :] for c in chunks]
    k_dec = [k[rsl(c)] * jnp.exp(g_last[c] - gfull[rsl(c)]) for c in chunks]
    g_row = [jnp.concatenate([gcum[rsl(c)], gcum[rsl(c)]], axis=0).T for c in chunks]
    dec = []
    for c in chunks:
        halves = []
        for pr in range(DN_HEADS // 2):
            ha, hb = 2 * pr, 2 * pr + 1
            col = jnp.where(low_half, gfull[rsl(c), hsl(ha)], gfull[rsl(c), hsl(hb)])
            row = jnp.where(low_half[0:1], g_row[c][DN_HEADS + ha:DN_HEADS + ha + 1, :],
                            g_row[c][DN_HEADS + hb:DN_HEADS + hb + 1, :])
            halves.append(col - row)
        dec.append(jnp.exp(jnp.where(tril, jnp.concatenate(halves, axis=1), NEG)))
    kq = [lax.dot_general(jnp.concatenate([kb[rsl(c)].astype(BF16), q_ref[rsl(c), :]], axis=0),
                          jnp.where(bd_k, jnp.concatenate([k_ref[rsl(c), :]] * DN_HEADS, axis=0),
                                    jnp.zeros((wide, DN_W), BF16)),
                          NT_DIMS, preferred_element_type=F32) for c in chunks]
    tick()
    a_low = [jnp.where(strict, m[0:CHUNK] * d, 0.0) for m, d in zip(kq, dec)]
    attn = [jnp.where(tril, m[CHUNK:2 * CHUNK] * d, 0.0) for m, d in zip(kq, dec)]
    bmat = [-a for a in a_low]
    ymat = [eye + b for b in bmat]
    bmat = [jnp.dot(b.astype(BF16), block_diag(b), preferred_element_type=F32) for b in bmat]
    tick()
    for _ in range(4):
        prod = [jnp.dot(jnp.concatenate([y, b], axis=0).astype(BF16), block_diag(b), preferred_element_type=F32)
                for y, b in zip(ymat, bmat)]
        tick()
        ymat = [y + p[0:CHUNK] for y, p in zip(ymat, prod)]
        bmat = [p[CHUNK:2 * CHUNK] for p in prod]
    ymat = [y + jnp.dot(y.astype(BF16), block_diag(b), preferred_element_type=F32) for y, b in zip(ymat, bmat)]
    tick()
    rhs = [jnp.concatenate([jnp.concatenate([kbg[rsl(c), hsl(hd)], vb[rsl(c), hsl(hd)]], axis=1)
                            for hd in range(DN_HEADS)], axis=0).astype(BF16) for c in chunks]
    wu = [jnp.dot(block_diag(y), r, preferred_element_type=F32) for y, r in zip(ymat, rhs)]
    tick()
    wut = [m.T.astype(BF16) for m in wu]
    kdbd = [jnp.where(bd_k, jnp.concatenate([k_dec[c].astype(BF16)] * DN_HEADS, axis=0),
                      jnp.zeros((wide, DN_W), BF16)) for c in chunks]
    pbt = [jnp.dot(a, b, preferred_element_type=F32) for a, b in zip(wut, kdbd)]
    tick()
    aw = [jnp.dot(block_diag(a), m.astype(BF16), preferred_element_type=F32) for a, m in zip(attn, wu)]
    qpt = [(jnp.concatenate([q_dec[rsl(c), hsl(hd)] for hd in range(DN_HEADS)], axis=0) - a[:, 0:DN_HD]).T
           .astype(BF16) for c, a in zip(chunks, aw)]
    while pending:
        tick()
    for c in chunks:
        for hd in range(DN_HEADS):
            pair = slice((hd // 2) * LANES, (hd // 2 + 1) * LANES)
            w_s[cur, c, hd] = jnp.concatenate([pbt[c][0:DN_HD, hsl(hd)].astype(BF16), qpt[c][:, pair]], axis=1)
            bt_s[cur, c, hd] = pbt[c][DN_HD:2 * DN_HD, hsl(hd)].astype(BF16)
            op_s[cur, rsl(c), hsl(hd)] = aw[c][hd * CHUNK:(hd + 1) * CHUNK, DN_HD:2 * DN_HD].astype(BF16)
        gl_rows = [jnp.exp(g_last[c][:, hsl(hd)]) for hd in range(DN_HEADS)]
        gl_rows.append(jnp.zeros((SUBLANES - DN_HEADS, DN_HD), F32))
        gl_s[cur, c] = jnp.concatenate(gl_rows, axis=0)


def _dn_call(qn, kn, v, bg, zdn, nw_row, rows):
    s = qn.shape[0]
    nblk = s // rows
    ncb = rows // CHUNK
    cur = lambda n: pl.BlockSpec((rows, n), lambda i: (jnp.minimum(i, nblk - 1), 0))
    prv = lambda n: pl.BlockSpec((rows, n), lambda i: (jnp.maximum(i - 1, 0), 0))
    return pl.pallas_call(
        _dn_body,
        out_shape=jax.ShapeDtypeStruct((s, DN_W), BF16),
        grid=(nblk + 1,),
        in_specs=[cur(DN_W), cur(DN_W), cur(DN_W), cur(LANES), prv(DN_W), pl.BlockSpec((1, DN_HD), lambda i: (0, 0))],
        out_specs=prv(DN_W),
        scratch_shapes=[pltpu.VMEM((2, ncb, DN_HEADS, DN_HD, 2 * DN_HD), BF16),
                        pltpu.VMEM((2, ncb, DN_HEADS, DN_HD, DN_HD), BF16),
                        pltpu.VMEM((2, rows, DN_W), BF16),
                        pltpu.VMEM((2, ncb, SUBLANES, DN_HD), F32),
                        pltpu.VMEM((DN_HEADS, DN_HD, DN_HD), F32)],
        compiler_params=_params(("arbitrary",)),
        name="deltanet",
    )(qn, kn, v, bg, zdn, nw_row)


def _attn_body(*refs):
    ins, scratch = refs[:-6], refs[-6:]
    for p in range(ATT_PAIRS):
        _attn_pair(*[r.at[:, p * LANES:(p + 1) * LANES] for r in ins], *scratch)


def _attn_pair(q1, q4, q16, k1, k4, k16, v1, v4, v16, kp1, vp1, kp4a, kp4b, kp4c, kp4d,
               vp4a, vp4b, vp4c, vp4d, kp16, vp16, z_ref, anw_ref, o_ref, m_s, l_s, a_s, m_n, l_n, a_n):
    d4, d16 = DILATIONS[1], DILATIONS[2]
    seg4 = SUPER // d4
    kp4 = (kp4a, kp4b, kp4c, kp4d)
    vp4 = (vp4a, vp4b, vp4c, vp4d)
    kj = _iota((2 * Q_BLOCK, Q_BLOCK), 0)
    qi = _iota((2 * Q_BLOCK, Q_BLOCK), 1)
    rel = Q_BLOCK + qi - kj
    band = (rel >= 0) & (rel <= W_SUB)
    bias_in = jnp.where(band, 0.0, NEG).astype(BF16)
    bias_first = jnp.where(band & (kj >= Q_BLOCK), 0.0, NEG).astype(BF16)
    bias_edge = jnp.where(pl.program_id(0) == 0, bias_first, bias_in)
    one_hot = jnp.where(kj % Q_BLOCK == qi, 1.0, 0.0).astype(BF16)
    lane = _iota((Q_BLOCK, LANES), 1)
    lane_k = _iota((2 * Q_BLOCK, LANES), 1)
    head_a_q = (lane % AT_HD) < (AT_HD // 2)
    head_a = lane < AT_HD
    head_a_k = lane_k < AT_HD
    ro = _iota((4 * Q_BLOCK, LANES), 0)
    ones_rhs = jnp.where((ro < 2 * Q_BLOCK) == (_iota((4 * Q_BLOCK, LANES), 1) < AT_HD), 1.0, 0.0).astype(BF16)
    sr = _iota((2 * LANES, LANES), 0)
    ssq_rhs = jnp.where((sr % LANES) // AT_HD == _iota((2 * LANES, LANES), 1) // AT_HD, 1.0, 0.0).astype(BF16)
    zq = jnp.zeros((Q_BLOCK, LANES), BF16)
    zv = jnp.zeros((2 * Q_BLOCK, LANES), BF16)

    def scores(load):
        q, kb, vb, bias = load()
        q2 = jnp.concatenate([jnp.where(head_a_q, q, zq), jnp.where(head_a_q, zq, q)], axis=0)
        s = lax.dot_general(jnp.concatenate([q2, one_hot], axis=1), jnp.concatenate([kb, bias], axis=1),
                            NT_DIMS, preferred_element_type=F32)
        m = jnp.max(s, axis=-1, keepdims=True)
        return jnp.exp2(s - m).astype(BF16), m, vb

    def values(p, m, vb):
        pc = jnp.concatenate([p[0:Q_BLOCK], p[Q_BLOCK:2 * Q_BLOCK]], axis=1)
        rhs = jnp.concatenate([jnp.concatenate([jnp.where(head_a_k, vb, zv), jnp.where(head_a_k, zv, vb)], axis=0),
                               ones_rhs], axis=1)
        r = jnp.dot(pc, rhs, preferred_element_type=F32)
        return r[:, 0:LANES], r[:, LANES:2 * LANES], jnp.where(head_a, m[0:Q_BLOCK], m[Q_BLOCK:2 * Q_BLOCK])

    def run(loads, finish, post=None):
        n = len(loads)
        pending, done = {}, {}
        for i in range(n + ATT_LAG + 1):
            if i < n:
                pending[i] = scores(loads[i])
            j = i - ATT_LAG
            if 0 <= j < n:
                done[j] = finish(j, *values(*pending.pop(j)))
            if post is not None and 0 <= j - 1 < n:
                post(j - 1, done.pop(j - 1))

    def merged(acc, l, m, m_old, l_old, a_old):
        m_new = jnp.maximum(m_old, m)
        w_old = jnp.exp2(m_old - m_new)
        w_new = jnp.exp2(m - m_new)
        return a_old * w_old + acc * w_new, l_old * w_old + l * w_new, m_new

    def load16(r):
        rs = slice(r * Q_BLOCK, (r + 1) * Q_BLOCK)
        return lambda: (q16[rs, :], jnp.concatenate([kp16[rs, :], k16[rs, :]], axis=0),
                        jnp.concatenate([vp16[rs, :], v16[rs, :]], axis=0), bias_edge)

    def finish16(r, acc, l, m):
        rows = pl.ds((r % d4) * seg4 + r // d4, Q_BLOCK, stride=d4)
        a_s[rows, :] = acc
        l_s[rows, :] = l
        m_s[rows, :] = m

    run([load16(r) for r in range(d16)], finish16)

    def load4(r, b):
        base = r * seg4 + b * Q_BLOCK
        if b == 0:
            return lambda: (q4[base:base + Q_BLOCK, :],
                            jnp.concatenate([kp4[r][...], k4[base:base + Q_BLOCK, :]], axis=0),
                            jnp.concatenate([vp4[r][...], v4[base:base + Q_BLOCK, :]], axis=0), bias_edge)
        return lambda: (q4[base:base + Q_BLOCK, :], k4[base - Q_BLOCK:base + Q_BLOCK, :],
                        v4[base - Q_BLOCK:base + Q_BLOCK, :], bias_in)

    blocks4 = [(r, b) for b in range(seg4 // Q_BLOCK) for r in range(d4)]

    def finish4(j, acc, l, m):
        r, b = blocks4[j]
        rows = slice(r * seg4 + b * Q_BLOCK, r * seg4 + (b + 1) * Q_BLOCK)
        acc, l, m = merged(acc, l, m, m_s[rows, :], l_s[rows, :], a_s[rows, :])
        a_s[rows, :] = acc
        l_s[rows, :] = l
        m_s[rows, :] = m

    run([load4(r, b) for r, b in blocks4], finish4)

    for c in range(d4):
        src = slice(c * seg4, (c + 1) * seg4)
        dst = pl.ds(c, seg4, stride=d4)
        m_n[dst, :] = m_s[src, :]
        l_n[dst, :] = l_s[src, :]
        a_n[dst, :] = a_s[src, :]

    def load1(b):
        rs = slice(b * Q_BLOCK, (b + 1) * Q_BLOCK)
        if b == 0:
            return lambda: (q1[rs, :], jnp.concatenate([kp1[...], k1[rs, :]], axis=0),
                            jnp.concatenate([vp1[...], v1[rs, :]], axis=0), bias_edge)
        ks = slice((b - 1) * Q_BLOCK, (b + 1) * Q_BLOCK)
        return lambda: (q1[rs, :], k1[ks, :], v1[ks, :], bias_in)

    def finish1(b, acc, l, m):
        rs = slice(b * Q_BLOCK, (b + 1) * Q_BLOCK)
        acc, l, _ = merged(acc, l, m, m_n[rs, :], l_n[rs, :], a_n[rs, :])
        o = acc / l
        sq = o * o
        hi = sq.astype(BF16)
        return o, jnp.concatenate([hi, (sq - hi.astype(F32)).astype(BF16)], axis=1)

    def post1(b, res):
        o, sq2 = res
        rs = slice(b * Q_BLOCK, (b + 1) * Q_BLOCK)
        ssq = jnp.dot(sq2, ssq_rhs, preferred_element_type=F32)
        o_ref[rs, :] = (o * lax.rsqrt(ssq * (1.0 / AT_HD) + EPS) * anw_ref[...]
                        * z_ref[rs, :].astype(F32)).astype(BF16)

    run([load1(b) for b in range(SUPER // Q_BLOCK)], finish1, post1)


def _attn_call(aq, ak, av, zat, anw_row):
    s = zat.shape[0]
    d4 = DILATIONS[1]
    wl = ATT_PAIRS * LANES
    per_super = SUPER // Q_BLOCK
    cur = pl.BlockSpec((SUPER, wl), lambda m, g: (m, g))
    prev16 = pl.BlockSpec((SUPER, wl), lambda m, g: (jnp.maximum(m - 1, 0), g))
    prev1 = pl.BlockSpec((Q_BLOCK, wl), lambda m, g: (jnp.maximum(m * per_super - 1, 0), g))
    prev4 = [pl.BlockSpec((Q_BLOCK, wl),
                          lambda m, g, r=r: (jnp.maximum((m - 1) * per_super + (r + 1) * (per_super // d4) - 1, 0), g))
             for r in range(d4)]
    state = pltpu.VMEM((SUPER, LANES), F32)
    return pl.pallas_call(
        _attn_body,
        out_shape=jax.ShapeDtypeStruct((s, AT_W), BF16),
        grid=(s // SUPER, AT_W // wl),
        in_specs=[cur] * 9 + [prev1, prev1] + prev4 + prev4 + [prev16, prev16, cur,
                                                               pl.BlockSpec((1, wl), lambda m, g: (0, g))],
        out_specs=cur,
        scratch_shapes=[state] * 6,
        compiler_params=_params(("arbitrary", "arbitrary")),
        name="attn",
    )(*aq, *ak, *av, ak[0], av[0], *([ak[1]] * d4), *([av[1]] * d4), ak[2], av[2], zat, anw_row)


def _out_body(x_ref, odn_ref, oat_ref, gate_ref, w_ref, fnw_ref, o_ref):
    mixin = jnp.concatenate([odn_ref[...], oat_ref[...]], axis=1)
    mix = jnp.dot(mixin, w_ref[...], preferred_element_type=F32)
    y = x_ref[...] + gate_ref[...] * mix
    ms = jnp.mean(y * y, axis=-1, keepdims=True)
    o_ref[...] = y * lax.rsqrt(ms + EPS) * fnw_ref[...]


def _out_call(x2, odn, oat, gate, w_out, fnw_row, tm):
    s, d = x2.shape
    tile = lambda n: pl.BlockSpec((tm, n), lambda i: (i, 0))
    row = lambda n: pl.BlockSpec((1, n), lambda i: (0, 0))
    return pl.pallas_call(
        _out_body,
        out_shape=jax.ShapeDtypeStruct((s, d), F32),
        grid=(s // tm,),
        in_specs=[tile(d), tile(DN_W), tile(AT_W), row(d), pl.BlockSpec(w_out.shape, lambda i: (0, 0)), row(d)],
        out_specs=tile(d),
        compiler_params=_params(("arbitrary",)),
        name="out",
    )(x2, odn, oat, gate, w_out, fnw_row)


def kernel(x, c, positions, w_mod, b_mod, norm_w, w_in, conv_w, a_log, dt_bias, dn_norm_w, at_norm_w,
           w_out, final_norm_w):
    b, s, d = x.shape
    assert b == 1 and d == D_MODEL and w_mod.shape[0] == 1
    x2 = x.reshape(s, d)

    mod = _mod_call(c.reshape(d, 1), w_mod[0], b_mod[0].reshape(1, 3 * d))
    shift, scale, gate = mod[:, 0:d], mod[:, d:2 * d], mod[:, 2 * d:3 * d]

    w_all, w_out_b = _wpack_call(w_in, w_out)
    pad_heads = lambda v: jnp.pad(v.reshape(1, DN_HEADS), ((0, 0), (DN_HEADS, LANES - 2 * DN_HEADS)))
    half = AT_HD // 2
    invf = (ROPE_THETA ** (-jnp.arange(half, dtype=F32) / half)).reshape(half, 1)
    sgn = jnp.where(jnp.arange(LANES) < LANES // 2, -1.0, 1.0).astype(F32).reshape(1, LANES)

    assert s % SUPER == 0
    outs = _proj_call(x2, positions.reshape(1, s), norm_w[0].reshape(1, d), scale, shift, w_all, conv_w[0],
                      pad_heads(a_log[0]), pad_heads(dt_bias[0]), invf, sgn, tm=512)
    qn, kn, v, zdn, bg, zat = outs[0:6]
    flat = lambda a: a.reshape(s, AT_W)
    aq, ak, av = [[flat(a) for a in outs[j:j + 3]] for j in (6, 9, 12)]

    odn = _dn_call(qn, kn, v, bg, zdn, dn_norm_w[0].reshape(1, DN_HD), rows=512)
    oat = _attn_call(aq, ak, av, zat, jnp.tile(at_norm_w[0], AT_HEADS).reshape(1, AT_W))

    out = _out_call(x2, odn, oat, gate, w_out_b, final_norm_w.reshape(1, d), tm=1024)
    return out.reshape(b, s, d)
```

```python
import math

import jax
import jax.numpy as jnp
from jax import lax
from jax.experimental import pallas as pl
from jax.experimental.pallas import tpu as pltpu

D_MODEL = 1024
DN_HEADS = 4
DN_HD = 128
DN_W = DN_HEADS * DN_HD
AT_HEADS = 8
AT_HD = 64
AT_W = AT_HEADS * AT_HD
CONV_K = 4
CHUNK = 64
DILATIONS = (1, 4, 16)
W_SUB = 128
Q_BLOCK = 128
SUPER = Q_BLOCK * DILATIONS[-1]
ATT_PAIRS = 2
ATT_LAG = 2
ROPE_THETA = 10000.0
EPS = 1e-6
NEG = -1e30

LANES = 128
SUBLANES = 8
MXU_COLS = 256
PROJ_COLS = 512
VMEM_LIMIT = 48 * 1024 * 1024

C_QKV, C_Z, C_AQ, C_AK, C_AV, C_AZ, C_BA = 0, 1536, 2048, 2560, 3072, 3584, 4096
W_COLS = C_BA + LANES

F32 = jnp.float32
BF16 = jnp.bfloat16
NT_DIMS = (((1,), (1,)), ((), ()))


def _sigmoid(v):
    return 1.0 / (1.0 + jnp.exp2(v * (-math.log2(math.e))))


def _iota(shape, dim):
    return lax.broadcasted_iota(jnp.int32, shape, dim)


def _params(sem):
    return pltpu.CompilerParams(dimension_semantics=sem, vmem_limit_bytes=VMEM_LIMIT)


def _mod_body(c_ref, w_ref, b_ref, o_ref):
    c = c_ref[...]
    sc = c * _sigmoid(c)
    o_ref[...] = jnp.sum(sc * w_ref[...], axis=0, keepdims=True) + b_ref[...]


def _mod_call(c_col, w_mod, b_mod):
    d, n = w_mod.shape
    tn = 512
    return pl.pallas_call(
        _mod_body,
        out_shape=jax.ShapeDtypeStruct((1, n), F32),
        grid=(n // tn,),
        in_specs=[pl.BlockSpec((d, 1), lambda j: (0, 0)),
                  pl.BlockSpec((d, tn), lambda j: (0, j)),
                  pl.BlockSpec((1, tn), lambda j: (0, j))],
        out_specs=pl.BlockSpec((1, tn), lambda j: (0, j)),
        compiler_params=_params(("arbitrary",)),
        name="mod",
    )(c_col, w_mod, b_mod)


def _wpack_body(wt_ref, wo_ref, o_ref, oo_ref):
    kb = wt_ref.shape[2]
    o_b = 4 * DN_W
    o_q = o_b + 2 * DN_HEADS
    half = AT_HD // 2

    def put(col0, src):
        o_ref[:, col0:col0 + LANES] = src.T.astype(BF16)

    for g in range(o_b // LANES):
        put(C_QKV + g * LANES, wt_ref[0, g * LANES:(g + 1) * LANES, :])
    for g in range(2 * AT_W // LANES):
        base = o_q + g * LANES
        put(C_AQ + g * LANES, jnp.concatenate([wt_ref[0, base:base + half, :],
                                               wt_ref[0, base + 2 * half:base + 3 * half, :],
                                               wt_ref[0, base + half:base + 2 * half, :],
                                               wt_ref[0, base + 3 * half:base + 4 * half, :]], axis=0))
    for g in range(2 * AT_W // LANES):
        base = o_q + 2 * AT_W + g * LANES
        put(C_AV + g * LANES, wt_ref[0, base:base + LANES, :])
    put(C_BA, jnp.concatenate([wt_ref[0, o_b:o_b + 2 * DN_HEADS, :],
                               jnp.zeros((LANES - 2 * DN_HEADS, kb), F32)], axis=0))
    oo_ref[...] = wo_ref[0].astype(BF16)


def _wpack_call(w_in, w_out):
    _, d, n_in = w_in.shape
    n_out = w_out.shape[2]
    kb = 256
    w_t = jnp.swapaxes(w_in, 1, 2)
    return pl.pallas_call(
        _wpack_body,
        out_shape=[jax.ShapeDtypeStruct((d, W_COLS), BF16), jax.ShapeDtypeStruct((w_out.shape[1], n_out), BF16)],
        grid=(d // kb,),
        in_specs=[pl.BlockSpec((1, n_in, kb), lambda i: (0, 0, i)),
                  pl.BlockSpec((1, kb, n_out), lambda i: (0, i, 0))],
        out_specs=[pl.BlockSpec((kb, W_COLS), lambda i: (i, 0)), pl.BlockSpec((kb, n_out), lambda i: (i, 0))],
        compiler_params=_params(("arbitrary",)),
        name="wpack",
    )(w_t, w_out)


def _proj_body(x_ref, pos_ref, nw_ref, sc_ref, sh_ref, w_ref, cw_ref, alog_ref, dtb_ref, invf_ref,
               sgn_ref, dn_ref, bg_ref, at_ref, r4_ref, r16_ref, cbuf_ref, hb_ref, pbuf_ref, p4buf_ref):
    qn_ref, kn_ref, v_ref, zdn_ref = [dn_ref.at[:, j * DN_W:(j + 1) * DN_W] for j in range(4)]
    aq1_ref, ak1_ref, av1_ref, zat_ref = [at_ref.at[:, j * AT_W:(j + 1) * AT_W] for j in range(4)]
    aq4_ref, ak4_ref, av4_ref = [r4_ref.at[:, :, :, j * AT_W:(j + 1) * AT_W] for j in range(3)]
    aq16_ref, ak16_ref, av16_ref = [r16_ref.at[:, :, :, j * AT_W:(j + 1) * AT_W] for j in range(3)]
    i = pl.program_id(0)
    tm = x_ref.shape[0]
    x = x_ref[...]
    ms = jnp.mean(x * x, axis=-1, keepdims=True)
    hb_ref[...] = ((x * lax.rsqrt(ms + EPS)) * (nw_ref[...] * (1.0 + sc_ref[...])) + sh_ref[...]).astype(BF16)

    def proj(c0, width=PROJ_COLS):
        return jnp.dot(hb_ref[...], w_ref[:, c0:c0 + width], preferred_element_type=F32)

    def silu(v):
        return v * _sigmoid(v)

    @pl.when(i == 0)
    def _():
        cbuf_ref[:, 0:SUBLANES, :] = jnp.zeros((cbuf_ref.shape[0], SUBLANES, LANES), F32)

    dyn0 = jnp.minimum(i, 0)
    for c0 in range(0, 3 * DN_W, PROJ_COLS):
        xg = proj(C_QKV + c0)
        halves = []
        for hh in range(PROJ_COLS // LANES):
            grp = c0 // LANES + hh
            ls = slice(c0 + hh * LANES, c0 + (hh + 1) * LANES)
            xh = xg[:, hh * LANES:(hh + 1) * LANES]
            cbuf_ref[grp, SUBLANES:SUBLANES + tm, :] = xh
            acc = xh * cw_ref[CONV_K - 1:CONV_K, ls]
            for j in range(1, CONV_K):
                acc = acc + cbuf_ref[grp, pl.ds(dyn0 + SUBLANES - j, tm), :] * cw_ref[CONV_K - 1 - j:CONV_K - j, ls]
            cbuf_ref[grp, 0:SUBLANES, :] = xh[tm - SUBLANES:tm]
            halves.append(acc)
        conv = jnp.concatenate(halves, axis=1)
        y = silu(conv)
        kind, off = c0 // DN_W, c0 % DN_W
        for hh in range(PROJ_COLS // DN_HD):
            yh = y[:, hh * DN_HD:(hh + 1) * DN_HD]
            dst = slice(off + hh * DN_HD, off + (hh + 1) * DN_HD)
            if kind == 2:
                v_ref[:, dst] = yh.astype(BF16)
            else:
                inv = lax.rsqrt(jnp.sum(yh * yh, axis=-1, keepdims=True) + EPS)
                if kind == 0:
                    qn_ref[:, dst] = (yh * (inv * (DN_HD ** -0.5))).astype(BF16)
                else:
                    kn_ref[:, dst] = (yh * inv).astype(BF16)

    for c0 in range(0, DN_W, PROJ_COLS):
        zdn_ref[:, c0:c0 + PROJ_COLS] = silu(proj(C_Z + c0)).astype(BF16)
        zat_ref[:, c0:c0 + PROJ_COLS] = silu(proj(C_AZ + c0)).astype(BF16)

    ba = proj(C_BA, LANES)
    lane = _iota(ba.shape, 1)
    a = ba + dtb_ref[...]
    softplus = jnp.maximum(a, 0.0) + jnp.log(1.0 + jnp.exp(-jnp.abs(a)))
    g = -jnp.exp(alog_ref[...]) * softplus
    bg_ref[...] = jnp.where(lane < DN_HEADS, _sigmoid(ba), jnp.where(lane < 2 * DN_HEADS, g, 0.0))

    ang = invf_ref[...] * pos_ref[...].astype(F32)
    reps = LANES // (AT_HD // 2)
    cos = jnp.concatenate([jnp.cos(ang)] * reps, axis=0).T
    sin = jnp.concatenate([jnp.sin(ang)] * reps, axis=0).T * sgn_ref[...]
    qscale = (AT_HD ** -0.5) * math.log2(math.e)
    d4, d16 = DILATIONS[1], DILATIONS[2]
    n4, n16 = tm // d4, tm // d16

    def emit(val, gp, nat_ref, p4_ref, p16_ref):
        sl = slice(gp * LANES, (gp + 1) * LANES)
        nat_ref[:, sl] = val.astype(BF16)
        pbuf_ref[gp] = val
        for c in range(d4):
            seg = pbuf_ref[gp, pl.ds(c, n4, stride=d4), :]
            p4_ref[0, c, :, sl] = seg.astype(BF16)
            p4buf_ref[gp, c * n4:(c + 1) * n4, :] = seg
        for r in range(d16):
            seg = p4buf_ref[gp, pl.ds((r % d4) * n4 + r // d4, n16, stride=d4), :]
            p16_ref[0, r, :, sl] = seg.astype(BF16)

    def rope(v):
        return v * cos + pltpu.roll(v, LANES // 2, axis=1) * sin

    for c0 in range(0, AT_W, PROJ_COLS):
        aq = proj(C_AQ + c0)
        ak = proj(C_AK + c0)
        av = proj(C_AV + c0)
        for hh in range(PROJ_COLS // LANES):
            gp = c0 // LANES + hh
            ls = slice(hh * LANES, (hh + 1) * LANES)
            emit(rope(aq[:, ls]) * qscale, gp, aq1_ref, aq4_ref, aq16_ref)
            emit(rope(ak[:, ls]), gp, ak1_ref, ak4_ref, ak16_ref)
            emit(av[:, ls], gp, av1_ref, av4_ref, av16_ref)


def _proj_call(x2, pos_row, nw, scale, shift, w_all, cw, alog_row, dtb_row, invf, sgn, tm):
    s, d = x2.shape
    row = lambda n: pl.BlockSpec((1, n), lambda i: (0, 0))
    full = lambda a: pl.BlockSpec(a.shape, lambda i: (0, 0))
    tile = lambda n: pl.BlockSpec((tm, n), lambda i: (i, 0))
    spb = SUPER // tm
    res_shape = lambda dil: jax.ShapeDtypeStruct((s // SUPER, dil, SUPER // dil, 3 * AT_W), BF16)
    res_spec = lambda dil: pl.BlockSpec((1, dil, tm // dil, 3 * AT_W), lambda i: (i // spb, 0, i % spb, 0))
    bf = lambda n: jax.ShapeDtypeStruct((s, n), BF16)
    outs = [bf(4 * DN_W), jax.ShapeDtypeStruct((s, LANES), F32), bf(4 * AT_W),
            res_shape(DILATIONS[1]), res_shape(DILATIONS[2])]
    out_specs = [tile(4 * DN_W), tile(LANES), tile(4 * AT_W), res_spec(DILATIONS[1]), res_spec(DILATIONS[2])]
    return pl.pallas_call(
        _proj_body,
        out_shape=outs,
        grid=(s // tm,),
        in_specs=[tile(d), pl.BlockSpec((1, tm), lambda i: (0, i)), row(d), row(d), row(d),
                  pl.BlockSpec(w_all.shape, lambda i: (0, 0), pipeline_mode=pl.Buffered(1)),
                  full(cw), row(LANES), row(LANES), full(invf), row(LANES)],
        out_specs=out_specs,
        scratch_shapes=[pltpu.VMEM((3 * DN_W // LANES, SUBLANES + tm, LANES), F32), pltpu.VMEM((tm, d), BF16),
                        pltpu.VMEM((AT_W // LANES, tm, LANES), F32),
                        pltpu.VMEM((AT_W // LANES, tm, LANES), F32)],
        compiler_params=_params(("arbitrary",)),
        name="proj",
    )(x2, pos_row, nw, scale, shift, w_all, cw, alog_row, dtb_row, invf, sgn)


def _dn_body(q_ref, k_ref, v_ref, bg_ref, z_ref, nw_ref, o_ref, w_s, bt_s, op_s, gl_s, st_ref):
    step = pl.program_id(0)
    cur = step % 2
    prv = 1 - cur

    @pl.when(step == 0)
    def _():
        st_ref[...] = jnp.zeros(st_ref.shape, F32)
        w_s[1] = jnp.zeros(w_s.shape[1:], BF16)
        bt_s[1] = jnp.zeros(bt_s.shape[1:], BF16)
        op_s[1] = jnp.zeros(op_s.shape[1:], BF16)
        gl_s[1] = jnp.zeros(gl_s.shape[1:], F32)

    rows = q_ref.shape[0]
    nchunk = rows // CHUNK
    hsl = lambda hd: slice(hd * DN_HD, (hd + 1) * DN_HD)

    def scan_chunk(c):
        rs = slice(c * CHUNK, (c + 1) * CHUNK)
        for hd in range(DN_HEADS):
            st = st_ref[hd]
            r = jnp.dot(st.astype(BF16), w_s[prv, c, hd], preferred_element_type=F32)
            st_ref[hd] = gl_s[prv, c, hd:hd + 1, :] * st - r[:, 0:DN_HD] + bt_s[prv, c, hd].astype(F32)
            half = (hd % 2) * CHUNK
            o = r[:, DN_HD:2 * DN_HD].T[half:half + CHUNK] + op_s[prv, rs, hsl(hd)].astype(F32)
            ms = jnp.mean(o * o, axis=-1, keepdims=True)
            o_ref[rs, hsl(hd)] = (o * lax.rsqrt(ms + EPS) * nw_ref[...] * z_ref[rs, hsl(hd)].astype(F32)).astype(BF16)

    pending = list(range(nchunk))

    def tick():
        if pending:
            scan_chunk(pending.pop(0))

    wide = DN_HEADS * CHUNK
    bg = bg_ref[...]
    cs_rows = min(rows, MXU_COLS)
    ri = _iota((cs_rows, cs_rows), 0)
    ci = _iota((cs_rows, cs_rows), 1)
    tril_blk = jnp.where((ri // CHUNK == ci // CHUNK) & (ri >= ci), 1.0, 0.0).astype(BF16)
    g_hi = bg.astype(BF16)
    g_r1 = bg - g_hi.astype(F32)
    g_mid = g_r1.astype(BF16)
    g_lo = (g_r1 - g_mid.astype(F32)).astype(BF16)
    g3 = jnp.concatenate([g_hi, g_mid, g_lo], axis=1)
    csum = jnp.concatenate([jnp.dot(tril_blk, g3[r0:r0 + cs_rows], preferred_element_type=F32)
                            for r0 in range(0, rows, cs_rows)], axis=0)
    gcum = csum[:, 0:LANES] + csum[:, LANES:2 * LANES] + csum[:, 2 * LANES:3 * LANES]
    bfull = jnp.concatenate([jnp.broadcast_to(bg[:, hd:hd + 1], (rows, DN_HD)) for hd in range(DN_HEADS)], axis=1)
    gfull = jnp.concatenate([jnp.broadcast_to(gcum[:, DN_HEADS + hd:DN_HEADS + hd + 1], (rows, DN_HD))
                             for hd in range(DN_HEADS)], axis=1)
    eg = jnp.exp(gfull)
    q = q_ref[...].astype(F32)
    k = k_ref[...].astype(F32)
    kb = k * bfull
    vb = v_ref[...].astype(F32) * bfull
    q_dec = q * eg
    kbg = kb * eg

    r64 = _iota((CHUNK, wide), 0)
    j64 = _iota((CHUNK, wide), 1) % CHUNK
    tril = r64 >= j64
    strict = r64 > j64
    eye = jnp.where(r64 == j64, 1.0, 0.0).astype(F32)
    low_half = _iota((CHUNK, LANES), 1) < CHUNK
    bd64 = _iota((wide, wide), 0) // CHUNK == _iota((wide, wide), 1) // CHUNK
    bd_k = _iota((wide, DN_W), 0) // CHUNK == _iota((wide, DN_W), 1) // DN_HD

    def block_diag(m):
        return jnp.where(bd64, jnp.concatenate([m.astype(BF16)] * DN_HEADS, axis=0), jnp.zeros((wide, wide), BF16))

    rsl = lambda c: slice(c * CHUNK, (c + 1) * CHUNK)
    chunks = range(nchunk)
    g_last = [gfull[c * CHUNK + CHUNK - 1:(c + 1) * CHUNK, :] for c in chunks]
    k_dec = [k[rsl(c)] * jnp.exp(g_last[c] - gfull[rsl(c)]) for c in chunks]
    g_row = [jnp.concatenate([gcum[rsl(c)], gcum[rsl(c)]], axis=0).T for c in chunks]
    dec = []
    for c in chunks:
        halves = []
        for pr in range(DN_HEADS // 2):
            ha, hb = 2 * pr, 2 * pr + 1
            col = jnp.where(low_half, gfull[rsl(c), hsl(ha)], gfull[rsl(c), hsl(hb)])
            row = jnp.where(low_half[0:1], g_row[c][DN_HEADS + ha:DN_HEADS + ha + 1, :],
                            g_row[c][DN_HEADS + hb:DN_HEADS + hb + 1, :])
            halves.append(col - row)
        dec.append(jnp.exp(jnp.where(tril, jnp.concatenate(halves, axis=1), NEG)))
    kq = [lax.dot_general(jnp.concatenate([kb[rsl(c)].astype(BF16), q_ref[rsl(c), :]], axis=0),
                          jnp.where(bd_k, jnp.concatenate([k_ref[rsl(c), :]] * DN_HEADS, axis=0),
                                    jnp.zeros((wide, DN_W), BF16)),
                          NT_DIMS, preferred_element_type=F32) for c in chunks]
    tick()
    a_low = [jnp.where(strict, m[0:CHUNK] * d, 0.0) for m, d in zip(kq, dec)]
    attn = [jnp.where(tril, m[CHUNK:2 * CHUNK] * d, 0.0) for m, d in zip(kq, dec)]
    bmat = [-a for a in a_low]
    ymat = [eye + b for b in bmat]
    bmat = [jnp.dot(b.astype(BF16), block_diag(b), preferred_element_type=F32) for b in bmat]
    tick()
    for _ in range(4):
        prod = [jnp.dot(jnp.concatenate([y, b], axis=0).astype(BF16), block_diag(b), preferred_element_type=F32)
                for y, b in zip(ymat, bmat)]
        tick()
        ymat = [y + p[0:CHUNK] for y, p in zip(ymat, prod)]
        bmat = [p[CHUNK:2 * CHUNK] for p in prod]
    ymat = [y + jnp.dot(y.astype(BF16), block_diag(b), preferred_element_type=F32) for y, b in zip(ymat, bmat)]
    tick()
    rhs = [jnp.concatenate([jnp.concatenate([kbg[rsl(c), hsl(hd)], vb[rsl(c), hsl(hd)]], axis=1)
                            for hd in range(DN_HEADS)], axis=0).astype(BF16) for c in chunks]
    wu = [jnp.dot(block_diag(y), r, preferred_element_type=F32) for y, r in zip(ymat, rhs)]
    tick()
    wut = [m.T.astype(BF16) for m in wu]
    kdbd = [jnp.where(bd_k, jnp.concatenate([k_dec[c].astype(BF16)] * DN_HEADS, axis=0),
                      jnp.zeros((wide, DN_W), BF16)) for c in chunks]
    pbt = [jnp.dot(a, b, preferred_element_type=F32) for a, b in zip(wut, kdbd)]
    tick()
    aw = [jnp.dot(block_diag(a), m.astype(BF16), preferred_element_type=F32) for a, m in zip(attn, wu)]
    qpt = [(jnp.concatenate([q_dec[rsl(c), hsl(hd)] for hd in range(DN_HEADS)], axis=0) - a[:, 0:DN_HD]).T
           .astype(BF16) for c, a in zip(chunks, aw)]
    while pending:
        tick()
    for c in chunks:
        for hd in range(DN_HEADS):
            pair = slice((hd // 2) * LANES, (hd // 2 + 1) * LANES)
            w_s[cur, c, hd] = jnp.concatenate([pbt[c][0:DN_HD, hsl(hd)].astype(BF16), qpt[c][:, pair]], axis=1)
            bt_s[cur, c, hd] = pbt[c][DN_HD:2 * DN_HD, hsl(hd)].astype(BF16)
            op_s[cur, rsl(c), hsl(hd)] = aw[c][hd * CHUNK:(hd + 1) * CHUNK, DN_HD:2 * DN_HD].astype(BF16)
        gl_rows = [jnp.exp(g_last[c][:, hsl(hd)]) for hd in range(DN_HEADS)]
        gl_rows.append(jnp.zeros((SUBLANES - DN_HEADS, DN_HD), F32))
        gl_s[cur, c] = jnp.concatenate(gl_rows, axis=0)


def _dn_call(dn, bg, nw_row, rows):
    s = dn.shape[0]
    nblk = s // rows
    ncb = rows // CHUNK
    cur = lambda n, j=0: pl.BlockSpec((rows, n), lambda i: (jnp.minimum(i, nblk - 1), j))
    prv = lambda n, j=0: pl.BlockSpec((rows, n), lambda i: (jnp.maximum(i - 1, 0), j))
    return pl.pallas_call(
        _dn_body,
        out_shape=jax.ShapeDtypeStruct((s, DN_W), BF16),
        grid=(nblk + 1,),
        in_specs=[cur(DN_W, 0), cur(DN_W, 1), cur(DN_W, 2), cur(LANES), prv(DN_W, 3),
                  pl.BlockSpec((1, DN_HD), lambda i: (0, 0))],
        out_specs=prv(DN_W),
        scratch_shapes=[pltpu.VMEM((2, ncb, DN_HEADS, DN_HD, 2 * DN_HD), BF16),
                        pltpu.VMEM((2, ncb, DN_HEADS, DN_HD, DN_HD), BF16),
                        pltpu.VMEM((2, rows, DN_W), BF16),
                        pltpu.VMEM((2, ncb, SUBLANES, DN_HD), F32),
                        pltpu.VMEM((DN_HEADS, DN_HD, DN_HD), F32)],
        compiler_params=_params(("arbitrary",)),
        name="deltanet",
    )(dn, dn, dn, bg, dn, nw_row)


def _attn_body(*refs):
    ins, scratch = refs[:-6], refs[-6:]
    for p in range(ATT_PAIRS):
        _attn_pair(*[r.at[:, p * LANES:(p + 1) * LANES] for r in ins], *scratch)


def _attn_pair(q1, q4, q16, k1, k4, k16, v1, v4, v16, kp1, vp1, kp4a, kp4b, kp4c, kp4d,
               vp4a, vp4b, vp4c, vp4d, kp16, vp16, z_ref, anw_ref, o_ref, m_s, l_s, a_s, m_n, l_n, a_n):
    d4, d16 = DILATIONS[1], DILATIONS[2]
    seg4 = SUPER // d4
    kp4 = (kp4a, kp4b, kp4c, kp4d)
    vp4 = (vp4a, vp4b, vp4c, vp4d)
    kj = _iota((2 * Q_BLOCK, Q_BLOCK), 0)
    qi = _iota((2 * Q_BLOCK, Q_BLOCK), 1)
    rel = Q_BLOCK + qi - kj
    band = (rel >= 0) & (rel <= W_SUB)
    bias_in = jnp.where(band, 0.0, NEG).astype(BF16)
    bias_first = jnp.where(band & (kj >= Q_BLOCK), 0.0, NEG).astype(BF16)
    bias_edge = jnp.where(pl.program_id(0) == 0, bias_first, bias_in)
    one_hot = jnp.where(kj % Q_BLOCK == qi, 1.0, 0.0).astype(BF16)
    lane = _iota((Q_BLOCK, LANES), 1)
    lane_k = _iota((2 * Q_BLOCK, LANES), 1)
    head_a_q = (lane % AT_HD) < (AT_HD // 2)
    head_a = lane < AT_HD
    head_a_k = lane_k < AT_HD
    ro = _iota((4 * Q_BLOCK, LANES), 0)
    ones_rhs = jnp.where((ro < 2 * Q_BLOCK) == (_iota((4 * Q_BLOCK, LANES), 1) < AT_HD), 1.0, 0.0).astype(BF16)
    sr = _iota((2 * LANES, LANES), 0)
    ssq_rhs = jnp.where((sr % LANES) // AT_HD == _iota((2 * LANES, LANES), 1) // AT_HD, 1.0, 0.0).astype(BF16)
    zq = jnp.zeros((Q_BLOCK, LANES), BF16)
    zv = jnp.zeros((2 * Q_BLOCK, LANES), BF16)

    def scores(load):
        q, kb, vb, bias = load()
        q2 = jnp.concatenate([jnp.where(head_a_q, q, zq), jnp.where(head_a_q, zq, q)], axis=0)
        s = lax.dot_general(jnp.concatenate([q2, one_hot], axis=1), jnp.concatenate([kb, bias], axis=1),
                            NT_DIMS, preferred_element_type=F32)
        m = jnp.max(s, axis=-1, keepdims=True)
        return jnp.exp2(s - m).astype(BF16), m, vb

    def values(p, m, vb):
        pc = jnp.concatenate([p[0:Q_BLOCK], p[Q_BLOCK:2 * Q_BLOCK]], axis=1)
        rhs = jnp.concatenate([jnp.concatenate([jnp.where(head_a_k, vb, zv), jnp.where(head_a_k, zv, vb)], axis=0),
                               ones_rhs], axis=1)
        r = jnp.dot(pc, rhs, preferred_element_type=F32)
        return r[:, 0:LANES], r[:, LANES:2 * LANES], jnp.where(head_a, m[0:Q_BLOCK], m[Q_BLOCK:2 * Q_BLOCK])

    def run(loads, finish, post=None):
        n = len(loads)
        pending, done = {}, {}
        for i in range(n + ATT_LAG + 1):
            if i < n:
                pending[i] = scores(loads[i])
            j = i - ATT_LAG
            if 0 <= j < n:
                done[j] = finish(j, *values(*pending.pop(j)))
            if post is not None and 0 <= j - 1 < n:
                post(j - 1, done.pop(j - 1))

    def merged(acc, l, m, m_old, l_old, a_old):
        m_new = jnp.maximum(m_old, m)
        w_old = jnp.exp2(m_old - m_new)
        w_new = jnp.exp2(m - m_new)
        return a_old * w_old + acc * w_new, l_old * w_old + l * w_new, m_new

    def load16(r):
        rs = slice(r * Q_BLOCK, (r + 1) * Q_BLOCK)
        return lambda: (q16[rs, :], jnp.concatenate([kp16[rs, :], k16[rs, :]], axis=0),
                        jnp.concatenate([vp16[rs, :], v16[rs, :]], axis=0), bias_edge)

    def finish16(r, acc, l, m):
        rows = pl.ds((r % d4) * seg4 + r // d4, Q_BLOCK, stride=d4)
        a_s[rows, :] = acc
        l_s[rows, :] = l
        m_s[rows, :] = m

    run([load16(r) for r in range(d16)], finish16)

    def load4(r, b):
        base = r * seg4 + b * Q_BLOCK
        if b == 0:
            return lambda: (q4[base:base + Q_BLOCK, :],
                            jnp.concatenate([kp4[r][...], k4[base:base + Q_BLOCK, :]], axis=0),
                            jnp.concatenate([vp4[r][...], v4[base:base + Q_BLOCK, :]], axis=0), bias_edge)
        return lambda: (q4[base:base + Q_BLOCK, :], k4[base - Q_BLOCK:base + Q_BLOCK, :],
                        v4[base - Q_BLOCK:base + Q_BLOCK, :], bias_in)

    blocks4 = [(r, b) for b in range(seg4 // Q_BLOCK) for r in range(d4)]

    def finish4(j, acc, l, m):
        r, b = blocks4[j]
        rows = slice(r * seg4 + b * Q_BLOCK, r * seg4 + (b + 1) * Q_BLOCK)
        acc, l, m = merged(acc, l, m, m_s[rows, :], l_s[rows, :], a_s[rows, :])
        a_s[rows, :] = acc
        l_s[rows, :] = l
        m_s[rows, :] = m

    run([load4(r, b) for r, b in blocks4], finish4)

    for c in range(d4):
        src = slice(c * seg4, (c + 1) * seg4)
        dst = pl.ds(c, seg4, stride=d4)
        m_n[dst, :] = m_s[src, :]
        l_n[dst, :] = l_s[src, :]
        a_n[dst, :] = a_s[src, :]

    def load1(b):
        rs = slice(b * Q_BLOCK, (b + 1) * Q_BLOCK)
        if b == 0:
            return lambda: (q1[rs, :], jnp.concatenate([kp1[...], k1[rs, :]], axis=0),
                            jnp.concatenate([vp1[...], v1[rs, :]], axis=0), bias_edge)
        ks = slice((b - 1) * Q_BLOCK, (b + 1) * Q_BLOCK)
        return lambda: (q1[rs, :], k1[ks, :], v1[ks, :], bias_in)

    def finish1(b, acc, l, m):
        rs = slice(b * Q_BLOCK, (b + 1) * Q_BLOCK)
        acc, l, _ = merged(acc, l, m, m_n[rs, :], l_n[rs, :], a_n[rs, :])
        o = acc / l
        sq = o * o
        hi = sq.astype(BF16)
        return o, jnp.concatenate([hi, (sq - hi.astype(F32)).astype(BF16)], axis=1)

    def post1(b, res):
        o, sq2 = res
        rs = slice(b * Q_BLOCK, (b + 1) * Q_BLOCK)
        ssq = jnp.dot(sq2, ssq_rhs, preferred_element_type=F32)
        o_ref[rs, :] = (o * lax.rsqrt(ssq * (1.0 / AT_HD) + EPS) * anw_ref[...]
                        * z_ref[rs, :].astype(F32)).astype(BF16)

    run([load1(b) for b in range(SUPER // Q_BLOCK)], finish1, post1)


def _attn_call(at, r4, r16, anw_row):
    s = at.shape[0]
    d4 = DILATIONS[1]
    wl = ATT_PAIRS * LANES
    nw = AT_W // wl
    per_super = SUPER // Q_BLOCK
    cur = lambda j: pl.BlockSpec((SUPER, wl), lambda m, g: (m, j * nw + g))
    prev16 = lambda j: pl.BlockSpec((SUPER, wl), lambda m, g: (jnp.maximum(m - 1, 0), j * nw + g))
    prev1 = lambda j: pl.BlockSpec((Q_BLOCK, wl), lambda m, g: (jnp.maximum(m * per_super - 1, 0), j * nw + g))
    prev4 = lambda j: [pl.BlockSpec((Q_BLOCK, wl),
                                    lambda m, g, r=r: (jnp.maximum((m - 1) * per_super + (r + 1) * (per_super // d4) - 1, 0),
                                                       j * nw + g)) for r in range(d4)]
    state = pltpu.VMEM((SUPER, LANES), F32)
    q, k, v, z = 0, 1, 2, 3
    return pl.pallas_call(
        _attn_body,
        out_shape=jax.ShapeDtypeStruct((s, AT_W), BF16),
        grid=(s // SUPER, nw),
        in_specs=[cur(q)] * 3 + [cur(k)] * 3 + [cur(v)] * 3 + [prev1(k), prev1(v)] + prev4(k) + prev4(v) +
                 [prev16(k), prev16(v), cur(z), pl.BlockSpec((1, wl), lambda m, g: (0, g))],
        out_specs=pl.BlockSpec((SUPER, wl), lambda m, g: (m, g)),
        scratch_shapes=[state] * 6,
        compiler_params=_params(("arbitrary", "arbitrary")),
        name="attn",
    )(at, r4, r16, at, r4, r16, at, r4, r16, at, at, *([r4] * (2 * d4)), r16, r16, at, anw_row)


def _out_body(x_ref, odn_ref, oat_ref, gate_ref, w_ref, fnw_ref, o_ref):
    mixin = jnp.concatenate([odn_ref[...], oat_ref[...]], axis=1)
    mix = jnp.dot(mixin, w_ref[...], preferred_element_type=F32)
    y = x_ref[...] + gate_ref[...] * mix
    ms = jnp.mean(y * y, axis=-1, keepdims=True)
    o_ref[...] = y * lax.rsqrt(ms + EPS) * fnw_ref[...]


def _out_call(x2, odn, oat, gate, w_out, fnw_row, tm):
    s, d = x2.shape
    tile = lambda n: pl.BlockSpec((tm, n), lambda i: (i, 0))
    row = lambda n: pl.BlockSpec((1, n), lambda i: (0, 0))
    return pl.pallas_call(
        _out_body,
        out_shape=jax.ShapeDtypeStruct((s, d), F32),
        grid=(s // tm,),
        in_specs=[tile(d), tile(DN_W), tile(AT_W), row(d), pl.BlockSpec(w_out.shape, lambda i: (0, 0)), row(d)],
        out_specs=tile(d),
        compiler_params=_params(("arbitrary",)),
        name="out",
    )(x2, odn, oat, gate, w_out, fnw_row)


def kernel(x, c, positions, w_mod, b_mod, norm_w, w_in, conv_w, a_log, dt_bias, dn_norm_w, at_norm_w,
           w_out, final_norm_w):
    b, s, d = x.shape
    assert b == 1 and d == D_MODEL and w_mod.shape[0] == 1
    x2 = x.reshape(s, d)

    mod = _mod_call(c.reshape(d, 1), w_mod[0], b_mod[0].reshape(1, 3 * d))
    shift, scale, gate = mod[:, 0:d], mod[:, d:2 * d], mod[:, 2 * d:3 * d]

    w_all, w_out_b = _wpack_call(w_in, w_out)
    pad_heads = lambda v: jnp.pad(v.reshape(1, DN_HEADS), ((0, 0), (DN_HEADS, LANES - 2 * DN_HEADS)))
    half = AT_HD // 2
    invf = (ROPE_THETA ** (-jnp.arange(half, dtype=F32) / half)).reshape(half, 1)
    sgn = jnp.where(jnp.arange(LANES) < LANES // 2, -1.0, 1.0).astype(F32).reshape(1, LANES)

    assert s % SUPER == 0
    outs = _proj_call(x2, positions.reshape(1, s), norm_w[0].reshape(1, d), scale, shift, w_all, conv_w[0],
                      pad_heads(a_log[0]), pad_heads(dt_bias[0]), invf, sgn, tm=512)
    dn, bg, at, r4, r16 = outs
    odn = _dn_call(dn, bg, dn_norm_w[0].reshape(1, DN_HD), rows=512)
    oat = _attn_call(at, r4.reshape(s, 3 * AT_W), r16.reshape(s, 3 * AT_W),
                     jnp.tile(at_norm_w[0], AT_HEADS).reshape(1, AT_W))

    out = _out_call(x2, odn, oat, gate, w_out_b, final_norm_w.reshape(1, d), tm=1024)
    return out.reshape(b, s, d)
```

```python
import math

import jax
import jax.numpy as jnp
from jax import lax
from jax.experimental import pallas as pl
from jax.experimental.pallas import tpu as pltpu

D_MODEL = 1024
DN_HEADS = 4
DN_HD = 128
DN_W = DN_HEADS * DN_HD
AT_HEADS = 8
AT_HD = 64
AT_W = AT_HEADS * AT_HD
CONV_K = 4
CHUNK = 64
DILATIONS = (1, 4, 16)
W_SUB = 128
Q_BLOCK = 128
SUPER = Q_BLOCK * DILATIONS[-1]
ATT_PAIRS = 2
ATT_LAG = 2
ROPE_THETA = 10000.0
EPS = 1e-6
NEG = -1e30

LANES = 128
SUBLANES = 8
MXU_COLS = 256
PROJ_COLS = 512
VMEM_LIMIT = 48 * 1024 * 1024

C_QKV, C_Z, C_AQ, C_AK, C_AV, C_AZ, C_BA = 0, 1536, 2048, 2560, 3072, 3584, 4096
W_COLS = C_BA + LANES

F32 = jnp.float32
BF16 = jnp.bfloat16
NT_DIMS = (((1,), (1,)), ((), ()))


def _sigmoid(v):
    return 1.0 / (1.0 + jnp.exp2(v * (-math.log2(math.e))))


def _iota(shape, dim):
    return lax.broadcasted_iota(jnp.int32, shape, dim)


def _params(sem):
    return pltpu.CompilerParams(dimension_semantics=sem, vmem_limit_bytes=VMEM_LIMIT)


def _mod_body(c_ref, w_ref, b_ref, o_ref):
    c = c_ref[...]
    sc = c * _sigmoid(c)
    o_ref[...] = jnp.sum(sc * w_ref[...], axis=0, keepdims=True) + b_ref[...]


def _mod_call(c_col, w_mod, b_mod):
    d, n = w_mod.shape
    tn = 1024
    return pl.pallas_call(
        _mod_body,
        out_shape=jax.ShapeDtypeStruct((1, n), F32),
        grid=(n // tn,),
        in_specs=[pl.BlockSpec((d, 1), lambda j: (0, 0)),
                  pl.BlockSpec((d, tn), lambda j: (0, j)),
                  pl.BlockSpec((1, tn), lambda j: (0, j))],
        out_specs=pl.BlockSpec((1, tn), lambda j: (0, j)),
        compiler_params=_params(("arbitrary",)),
        name="mod",
    )(c_col, w_mod, b_mod)


def _wpack_body(wt_ref, wo_ref, o_ref, oo_ref):
    kb = wt_ref.shape[2]
    o_b = 4 * DN_W
    o_q = o_b + 2 * DN_HEADS
    half = AT_HD // 2

    def put(col0, src):
        o_ref[:, col0:col0 + LANES] = src.T.astype(BF16)

    for g in range(o_b // LANES):
        put(C_QKV + g * LANES, wt_ref[0, g * LANES:(g + 1) * LANES, :])
    for g in range(2 * AT_W // LANES):
        base = o_q + g * LANES
        put(C_AQ + g * LANES, jnp.concatenate([wt_ref[0, base:base + half, :],
                                               wt_ref[0, base + 2 * half:base + 3 * half, :],
                                               wt_ref[0, base + half:base + 2 * half, :],
                                               wt_ref[0, base + 3 * half:base + 4 * half, :]], axis=0))
    for g in range(2 * AT_W // LANES):
        base = o_q + 2 * AT_W + g * LANES
        put(C_AV + g * LANES, wt_ref[0, base:base + LANES, :])
    put(C_BA, jnp.concatenate([wt_ref[0, o_b:o_b + 2 * DN_HEADS, :],
                               jnp.zeros((LANES - 2 * DN_HEADS, kb), F32)], axis=0))
    oo_ref[...] = wo_ref[0].astype(BF16)


def _wpack_call(w_in, w_out):
    _, d, n_in = w_in.shape
    n_out = w_out.shape[2]
    kb = 512
    w_t = jnp.swapaxes(w_in, 1, 2)
    return pl.pallas_call(
        _wpack_body,
        out_shape=[jax.ShapeDtypeStruct((d, W_COLS), BF16), jax.ShapeDtypeStruct((w_out.shape[1], n_out), BF16)],
        grid=(d // kb,),
        in_specs=[pl.BlockSpec((1, n_in, kb), lambda i: (0, 0, i)),
                  pl.BlockSpec((1, kb, n_out), lambda i: (0, i, 0))],
        out_specs=[pl.BlockSpec((kb, W_COLS), lambda i: (i, 0)), pl.BlockSpec((kb, n_out), lambda i: (i, 0))],
        compiler_params=_params(("arbitrary",)),
        name="wpack",
    )(w_t, w_out)


def _proj_body(x_ref, pos_ref, nw_ref, sc_ref, sh_ref, w_ref, cw_ref, alog_ref, dtb_ref, invf_ref,
               sgn_ref, qn_ref, kn_ref, v_ref, zdn_ref, bg_ref, zat_ref,
               aq1_ref, aq4_ref, aq16_ref, ak1_ref, ak4_ref, ak16_ref, av1_ref, av4_ref, av16_ref,
               cbuf_ref, hb_ref, pbuf_ref, p4buf_ref):
    i = pl.program_id(0)
    tm = x_ref.shape[0]
    x = x_ref[...]
    ms = jnp.mean(x * x, axis=-1, keepdims=True)
    hb_ref[...] = ((x * lax.rsqrt(ms + EPS)) * (nw_ref[...] * (1.0 + sc_ref[...])) + sh_ref[...]).astype(BF16)

    def proj(c0, width=PROJ_COLS):
        return jnp.dot(hb_ref[...], w_ref[:, c0:c0 + width], preferred_element_type=F32)

    def silu(v):
        return v * _sigmoid(v)

    @pl.when(i == 0)
    def _():
        cbuf_ref[:, 0:SUBLANES, :] = jnp.zeros((cbuf_ref.shape[0], SUBLANES, LANES), F32)

    dyn0 = jnp.minimum(i, 0)
    for c0 in range(0, 3 * DN_W, PROJ_COLS):
        xg = proj(C_QKV + c0)
        halves = []
        for hh in range(PROJ_COLS // LANES):
            grp = c0 // LANES + hh
            ls = slice(c0 + hh * LANES, c0 + (hh + 1) * LANES)
            xh = xg[:, hh * LANES:(hh + 1) * LANES]
            cbuf_ref[grp, SUBLANES:SUBLANES + tm, :] = xh
            acc = xh * cw_ref[CONV_K - 1:CONV_K, ls]
            for j in range(1, CONV_K):
                acc = acc + cbuf_ref[grp, pl.ds(dyn0 + SUBLANES - j, tm), :] * cw_ref[CONV_K - 1 - j:CONV_K - j, ls]
            cbuf_ref[grp, 0:SUBLANES, :] = xh[tm - SUBLANES:tm]
            halves.append(acc)
        conv = jnp.concatenate(halves, axis=1)
        y = silu(conv)
        kind, off = c0 // DN_W, c0 % DN_W
        for hh in range(PROJ_COLS // DN_HD):
            yh = y[:, hh * DN_HD:(hh + 1) * DN_HD]
            dst = slice(off + hh * DN_HD, off + (hh + 1) * DN_HD)
            if kind == 2:
                v_ref[:, dst] = yh.astype(BF16)
            else:
                inv = lax.rsqrt(jnp.sum(yh * yh, axis=-1, keepdims=True) + EPS)
                if kind == 0:
                    qn_ref[:, dst] = (yh * (inv * (DN_HD ** -0.5))).astype(BF16)
                else:
                    kn_ref[:, dst] = (yh * inv).astype(BF16)

    for c0 in range(0, DN_W, PROJ_COLS):
        zdn_ref[:, c0:c0 + PROJ_COLS] = silu(proj(C_Z + c0)).astype(BF16)
        zat_ref[:, c0:c0 + PROJ_COLS] = silu(proj(C_AZ + c0)).astype(BF16)

    ba = proj(C_BA, LANES)
    lane = _iota(ba.shape, 1)
    a = ba + dtb_ref[...]
    softplus = jnp.maximum(a, 0.0) + jnp.log(1.0 + jnp.exp(-jnp.abs(a)))
    g = -jnp.exp(alog_ref[...]) * softplus
    bg_ref[...] = jnp.where(lane < DN_HEADS, _sigmoid(ba), jnp.where(lane < 2 * DN_HEADS, g, 0.0))

    ang = invf_ref[...] * pos_ref[...].astype(F32)
    reps = LANES // (AT_HD // 2)
    cos = jnp.concatenate([jnp.cos(ang)] * reps, axis=0).T
    sin = jnp.concatenate([jnp.sin(ang)] * reps, axis=0).T * sgn_ref[...]
    qscale = (AT_HD ** -0.5) * math.log2(math.e)
    d4, d16 = DILATIONS[1], DILATIONS[2]
    n4, n16 = tm // d4, tm // d16

    def emit(val, gp, nat_ref, p4_ref, p16_ref):
        sl = slice(gp * LANES, (gp + 1) * LANES)
        nat_ref[:, sl] = val.astype(BF16)
        pbuf_ref[gp] = val
        for c in range(d4):
            seg = pbuf_ref[gp, pl.ds(c, n4, stride=d4), :]
            p4_ref[0, c, :, sl] = seg.astype(BF16)
            p4buf_ref[gp, c * n4:(c + 1) * n4, :] = seg
        for r in range(d16):
            seg = p4buf_ref[gp, pl.ds((r % d4) * n4 + r // d4, n16, stride=d4), :]
            p16_ref[0, r, :, sl] = seg.astype(BF16)

    def rope(v):
        return v * cos + pltpu.roll(v, LANES // 2, axis=1) * sin

    for c0 in range(0, AT_W, PROJ_COLS):
        aq = proj(C_AQ + c0)
        ak = proj(C_AK + c0)
        av = proj(C_AV + c0)
        for hh in range(PROJ_COLS // LANES):
            gp = c0 // LANES + hh
            ls = slice(hh * LANES, (hh + 1) * LANES)
            emit(rope(aq[:, ls]) * qscale, gp, aq1_ref, aq4_ref, aq16_ref)
            emit(rope(ak[:, ls]), gp, ak1_ref, ak4_ref, ak16_ref)
            emit(av[:, ls], gp, av1_ref, av4_ref, av16_ref)


def _proj_call(x2, pos_row, nw, scale, shift, w_all, cw, alog_row, dtb_row, invf, sgn, tm):
    s, d = x2.shape
    row = lambda n: pl.BlockSpec((1, n), lambda i: (0, 0))
    full = lambda a: pl.BlockSpec(a.shape, lambda i: (0, 0))
    tile = lambda n: pl.BlockSpec((tm, n), lambda i: (i, 0))
    spb = SUPER // tm
    res_shape = lambda dil: jax.ShapeDtypeStruct((s // SUPER, dil, SUPER // dil, AT_W), BF16)
    res_spec = lambda dil: pl.BlockSpec((1, dil, tm // dil, AT_W), lambda i: (i // spb, 0, i % spb, 0))
    bf = lambda n: jax.ShapeDtypeStruct((s, n), BF16)
    outs = [bf(DN_W)] * 4 + [jax.ShapeDtypeStruct((s, LANES), F32), bf(AT_W)] + \
           [bf(AT_W), res_shape(DILATIONS[1]), res_shape(DILATIONS[2])] * 3
    out_specs = [tile(DN_W)] * 4 + [tile(LANES), tile(AT_W)] + \
                [tile(AT_W), res_spec(DILATIONS[1]), res_spec(DILATIONS[2])] * 3
    return pl.pallas_call(
        _proj_body,
        out_shape=outs,
        grid=(s // tm,),
        in_specs=[tile(d), pl.BlockSpec((1, tm), lambda i: (0, i)), row(d), row(d), row(d),
                  pl.BlockSpec(w_all.shape, lambda i: (0, 0), pipeline_mode=pl.Buffered(1)),
                  full(cw), row(LANES), row(LANES), full(invf), row(LANES)],
        out_specs=out_specs,
        scratch_shapes=[pltpu.VMEM((3 * DN_W // LANES, SUBLANES + tm, LANES), F32), pltpu.VMEM((tm, d), BF16),
                        pltpu.VMEM((AT_W // LANES, tm, LANES), F32),
                        pltpu.VMEM((AT_W // LANES, tm, LANES), F32)],
        compiler_params=_params(("arbitrary",)),
        name="proj",
    )(x2, pos_row, nw, scale, shift, w_all, cw, alog_row, dtb_row, invf, sgn)


def _dn_body(q_ref, k_ref, v_ref, bg_ref, z_ref, nw_ref, o_ref, w_s, bt_s, op_s, gl_s, st_ref):
    step = pl.program_id(0)
    cur = step % 2
    prv = 1 - cur

    @pl.when(step == 0)
    def _():
        st_ref[...] = jnp.zeros(st_ref.shape, F32)
        w_s[1] = jnp.zeros(w_s.shape[1:], BF16)
        bt_s[1] = jnp.zeros(bt_s.shape[1:], BF16)
        op_s[1] = jnp.zeros(op_s.shape[1:], BF16)
        gl_s[1] = jnp.zeros(gl_s.shape[1:], F32)

    rows = q_ref.shape[0]
    nchunk = rows // CHUNK
    hsl = lambda hd: slice(hd * DN_HD, (hd + 1) * DN_HD)

    def scan_chunk(c):
        rs = slice(c * CHUNK, (c + 1) * CHUNK)
        for hd in range(DN_HEADS):
            st = st_ref[hd]
            r = jnp.dot(st.astype(BF16), w_s[prv, c, hd], preferred_element_type=F32)
            st_ref[hd] = gl_s[prv, c, hd:hd + 1, :] * st - r[:, 0:DN_HD] + bt_s[prv, c, hd].astype(F32)
            half = (hd % 2) * CHUNK
            o = r[:, DN_HD:2 * DN_HD].T[half:half + CHUNK] + op_s[prv, rs, hsl(hd)].astype(F32)
            ms = jnp.mean(o * o, axis=-1, keepdims=True)
            o_ref[rs, hsl(hd)] = (o * lax.rsqrt(ms + EPS) * nw_ref[...] * z_ref[rs, hsl(hd)].astype(F32)).astype(BF16)

    pending = list(range(nchunk))

    def tick():
        if pending:
            scan_chunk(pending.pop(0))

    wide = DN_HEADS * CHUNK
    bg = bg_ref[...]
    cs_rows = min(rows, MXU_COLS)
    ri = _iota((cs_rows, cs_rows), 0)
    ci = _iota((cs_rows, cs_rows), 1)
    tril_blk = jnp.where((ri // CHUNK == ci // CHUNK) & (ri >= ci), 1.0, 0.0).astype(BF16)
    g_hi = bg.astype(BF16)
    g_r1 = bg - g_hi.astype(F32)
    g_mid = g_r1.astype(BF16)
    g_lo = (g_r1 - g_mid.astype(F32)).astype(BF16)
    g3 = jnp.concatenate([g_hi, g_mid, g_lo], axis=1)
    csum = jnp.concatenate([jnp.dot(tril_blk, g3[r0:r0 + cs_rows], preferred_element_type=F32)
                            for r0 in range(0, rows, cs_rows)], axis=0)
    gcum = csum[:, 0:LANES] + csum[:, LANES:2 * LANES] + csum[:, 2 * LANES:3 * LANES]
    bfull = jnp.concatenate([jnp.broadcast_to(bg[:, hd:hd + 1], (rows, DN_HD)) for hd in range(DN_HEADS)], axis=1)
    gfull = jnp.concatenate([jnp.broadcast_to(gcum[:, DN_HEADS + hd:DN_HEADS + hd + 1], (rows, DN_HD))
                             for hd in range(DN_HEADS)], axis=1)
    eg = jnp.exp(gfull)
    q = q_ref[...].astype(F32)
    k = k_ref[...].astype(F32)
    kb = k * bfull
    vb = v_ref[...].astype(F32) * bfull
    q_dec = q * eg
    kbg = kb * eg

    r64 = _iota((CHUNK, wide), 0)
    j64 = _iota((CHUNK, wide), 1) % CHUNK
    tril = r64 >= j64
    strict = r64 > j64
    eye = jnp.where(r64 == j64, 1.0, 0.0).astype(F32)
    low_half = _iota((CHUNK, LANES), 1) < CHUNK
    bd64 = _iota((wide, wide), 0) // CHUNK == _iota((wide, wide), 1) // CHUNK
    bd_k = _iota((wide, DN_W), 0) // CHUNK == _iota((wide, DN_W), 1) // DN_HD

    def block_diag(m):
        return jnp.where(bd64, jnp.concatenate([m.astype(BF16)] * DN_HEADS, axis=0), jnp.zeros((wide, wide), BF16))

    rsl = lambda c: slice(c * CHUNK, (c + 1) * CHUNK)
    chunks = range(nchunk)
    g_last = [gfull[c * CHUNK + CHUNK - 1:(c + 1) * CHUNK, :] for c in chunks]
    k_dec = [k[rsl(c)] * jnp.exp(g_last[c] - gfull[rsl(c)]) for c in chunks]
    g_row = [jnp.concatenate([gcum[rsl(c)], gcum[rsl(c)]], axis=0).T for c in chunks]
    dec = []
    for c in chunks:
        halves = []
        for pr in range(DN_HEADS // 2):
            ha, hb = 2 * pr, 2 * pr + 1
            col = jnp.where(low_half, gfull[rsl(c), hsl(ha)], gfull[rsl(c), hsl(hb)])
            row = jnp.where(low_half[0:1], g_row[c][DN_HEADS + ha:DN_HEADS + ha + 1, :],
                            g_row[c][DN_HEADS + hb:DN_HEADS + hb + 1, :])
            halves.append(col - row)
        dec.append(jnp.exp(jnp.where(tril, jnp.concatenate(halves, axis=1), NEG)))
    kq = [lax.dot_general(jnp.concatenate([kb[rsl(c)].astype(BF16), q_ref[rsl(c), :]], axis=0),
                          jnp.where(bd_k, jnp.concatenate([k_ref[rsl(c), :]] * DN_HEADS, axis=0),
                                    jnp.zeros((wide, DN_W), BF16)),
                          NT_DIMS, preferred_element_type=F32) for c in chunks]
    tick()
    a_low = [jnp.where(strict, m[0:CHUNK] * d, 0.0) for m, d in zip(kq, dec)]
    attn = [jnp.where(tril, m[CHUNK:2 * CHUNK] * d, 0.0) for m, d in zip(kq, dec)]
    bmat = [-a for a in a_low]
    ymat = [eye + b for b in bmat]
    bmat = [jnp.dot(b.astype(BF16), block_diag(b), preferred_element_type=F32) for b in bmat]
    tick()
    for _ in range(4):
        prod = [jnp.dot(jnp.concatenate([y, b], axis=0).astype(BF16), block_diag(b), preferred_element_type=F32)
                for y, b in zip(ymat, bmat)]
        tick()
        ymat = [y + p[0:CHUNK] for y, p in zip(ymat, prod)]
        bmat = [p[CHUNK:2 * CHUNK] for p in prod]
    ymat = [y + jnp.dot(y.astype(BF16), block_diag(b), preferred_element_type=F32) for y, b in zip(ymat, bmat)]
    tick()
    rhs = [jnp.concatenate([jnp.concatenate([kbg[rsl(c), hsl(hd)], vb[rsl(c), hsl(hd)]], axis=1)
                            for hd in range(DN_HEADS)], axis=0).astype(BF16) for c in chunks]
    wu = [jnp.dot(block_diag(y), r, preferred_element_type=F32) for y, r in zip(ymat, rhs)]
    tick()
    wut = [m.T.astype(BF16) for m in wu]
    kdbd = [jnp.where(bd_k, jnp.concatenate([k_dec[c].astype(BF16)] * DN_HEADS, axis=0),
                      jnp.zeros((wide, DN_W), BF16)) for c in chunks]
    pbt = [jnp.dot(a, b, preferred_element_type=F32) for a, b in zip(wut, kdbd)]
    tick()
    aw = [jnp.dot(block_diag(a), m.astype(BF16), preferred_element_type=F32) for a, m in zip(attn, wu)]
    qpt = [(jnp.concatenate([q_dec[rsl(c), hsl(hd)] for hd in range(DN_HEADS)], axis=0) - a[:, 0:DN_HD]).T
           .astype(BF16) for c, a in zip(chunks, aw)]
    while pending:
        tick()
    for c in chunks:
        for hd in range(DN_HEADS):
            pair = slice((hd // 2) * LANES, (hd // 2 + 1) * LANES)
            w_s[cur, c, hd] = jnp.concatenate([pbt[c][0:DN_HD, hsl(hd)].astype(BF16), qpt[c][:, pair]], axis=1)
            bt_s[cur, c, hd] = pbt[c][DN_HD:2 * DN_HD, hsl(hd)].astype(BF16)
            op_s[cur, rsl(c), hsl(hd)] = aw[c][hd * CHUNK:(hd + 1) * CHUNK, DN_HD:2 * DN_HD].astype(BF16)
        gl_rows = [jnp.exp(g_last[c][:, hsl(hd)]) for hd in range(DN_HEADS)]
        gl_rows.append(jnp.zeros((SUBLANES - DN_HEADS, DN_HD), F32))
        gl_s[cur, c] = jnp.concatenate(gl_rows, axis=0)


def _dn_call(qn, kn, v, bg, zdn, nw_row, rows):
    s = qn.shape[0]
    nblk = s // rows
    ncb = rows // CHUNK
    cur = lambda n: pl.BlockSpec((rows, n), lambda i: (jnp.minimum(i, nblk - 1), 0))
    prv = lambda n: pl.BlockSpec((rows, n), lambda i: (jnp.maximum(i - 1, 0), 0))
    return pl.pallas_call(
        _dn_body,
        out_shape=jax.ShapeDtypeStruct((s, DN_W), BF16),
        grid=(nblk + 1,),
        in_specs=[cur(DN_W), cur(DN_W), cur(DN_W), cur(LANES), prv(DN_W), pl.BlockSpec((1, DN_HD), lambda i: (0, 0))],
        out_specs=prv(DN_W),
        scratch_shapes=[pltpu.VMEM((2, ncb, DN_HEADS, DN_HD, 2 * DN_HD), BF16),
                        pltpu.VMEM((2, ncb, DN_HEADS, DN_HD, DN_HD), BF16),
                        pltpu.VMEM((2, rows, DN_W), BF16),
                        pltpu.VMEM((2, ncb, SUBLANES, DN_HD), F32),
                        pltpu.VMEM((DN_HEADS, DN_HD, DN_HD), F32)],
        compiler_params=_params(("arbitrary",)),
        name="deltanet",
    )(qn, kn, v, bg, zdn, nw_row)


def _attn_body(*refs):
    ins, scratch = refs[:-6], refs[-6:]
    for p in range(ATT_PAIRS):
        _attn_pair(*[r.at[:, p * LANES:(p + 1) * LANES] for r in ins], *scratch)


def _attn_pair(q1, q4, q16, k1, k4, k16, v1, v4, v16, kp1, vp1, kp4a, kp4b, kp4c, kp4d,
               vp4a, vp4b, vp4c, vp4d, kp16, vp16, z_ref, anw_ref, o_ref, m_s, l_s, a_s, m_n, l_n, a_n):
    d4, d16 = DILATIONS[1], DILATIONS[2]
    seg4 = SUPER // d4
    kp4 = (kp4a, kp4b, kp4c, kp4d)
    vp4 = (vp4a, vp4b, vp4c, vp4d)
    kj = _iota((2 * Q_BLOCK, Q_BLOCK), 0)
    qi = _iota((2 * Q_BLOCK, Q_BLOCK), 1)
    rel = Q_BLOCK + qi - kj
    band = (rel >= 0) & (rel <= W_SUB)
    bias_in = jnp.where(band, 0.0, NEG).astype(BF16)
    bias_first = jnp.where(band & (kj >= Q_BLOCK), 0.0, NEG).astype(BF16)
    bias_edge = jnp.where(pl.program_id(0) == 0, bias_first, bias_in)
    one_hot = jnp.where(kj % Q_BLOCK == qi, 1.0, 0.0).astype(BF16)
    lane = _iota((Q_BLOCK, LANES), 1)
    lane_k = _iota((2 * Q_BLOCK, LANES), 1)
    head_a_q = (lane % AT_HD) < (AT_HD // 2)
    head_a = lane < AT_HD
    head_a_k = lane_k < AT_HD
    ro = _iota((4 * Q_BLOCK, LANES), 0)
    ones_rhs = jnp.where((ro < 2 * Q_BLOCK) == (_iota((4 * Q_BLOCK, LANES), 1) < AT_HD), 1.0, 0.0).astype(BF16)
    sr = _iota((2 * LANES, LANES), 0)
    ssq_rhs = jnp.where((sr % LANES) // AT_HD == _iota((2 * LANES, LANES), 1) // AT_HD, 1.0, 0.0).astype(BF16)
    zq = jnp.zeros((Q_BLOCK, LANES), BF16)
    zv = jnp.zeros((2 * Q_BLOCK, LANES), BF16)

    def scores(load):
        q, kb, vb, bias = load()
        q2 = jnp.concatenate([jnp.where(head_a_q, q, zq), jnp.where(head_a_q, zq, q)], axis=0)
        s = lax.dot_general(jnp.concatenate([q2, one_hot], axis=1), jnp.concatenate([kb, bias], axis=1),
                            NT_DIMS, preferred_element_type=F32)
        m = jnp.max(s, axis=-1, keepdims=True)
        return jnp.exp2(s - m).astype(BF16), m, vb

    def values(p, m, vb):
        pc = jnp.concatenate([p[0:Q_BLOCK], p[Q_BLOCK:2 * Q_BLOCK]], axis=1)
        rhs = jnp.concatenate([jnp.concatenate([jnp.where(head_a_k, vb, zv), jnp.where(head_a_k, zv, vb)], axis=0),
                               ones_rhs], axis=1)
        r = jnp.dot(pc, rhs, preferred_element_type=F32)
        return r[:, 0:LANES], r[:, LANES:2 * LANES], jnp.where(head_a, m[0:Q_BLOCK], m[Q_BLOCK:2 * Q_BLOCK])

    def run(loads, finish, post=None):
        n = len(loads)
        pending, done = {}, {}
        for i in range(n + ATT_LAG + 1):
            if i < n:
                pending[i] = scores(loads[i])
            j = i - ATT_LAG
            if 0 <= j < n:
                done[j] = finish(j, *values(*pending.pop(j)))
            if post is not None and 0 <= j - 1 < n:
                post(j - 1, done.pop(j - 1))

    def merged(acc, l, m, m_old, l_old, a_old):
        m_new = jnp.maximum(m_old, m)
        w_old = jnp.exp2(m_old - m_new)
        w_new = jnp.exp2(m - m_new)
        return a_old * w_old + acc * w_new, l_old * w_old + l * w_new, m_new

    def load16(r):
        rs = slice(r * Q_BLOCK, (r + 1) * Q_BLOCK)
        return lambda: (q16[rs, :], jnp.concatenate([kp16[rs, :], k16[rs, :]], axis=0),
                        jnp.concatenate([vp16[rs, :], v16[rs, :]], axis=0), bias_edge)

    def finish16(r, acc, l, m):
        rows = pl.ds((r % d4) * seg4 + r // d4, Q_BLOCK, stride=d4)
        a_s[rows, :] = acc
        l_s[rows, :] = l
        m_s[rows, :] = m

    run([load16(r) for r in range(d16)], finish16)

    def load4(r, b):
        base = r * seg4 + b * Q_BLOCK
        if b == 0:
            return lambda: (q4[base:base + Q_BLOCK, :],
                            jnp.concatenate([kp4[r][...], k4[base:base + Q_BLOCK, :]], axis=0),
                            jnp.concatenate([vp4[r][...], v4[base:base + Q_BLOCK, :]], axis=0), bias_edge)
        return lambda: (q4[base:base + Q_BLOCK, :], k4[base - Q_BLOCK:base + Q_BLOCK, :],
                        v4[base - Q_BLOCK:base + Q_BLOCK, :], bias_in)

    blocks4 = [(r, b) for b in range(seg4 // Q_BLOCK) for r in range(d4)]

    def finish4(j, acc, l, m):
        r, b = blocks4[j]
        rows = slice(r * seg4 + b * Q_BLOCK, r * seg4 + (b + 1) * Q_BLOCK)
        acc, l, m = merged(acc, l, m, m_s[rows, :], l_s[rows, :], a_s[rows, :])
        a_s[rows, :] = acc
        l_s[rows, :] = l
        m_s[rows, :] = m

    run([load4(r, b) for r, b in blocks4], finish4)

    for c in range(d4):
        src = slice(c * seg4, (c + 1) * seg4)
        dst = pl.ds(c, seg4, stride=d4)
        m_n[dst, :] = m_s[src, :]
        l_n[dst, :] = l_s[src, :]
        a_n[dst, :] = a_s[src, :]

    def load1(b):
        rs = slice(b * Q_BLOCK, (b + 1) * Q_BLOCK)
        if b == 0:
            return lambda: (q1[rs, :], jnp.concatenate([kp1[...], k1[rs, :]], axis=0),
                            jnp.concatenate([vp1[...], v1[rs, :]], axis=0), bias_edge)
        ks = slice((b - 1) * Q_BLOCK, (b + 1) * Q_BLOCK)
        return lambda: (q1[rs, :], k1[ks, :], v1[ks, :], bias_in)

    def finish1(b, acc, l, m):
        rs = slice(b * Q_BLOCK, (b + 1) * Q_BLOCK)
        acc, l, _ = merged(acc, l, m, m_n[rs, :], l_n[rs, :], a_n[rs, :])
        o = acc / l
        sq = o * o
        hi = sq.astype(BF16)
        return o, jnp.concatenate([hi, (sq - hi.astype(F32)).astype(BF16)], axis=1)

    def post1(b, res):
        o, sq2 = res
        rs = slice(b * Q_BLOCK, (b + 1) * Q_BLOCK)
        ssq = jnp.dot(sq2, ssq_rhs, preferred_element_type=F32)
        o_ref[rs, :] = (o * lax.rsqrt(ssq * (1.0 / AT_HD) + EPS) * anw_ref[...]
                        * z_ref[rs, :].astype(F32)).astype(BF16)

    run([load1(b) for b in range(SUPER // Q_BLOCK)], finish1, post1)


def _attn_call(aq, ak, av, zat, anw_row):
    s = zat.shape[0]
    d4 = DILATIONS[1]
    wl = ATT_PAIRS * LANES
    per_super = SUPER // Q_BLOCK
    cur = pl.BlockSpec((SUPER, wl), lambda m, g: (m, g))
    prev16 = pl.BlockSpec((SUPER, wl), lambda m, g: (jnp.maximum(m - 1, 0), g))
    prev1 = pl.BlockSpec((Q_BLOCK, wl), lambda m, g: (jnp.maximum(m * per_super - 1, 0), g))
    prev4 = [pl.BlockSpec((Q_BLOCK, wl),
                          lambda m, g, r=r: (jnp.maximum((m - 1) * per_super + (r + 1) * (per_super // d4) - 1, 0), g))
             for r in range(d4)]
    state = pltpu.VMEM((SUPER, LANES), F32)
    return pl.pallas_call(
        _attn_body,
        out_shape=jax.ShapeDtypeStruct((s, AT_W), BF16),
        grid=(s // SUPER, AT_W // wl),
        in_specs=[cur] * 9 + [prev1, prev1] + prev4 + prev4 + [prev16, prev16, cur,
                                                               pl.BlockSpec((1, wl), lambda m, g: (0, g))],
        out_specs=cur,
        scratch_shapes=[state] * 6,
        compiler_params=_params(("arbitrary", "arbitrary")),
        name="attn",
    )(*aq, *ak, *av, ak[0], av[0], *([ak[1]] * d4), *([av[1]] * d4), ak[2], av[2], zat, anw_row)


def _out_body(x_ref, odn_ref, oat_ref, gate_ref, w_ref, fnw_ref, o_ref):
    mixin = jnp.concatenate([odn_ref[...], oat_ref[...]], axis=1)
    mix = jnp.dot(mixin, w_ref[...], preferred_element_type=F32)
    y = x_ref[...] + gate_ref[...] * mix
    ms = jnp.mean(y * y, axis=-1, keepdims=True)
    o_ref[...] = y * lax.rsqrt(ms + EPS) * fnw_ref[...]


def _out_call(x2, odn, oat, gate, w_out, fnw_row, tm):
    s, d = x2.shape
    tile = lambda n: pl.BlockSpec((tm, n), lambda i: (i, 0))
    row = lambda n: pl.BlockSpec((1, n), lambda i: (0, 0))
    return pl.pallas_call(
        _out_body,
        out_shape=jax.ShapeDtypeStruct((s, d), F32),
        grid=(s // tm,),
        in_specs=[tile(d), tile(DN_W), tile(AT_W), row(d), pl.BlockSpec(w_out.shape, lambda i: (0, 0)), row(d)],
        out_specs=tile(d),
        compiler_params=_params(("arbitrary",)),
        name="out",
    )(x2, odn, oat, gate, w_out, fnw_row)


def kernel(x, c, positions, w_mod, b_mod, norm_w, w_in, conv_w, a_log, dt_bias, dn_norm_w, at_norm_w,
           w_out, final_norm_w):
    b, s, d = x.shape
    assert b == 1 and d == D_MODEL and w_mod.shape[0] == 1
    x2 = x.reshape(s, d)

    mod = _mod_call(c.reshape(d, 1), w_mod[0], b_mod[0].reshape(1, 3 * d))
    shift, scale, gate = mod[:, 0:d], mod[:, d:2 * d], mod[:, 2 * d:3 * d]

    w_all, w_out_b = _wpack_call(w_in, w_out)
    pad_heads = lambda v: jnp.pad(v.reshape(1, DN_HEADS), ((0, 0), (DN_HEADS, LANES - 2 * DN_HEADS)))
    half = AT_HD // 2
    invf = (ROPE_THETA ** (-jnp.arange(half, dtype=F32) / half)).reshape(half, 1)
    sgn = jnp.where(jnp.arange(LANES) < LANES // 2, -1.0, 1.0).astype(F32).reshape(1, LANES)

    assert s % SUPER == 0
    outs = _proj_call(x2, positions.reshape(1, s), norm_w[0].reshape(1, d), scale, shift, w_all, conv_w[0],
                      pad_heads(a_log[0]), pad_heads(dt_bias[0]), invf, sgn, tm=512)
    qn, kn, v, zdn, bg, zat = outs[0:6]
    flat = lambda a: a.reshape(s, AT_W)
    aq, ak, av = [[flat(a) for a in outs[j:j + 3]] for j in (6, 9, 12)]

    odn = _dn_call(qn, kn, v, bg, zdn, dn_norm_w[0].reshape(1, DN_HD), rows=512)
    oat = _attn_call(aq, ak, av, zat, jnp.tile(at_norm_w[0], AT_HEADS).reshape(1, AT_W))

    out = _out_call(x2, odn, oat, gate, w_out_b, final_norm_w.reshape(1, d), tm=1024)
    return out.reshape(b, s, d)
```

```python
import math

import jax
import jax.numpy as jnp
from jax import lax
from jax.experimental import pallas as pl
from jax.experimental.pallas import tpu as pltpu

D_MODEL = 1024
DN_HEADS = 4
DN_HD = 128
DN_W = DN_HEADS * DN_HD
AT_HEADS = 8
AT_HD = 64
AT_W = AT_HEADS * AT_HD
CONV_K = 4
CHUNK = 64
DILATIONS = (1, 4, 16)
W_SUB = 128
Q_BLOCK = 128
SUPER = Q_BLOCK * DILATIONS[-1]
OUT_IN_BUFS = 3
ATT_PAIRS = 2
ATT_LAG = 2
ROPE_THETA = 10000.0
EPS = 1e-6
NEG = -1e30

LANES = 128
SUBLANES = 8
MXU_COLS = 256
PROJ_COLS = 512
VMEM_LIMIT = 48 * 1024 * 1024

C_QKV, C_Z, C_AQ, C_AK, C_AV, C_AZ, C_BA = 0, 1536, 2048, 2560, 3072, 3584, 4096
W_COLS = C_BA + LANES

F32 = jnp.float32
BF16 = jnp.bfloat16
NT_DIMS = (((1,), (1,)), ((), ()))


def _sigmoid(v):
    return 1.0 / (1.0 + jnp.exp2(v * (-math.log2(math.e))))


def _iota(shape, dim):
    return lax.broadcasted_iota(jnp.int32, shape, dim)


def _params(sem):
    return pltpu.CompilerParams(dimension_semantics=sem, vmem_limit_bytes=VMEM_LIMIT)


def _mod_body(c_ref, w_ref, b_ref, o_ref):
    c = c_ref[...]
    sc = c * _sigmoid(c)
    o_ref[...] = jnp.sum(sc * w_ref[...], axis=0, keepdims=True) + b_ref[...]


def _mod_call(c_col, w_mod, b_mod):
    d, n = w_mod.shape
    tn = 1024
    return pl.pallas_call(
        _mod_body,
        out_shape=jax.ShapeDtypeStruct((1, n), F32),
        grid=(n // tn,),
        in_specs=[pl.BlockSpec((d, 1), lambda j: (0, 0)),
                  pl.BlockSpec((d, tn), lambda j: (0, j)),
                  pl.BlockSpec((1, tn), lambda j: (0, j))],
        out_specs=pl.BlockSpec((1, tn), lambda j: (0, j)),
        compiler_params=_params(("arbitrary",)),
        name="mod",
    )(c_col, w_mod, b_mod)


def _wpack_body(wt_ref, wo_ref, o_ref, oo_ref):
    kb = wt_ref.shape[2]
    o_b = 4 * DN_W
    o_q = o_b + 2 * DN_HEADS
    half = AT_HD // 2

    def put(col0, src):
        o_ref[:, col0:col0 + LANES] = src.T.astype(BF16)

    for g in range(o_b // LANES):
        put(C_QKV + g * LANES, wt_ref[0, g * LANES:(g + 1) * LANES, :])
    for g in range(2 * AT_W // LANES):
        base = o_q + g * LANES
        put(C_AQ + g * LANES, jnp.concatenate([wt_ref[0, base:base + half, :],
                                               wt_ref[0, base + 2 * half:base + 3 * half, :],
                                               wt_ref[0, base + half:base + 2 * half, :],
                                               wt_ref[0, base + 3 * half:base + 4 * half, :]], axis=0))
    for g in range(2 * AT_W // LANES):
        base = o_q + 2 * AT_W + g * LANES
        put(C_AV + g * LANES, wt_ref[0, base:base + LANES, :])
    put(C_BA, jnp.concatenate([wt_ref[0, o_b:o_b + 2 * DN_HEADS, :],
                               jnp.zeros((LANES - 2 * DN_HEADS, kb), F32)], axis=0))
    oo_ref[...] = wo_ref[0].astype(BF16)


def _wpack_call(w_in, w_out):
    _, d, n_in = w_in.shape
    n_out = w_out.shape[2]
    kb = 512
    w_t = jnp.swapaxes(w_in, 1, 2)
    return pl.pallas_call(
        _wpack_body,
        out_shape=[jax.ShapeDtypeStruct((d, W_COLS), BF16), jax.ShapeDtypeStruct((w_out.shape[1], n_out), BF16)],
        grid=(d // kb,),
        in_specs=[pl.BlockSpec((1, n_in, kb), lambda i: (0, 0, i)),
                  pl.BlockSpec((1, kb, n_out), lambda i: (0, i, 0))],
        out_specs=[pl.BlockSpec((kb, W_COLS), lambda i: (i, 0)), pl.BlockSpec((kb, n_out), lambda i: (i, 0))],
        compiler_params=_params(("arbitrary",)),
        name="wpack",
    )(w_t, w_out)


def _proj_body(x_ref, pos_ref, nw_ref, sc_ref, sh_ref, w_ref, cw_ref, alog_ref, dtb_ref, invf_ref,
               sgn_ref, qn_ref, kn_ref, v_ref, zdn_ref, bg_ref, zat_ref,
               aq1_ref, aq4_ref, aq16_ref, ak1_ref, ak4_ref, ak16_ref, av1_ref, av4_ref, av16_ref,
               cbuf_ref, hb_ref, pbuf_ref, p4buf_ref):
    i = pl.program_id(0)
    tm = x_ref.shape[0]
    x = x_ref[...]
    ms = jnp.mean(x * x, axis=-1, keepdims=True)
    hb_ref[...] = ((x * lax.rsqrt(ms + EPS)) * (nw_ref[...] * (1.0 + sc_ref[...])) + sh_ref[...]).astype(BF16)

    def proj(c0, width=PROJ_COLS):
        return jnp.dot(hb_ref[...], w_ref[:, c0:c0 + width], preferred_element_type=F32)

    def silu(v):
        return v * _sigmoid(v)

    @pl.when(i == 0)
    def _():
        cbuf_ref[:, 0:SUBLANES, :] = jnp.zeros((cbuf_ref.shape[0], SUBLANES, LANES), F32)

    dyn0 = jnp.minimum(i, 0)
    for c0 in range(0, 3 * DN_W, PROJ_COLS):
        xg = proj(C_QKV + c0)
        halves = []
        for hh in range(PROJ_COLS // LANES):
            grp = c0 // LANES + hh
            ls = slice(c0 + hh * LANES, c0 + (hh + 1) * LANES)
            xh = xg[:, hh * LANES:(hh + 1) * LANES]
            cbuf_ref[grp, SUBLANES:SUBLANES + tm, :] = xh
            acc = xh * cw_ref[CONV_K - 1:CONV_K, ls]
            for j in range(1, CONV_K):
                acc = acc + cbuf_ref[grp, pl.ds(dyn0 + SUBLANES - j, tm), :] * cw_ref[CONV_K - 1 - j:CONV_K - j, ls]
            cbuf_ref[grp, 0:SUBLANES, :] = xh[tm - SUBLANES:tm]
            halves.append(acc)
        conv = jnp.concatenate(halves, axis=1)
        y = silu(conv)
        kind, off = c0 // DN_W, c0 % DN_W
        for hh in range(PROJ_COLS // DN_HD):
            yh = y[:, hh * DN_HD:(hh + 1) * DN_HD]
            dst = slice(off + hh * DN_HD, off + (hh + 1) * DN_HD)
            if kind == 2:
                v_ref[:, dst] = yh.astype(BF16)
            else:
                inv = lax.rsqrt(jnp.sum(yh * yh, axis=-1, keepdims=True) + EPS)
                if kind == 0:
                    qn_ref[:, dst] = (yh * (inv * (DN_HD ** -0.5))).astype(BF16)
                else:
                    kn_ref[:, dst] = (yh * inv).astype(BF16)

    for c0 in range(0, DN_W, PROJ_COLS):
        zdn_ref[:, c0:c0 + PROJ_COLS] = silu(proj(C_Z + c0)).astype(BF16)
        zat_ref[:, c0:c0 + PROJ_COLS] = silu(proj(C_AZ + c0)).astype(BF16)

    ba = proj(C_BA, LANES)
    lane = _iota(ba.shape, 1)
    a = ba + dtb_ref[...]
    softplus = jnp.maximum(a, 0.0) + jnp.log(1.0 + jnp.exp(-jnp.abs(a)))
    g = -jnp.exp(alog_ref[...]) * softplus
    bg_ref[...] = jnp.where(lane < DN_HEADS, _sigmoid(ba), jnp.where(lane < 2 * DN_HEADS, g, 0.0))

    ang = invf_ref[...] * pos_ref[...].astype(F32)
    reps = LANES // (AT_HD // 2)
    cos = jnp.concatenate([jnp.cos(ang)] * reps, axis=0).T
    sin = jnp.concatenate([jnp.sin(ang)] * reps, axis=0).T * sgn_ref[...]
    qscale = (AT_HD ** -0.5) * math.log2(math.e)
    d4, d16 = DILATIONS[1], DILATIONS[2]
    n4, n16 = tm // d4, tm // d16

    def emit(val, gp, nat_ref, p4_ref, p16_ref):
        sl = slice(gp * LANES, (gp + 1) * LANES)
        nat_ref[:, sl] = val.astype(BF16)
        pbuf_ref[gp] = val
        for c in range(d4):
            seg = pbuf_ref[gp, pl.ds(c, n4, stride=d4), :]
            p4_ref[0, c, :, sl] = seg.astype(BF16)
            p4buf_ref[gp, c * n4:(c + 1) * n4, :] = seg
        for r in range(d16):
            seg = p4buf_ref[gp, pl.ds((r % d4) * n4 + r // d4, n16, stride=d4), :]
            p16_ref[0, r, :, sl] = seg.astype(BF16)

    def rope(v):
        return v * cos + pltpu.roll(v, LANES // 2, axis=1) * sin

    for c0 in range(0, AT_W, PROJ_COLS):
        aq = proj(C_AQ + c0)
        ak = proj(C_AK + c0)
        av = proj(C_AV + c0)
        for hh in range(PROJ_COLS // LANES):
            gp = c0 // LANES + hh
            ls = slice(hh * LANES, (hh + 1) * LANES)
            emit(rope(aq[:, ls]) * qscale, gp, aq1_ref, aq4_ref, aq16_ref)
            emit(rope(ak[:, ls]), gp, ak1_ref, ak4_ref, ak16_ref)
            emit(av[:, ls], gp, av1_ref, av4_ref, av16_ref)


def _proj_call(x2, pos_row, nw, scale, shift, w_all, cw, alog_row, dtb_row, invf, sgn, tm):
    s, d = x2.shape
    row = lambda n: pl.BlockSpec((1, n), lambda i: (0, 0))
    full = lambda a: pl.BlockSpec(a.shape, lambda i: (0, 0))
    tile = lambda n: pl.BlockSpec((tm, n), lambda i: (i, 0))
    spb = SUPER // tm
    res_shape = lambda dil: jax.ShapeDtypeStruct((s // SUPER, dil, SUPER // dil, AT_W), BF16)
    res_spec = lambda dil: pl.BlockSpec((1, dil, tm // dil, AT_W), lambda i: (i // spb, 0, i % spb, 0))
    bf = lambda n: jax.ShapeDtypeStruct((s, n), BF16)
    outs = [bf(DN_W)] * 4 + [jax.ShapeDtypeStruct((s, LANES), F32), bf(AT_W)] + \
           [bf(AT_W), res_shape(DILATIONS[1]), res_shape(DILATIONS[2])] * 3
    out_specs = [tile(DN_W)] * 4 + [tile(LANES), tile(AT_W)] + \
                [tile(AT_W), res_spec(DILATIONS[1]), res_spec(DILATIONS[2])] * 3
    return pl.pallas_call(
        _proj_body,
        out_shape=outs,
        grid=(s // tm,),
        in_specs=[tile(d), pl.BlockSpec((1, tm), lambda i: (0, i)), row(d), row(d), row(d),
                  pl.BlockSpec(w_all.shape, lambda i: (0, 0), pipeline_mode=pl.Buffered(1)),
                  full(cw), row(LANES), row(LANES), full(invf), row(LANES)],
        out_specs=out_specs,
        scratch_shapes=[pltpu.VMEM((3 * DN_W // LANES, SUBLANES + tm, LANES), F32), pltpu.VMEM((tm, d), BF16),
                        pltpu.VMEM((AT_W // LANES, tm, LANES), F32),
                        pltpu.VMEM((AT_W // LANES, tm, LANES), F32)],
        compiler_params=_params(("arbitrary",)),
        name="proj",
    )(x2, pos_row, nw, scale, shift, w_all, cw, alog_row, dtb_row, invf, sgn)


def _dn_body(q_ref, k_ref, v_ref, bg_ref, z_ref, nw_ref, o_ref, w_s, bt_s, op_s, gl_s, st_ref):
    step = pl.program_id(0)
    cur = step % 2
    prv = 1 - cur

    @pl.when(step == 0)
    def _():
        st_ref[...] = jnp.zeros(st_ref.shape, F32)
        w_s[1] = jnp.zeros(w_s.shape[1:], BF16)
        bt_s[1] = jnp.zeros(bt_s.shape[1:], BF16)
        op_s[1] = jnp.zeros(op_s.shape[1:], BF16)
        gl_s[1] = jnp.zeros(gl_s.shape[1:], F32)

    rows = q_ref.shape[0]
    nchunk = rows // CHUNK
    hsl = lambda hd: slice(hd * DN_HD, (hd + 1) * DN_HD)

    def scan_chunk(c):
        rs = slice(c * CHUNK, (c + 1) * CHUNK)
        for hd in range(DN_HEADS):
            st = st_ref[hd]
            r = jnp.dot(st.astype(BF16), w_s[prv, c, hd], preferred_element_type=F32)
            st_ref[hd] = gl_s[prv, c, hd:hd + 1, :] * st - r[:, 0:DN_HD] + bt_s[prv, c, hd].astype(F32)
            half = (hd % 2) * CHUNK
            o = r[:, DN_HD:2 * DN_HD].T[half:half + CHUNK] + op_s[prv, rs, hsl(hd)].astype(F32)
            ms = jnp.mean(o * o, axis=-1, keepdims=True)
            o_ref[rs, hsl(hd)] = (o * lax.rsqrt(ms + EPS) * nw_ref[...] * z_ref[rs, hsl(hd)].astype(F32)).astype(BF16)

    pending = list(range(nchunk))

    def tick():
        if pending:
            scan_chunk(pending.pop(0))

    wide = DN_HEADS * CHUNK
    bg = bg_ref[...]
    cs_rows = min(rows, MXU_COLS)
    ri = _iota((cs_rows, cs_rows), 0)
    ci = _iota((cs_rows, cs_rows), 1)
    tril_blk = jnp.where((ri // CHUNK == ci // CHUNK) & (ri >= ci), 1.0, 0.0).astype(BF16)
    g_hi = bg.astype(BF16)
    g_r1 = bg - g_hi.astype(F32)
    g_mid = g_r1.astype(BF16)
    g_lo = (g_r1 - g_mid.astype(F32)).astype(BF16)
    g3 = jnp.concatenate([g_hi, g_mid, g_lo], axis=1)
    csum = jnp.concatenate([jnp.dot(tril_blk, g3[r0:r0 + cs_rows], preferred_element_type=F32)
                            for r0 in range(0, rows, cs_rows)], axis=0)
    gcum = csum[:, 0:LANES] + csum[:, LANES:2 * LANES] + csum[:, 2 * LANES:3 * LANES]
    bfull = jnp.concatenate([jnp.broadcast_to(bg[:, hd:hd + 1], (rows, DN_HD)) for hd in range(DN_HEADS)], axis=1)
    gfull = jnp.concatenate([jnp.broadcast_to(gcum[:, DN_HEADS + hd:DN_HEADS + hd + 1], (rows, DN_HD))
                             for hd in range(DN_HEADS)], axis=1)
    eg = jnp.exp(gfull)
    q = q_ref[...].astype(F32)
    k = k_ref[...].astype(F32)
    kb = k * bfull
    vb = v_ref[...].astype(F32) * bfull
    q_dec = q * eg
    kbg = kb * eg

    r64 = _iota((CHUNK, wide), 0)
    j64 = _iota((CHUNK, wide), 1) % CHUNK
    tril = r64 >= j64
    strict = r64 > j64
    eye = jnp.where(r64 == j64, 1.0, 0.0).astype(F32)
    low_half = _iota((CHUNK, LANES), 1) < CHUNK
    bd64 = _iota((wide, wide), 0) // CHUNK == _iota((wide, wide), 1) // CHUNK
    bd_k = _iota((wide, DN_W), 0) // CHUNK == _iota((wide, DN_W), 1) // DN_HD

    def block_diag(m):
        return jnp.where(bd64, jnp.concatenate([m.astype(BF16)] * DN_HEADS, axis=0), jnp.zeros((wide, wide), BF16))

    rsl = lambda c: slice(c * CHUNK, (c + 1) * CHUNK)
    chunks = range(nchunk)
    g_last = [gfull[c * CHUNK + CHUNK - 1:(c + 1) * CHUNK, :] for c in chunks]
    k_dec = [k[rsl(c)] * jnp.exp(g_last[c] - gfull[rsl(c)]) for c in chunks]
    g_row = [jnp.concatenate([gcum[rsl(c)], gcum[rsl(c)]], axis=0).T for c in chunks]
    dec = []
    for c in chunks:
        halves = []
        for pr in range(DN_HEADS // 2):
            ha, hb = 2 * pr, 2 * pr + 1
            col = jnp.where(low_half, gfull[rsl(c), hsl(ha)], gfull[rsl(c), hsl(hb)])
            row = jnp.where(low_half[0:1], g_row[c][DN_HEADS + ha:DN_HEADS + ha + 1, :],
                            g_row[c][DN_HEADS + hb:DN_HEADS + hb + 1, :])
            halves.append(col - row)
        dec.append(jnp.exp(jnp.where(tril, jnp.concatenate(halves, axis=1), NEG)))
    kq = [lax.dot_general(jnp.concatenate([kb[rsl(c)].astype(BF16), q_ref[rsl(c), :]], axis=0),
                          jnp.where(bd_k, jnp.concatenate([k_ref[rsl(c), :]] * DN_HEADS, axis=0),
                                    jnp.zeros((wide, DN_W), BF16)),
                          NT_DIMS, preferred_element_type=F32) for c in chunks]
    tick()
    a_low = [jnp.where(strict, m[0:CHUNK] * d, 0.0) for m, d in zip(kq, dec)]
    attn = [jnp.where(tril, m[CHUNK:2 * CHUNK] * d, 0.0) for m, d in zip(kq, dec)]
    bmat = [-a for a in a_low]
    ymat = [eye + b for b in bmat]
    bmat = [jnp.dot(b.astype(BF16), block_diag(b), preferred_element_type=F32) for b in bmat]
    tick()
    for _ in range(4):
        prod = [jnp.dot(jnp.concatenate([y, b], axis=0).astype(BF16), block_diag(b), preferred_element_type=F32)
                for y, b in zip(ymat, bmat)]
        tick()
        ymat = [y + p[0:CHUNK] for y, p in zip(ymat, prod)]
        bmat = [p[CHUNK:2 * CHUNK] for p in prod]
    ymat = [y + jnp.dot(y.astype(BF16), block_diag(b), preferred_element_type=F32) for y, b in zip(ymat, bmat)]
    tick()
    rhs = [jnp.concatenate([jnp.concatenate([kbg[rsl(c), hsl(hd)], vb[rsl(c), hsl(hd)]], axis=1)
                            for hd in range(DN_HEADS)], axis=0).astype(BF16) for c in chunks]
    wu = [jnp.dot(block_diag(y), r, preferred_element_type=F32) for y, r in zip(ymat, rhs)]
    tick()
    wut = [m.T.astype(BF16) for m in wu]
    kdbd = [jnp.where(bd_k, jnp.concatenate([k_dec[c].astype(BF16)] * DN_HEADS, axis=0),
                      jnp.zeros((wide, DN_W), BF16)) for c in chunks]
    pbt = [jnp.dot(a, b, preferred_element_type=F32) for a, b in zip(wut, kdbd)]
    tick()
    aw = [jnp.dot(block_diag(a), m.astype(BF16), preferred_element_type=F32) for a, m in zip(attn, wu)]
    qpt = [(jnp.concatenate([q_dec[rsl(c), hsl(hd)] for hd in range(DN_HEADS)], axis=0) - a[:, 0:DN_HD]).T
           .astype(BF16) for c, a in zip(chunks, aw)]
    while pending:
        tick()
    for c in chunks:
        for hd in range(DN_HEADS):
            pair = slice((hd // 2) * LANES, (hd // 2 + 1) * LANES)
            w_s[cur, c, hd] = jnp.concatenate([pbt[c][0:DN_HD, hsl(hd)].astype(BF16), qpt[c][:, pair]], axis=1)
            bt_s[cur, c, hd] = pbt[c][DN_HD:2 * DN_HD, hsl(hd)].astype(BF16)
            op_s[cur, rsl(c), hsl(hd)] = aw[c][hd * CHUNK:(hd + 1) * CHUNK, DN_HD:2 * DN_HD].astype(BF16)
        gl_rows = [jnp.exp(g_last[c][:, hsl(hd)]) for hd in range(DN_HEADS)]
        gl_rows.append(jnp.zeros((SUBLANES - DN_HEADS, DN_HD), F32))
        gl_s[cur, c] = jnp.concatenate(gl_rows, axis=0)


def _dn_call(qn, kn, v, bg, zdn, nw_row, rows):
    s = qn.shape[0]
    nblk = s // rows
    ncb = rows // CHUNK
    cur = lambda n: pl.BlockSpec((rows, n), lambda i: (jnp.minimum(i, nblk - 1), 0))
    prv = lambda n: pl.BlockSpec((rows, n), lambda i: (jnp.maximum(i - 1, 0), 0))
    return pl.pallas_call(
        _dn_body,
        out_shape=jax.ShapeDtypeStruct((s, DN_W), BF16),
        grid=(nblk + 1,),
        in_specs=[cur(DN_W), cur(DN_W), cur(DN_W), cur(LANES), prv(DN_W), pl.BlockSpec((1, DN_HD), lambda i: (0, 0))],
        out_specs=prv(DN_W),
        scratch_shapes=[pltpu.VMEM((2, ncb, DN_HEADS, DN_HD, 2 * DN_HD), BF16),
                        pltpu.VMEM((2, ncb, DN_HEADS, DN_HD, DN_HD), BF16),
                        pltpu.VMEM((2, rows, DN_W), BF16),
                        pltpu.VMEM((2, ncb, SUBLANES, DN_HD), F32),
                        pltpu.VMEM((DN_HEADS, DN_HD, DN_HD), F32)],
        compiler_params=_params(("arbitrary",)),
        name="deltanet",
    )(qn, kn, v, bg, zdn, nw_row)


def _attn_body(*refs):
    ins, scratch = refs[:-6], refs[-6:]
    for p in range(ATT_PAIRS):
        _attn_pair(*[r.at[:, p * LANES:(p + 1) * LANES] for r in ins], *scratch)


def _attn_pair(q1, q4, q16, k1, k4, k16, v1, v4, v16, kp1, vp1, kp4a, kp4b, kp4c, kp4d,
               vp4a, vp4b, vp4c, vp4d, kp16, vp16, z_ref, anw_ref, o_ref, m_s, l_s, a_s, m_n, l_n, a_n):
    d4, d16 = DILATIONS[1], DILATIONS[2]
    seg4 = SUPER // d4
    kp4 = (kp4a, kp4b, kp4c, kp4d)
    vp4 = (vp4a, vp4b, vp4c, vp4d)
    kj = _iota((2 * Q_BLOCK, Q_BLOCK), 0)
    qi = _iota((2 * Q_BLOCK, Q_BLOCK), 1)
    rel = Q_BLOCK + qi - kj
    band = (rel >= 0) & (rel <= W_SUB)
    bias_in = jnp.where(band, 0.0, NEG).astype(BF16)
    bias_first = jnp.where(band & (kj >= Q_BLOCK), 0.0, NEG).astype(BF16)
    bias_edge = jnp.where(pl.program_id(0) == 0, bias_first, bias_in)
    one_hot = jnp.where(kj % Q_BLOCK == qi, 1.0, 0.0).astype(BF16)
    lane = _iota((Q_BLOCK, LANES), 1)
    lane_k = _iota((2 * Q_BLOCK, LANES), 1)
    head_a_q = (lane % AT_HD) < (AT_HD // 2)
    head_a = lane < AT_HD
    head_a_k = lane_k < AT_HD
    ro = _iota((4 * Q_BLOCK, LANES), 0)
    ones_rhs = jnp.where((ro < 2 * Q_BLOCK) == (_iota((4 * Q_BLOCK, LANES), 1) < AT_HD), 1.0, 0.0).astype(BF16)
    sr = _iota((2 * LANES, LANES), 0)
    ssq_rhs = jnp.where((sr % LANES) // AT_HD == _iota((2 * LANES, LANES), 1) // AT_HD, 1.0, 0.0).astype(BF16)
    zq = jnp.zeros((Q_BLOCK, LANES), BF16)
    zv = jnp.zeros((2 * Q_BLOCK, LANES), BF16)

    def scores(load):
        q, kb, vb, bias = load()
        q2 = jnp.concatenate([jnp.where(head_a_q, q, zq), jnp.where(head_a_q, zq, q)], axis=0)
        s = lax.dot_general(jnp.concatenate([q2, one_hot], axis=1), jnp.concatenate([kb, bias], axis=1),
                            NT_DIMS, preferred_element_type=F32)
        m = jnp.max(s, axis=-1, keepdims=True)
        return jnp.exp2(s - m).astype(BF16), m, vb

    def values(p, m, vb):
        pc = jnp.concatenate([p[0:Q_BLOCK], p[Q_BLOCK:2 * Q_BLOCK]], axis=1)
        rhs = jnp.concatenate([jnp.concatenate([jnp.where(head_a_k, vb, zv), jnp.where(head_a_k, zv, vb)], axis=0),
                               ones_rhs], axis=1)
        r = jnp.dot(pc, rhs, preferred_element_type=F32)
        return r[:, 0:LANES], r[:, LANES:2 * LANES], jnp.where(head_a, m[0:Q_BLOCK], m[Q_BLOCK:2 * Q_BLOCK])

    def run(loads, finish, post=None):
        n = len(loads)
        pending, done = {}, {}
        for i in range(n + ATT_LAG + 1):
            if i < n:
                pending[i] = scores(loads[i])
            j = i - ATT_LAG
            if 0 <= j < n:
                done[j] = finish(j, *values(*pending.pop(j)))
            if post is not None and 0 <= j - 1 < n:
                post(j - 1, done.pop(j - 1))

    def merged(acc, l, m, m_old, l_old, a_old):
        m_new = jnp.maximum(m_old, m)
        w_old = jnp.exp2(m_old - m_new)
        w_new = jnp.exp2(m - m_new)
        return a_old * w_old + acc * w_new, l_old * w_old + l * w_new, m_new

    def load16(r):
        rs = slice(r * Q_BLOCK, (r + 1) * Q_BLOCK)
        return lambda: (q16[rs, :], jnp.concatenate([kp16[rs, :], k16[rs, :]], axis=0),
                        jnp.concatenate([vp16[rs, :], v16[rs, :]], axis=0), bias_edge)

    def finish16(r, acc, l, m):
        rows = pl.ds((r % d4) * seg4 + r // d4, Q_BLOCK, stride=d4)
        a_s[rows, :] = acc
        l_s[rows, :] = l
        m_s[rows, :] = m

    run([load16(r) for r in range(d16)], finish16)

    def load4(r, b):
        base = r * seg4 + b * Q_BLOCK
        if b == 0:
            return lambda: (q4[base:base + Q_BLOCK, :],
                            jnp.concatenate([kp4[r][...], k4[base:base + Q_BLOCK, :]], axis=0),
                            jnp.concatenate([vp4[r][...], v4[base:base + Q_BLOCK, :]], axis=0), bias_edge)
        return lambda: (q4[base:base + Q_BLOCK, :], k4[base - Q_BLOCK:base + Q_BLOCK, :],
                        v4[base - Q_BLOCK:base + Q_BLOCK, :], bias_in)

    blocks4 = [(r, b) for b in range(seg4 // Q_BLOCK) for r in range(d4)]

    def finish4(j, acc, l, m):
        r, b = blocks4[j]
        rows = slice(r * seg4 + b * Q_BLOCK, r * seg4 + (b + 1) * Q_BLOCK)
        acc, l, m = merged(acc, l, m, m_s[rows, :], l_s[rows, :], a_s[rows, :])
        a_s[rows, :] = acc
        l_s[rows, :] = l
        m_s[rows, :] = m

    run([load4(r, b) for r, b in blocks4], finish4)

    for c in range(d4):
        src = slice(c * seg4, (c + 1) * seg4)
        dst = pl.ds(c, seg4, stride=d4)
        m_n[dst, :] = m_s[src, :]
        l_n[dst, :] = l_s[src, :]
        a_n[dst, :] = a_s[src, :]

    def load1(b):
        rs = slice(b * Q_BLOCK, (b + 1) * Q_BLOCK)
        if b == 0:
            return lambda: (q1[rs, :], jnp.concatenate([kp1[...], k1[rs, :]], axis=0),
                            jnp.concatenate([vp1[...], v1[rs, :]], axis=0), bias_edge)
        ks = slice((b - 1) * Q_BLOCK, (b + 1) * Q_BLOCK)
        return lambda: (q1[rs, :], k1[ks, :], v1[ks, :], bias_in)

    def finish1(b, acc, l, m):
        rs = slice(b * Q_BLOCK, (b + 1) * Q_BLOCK)
        acc, l, _ = merged(acc, l, m, m_n[rs, :], l_n[rs, :], a_n[rs, :])
        o = acc / l
        sq = o * o
        hi = sq.astype(BF16)
        return o, jnp.concatenate([hi, (sq - hi.astype(F32)).astype(BF16)], axis=1)

    def post1(b, res):
        o, sq2 = res
        rs = slice(b * Q_BLOCK, (b + 1) * Q_BLOCK)
        ssq = jnp.dot(sq2, ssq_rhs, preferred_element_type=F32)
        o_ref[rs, :] = (o * lax.rsqrt(ssq * (1.0 / AT_HD) + EPS) * anw_ref[...]
                        * z_ref[rs, :].astype(F32)).astype(BF16)

    run([load1(b) for b in range(SUPER // Q_BLOCK)], finish1, post1)


def _attn_call(aq, ak, av, zat, anw_row):
    s = zat.shape[0]
    d4 = DILATIONS[1]
    wl = ATT_PAIRS * LANES
    per_super = SUPER // Q_BLOCK
    cur = pl.BlockSpec((SUPER, wl), lambda m, g: (m, g))
    prev16 = pl.BlockSpec((SUPER, wl), lambda m, g: (jnp.maximum(m - 1, 0), g))
    prev1 = pl.BlockSpec((Q_BLOCK, wl), lambda m, g: (jnp.maximum(m * per_super - 1, 0), g))
    prev4 = [pl.BlockSpec((Q_BLOCK, wl),
                          lambda m, g, r=r: (jnp.maximum((m - 1) * per_super + (r + 1) * (per_super // d4) - 1, 0), g))
             for r in range(d4)]
    state = pltpu.VMEM((SUPER, LANES), F32)
    return pl.pallas_call(
        _attn_body,
        out_shape=jax.ShapeDtypeStruct((s, AT_W), BF16),
        grid=(s // SUPER, AT_W // wl),
        in_specs=[cur] * 9 + [prev1, prev1] + prev4 + prev4 + [prev16, prev16, cur,
                                                               pl.BlockSpec((1, wl), lambda m, g: (0, g))],
        out_specs=cur,
        scratch_shapes=[state] * 6,
        compiler_params=_params(("arbitrary", "arbitrary")),
        name="attn",
    )(*aq, *ak, *av, ak[0], av[0], *([ak[1]] * d4), *([av[1]] * d4), ak[2], av[2], zat, anw_row)


def _out_body(x_hbm, odn_hbm, oat_hbm, gate_ref, w_ref, fnw_ref, o_hbm, xb, db, ab, ob, in_sem, out_sem):
    tm = xb.shape[1]
    ntile = x_hbm.shape[0] // tm

    def in_copies(t, slot):
        rows = pl.ds(pl.multiple_of(t * tm, tm), tm)
        return (pltpu.make_async_copy(x_hbm.at[rows, :], xb.at[slot], in_sem.at[0, slot]),
                pltpu.make_async_copy(odn_hbm.at[rows, :], db.at[slot], in_sem.at[1, slot]),
                pltpu.make_async_copy(oat_hbm.at[rows, :], ab.at[slot], in_sem.at[2, slot]))

    def out_copy(t, slot):
        rows = pl.ds(pl.multiple_of(t * tm, tm), tm)
        return pltpu.make_async_copy(ob.at[slot], o_hbm.at[rows, :], out_sem.at[slot])

    for t in range(OUT_IN_BUFS):
        for cp in in_copies(t, t):
            cp.start()

    def step(t, carry):
        slot = t % OUT_IN_BUFS
        oslot = t % 2
        for cp in in_copies(t, slot):
            cp.wait()

        @pl.when(t >= 2)
        def _():
            out_copy(t - 2, oslot).wait()

        mixin = jnp.concatenate([db[slot], ab[slot]], axis=1)
        mix = jnp.dot(mixin, w_ref[...], preferred_element_type=F32)
        y = xb[slot] + gate_ref[...] * mix
        ms = jnp.mean(y * y, axis=-1, keepdims=True)
        ob[oslot] = y * lax.rsqrt(ms + EPS) * fnw_ref[...]
        out_copy(t, oslot).start()

        @pl.when(t + OUT_IN_BUFS < ntile)
        def _():
            for cp in in_copies(t + OUT_IN_BUFS, slot):
                cp.start()

        return carry

    lax.fori_loop(0, ntile, step, 0)
    out_copy(ntile - 2, (ntile - 2) % 2).wait()
    out_copy(ntile - 1, (ntile - 1) % 2).wait()


def _out_call(x2, odn, oat, gate, w_out, fnw_row, tm):
    s, d = x2.shape
    assert s // tm >= max(OUT_IN_BUFS, 2)
    hbm = pl.BlockSpec(memory_space=pl.ANY)
    vmem = pl.BlockSpec(memory_space=pltpu.VMEM)
    return pl.pallas_call(
        _out_body,
        out_shape=jax.ShapeDtypeStruct((s, d), F32),
        in_specs=[hbm, hbm, hbm, vmem, vmem, vmem],
        out_specs=hbm,
        scratch_shapes=[pltpu.VMEM((OUT_IN_BUFS, tm, d), F32), pltpu.VMEM((OUT_IN_BUFS, tm, DN_W), BF16),
                        pltpu.VMEM((OUT_IN_BUFS, tm, AT_W), BF16), pltpu.VMEM((2, tm, d), F32),
                        pltpu.SemaphoreType.DMA((3, OUT_IN_BUFS)), pltpu.SemaphoreType.DMA((2,))],
        compiler_params=pltpu.CompilerParams(vmem_limit_bytes=VMEM_LIMIT),
        name="out",
    )(x2, odn, oat, gate, w_out, fnw_row)


def kernel(x, c, positions, w_mod, b_mod, norm_w, w_in, conv_w, a_log, dt_bias, dn_norm_w, at_norm_w,
           w_out, final_norm_w):
    b, s, d = x.shape
    assert b == 1 and d == D_MODEL and w_mod.shape[0] == 1
    x2 = x.reshape(s, d)

    mod = _mod_call(c.reshape(d, 1), w_mod[0], b_mod[0].reshape(1, 3 * d))
    shift, scale, gate = mod[:, 0:d], mod[:, d:2 * d], mod[:, 2 * d:3 * d]

    w_all, w_out_b = _wpack_call(w_in, w_out)
    pad_heads = lambda v: jnp.pad(v.reshape(1, DN_HEADS), ((0, 0), (DN_HEADS, LANES - 2 * DN_HEADS)))
    half = AT_HD // 2
    invf = (ROPE_THETA ** (-jnp.arange(half, dtype=F32) / half)).reshape(half, 1)
    sgn = jnp.where(jnp.arange(LANES) < LANES // 2, -1.0, 1.0).astype(F32).reshape(1, LANES)

    assert s % SUPER == 0
    outs = _proj_call(x2, positions.reshape(1, s), norm_w[0].reshape(1, d), scale, shift, w_all, conv_w[0],
                      pad_heads(a_log[0]), pad_heads(dt_bias[0]), invf, sgn, tm=512)
    qn, kn, v, zdn, bg, zat = outs[0:6]
    flat = lambda a: a.reshape(s, AT_W)
    aq, ak, av = [[flat(a) for a in outs[j:j + 3]] for j in (6, 9, 12)]

    odn = _dn_call(qn, kn, v, bg, zdn, dn_norm_w[0].reshape(1, DN_HD), rows=512)
    oat = _attn_call(aq, ak, av, zat, jnp.tile(at_norm_w[0], AT_HEADS).reshape(1, AT_W))

    out = _out_call(x2, odn, oat, gate, w_out_b, final_norm_w.reshape(1, d), tm=1024)
    return out.reshape(b, s, d)
```

```python
import math

import jax
import jax.numpy as jnp
from jax import lax
from jax.experimental import pallas as pl
from jax.experimental.pallas import tpu as pltpu

D_MODEL = 1024
DN_HEADS = 4
DN_HD = 128
DN_W = DN_HEADS * DN_HD
AT_HEADS = 8
AT_HD = 64
AT_W = AT_HEADS * AT_HD
CONV_K = 4
CHUNK = 64
DILATIONS = (1, 4, 16)
W_SUB = 128
Q_BLOCK = 128
SUPER = Q_BLOCK * DILATIONS[-1]
OUT_IN_BUFS = 4
ATT_PAIRS = 2
ATT_LAG = 2
ROPE_THETA = 10000.0
EPS = 1e-6
NEG = -1e30

LANES = 128
SUBLANES = 8
MXU_COLS = 256
PROJ_COLS = 512
VMEM_LIMIT = 48 * 1024 * 1024

C_QKV, C_Z, C_AQ, C_AK, C_AV, C_AZ, C_BA = 0, 1536, 2048, 2560, 3072, 3584, 4096
W_COLS = C_BA + LANES

F32 = jnp.float32
BF16 = jnp.bfloat16
NT_DIMS = (((1,), (1,)), ((), ()))


def _sigmoid(v):
    return 1.0 / (1.0 + jnp.exp2(v * (-math.log2(math.e))))


def _iota(shape, dim):
    return lax.broadcasted_iota(jnp.int32, shape, dim)


def _params(sem):
    return pltpu.CompilerParams(dimension_semantics=sem, vmem_limit_bytes=VMEM_LIMIT)


def _mod_body(c_ref, w_ref, b_ref, o_ref):
    c = c_ref[...]
    sc = c * _sigmoid(c)
    o_ref[...] = jnp.sum(sc * w_ref[...], axis=0, keepdims=True) + b_ref[...]


def _mod_call(c_col, w_mod, b_mod):
    d, n = w_mod.shape
    tn = 1024
    return pl.pallas_call(
        _mod_body,
        out_shape=jax.ShapeDtypeStruct((1, n), F32),
        grid=(n // tn,),
        in_specs=[pl.BlockSpec((d, 1), lambda j: (0, 0)),
                  pl.BlockSpec((d, tn), lambda j: (0, j)),
                  pl.BlockSpec((1, tn), lambda j: (0, j))],
        out_specs=pl.BlockSpec((1, tn), lambda j: (0, j)),
        compiler_params=_params(("arbitrary",)),
        name="mod",
    )(c_col, w_mod, b_mod)


def _wpack_body(wt_ref, wo_ref, o_ref, oo_ref):
    kb = wt_ref.shape[2]
    o_b = 4 * DN_W
    o_q = o_b + 2 * DN_HEADS
    half = AT_HD // 2

    def put(col0, src):
        o_ref[:, col0:col0 + LANES] = src.T.astype(BF16)

    for g in range(o_b // LANES):
        put(C_QKV + g * LANES, wt_ref[0, g * LANES:(g + 1) * LANES, :])
    for g in range(2 * AT_W // LANES):
        base = o_q + g * LANES
        put(C_AQ + g * LANES, jnp.concatenate([wt_ref[0, base:base + half, :],
                                               wt_ref[0, base + 2 * half:base + 3 * half, :],
                                               wt_ref[0, base + half:base + 2 * half, :],
                                               wt_ref[0, base + 3 * half:base + 4 * half, :]], axis=0))
    for g in range(2 * AT_W // LANES):
        base = o_q + 2 * AT_W + g * LANES
        put(C_AV + g * LANES, wt_ref[0, base:base + LANES, :])
    put(C_BA, jnp.concatenate([wt_ref[0, o_b:o_b + 2 * DN_HEADS, :],
                               jnp.zeros((LANES - 2 * DN_HEADS, kb), F32)], axis=0))
    oo_ref[...] = wo_ref[0].astype(BF16)


def _wpack_call(w_in, w_out):
    _, d, n_in = w_in.shape
    n_out = w_out.shape[2]
    kb = 512
    w_t = jnp.swapaxes(w_in, 1, 2)
    return pl.pallas_call(
        _wpack_body,
        out_shape=[jax.ShapeDtypeStruct((d, W_COLS), BF16), jax.ShapeDtypeStruct((w_out.shape[1], n_out), BF16)],
        grid=(d // kb,),
        in_specs=[pl.BlockSpec((1, n_in, kb), lambda i: (0, 0, i)),
                  pl.BlockSpec((1, kb, n_out), lambda i: (0, i, 0))],
        out_specs=[pl.BlockSpec((kb, W_COLS), lambda i: (i, 0)), pl.BlockSpec((kb, n_out), lambda i: (i, 0))],
        compiler_params=_params(("arbitrary",)),
        name="wpack",
    )(w_t, w_out)


def _proj_body(x_ref, pos_ref, nw_ref, sc_ref, sh_ref, w_ref, cw_ref, alog_ref, dtb_ref, invf_ref,
               sgn_ref, qn_ref, kn_ref, v_ref, zdn_ref, bg_ref, zat_ref,
               aq1_ref, aq4_ref, aq16_ref, ak1_ref, ak4_ref, ak16_ref, av1_ref, av4_ref, av16_ref,
               cbuf_ref, hb_ref, pbuf_ref, p4buf_ref):
    i = pl.program_id(0)
    tm = x_ref.shape[0]
    x = x_ref[...]
    ms = jnp.mean(x * x, axis=-1, keepdims=True)
    hb_ref[...] = ((x * lax.rsqrt(ms + EPS)) * (nw_ref[...] * (1.0 + sc_ref[...])) + sh_ref[...]).astype(BF16)

    def proj(c0, width=PROJ_COLS):
        return jnp.dot(hb_ref[...], w_ref[:, c0:c0 + width], preferred_element_type=F32)

    def silu(v):
        return v * _sigmoid(v)

    @pl.when(i == 0)
    def _():
        cbuf_ref[:, 0:SUBLANES, :] = jnp.zeros((cbuf_ref.shape[0], SUBLANES, LANES), F32)

    dyn0 = jnp.minimum(i, 0)
    for c0 in range(0, 3 * DN_W, PROJ_COLS):
        xg = proj(C_QKV + c0)
        halves = []
        for hh in range(PROJ_COLS // LANES):
            grp = c0 // LANES + hh
            ls = slice(c0 + hh * LANES, c0 + (hh + 1) * LANES)
            xh = xg[:, hh * LANES:(hh + 1) * LANES]
            cbuf_ref[grp, SUBLANES:SUBLANES + tm, :] = xh
            acc = xh * cw_ref[CONV_K - 1:CONV_K, ls]
            for j in range(1, CONV_K):
                acc = acc + cbuf_ref[grp, pl.ds(dyn0 + SUBLANES - j, tm), :] * cw_ref[CONV_K - 1 - j:CONV_K - j, ls]
            cbuf_ref[grp, 0:SUBLANES, :] = xh[tm - SUBLANES:tm]
            halves.append(acc)
        conv = jnp.concatenate(halves, axis=1)
        y = silu(conv)
        kind, off = c0 // DN_W, c0 % DN_W
        for hh in range(PROJ_COLS // DN_HD):
            yh = y[:, hh * DN_HD:(hh + 1) * DN_HD]
            dst = slice(off + hh * DN_HD, off + (hh + 1) * DN_HD)
            if kind == 2:
                v_ref[:, dst] = yh.astype(BF16)
            else:
                inv = lax.rsqrt(jnp.sum(yh * yh, axis=-1, keepdims=True) + EPS)
                if kind == 0:
                    qn_ref[:, dst] = (yh * (inv * (DN_HD ** -0.5))).astype(BF16)
                else:
                    kn_ref[:, dst] = (yh * inv).astype(BF16)

    for c0 in range(0, DN_W, PROJ_COLS):
        zdn_ref[:, c0:c0 + PROJ_COLS] = silu(proj(C_Z + c0)).astype(BF16)
        zat_ref[:, c0:c0 + PROJ_COLS] = silu(proj(C_AZ + c0)).astype(BF16)

    ba = proj(C_BA, LANES)
    lane = _iota(ba.shape, 1)
    a = ba + dtb_ref[...]
    softplus = jnp.maximum(a, 0.0) + jnp.log(1.0 + jnp.exp(-jnp.abs(a)))
    g = -jnp.exp(alog_ref[...]) * softplus
    bg_ref[...] = jnp.where(lane < DN_HEADS, _sigmoid(ba), jnp.where(lane < 2 * DN_HEADS, g, 0.0))

    ang = invf_ref[...] * pos_ref[...].astype(F32)
    reps = LANES // (AT_HD // 2)
    cos = jnp.concatenate([jnp.cos(ang)] * reps, axis=0).T
    sin = jnp.concatenate([jnp.sin(ang)] * reps, axis=0).T * sgn_ref[...]
    qscale = (AT_HD ** -0.5) * math.log2(math.e)
    d4, d16 = DILATIONS[1], DILATIONS[2]
    n4, n16 = tm // d4, tm // d16

    def emit(val, gp, nat_ref, p4_ref, p16_ref):
        sl = slice(gp * LANES, (gp + 1) * LANES)
        nat_ref[:, sl] = val.astype(BF16)
        pbuf_ref[gp] = val
        for c in range(d4):
            seg = pbuf_ref[gp, pl.ds(c, n4, stride=d4), :]
            p4_ref[0, c, :, sl] = seg.astype(BF16)
            p4buf_ref[gp, c * n4:(c + 1) * n4, :] = seg
        for r in range(d16):
            seg = p4buf_ref[gp, pl.ds((r % d4) * n4 + r // d4, n16, stride=d4), :]
            p16_ref[0, r, :, sl] = seg.astype(BF16)

    def rope(v):
        return v * cos + pltpu.roll(v, LANES // 2, axis=1) * sin

    for c0 in range(0, AT_W, PROJ_COLS):
        aq = proj(C_AQ + c0)
        ak = proj(C_AK + c0)
        av = proj(C_AV + c0)
        for hh in range(PROJ_COLS // LANES):
            gp = c0 // LANES + hh
            ls = slice(hh * LANES, (hh + 1) * LANES)
            emit(rope(aq[:, ls]) * qscale, gp, aq1_ref, aq4_ref, aq16_ref)
            emit(rope(ak[:, ls]), gp, ak1_ref, ak4_ref, ak16_ref)
            emit(av[:, ls], gp, av1_ref, av4_ref, av16_ref)


def _proj_call(x2, pos_row, nw, scale, shift, w_all, cw, alog_row, dtb_row, invf, sgn, tm):
    s, d = x2.shape
    row = lambda n: pl.BlockSpec((1, n), lambda i: (0, 0))
    full = lambda a: pl.BlockSpec(a.shape, lambda i: (0, 0))
    tile = lambda n: pl.BlockSpec((tm, n), lambda i: (i, 0))
    spb = SUPER // tm
    res_shape = lambda dil: jax.ShapeDtypeStruct((s // SUPER, dil, SUPER // dil, AT_W), BF16)
    res_spec = lambda dil: pl.BlockSpec((1, dil, tm // dil, AT_W), lambda i: (i // spb, 0, i % spb, 0))
    bf = lambda n: jax.ShapeDtypeStruct((s, n), BF16)
    outs = [bf(DN_W)] * 4 + [jax.ShapeDtypeStruct((s, LANES), F32), bf(AT_W)] + \
           [bf(AT_W), res_shape(DILATIONS[1]), res_shape(DILATIONS[2])] * 3
    out_specs = [tile(DN_W)] * 4 + [tile(LANES), tile(AT_W)] + \
                [tile(AT_W), res_spec(DILATIONS[1]), res_spec(DILATIONS[2])] * 3
    return pl.pallas_call(
        _proj_body,
        out_shape=outs,
        grid=(s // tm,),
        in_specs=[tile(d), pl.BlockSpec((1, tm), lambda i: (0, i)), row(d), row(d), row(d),
                  pl.BlockSpec(w_all.shape, lambda i: (0, 0), pipeline_mode=pl.Buffered(1)),
                  full(cw), row(LANES), row(LANES), full(invf), row(LANES)],
        out_specs=out_specs,
        scratch_shapes=[pltpu.VMEM((3 * DN_W // LANES, SUBLANES + tm, LANES), F32), pltpu.VMEM((tm, d), BF16),
                        pltpu.VMEM((AT_W // LANES, tm, LANES), F32),
                        pltpu.VMEM((AT_W // LANES, tm, LANES), F32)],
        compiler_params=_params(("arbitrary",)),
        name="proj",
    )(x2, pos_row, nw, scale, shift, w_all, cw, alog_row, dtb_row, invf, sgn)


def _dn_body(q_ref, k_ref, v_ref, bg_ref, z_ref, nw_ref, o_ref, w_s, bt_s, op_s, gl_s, st_ref):
    step = pl.program_id(0)
    cur = step % 2
    prv = 1 - cur

    @pl.when(step == 0)
    def _():
        st_ref[...] = jnp.zeros(st_ref.shape, F32)
        w_s[1] = jnp.zeros(w_s.shape[1:], BF16)
        bt_s[1] = jnp.zeros(bt_s.shape[1:], BF16)
        op_s[1] = jnp.zeros(op_s.shape[1:], BF16)
        gl_s[1] = jnp.zeros(gl_s.shape[1:], F32)

    rows = q_ref.shape[0]
    nchunk = rows // CHUNK
    hsl = lambda hd: slice(hd * DN_HD, (hd + 1) * DN_HD)

    def scan_chunk(c):
        rs = slice(c * CHUNK, (c + 1) * CHUNK)
        for hd in range(DN_HEADS):
            st = st_ref[hd]
            r = jnp.dot(st.astype(BF16), w_s[prv, c, hd], preferred_element_type=F32)
            st_ref[hd] = gl_s[prv, c, hd:hd + 1, :] * st - r[:, 0:DN_HD] + bt_s[prv, c, hd].astype(F32)
            half = (hd % 2) * CHUNK
            o = r[:, DN_HD:2 * DN_HD].T[half:half + CHUNK] + op_s[prv, rs, hsl(hd)].astype(F32)
            ms = jnp.mean(o * o, axis=-1, keepdims=True)
            o_ref[rs, hsl(hd)] = (o * lax.rsqrt(ms + EPS) * nw_ref[...] * z_ref[rs, hsl(hd)].astype(F32)).astype(BF16)

    pending = list(range(nchunk))

    def tick():
        if pending:
            scan_chunk(pending.pop(0))

    wide = DN_HEADS * CHUNK
    bg = bg_ref[...]
    cs_rows = min(rows, MXU_COLS)
    ri = _iota((cs_rows, cs_rows), 0)
    ci = _iota((cs_rows, cs_rows), 1)
    tril_blk = jnp.where((ri // CHUNK == ci // CHUNK) & (ri >= ci), 1.0, 0.0).astype(BF16)
    g_hi = bg.astype(BF16)
    g_r1 = bg - g_hi.astype(F32)
    g_mid = g_r1.astype(BF16)
    g_lo = (g_r1 - g_mid.astype(F32)).astype(BF16)
    g3 = jnp.concatenate([g_hi, g_mid, g_lo], axis=1)
    csum = jnp.concatenate([jnp.dot(tril_blk, g3[r0:r0 + cs_rows], preferred_element_type=F32)
                            for r0 in range(0, rows, cs_rows)], axis=0)
    gcum = csum[:, 0:LANES] + csum[:, LANES:2 * LANES] + csum[:, 2 * LANES:3 * LANES]
    bfull = jnp.concatenate([jnp.broadcast_to(bg[:, hd:hd + 1], (rows, DN_HD)) for hd in range(DN_HEADS)], axis=1)
    gfull = jnp.concatenate([jnp.broadcast_to(gcum[:, DN_HEADS + hd:DN_HEADS + hd + 1], (rows, DN_HD))
                             for hd in range(DN_HEADS)], axis=1)
    eg = jnp.exp(gfull)
    q = q_ref[...].astype(F32)
    k = k_ref[...].astype(F32)
    kb = k * bfull
    vb = v_ref[...].astype(F32) * bfull
    q_dec = q * eg
    kbg = kb * eg

    r64 = _iota((CHUNK, wide), 0)
    j64 = _iota((CHUNK, wide), 1) % CHUNK
    tril = r64 >= j64
    strict = r64 > j64
    eye = jnp.where(r64 == j64, 1.0, 0.0).astype(F32)
    low_half = _iota((CHUNK, LANES), 1) < CHUNK
    bd64 = _iota((wide, wide), 0) // CHUNK == _iota((wide, wide), 1) // CHUNK
    bd_k = _iota((wide, DN_W), 0) // CHUNK == _iota((wide, DN_W), 1) // DN_HD

    def block_diag(m):
        return jnp.where(bd64, jnp.concatenate([m.astype(BF16)] * DN_HEADS, axis=0), jnp.zeros((wide, wide), BF16))

    rsl = lambda c: slice(c * CHUNK, (c + 1) * CHUNK)
    chunks = range(nchunk)
    g_last = [gfull[c * CHUNK + CHUNK - 1:(c + 1) * CHUNK, :] for c in chunks]
    k_dec = [k[rsl(c)] * jnp.exp(g_last[c] - gfull[rsl(c)]) for c in chunks]
    g_row = [jnp.concatenate([gcum[rsl(c)], gcum[rsl(c)]], axis=0).T for c in chunks]
    dec = []
    for c in chunks:
        halves = []
        for pr in range(DN_HEADS // 2):
            ha, hb = 2 * pr, 2 * pr + 1
            col = jnp.where(low_half, gfull[rsl(c), hsl(ha)], gfull[rsl(c), hsl(hb)])
            row = jnp.where(low_half[0:1], g_row[c][DN_HEADS + ha:DN_HEADS + ha + 1, :],
                            g_row[c][DN_HEADS + hb:DN_HEADS + hb + 1, :])
            halves.append(col - row)
        dec.append(jnp.exp(jnp.where(tril, jnp.concatenate(halves, axis=1), NEG)))
    kq = [lax.dot_general(jnp.concatenate([kb[rsl(c)].astype(BF16), q_ref[rsl(c), :]], axis=0),
                          jnp.where(bd_k, jnp.concatenate([k_ref[rsl(c), :]] * DN_HEADS, axis=0),
                                    jnp.zeros((wide, DN_W), BF16)),
                          NT_DIMS, preferred_element_type=F32) for c in chunks]
    tick()
    a_low = [jnp.where(strict, m[0:CHUNK] * d, 0.0) for m, d in zip(kq, dec)]
    attn = [jnp.where(tril, m[CHUNK:2 * CHUNK] * d, 0.0) for m, d in zip(kq, dec)]
    bmat = [-a for a in a_low]
    ymat = [eye + b for b in bmat]
    bmat = [jnp.dot(b.astype(BF16), block_diag(b), preferred_element_type=F32) for b in bmat]
    tick()
    for _ in range(4):
        prod = [jnp.dot(jnp.concatenate([y, b], axis=0).astype(BF16), block_diag(b), preferred_element_type=F32)
                for y, b in zip(ymat, bmat)]
        tick()
        ymat = [y + p[0:CHUNK] for y, p in zip(ymat, prod)]
        bmat = [p[CHUNK:2 * CHUNK] for p in prod]
    ymat = [y + jnp.dot(y.astype(BF16), block_diag(b), preferred_element_type=F32) for y, b in zip(ymat, bmat)]
    tick()
    rhs = [jnp.concatenate([jnp.concatenate([kbg[rsl(c), hsl(hd)], vb[rsl(c), hsl(hd)]], axis=1)
                            for hd in range(DN_HEADS)], axis=0).astype(BF16) for c in chunks]
    wu = [jnp.dot(block_diag(y), r, preferred_element_type=F32) for y, r in zip(ymat, rhs)]
    tick()
    wut = [m.T.astype(BF16) for m in wu]
    kdbd = [jnp.where(bd_k, jnp.concatenate([k_dec[c].astype(BF16)] * DN_HEADS, axis=0),
                      jnp.zeros((wide, DN_W), BF16)) for c in chunks]
    pbt = [jnp.dot(a, b, preferred_element_type=F32) for a, b in zip(wut, kdbd)]
    tick()
    aw = [jnp.dot(block_diag(a), m.astype(BF16), preferred_element_type=F32) for a, m in zip(attn, wu)]
    qpt = [(jnp.concatenate([q_dec[rsl(c), hsl(hd)] for hd in range(DN_HEADS)], axis=0) - a[:, 0:DN_HD]).T
           .astype(BF16) for c, a in zip(chunks, aw)]
    while pending:
        tick()
    for c in chunks:
        for hd in range(DN_HEADS):
            pair = slice((hd // 2) * LANES, (hd // 2 + 1) * LANES)
            w_s[cur, c, hd] = jnp.concatenate([pbt[c][0:DN_HD, hsl(hd)].astype(BF16), qpt[c][:, pair]], axis=1)
            bt_s[cur, c, hd] = pbt[c][DN_HD:2 * DN_HD, hsl(hd)].astype(BF16)
            op_s[cur, rsl(c), hsl(hd)] = aw[c][hd * CHUNK:(hd + 1) * CHUNK, DN_HD:2 * DN_HD].astype(BF16)
        gl_rows = [jnp.exp(g_last[c][:, hsl(hd)]) for hd in range(DN_HEADS)]
        gl_rows.append(jnp.zeros((SUBLANES - DN_HEADS, DN_HD), F32))
        gl_s[cur, c] = jnp.concatenate(gl_rows, axis=0)


def _dn_call(qn, kn, v, bg, zdn, nw_row, rows):
    s = qn.shape[0]
    nblk = s // rows
    ncb = rows // CHUNK
    cur = lambda n: pl.BlockSpec((rows, n), lambda i: (jnp.minimum(i, nblk - 1), 0))
    prv = lambda n: pl.BlockSpec((rows, n), lambda i: (jnp.maximum(i - 1, 0), 0))
    return pl.pallas_call(
        _dn_body,
        out_shape=jax.ShapeDtypeStruct((s, DN_W), BF16),
        grid=(nblk + 1,),
        in_specs=[cur(DN_W), cur(DN_W), cur(DN_W), cur(LANES), prv(DN_W), pl.BlockSpec((1, DN_HD), lambda i: (0, 0))],
        out_specs=prv(DN_W),
        scratch_shapes=[pltpu.VMEM((2, ncb, DN_HEADS, DN_HD, 2 * DN_HD), BF16),
                        pltpu.VMEM((2, ncb, DN_HEADS, DN_HD, DN_HD), BF16),
                        pltpu.VMEM((2, rows, DN_W), BF16),
                        pltpu.VMEM((2, ncb, SUBLANES, DN_HD), F32),
                        pltpu.VMEM((DN_HEADS, DN_HD, DN_HD), F32)],
        compiler_params=_params(("arbitrary",)),
        name="deltanet",
    )(qn, kn, v, bg, zdn, nw_row)


def _attn_body(*refs):
    ins, scratch = refs[:-6], refs[-6:]
    for p in range(ATT_PAIRS):
        _attn_pair(*[r.at[:, p * LANES:(p + 1) * LANES] for r in ins], *scratch)


def _attn_pair(q1, q4, q16, k1, k4, k16, v1, v4, v16, kp1, vp1, kp4a, kp4b, kp4c, kp4d,
               vp4a, vp4b, vp4c, vp4d, kp16, vp16, z_ref, anw_ref, o_ref, m_s, l_s, a_s, m_n, l_n, a_n):
    d4, d16 = DILATIONS[1], DILATIONS[2]
    seg4 = SUPER // d4
    kp4 = (kp4a, kp4b, kp4c, kp4d)
    vp4 = (vp4a, vp4b, vp4c, vp4d)
    kj = _iota((2 * Q_BLOCK, Q_BLOCK), 0)
    qi = _iota((2 * Q_BLOCK, Q_BLOCK), 1)
    rel = Q_BLOCK + qi - kj
    band = (rel >= 0) & (rel <= W_SUB)
    bias_in = jnp.where(band, 0.0, NEG).astype(BF16)
    bias_first = jnp.where(band & (kj >= Q_BLOCK), 0.0, NEG).astype(BF16)
    bias_edge = jnp.where(pl.program_id(0) == 0, bias_first, bias_in)
    one_hot = jnp.where(kj % Q_BLOCK == qi, 1.0, 0.0).astype(BF16)
    lane = _iota((Q_BLOCK, LANES), 1)
    lane_k = _iota((2 * Q_BLOCK, LANES), 1)
    head_a_q = (lane % AT_HD) < (AT_HD // 2)
    head_a = lane < AT_HD
    head_a_k = lane_k < AT_HD
    ro = _iota((4 * Q_BLOCK, LANES), 0)
    ones_rhs = jnp.where((ro < 2 * Q_BLOCK) == (_iota((4 * Q_BLOCK, LANES), 1) < AT_HD), 1.0, 0.0).astype(BF16)
    sr = _iota((2 * LANES, LANES), 0)
    ssq_rhs = jnp.where((sr % LANES) // AT_HD == _iota((2 * LANES, LANES), 1) // AT_HD, 1.0, 0.0).astype(BF16)
    zq = jnp.zeros((Q_BLOCK, LANES), BF16)
    zv = jnp.zeros((2 * Q_BLOCK, LANES), BF16)

    def scores(load):
        q, kb, vb, bias = load()
        q2 = jnp.concatenate([jnp.where(head_a_q, q, zq), jnp.where(head_a_q, zq, q)], axis=0)
        s = lax.dot_general(jnp.concatenate([q2, one_hot], axis=1), jnp.concatenate([kb, bias], axis=1),
                            NT_DIMS, preferred_element_type=F32)
        m = jnp.max(s, axis=-1, keepdims=True)
        return jnp.exp2(s - m).astype(BF16), m, vb

    def values(p, m, vb):
        pc = jnp.concatenate([p[0:Q_BLOCK], p[Q_BLOCK:2 * Q_BLOCK]], axis=1)
        rhs = jnp.concatenate([jnp.concatenate([jnp.where(head_a_k, vb, zv), jnp.where(head_a_k, zv, vb)], axis=0),
                               ones_rhs], axis=1)
        r = jnp.dot(pc, rhs, preferred_element_type=F32)
        return r[:, 0:LANES], r[:, LANES:2 * LANES], jnp.where(head_a, m[0:Q_BLOCK], m[Q_BLOCK:2 * Q_BLOCK])

    def run(loads, finish, post=None):
        n = len(loads)
        pending, done = {}, {}
        for i in range(n + ATT_LAG + 1):
            if i < n:
                pending[i] = scores(loads[i])
            j = i - ATT_LAG
            if 0 <= j < n:
                done[j] = finish(j, *values(*pending.pop(j)))
            if post is not None and 0 <= j - 1 < n:
                post(j - 1, done.pop(j - 1))

    def merged(acc, l, m, m_old, l_old, a_old):
        m_new = jnp.maximum(m_old, m)
        w_old = jnp.exp2(m_old - m_new)
        w_new = jnp.exp2(m - m_new)
        return a_old * w_old + acc * w_new, l_old * w_old + l * w_new, m_new

    def load16(r):
        rs = slice(r * Q_BLOCK, (r + 1) * Q_BLOCK)
        return lambda: (q16[rs, :], jnp.concatenate([kp16[rs, :], k16[rs, :]], axis=0),
                        jnp.concatenate([vp16[rs, :], v16[rs, :]], axis=0), bias_edge)

    def finish16(r, acc, l, m):
        rows = pl.ds((r % d4) * seg4 + r // d4, Q_BLOCK, stride=d4)
        a_s[rows, :] = acc
        l_s[rows, :] = l
        m_s[rows, :] = m

    run([load16(r) for r in range(d16)], finish16)

    def load4(r, b):
        base = r * seg4 + b * Q_BLOCK
        if b == 0:
            return lambda: (q4[base:base + Q_BLOCK, :],
                            jnp.concatenate([kp4[r][...], k4[base:base + Q_BLOCK, :]], axis=0),
                            jnp.concatenate([vp4[r][...], v4[base:base + Q_BLOCK, :]], axis=0), bias_edge)
        return lambda: (q4[base:base + Q_BLOCK, :], k4[base - Q_BLOCK:base + Q_BLOCK, :],
                        v4[base - Q_BLOCK:base + Q_BLOCK, :], bias_in)

    blocks4 = [(r, b) for b in range(seg4 // Q_BLOCK) for r in range(d4)]

    def finish4(j, acc, l, m):
        r, b = blocks4[j]
        rows = slice(r * seg4 + b * Q_BLOCK, r * seg4 + (b + 1) * Q_BLOCK)
        acc, l, m = merged(acc, l, m, m_s[rows, :], l_s[rows, :], a_s[rows, :])
        a_s[rows, :] = acc
        l_s[rows, :] = l
        m_s[rows, :] = m

    run([load4(r, b) for r, b in blocks4], finish4)

    for c in range(d4):
        src = slice(c * seg4, (c + 1) * seg4)
        dst = pl.ds(c, seg4, stride=d4)
        m_n[dst, :] = m_s[src, :]
        l_n[dst, :] = l_s[src, :]
        a_n[dst, :] = a_s[src, :]

    def load1(b):
        rs = slice(b * Q_BLOCK, (b + 1) * Q_BLOCK)
        if b == 0:
            return lambda: (q1[rs, :], jnp.concatenate([kp1[...], k1[rs, :]], axis=0),
                            jnp.concatenate([vp1[...], v1[rs, :]], axis=0), bias_edge)
        ks = slice((b - 1) * Q_BLOCK, (b + 1) * Q_BLOCK)
        return lambda: (q1[rs, :], k1[ks, :], v1[ks, :], bias_in)

    def finish1(b, acc, l, m):
        rs = slice(b * Q_BLOCK, (b + 1) * Q_BLOCK)
        acc, l, _ = merged(acc, l, m, m_n[rs, :], l_n[rs, :], a_n[rs, :])
        o = acc / l
        sq = o * o
        hi = sq.astype(BF16)
        return o, jnp.concatenate([hi, (sq - hi.astype(F32)).astype(BF16)], axis=1)

    def post1(b, res):
        o, sq2 = res
        rs = slice(b * Q_BLOCK, (b + 1) * Q_BLOCK)
        ssq = jnp.dot(sq2, ssq_rhs, preferred_element_type=F32)
        o_ref[rs, :] = (o * lax.rsqrt(ssq * (1.0 / AT_HD) + EPS) * anw_ref[...]
                        * z_ref[rs, :].astype(F32)).astype(BF16)

    run([load1(b) for b in range(SUPER // Q_BLOCK)], finish1, post1)


def _attn_call(aq, ak, av, zat, anw_row):
    s = zat.shape[0]
    d4 = DILATIONS[1]
    wl = ATT_PAIRS * LANES
    per_super = SUPER // Q_BLOCK
    cur = pl.BlockSpec((SUPER, wl), lambda m, g: (m, g))
    prev16 = pl.BlockSpec((SUPER, wl), lambda m, g: (jnp.maximum(m - 1, 0), g))
    prev1 = pl.BlockSpec((Q_BLOCK, wl), lambda m, g: (jnp.maximum(m * per_super - 1, 0), g))
    prev4 = [pl.BlockSpec((Q_BLOCK, wl),
                          lambda m, g, r=r: (jnp.maximum((m - 1) * per_super + (r + 1) * (per_super // d4) - 1, 0), g))
             for r in range(d4)]
    state = pltpu.VMEM((SUPER, LANES), F32)
    return pl.pallas_call(
        _attn_body,
        out_shape=jax.ShapeDtypeStruct((s, AT_W), BF16),
        grid=(s // SUPER, AT_W // wl),
        in_specs=[cur] * 9 + [prev1, prev1] + prev4 + prev4 + [prev16, prev16, cur,
                                                               pl.BlockSpec((1, wl), lambda m, g: (0, g))],
        out_specs=cur,
        scratch_shapes=[state] * 6,
        compiler_params=_params(("arbitrary", "arbitrary")),
        name="attn",
    )(*aq, *ak, *av, ak[0], av[0], *([ak[1]] * d4), *([av[1]] * d4), ak[2], av[2], zat, anw_row)


def _out_body(x_hbm, odn_hbm, oat_hbm, gate_ref, w_ref, fnw_ref, o_hbm, xb, db, ab, ob, in_sem, out_sem):
    tm = xb.shape[1]
    ntile = x_hbm.shape[0] // tm

    def in_copies(t, slot):
        rows = pl.ds(pl.multiple_of(t * tm, tm), tm)
        return (pltpu.make_async_copy(x_hbm.at[rows, :], xb.at[slot], in_sem.at[0, slot]),
                pltpu.make_async_copy(odn_hbm.at[rows, :], db.at[slot], in_sem.at[1, slot]),
                pltpu.make_async_copy(oat_hbm.at[rows, :], ab.at[slot], in_sem.at[2, slot]))

    def out_copy(t, slot):
        rows = pl.ds(pl.multiple_of(t * tm, tm), tm)
        return pltpu.make_async_copy(ob.at[slot], o_hbm.at[rows, :], out_sem.at[slot])

    for t in range(OUT_IN_BUFS):
        for cp in in_copies(t, t):
            cp.start()

    def step(t, carry):
        slot = t % OUT_IN_BUFS
        oslot = t % 2
        for cp in in_copies(t, slot):
            cp.wait()

        @pl.when(t >= 2)
        def _():
            out_copy(t - 2, oslot).wait()

        mixin = jnp.concatenate([db[slot], ab[slot]], axis=1)
        mix = jnp.dot(mixin, w_ref[...], preferred_element_type=F32)
        y = xb[slot] + gate_ref[...] * mix
        ms = jnp.mean(y * y, axis=-1, keepdims=True)
        ob[oslot] = y * lax.rsqrt(ms + EPS) * fnw_ref[...]
        out_copy(t, oslot).start()

        @pl.when(t + OUT_IN_BUFS < ntile)
        def _():
            for cp in in_copies(t + OUT_IN_BUFS, slot):
                cp.start()

        return carry

    lax.fori_loop(0, ntile, step, 0)
    out_copy(ntile - 2, (ntile - 2) % 2).wait()
    out_copy(ntile - 1, (ntile - 1) % 2).wait()


def _out_call(x2, odn, oat, gate, w_out, fnw_row, tm):
    s, d = x2.shape
    assert s // tm >= max(OUT_IN_BUFS, 2)
    hbm = pl.BlockSpec(memory_space=pl.ANY)
    vmem = pl.BlockSpec(memory_space=pltpu.VMEM)
    return pl.pallas_call(
        _out_body,
        out_shape=jax.ShapeDtypeStruct((s, d), F32),
        in_specs=[hbm, hbm, hbm, vmem, vmem, vmem],
        out_specs=hbm,
        scratch_shapes=[pltpu.VMEM((OUT_IN_BUFS, tm, d), F32), pltpu.VMEM((OUT_IN_BUFS, tm, DN_W), BF16),
                        pltpu.VMEM((OUT_IN_BUFS, tm, AT_W), BF16), pltpu.VMEM((2, tm, d), F32),
                        pltpu.SemaphoreType.DMA((3, OUT_IN_BUFS)), pltpu.SemaphoreType.DMA((2,))],
        compiler_params=pltpu.CompilerParams(vmem_limit_bytes=VMEM_LIMIT),
        name="out",
    )(x2, odn, oat, gate, w_out, fnw_row)


def kernel(x, c, positions, w_mod, b_mod, norm_w, w_in, conv_w, a_log, dt_bias, dn_norm_w, at_norm_w,
           w_out, final_norm_w):
    b, s, d = x.shape
    assert b == 1 and d == D_MODEL and w_mod.shape[0] == 1
    x2 = x.reshape(s, d)

    mod = _mod_call(c.reshape(d, 1), w_mod[0], b_mod[0].reshape(1, 3 * d))
    shift, scale, gate = mod[:, 0:d], mod[:, d:2 * d], mod[:, 2 * d:3 * d]

    w_all, w_out_b = _wpack_call(w_in, w_out)
    pad_heads = lambda v: jnp.pad(v.reshape(1, DN_HEADS), ((0, 0), (DN_HEADS, LANES - 2 * DN_HEADS)))
    half = AT_HD // 2
    invf = (ROPE_THETA ** (-jnp.arange(half, dtype=F32) / half)).reshape(half, 1)
    sgn = jnp.where(jnp.arange(LANES) < LANES // 2, -1.0, 1.0).astype(F32).reshape(1, LANES)

    assert s % SUPER == 0
    outs = _proj_call(x2, positions.reshape(1, s), norm_w[0].reshape(1, d), scale, shift, w_all, conv_w[0],
                      pad_heads(a_log[0]), pad_heads(dt_bias[0]), invf, sgn, tm=512)
    qn, kn, v, zdn, bg, zat = outs[0:6]
    flat = lambda a: a.reshape(s, AT_W)
    aq, ak, av = [[flat(a) for a in outs[j:j + 3]] for j in (6, 9, 12)]

    odn = _dn_call(qn, kn, v, bg, zdn, dn_norm_w[0].reshape(1, DN_HD), rows=512)
    oat = _attn_call(aq, ak, av, zat, jnp.tile(at_norm_w[0], AT_HEADS).reshape(1, AT_W))

    out = _out_call(x2, odn, oat, gate, w_out_b, final_norm_w.reshape(1, d), tm=512)
    return out.reshape(b, s, d)
```

```python
import math

import jax
import jax.numpy as jnp
from jax import lax
from jax.experimental import pallas as pl
from jax.experimental.pallas import tpu as pltpu

D_MODEL = 1024
DN_HEADS = 4
DN_HD = 128
DN_W = DN_HEADS * DN_HD
AT_HEADS = 8
AT_HD = 64
AT_W = AT_HEADS * AT_HD
CONV_K = 4
CHUNK = 64
DILATIONS = (1, 4, 16)
W_SUB = 128
Q_BLOCK = 128
SUPER = Q_BLOCK * DILATIONS[-1]
OUT_IN_BUFS = 3
ATT_PAIRS = 2
ATT_LAG = 2
ROPE_THETA = 10000.0
EPS = 1e-6
NEG = -1e30

LANES = 128
SUBLANES = 8
MXU_COLS = 256
PROJ_COLS = 512
VMEM_LIMIT = 48 * 1024 * 1024

C_QKV, C_Z, C_AQ, C_AK, C_AV, C_AZ, C_BA = 0, 1536, 2048, 2560, 3072, 3584, 4096
W_COLS = C_BA + LANES

F32 = jnp.float32
BF16 = jnp.bfloat16
NT_DIMS = (((1,), (1,)), ((), ()))


def _sigmoid(v):
    return 1.0 / (1.0 + jnp.exp2(v * (-math.log2(math.e))))


def _iota(shape, dim):
    return lax.broadcasted_iota(jnp.int32, shape, dim)


def _params(sem):
    return pltpu.CompilerParams(dimension_semantics=sem, vmem_limit_bytes=VMEM_LIMIT)


def _mod_body(c_ref, w_ref, b_ref, o_ref):
    c = c_ref[...]
    sc = c * _sigmoid(c)
    o_ref[...] = jnp.sum(sc * w_ref[...], axis=0, keepdims=True) + b_ref[...]


def _mod_call(c_col, w_mod, b_mod):
    d, n = w_mod.shape
    tn = 1024
    return pl.pallas_call(
        _mod_body,
        out_shape=jax.ShapeDtypeStruct((1, n), F32),
        grid=(n // tn,),
        in_specs=[pl.BlockSpec((d, 1), lambda j: (0, 0)),
                  pl.BlockSpec((d, tn), lambda j: (0, j)),
                  pl.BlockSpec((1, tn), lambda j: (0, j))],
        out_specs=pl.BlockSpec((1, tn), lambda j: (0, j)),
        compiler_params=_params(("arbitrary",)),
        name="mod",
    )(c_col, w_mod, b_mod)


def _wpack_body(wt_ref, wo_ref, o_ref, oo_ref):
    kb = wt_ref.shape[2]
    o_b = 4 * DN_W
    o_q = o_b + 2 * DN_HEADS
    half = AT_HD // 2

    def put(col0, src):
        o_ref[:, col0:col0 + LANES] = src.T.astype(BF16)

    for g in range(o_b // LANES):
        put(C_QKV + g * LANES, wt_ref[0, g * LANES:(g + 1) * LANES, :])
    for g in range(2 * AT_W // LANES):
        base = o_q + g * LANES
        put(C_AQ + g * LANES, jnp.concatenate([wt_ref[0, base:base + half, :],
                                               wt_ref[0, base + 2 * half:base + 3 * half, :],
                                               wt_ref[0, base + half:base + 2 * half, :],
                                               wt_ref[0, base + 3 * half:base + 4 * half, :]], axis=0))
    for g in range(2 * AT_W // LANES):
        base = o_q + 2 * AT_W + g * LANES
        put(C_AV + g * LANES, wt_ref[0, base:base + LANES, :])
    put(C_BA, jnp.concatenate([wt_ref[0, o_b:o_b + 2 * DN_HEADS, :],
                               jnp.zeros((LANES - 2 * DN_HEADS, kb), F32)], axis=0))
    oo_ref[...] = wo_ref[0].astype(BF16)


def _wpack_call(w_in, w_out):
    _, d, n_in = w_in.shape
    n_out = w_out.shape[2]
    kb = 512
    w_t = jnp.swapaxes(w_in, 1, 2)
    return pl.pallas_call(
        _wpack_body,
        out_shape=[jax.ShapeDtypeStruct((d, W_COLS), BF16), jax.ShapeDtypeStruct((w_out.shape[1], n_out), BF16)],
        grid=(d // kb,),
        in_specs=[pl.BlockSpec((1, n_in, kb), lambda i: (0, 0, i)),
                  pl.BlockSpec((1, kb, n_out), lambda i: (0, i, 0))],
        out_specs=[pl.BlockSpec((kb, W_COLS), lambda i: (i, 0)), pl.BlockSpec((kb, n_out), lambda i: (i, 0))],
        compiler_params=_params(("arbitrary",)),
        name="wpack",
    )(w_t, w_out)


def _normalise(x_ref, nw_ref, sc_ref, sh_ref, hb_ref):
    x = x_ref[...]
    ms = jnp.mean(x * x, axis=-1, keepdims=True)
    hb_ref[...] = ((x * lax.rsqrt(ms + EPS)) * (nw_ref[...] * (1.0 + sc_ref[...])) + sh_ref[...]).astype(BF16)


def _silu(v):
    return v * _sigmoid(v)


def _proj_dn_body(x_ref, nw_ref, sc_ref, sh_ref, w_ref, cw_ref, alog_ref, dtb_ref,
                  qn_ref, kn_ref, v_ref, zdn_ref, bg_ref, cbuf_ref, hb_ref):
    i = pl.program_id(0)
    tm = x_ref.shape[0]
    _normalise(x_ref, nw_ref, sc_ref, sh_ref, hb_ref)

    def proj(c0, width=PROJ_COLS):
        return jnp.dot(hb_ref[...], w_ref[:, c0:c0 + width], preferred_element_type=F32)

    @pl.when(i == 0)
    def _():
        cbuf_ref[:, 0:SUBLANES, :] = jnp.zeros((cbuf_ref.shape[0], SUBLANES, LANES), F32)

    dyn0 = jnp.minimum(i, 0)
    for c0 in range(0, 3 * DN_W, PROJ_COLS):
        xg = proj(C_QKV + c0)
        halves = []
        for hh in range(PROJ_COLS // LANES):
            grp = c0 // LANES + hh
            ls = slice(c0 + hh * LANES, c0 + (hh + 1) * LANES)
            xh = xg[:, hh * LANES:(hh + 1) * LANES]
            cbuf_ref[grp, SUBLANES:SUBLANES + tm, :] = xh
            acc = xh * cw_ref[CONV_K - 1:CONV_K, ls]
            for j in range(1, CONV_K):
                acc = acc + cbuf_ref[grp, pl.ds(dyn0 + SUBLANES - j, tm), :] * cw_ref[CONV_K - 1 - j:CONV_K - j, ls]
            cbuf_ref[grp, 0:SUBLANES, :] = xh[tm - SUBLANES:tm]
            halves.append(acc)
        y = _silu(jnp.concatenate(halves, axis=1))
        kind, off = c0 // DN_W, c0 % DN_W
        for hh in range(PROJ_COLS // DN_HD):
            yh = y[:, hh * DN_HD:(hh + 1) * DN_HD]
            dst = slice(off + hh * DN_HD, off + (hh + 1) * DN_HD)
            if kind == 2:
                v_ref[:, dst] = yh.astype(BF16)
            else:
                inv = lax.rsqrt(jnp.sum(yh * yh, axis=-1, keepdims=True) + EPS)
                if kind == 0:
                    qn_ref[:, dst] = (yh * (inv * (DN_HD ** -0.5))).astype(BF16)
                else:
                    kn_ref[:, dst] = (yh * inv).astype(BF16)

    for c0 in range(0, DN_W, PROJ_COLS):
        zdn_ref[:, c0:c0 + PROJ_COLS] = _silu(proj(C_Z + c0)).astype(BF16)

    ba = proj(C_BA, LANES)
    lane = _iota(ba.shape, 1)
    a = ba + dtb_ref[...]
    softplus = jnp.maximum(a, 0.0) + jnp.log(1.0 + jnp.exp(-jnp.abs(a)))
    g = -jnp.exp(alog_ref[...]) * softplus
    bg_ref[...] = jnp.where(lane < DN_HEADS, _sigmoid(ba), jnp.where(lane < 2 * DN_HEADS, g, 0.0))


def _proj_at_body(x_ref, pos_ref, nw_ref, sc_ref, sh_ref, w_ref, invf_ref, sgn_ref, zat_ref,
                  aq1_ref, aq4_ref, aq16_ref, ak1_ref, ak4_ref, ak16_ref, av1_ref, av4_ref, av16_ref,
                  hb_ref, pbuf_ref, p4buf_ref):
    tm = x_ref.shape[0]
    _normalise(x_ref, nw_ref, sc_ref, sh_ref, hb_ref)

    def proj(c0, width=PROJ_COLS):
        return jnp.dot(hb_ref[...], w_ref[:, c0:c0 + width], preferred_element_type=F32)

    for c0 in range(0, AT_W, PROJ_COLS):
        zat_ref[:, c0:c0 + PROJ_COLS] = _silu(proj(C_AZ + c0)).astype(BF16)

    ang = invf_ref[...] * pos_ref[...].astype(F32)
    reps = LANES // (AT_HD // 2)
    cos = jnp.concatenate([jnp.cos(ang)] * reps, axis=0).T
    sin = jnp.concatenate([jnp.sin(ang)] * reps, axis=0).T * sgn_ref[...]
    qscale = (AT_HD ** -0.5) * math.log2(math.e)
    d4, d16 = DILATIONS[1], DILATIONS[2]
    n4, n16 = tm // d4, tm // d16

    def emit(val, gp, nat_ref, p4_ref, p16_ref):
        sl = slice(gp * LANES, (gp + 1) * LANES)
        nat_ref[:, sl] = val.astype(BF16)
        pbuf_ref[gp] = val
        for c in range(d4):
            seg = pbuf_ref[gp, pl.ds(c, n4, stride=d4), :]
            p4_ref[0, c, :, sl] = seg.astype(BF16)
            p4buf_ref[gp, c * n4:(c + 1) * n4, :] = seg
        for r in range(d16):
            seg = p4buf_ref[gp, pl.ds((r % d4) * n4 + r // d4, n16, stride=d4), :]
            p16_ref[0, r, :, sl] = seg.astype(BF16)

    def rope(v):
        return v * cos + pltpu.roll(v, LANES // 2, axis=1) * sin

    for c0 in range(0, AT_W, PROJ_COLS):
        aq = proj(C_AQ + c0)
        ak = proj(C_AK + c0)
        av = proj(C_AV + c0)
        for hh in range(PROJ_COLS // LANES):
            gp = c0 // LANES + hh
            ls = slice(hh * LANES, (hh + 1) * LANES)
            emit(rope(aq[:, ls]) * qscale, gp, aq1_ref, aq4_ref, aq16_ref)
            emit(rope(ak[:, ls]), gp, ak1_ref, ak4_ref, ak16_ref)
            emit(av[:, ls], gp, av1_ref, av4_ref, av16_ref)


def _proj_call(x2, pos_row, nw, scale, shift, w_all, cw, alog_row, dtb_row, invf, sgn, tm):
    s, d = x2.shape
    row = lambda n: pl.BlockSpec((1, n), lambda i: (0, 0))
    full = lambda a: pl.BlockSpec(a.shape, lambda i: (0, 0))
    tile = lambda n: pl.BlockSpec((tm, n), lambda i: (i, 0))
    wspec = pl.BlockSpec(w_all.shape, lambda i: (0, 0), pipeline_mode=pl.Buffered(1))
    bf = lambda n: jax.ShapeDtypeStruct((s, n), BF16)
    dn_outs = pl.pallas_call(
        _proj_dn_body,
        out_shape=[bf(DN_W)] * 4 + [jax.ShapeDtypeStruct((s, LANES), F32)],
        grid=(s // tm,),
        in_specs=[tile(d), row(d), row(d), row(d), wspec, full(cw), row(LANES), row(LANES)],
        out_specs=[tile(DN_W)] * 4 + [tile(LANES)],
        scratch_shapes=[pltpu.VMEM((3 * DN_W // LANES, SUBLANES + tm, LANES), F32), pltpu.VMEM((tm, d), BF16)],
        compiler_params=_params(("arbitrary",)),
        name="proj_dn",
    )(x2, nw, scale, shift, w_all, cw, alog_row, dtb_row)
    spb = SUPER // tm
    res_shape = lambda dil: jax.ShapeDtypeStruct((s // SUPER, dil, SUPER // dil, AT_W), BF16)
    res_spec = lambda dil: pl.BlockSpec((1, dil, tm // dil, AT_W), lambda i: (i // spb, 0, i % spb, 0))
    at_outs = pl.pallas_call(
        _proj_at_body,
        out_shape=[bf(AT_W)] + [bf(AT_W), res_shape(DILATIONS[1]), res_shape(DILATIONS[2])] * 3,
        grid=(s // tm,),
        in_specs=[tile(d), pl.BlockSpec((1, tm), lambda i: (0, i)), row(d), row(d), row(d), wspec,
                  full(invf), row(LANES)],
        out_specs=[tile(AT_W)] + [tile(AT_W), res_spec(DILATIONS[1]), res_spec(DILATIONS[2])] * 3,
        scratch_shapes=[pltpu.VMEM((tm, d), BF16), pltpu.VMEM((AT_W // LANES, tm, LANES), F32),
                        pltpu.VMEM((AT_W // LANES, tm, LANES), F32)],
        compiler_params=_params(("arbitrary",)),
        name="proj_at",
    )(x2, pos_row, nw, scale, shift, w_all, invf, sgn)
    return list(dn_outs) + list(at_outs)


def _dn_body(q_ref, k_ref, v_ref, bg_ref, z_ref, nw_ref, o_ref, w_s, bt_s, op_s, gl_s, st_ref):
    step = pl.program_id(0)
    cur = step % 2
    prv = 1 - cur

    @pl.when(step == 0)
    def _():
        st_ref[...] = jnp.zeros(st_ref.shape, F32)
        w_s[1] = jnp.zeros(w_s.shape[1:], BF16)
        bt_s[1] = jnp.zeros(bt_s.shape[1:], BF16)
        op_s[1] = jnp.zeros(op_s.shape[1:], BF16)
        gl_s[1] = jnp.zeros(gl_s.shape[1:], F32)

    rows = q_ref.shape[0]
    nchunk = rows // CHUNK
    hsl = lambda hd: slice(hd * DN_HD, (hd + 1) * DN_HD)

    def scan_chunk(c):
        rs = slice(c * CHUNK, (c + 1) * CHUNK)
        for hd in range(DN_HEADS):
            st = st_ref[hd]
            r = jnp.dot(st.astype(BF16), w_s[prv, c, hd], preferred_element_type=F32)
            st_ref[hd] = gl_s[prv, c, hd:hd + 1, :] * st - r[:, 0:DN_HD] + bt_s[prv, c, hd].astype(F32)
            half = (hd % 2) * CHUNK
            o = r[:, DN_HD:2 * DN_HD].T[half:half + CHUNK] + op_s[prv, rs, hsl(hd)].astype(F32)
            ms = jnp.mean(o * o, axis=-1, keepdims=True)
            o_ref[rs, hsl(hd)] = (o * lax.rsqrt(ms + EPS) * nw_ref[...] * z_ref[rs, hsl(hd)].astype(F32)).astype(BF16)

    pending = list(range(nchunk))

    def tick():
        if pending:
            scan_chunk(pending.pop(0))

    wide = DN_HEADS * CHUNK
    bg = bg_ref[...]
    cs_rows = min(rows, MXU_COLS)
    ri = _iota((cs_rows, cs_rows), 0)
    ci = _iota((cs_rows, cs_rows), 1)
    tril_blk = jnp.where((ri // CHUNK == ci // CHUNK) & (ri >= ci), 1.0, 0.0).astype(BF16)
    g_hi = bg.astype(BF16)
    g_r1 = bg - g_hi.astype(F32)
    g_mid = g_r1.astype(BF16)
    g_lo = (g_r1 - g_mid.astype(F32)).astype(BF16)
    g3 = jnp.concatenate([g_hi, g_mid, g_lo], axis=1)
    csum = jnp.concatenate([jnp.dot(tril_blk, g3[r0:r0 + cs_rows], preferred_element_type=F32)
                            for r0 in range(0, rows, cs_rows)], axis=0)
    gcum = csum[:, 0:LANES] + csum[:, LANES:2 * LANES] + csum[:, 2 * LANES:3 * LANES]
    bfull = jnp.concatenate([jnp.broadcast_to(bg[:, hd:hd + 1], (rows, DN_HD)) for hd in range(DN_HEADS)], axis=1)
    gfull = jnp.concatenate([jnp.broadcast_to(gcum[:, DN_HEADS + hd:DN_HEADS + hd + 1], (rows, DN_HD))
                             for hd in range(DN_HEADS)], axis=1)
    eg = jnp.exp(gfull)
    q = q_ref[...].astype(F32)
    k = k_ref[...].astype(F32)
    kb = k * bfull
    vb = v_ref[...].astype(F32) * bfull
    q_dec = q * eg
    kbg = kb * eg

    r64 = _iota((CHUNK, wide), 0)
    j64 = _iota((CHUNK, wide), 1) % CHUNK
    tril = r64 >= j64
    strict = r64 > j64
    eye = jnp.where(r64 == j64, 1.0, 0.0).astype(F32)
    low_half = _iota((CHUNK, LANES), 1) < CHUNK
    bd64 = _iota((wide, wide), 0) // CHUNK == _iota((wide, wide), 1) // CHUNK
    bd_k = _iota((wide, DN_W), 0) // CHUNK == _iota((wide, DN_W), 1) // DN_HD

    def block_diag(m):
        return jnp.where(bd64, jnp.concatenate([m.astype(BF16)] * DN_HEADS, axis=0), jnp.zeros((wide, wide), BF16))

    rsl = lambda c: slice(c * CHUNK, (c + 1) * CHUNK)
    chunks = range(nchunk)
    g_last = [gfull[c * CHUNK + CHUNK - 1:(c + 1) * CHUNK, :] for c in chunks]
    k_dec = [k[rsl(c)] * jnp.exp(g_last[c] - gfull[rsl(c)]) for c in chunks]
    g_row = [jnp.concatenate([gcum[rsl(c)], gcum[rsl(c)]], axis=0).T for c in chunks]
    dec = []
    for c in chunks:
        halves = []
        for pr in range(DN_HEADS // 2):
            ha, hb = 2 * pr, 2 * pr + 1
            col = jnp.where(low_half, gfull[rsl(c), hsl(ha)], gfull[rsl(c), hsl(hb)])
            row = jnp.where(low_half[0:1], g_row[c][DN_HEADS + ha:DN_HEADS + ha + 1, :],
                            g_row[c][DN_HEADS + hb:DN_HEADS + hb + 1, :])
            halves.append(col - row)
        dec.append(jnp.exp(jnp.where(tril, jnp.concatenate(halves, axis=1), NEG)))
    kq = [lax.dot_general(jnp.concatenate([kb[rsl(c)].astype(BF16), q_ref[rsl(c), :]], axis=0),
                          jnp.where(bd_k, jnp.concatenate([k_ref[rsl(c), :]] * DN_HEADS, axis=0),
                                    jnp.zeros((wide, DN_W), BF16)),
                          NT_DIMS, preferred_element_type=F32) for c in chunks]
    tick()
    a_low = [jnp.where(strict, m[0:CHUNK] * d, 0.0) for m, d in zip(kq, dec)]
    attn = [jnp.where(tril, m[CHUNK:2 * CHUNK] * d, 0.0) for m, d in zip(kq, dec)]
    bmat = [-a for a in a_low]
    ymat = [eye + b for b in bmat]
    bmat = [jnp.dot(b.astype(BF16), block_diag(b), preferred_element_type=F32) for b in bmat]
    tick()
    for _ in range(4):
        prod = [jnp.dot(jnp.concatenate([y, b], axis=0).astype(BF16), block_diag(b), preferred_element_type=F32)
                for y, b in zip(ymat, bmat)]
        tick()
        ymat = [y + p[0:CHUNK] for y, p in zip(ymat, prod)]
        bmat = [p[CHUNK:2 * CHUNK] for p in prod]
    ymat = [y + jnp.dot(y.astype(BF16), block_diag(b), preferred_element_type=F32) for y, b in zip(ymat, bmat)]
    tick()
    rhs = [jnp.concatenate([jnp.concatenate([kbg[rsl(c), hsl(hd)], vb[rsl(c), hsl(hd)]], axis=1)
                            for hd in range(DN_HEADS)], axis=0).astype(BF16) for c in chunks]
    wu = [jnp.dot(block_diag(y), r, preferred_element_type=F32) for y, r in zip(ymat, rhs)]
    tick()
    wut = [m.T.astype(BF16) for m in wu]
    kdbd = [jnp.where(bd_k, jnp.concatenate([k_dec[c].astype(BF16)] * DN_HEADS, axis=0),
                      jnp.zeros((wide, DN_W), BF16)) for c in chunks]
    pbt = [jnp.dot(a, b, preferred_element_type=F32) for a, b in zip(wut, kdbd)]
    tick()
    aw = [jnp.dot(block_diag(a), m.astype(BF16), preferred_element_type=F32) for a, m in zip(attn, wu)]
    qpt = [(jnp.concatenate([q_dec[rsl(c), hsl(hd)] for hd in range(DN_HEADS)], axis=0) - a[:, 0:DN_HD]).T
           .astype(BF16) for c, a in zip(chunks, aw)]
    while pending:
        tick()
    for c in chunks:
        for hd in range(DN_HEADS):
            pair = slice((hd // 2) * LANES, (hd // 2 + 1) * LANES)
            w_s[cur, c, hd] = jnp.concatenate([pbt[c][0:DN_HD, hsl(hd)].astype(BF16), qpt[c][:, pair]], axis=1)
            bt_s[cur, c, hd] = pbt[c][DN_HD:2 * DN_HD, hsl(hd)].astype(BF16)
            op_s[cur, rsl(c), hsl(hd)] = aw[c][hd * CHUNK:(hd + 1) * CHUNK, DN_HD:2 * DN_HD].astype(BF16)
        gl_rows = [jnp.exp(g_last[c][:, hsl(hd)]) for hd in range(DN_HEADS)]
        gl_rows.append(jnp.zeros((SUBLANES - DN_HEADS, DN_HD), F32))
        gl_s[cur, c] = jnp.concatenate(gl_rows, axis=0)


def _dn_call(qn, kn, v, bg, zdn, nw_row, rows):
    s = qn.shape[0]
    nblk = s // rows
    ncb = rows // CHUNK
    cur = lambda n: pl.BlockSpec((rows, n), lambda i: (jnp.minimum(i, nblk - 1), 0))
    prv = lambda n: pl.BlockSpec((rows, n), lambda i: (jnp.maximum(i - 1, 0), 0))
    return pl.pallas_call(
        _dn_body,
        out_shape=jax.ShapeDtypeStruct((s, DN_W), BF16),
        grid=(nblk + 1,),
        in_specs=[cur(DN_W), cur(DN_W), cur(DN_W), cur(LANES), prv(DN_W), pl.BlockSpec((1, DN_HD), lambda i: (0, 0))],
        out_specs=prv(DN_W),
        scratch_shapes=[pltpu.VMEM((2, ncb, DN_HEADS, DN_HD, 2 * DN_HD), BF16),
                        pltpu.VMEM((2, ncb, DN_HEADS, DN_HD, DN_HD), BF16),
                        pltpu.VMEM((2, rows, DN_W), BF16),
                        pltpu.VMEM((2, ncb, SUBLANES, DN_HD), F32),
                        pltpu.VMEM((DN_HEADS, DN_HD, DN_HD), F32)],
        compiler_params=_params(("arbitrary",)),
        name="deltanet",
    )(qn, kn, v, bg, zdn, nw_row)


def _attn_body(*refs):
    ins, scratch = refs[:-6], refs[-6:]
    for p in range(ATT_PAIRS):
        _attn_pair(*[r.at[:, p * LANES:(p + 1) * LANES] for r in ins], *scratch)


def _attn_pair(q1, q4, q16, k1, k4, k16, v1, v4, v16, kp1, vp1, kp4a, kp4b, kp4c, kp4d,
               vp4a, vp4b, vp4c, vp4d, kp16, vp16, z_ref, anw_ref, o_ref, m_s, l_s, a_s, m_n, l_n, a_n):
    d4, d16 = DILATIONS[1], DILATIONS[2]
    seg4 = SUPER // d4
    kp4 = (kp4a, kp4b, kp4c, kp4d)
    vp4 = (vp4a, vp4b, vp4c, vp4d)
    kj = _iota((2 * Q_BLOCK, Q_BLOCK), 0)
    qi = _iota((2 * Q_BLOCK, Q_BLOCK), 1)
    rel = Q_BLOCK + qi - kj
    band = (rel >= 0) & (rel <= W_SUB)
    bias_in = jnp.where(band, 0.0, NEG).astype(BF16)
    bias_first = jnp.where(band & (kj >= Q_BLOCK), 0.0, NEG).astype(BF16)
    bias_edge = jnp.where(pl.program_id(0) == 0, bias_first, bias_in)
    one_hot = jnp.where(kj % Q_BLOCK == qi, 1.0, 0.0).astype(BF16)
    lane = _iota((Q_BLOCK, LANES), 1)
    lane_k = _iota((2 * Q_BLOCK, LANES), 1)
    head_a_q = (lane % AT_HD) < (AT_HD // 2)
    head_a = lane < AT_HD
    head_a_k = lane_k < AT_HD
    ro = _iota((4 * Q_BLOCK, LANES), 0)
    ones_rhs = jnp.where((ro < 2 * Q_BLOCK) == (_iota((4 * Q_BLOCK, LANES), 1) < AT_HD), 1.0, 0.0).astype(BF16)
    sr = _iota((2 * LANES, LANES), 0)
    ssq_rhs = jnp.where((sr % LANES) // AT_HD == _iota((2 * LANES, LANES), 1) // AT_HD, 1.0, 0.0).astype(BF16)
    zq = jnp.zeros((Q_BLOCK, LANES), BF16)
    zv = jnp.zeros((2 * Q_BLOCK, LANES), BF16)

    def scores(load):
        q, kb, vb, bias = load()
        q2 = jnp.concatenate([jnp.where(head_a_q, q, zq), jnp.where(head_a_q, zq, q)], axis=0)
        s = lax.dot_general(jnp.concatenate([q2, one_hot], axis=1), jnp.concatenate([kb, bias], axis=1),
                            NT_DIMS, preferred_element_type=F32)
        m = jnp.max(s, axis=-1, keepdims=True)
        return jnp.exp2(s - m).astype(BF16), m, vb

    def values(p, m, vb):
        pc = jnp.concatenate([p[0:Q_BLOCK], p[Q_BLOCK:2 * Q_BLOCK]], axis=1)
        rhs = jnp.concatenate([jnp.concatenate([jnp.where(head_a_k, vb, zv), jnp.where(head_a_k, zv, vb)], axis=0),
                               ones_rhs], axis=1)
        r = jnp.dot(pc, rhs, preferred_element_type=F32)
        return r[:, 0:LANES], r[:, LANES:2 * LANES], jnp.where(head_a, m[0:Q_BLOCK], m[Q_BLOCK:2 * Q_BLOCK])

    def run(loads, finish, post=None):
        n = len(loads)
        pending, done = {}, {}
        for i in range(n + ATT_LAG + 1):
            if i < n:
                pending[i] = scores(loads[i])
            j = i - ATT_LAG
            if 0 <= j < n:
                done[j] = finish(j, *values(*pending.pop(j)))
            if post is not None and 0 <= j - 1 < n:
                post(j - 1, done.pop(j - 1))

    def merged(acc, l, m, m_old, l_old, a_old):
        m_new = jnp.maximum(m_old, m)
        w_old = jnp.exp2(m_old - m_new)
        w_new = jnp.exp2(m - m_new)
        return a_old * w_old + acc * w_new, l_old * w_old + l * w_new, m_new

    def load16(r):
        rs = slice(r * Q_BLOCK, (r + 1) * Q_BLOCK)
        return lambda: (q16[rs, :], jnp.concatenate([kp16[rs, :], k16[rs, :]], axis=0),
                        jnp.concatenate([vp16[rs, :], v16[rs, :]], axis=0), bias_edge)

    def finish16(r, acc, l, m):
        rows = pl.ds((r % d4) * seg4 + r // d4, Q_BLOCK, stride=d4)
        a_s[rows, :] = acc
        l_s[rows, :] = l
        m_s[rows, :] = m

    run([load16(r) for r in range(d16)], finish16)

    def load4(r, b):
        base = r * seg4 + b * Q_BLOCK
        if b == 0:
            return lambda: (q4[base:base + Q_BLOCK, :],
                            jnp.concatenate([kp4[r][...], k4[base:base + Q_BLOCK, :]], axis=0),
                            jnp.concatenate([vp4[r][...], v4[base:base + Q_BLOCK, :]], axis=0), bias_edge)
        return lambda: (q4[base:base + Q_BLOCK, :], k4[base - Q_BLOCK:base + Q_BLOCK, :],
                        v4[base - Q_BLOCK:base + Q_BLOCK, :], bias_in)

    blocks4 = [(r, b) for b in range(seg4 // Q_BLOCK) for r in range(d4)]

    def finish4(j, acc, l, m):
        r, b = blocks4[j]
        rows = slice(r * seg4 + b * Q_BLOCK, r * seg4 + (b + 1) * Q_BLOCK)
        acc, l, m = merged(acc, l, m, m_s[rows, :], l_s[rows, :], a_s[rows, :])
        a_s[rows, :] = acc
        l_s[rows, :] = l
        m_s[rows, :] = m

    run([load4(r, b) for r, b in blocks4], finish4)

    for c in range(d4):
        src = slice(c * seg4, (c + 1) * seg4)
        dst = pl.ds(c, seg4, stride=d4)
        m_n[dst, :] = m_s[src, :]
        l_n[dst, :] = l_s[src, :]
        a_n[dst, :] = a_s[src, :]

    def load1(b):
        rs = slice(b * Q_BLOCK, (b + 1) * Q_BLOCK)
        if b == 0:
            return lambda: (q1[rs, :], jnp.concatenate([kp1[...], k1[rs, :]], axis=0),
                            jnp.concatenate([vp1[...], v1[rs, :]], axis=0), bias_edge)
        ks = slice((b - 1) * Q_BLOCK, (b + 1) * Q_BLOCK)
        return lambda: (q1[rs, :], k1[ks, :], v1[ks, :], bias_in)

    def finish1(b, acc, l, m):
        rs = slice(b * Q_BLOCK, (b + 1) * Q_BLOCK)
        acc, l, _ = merged(acc, l, m, m_n[rs, :], l_n[rs, :], a_n[rs, :])
        o = acc / l
        sq = o * o
        hi = sq.astype(BF16)
        return o, jnp.concatenate([hi, (sq - hi.astype(F32)).astype(BF16)], axis=1)

    def post1(b, res):
        o, sq2 = res
        rs = slice(b * Q_BLOCK, (b + 1) * Q_BLOCK)
        ssq = jnp.dot(sq2, ssq_rhs, preferred_element_type=F32)
        o_ref[rs, :] = (o * lax.rsqrt(ssq * (1.0 / AT_HD) + EPS) * anw_ref[...]
                        * z_ref[rs, :].astype(F32)).astype(BF16)

    run([load1(b) for b in range(SUPER // Q_BLOCK)], finish1, post1)


def _attn_call(aq, ak, av, zat, anw_row):
    s = zat.shape[0]
    d4 = DILATIONS[1]
    wl = ATT_PAIRS * LANES
    per_super = SUPER // Q_BLOCK
    cur = pl.BlockSpec((SUPER, wl), lambda m, g: (m, g))
    prev16 = pl.BlockSpec((SUPER, wl), lambda m, g: (jnp.maximum(m - 1, 0), g))
    prev1 = pl.BlockSpec((Q_BLOCK, wl), lambda m, g: (jnp.maximum(m * per_super - 1, 0), g))
    prev4 = [pl.BlockSpec((Q_BLOCK, wl),
                          lambda m, g, r=r: (jnp.maximum((m - 1) * per_super + (r + 1) * (per_super // d4) - 1, 0), g))
             for r in range(d4)]
    state = pltpu.VMEM((SUPER, LANES), F32)
    return pl.pallas_call(
        _attn_body,
        out_shape=jax.ShapeDtypeStruct((s, AT_W), BF16),
        grid=(s // SUPER, AT_W // wl),
        in_specs=[cur] * 9 + [prev1, prev1] + prev4 + prev4 + [prev16, prev16, cur,
                                                               pl.BlockSpec((1, wl), lambda m, g: (0, g))],
        out_specs=cur,
        scratch_shapes=[state] * 6,
        compiler_params=_params(("arbitrary", "arbitrary")),
        name="attn",
    )(*aq, *ak, *av, ak[0], av[0], *([ak[1]] * d4), *([av[1]] * d4), ak[2], av[2], zat, anw_row)


def _out_body(x_hbm, odn_hbm, oat_hbm, gate_ref, w_ref, fnw_ref, o_hbm, xb, db, ab, ob, in_sem, out_sem):
    tm = xb.shape[1]
    ntile = x_hbm.shape[0] // tm

    def in_copies(t, slot):
        rows = pl.ds(pl.multiple_of(t * tm, tm), tm)
        return (pltpu.make_async_copy(x_hbm.at[rows, :], xb.at[slot], in_sem.at[0, slot]),
                pltpu.make_async_copy(odn_hbm.at[rows, :], db.at[slot], in_sem.at[1, slot]),
                pltpu.make_async_copy(oat_hbm.at[rows, :], ab.at[slot], in_sem.at[2, slot]))

    def out_copy(t, slot):
        rows = pl.ds(pl.multiple_of(t * tm, tm), tm)
        return pltpu.make_async_copy(ob.at[slot], o_hbm.at[rows, :], out_sem.at[slot])

    for t in range(OUT_IN_BUFS):
        for cp in in_copies(t, t):
            cp.start()

    def step(t, carry):
        slot = t % OUT_IN_BUFS
        oslot = t % 2
        for cp in in_copies(t, slot):
            cp.wait()

        @pl.when(t >= 2)
        def _():
            out_copy(t - 2, oslot).wait()

        mixin = jnp.concatenate([db[slot], ab[slot]], axis=1)
        mix = jnp.dot(mixin, w_ref[...], preferred_element_type=F32)
        y = xb[slot] + gate_ref[...] * mix
        ms = jnp.mean(y * y, axis=-1, keepdims=True)
        ob[oslot] = y * lax.rsqrt(ms + EPS) * fnw_ref[...]
        out_copy(t, oslot).start()

        @pl.when(t + OUT_IN_BUFS < ntile)
        def _():
            for cp in in_copies(t + OUT_IN_BUFS, slot):
                cp.start()

        return carry

    lax.fori_loop(0, ntile, step, 0)
    out_copy(ntile - 2, (ntile - 2) % 2).wait()
    out_copy(ntile - 1, (ntile - 1) % 2).wait()


def _out_call(x2, odn, oat, gate, w_out, fnw_row, tm):
    s, d = x2.shape
    assert s // tm >= max(OUT_IN_BUFS, 2)
    hbm = pl.BlockSpec(memory_space=pl.ANY)
    vmem = pl.BlockSpec(memory_space=pltpu.VMEM)
    return pl.pallas_call(
        _out_body,
        out_shape=jax.ShapeDtypeStruct((s, d), F32),
        in_specs=[hbm, hbm, hbm, vmem, vmem, vmem],
        out_specs=hbm,
        scratch_shapes=[pltpu.VMEM((OUT_IN_BUFS, tm, d), F32), pltpu.VMEM((OUT_IN_BUFS, tm, DN_W), BF16),
                        pltpu.VMEM((OUT_IN_BUFS, tm, AT_W), BF16), pltpu.VMEM((2, tm, d), F32),
                        pltpu.SemaphoreType.DMA((3, OUT_IN_BUFS)), pltpu.SemaphoreType.DMA((2,))],
        compiler_params=pltpu.CompilerParams(vmem_limit_bytes=VMEM_LIMIT),
        name="out",
    )(x2, odn, oat, gate, w_out, fnw_row)


def kernel(x, c, positions, w_mod, b_mod, norm_w, w_in, conv_w, a_log, dt_bias, dn_norm_w, at_norm_w,
           w_out, final_norm_w):
    b, s, d = x.shape
    assert b == 1 and d == D_MODEL and w_mod.shape[0] == 1
    x2 = x.reshape(s, d)

    mod = _mod_call(c.reshape(d, 1), w_mod[0], b_mod[0].reshape(1, 3 * d))
    shift, scale, gate = mod[:, 0:d], mod[:, d:2 * d], mod[:, 2 * d:3 * d]

    w_all, w_out_b = _wpack_call(w_in, w_out)
    pad_heads = lambda v: jnp.pad(v.reshape(1, DN_HEADS), ((0, 0), (DN_HEADS, LANES - 2 * DN_HEADS)))
    half = AT_HD // 2
    invf = (ROPE_THETA ** (-jnp.arange(half, dtype=F32) / half)).reshape(half, 1)
    sgn = jnp.where(jnp.arange(LANES) < LANES // 2, -1.0, 1.0).astype(F32).reshape(1, LANES)

    assert s % SUPER == 0
    outs = _proj_call(x2, positions.reshape(1, s), norm_w[0].reshape(1, d), scale, shift, w_all, conv_w[0],
                      pad_heads(a_log[0]), pad_heads(dt_bias[0]), invf, sgn, tm=512)
    qn, kn, v, zdn, bg, zat = outs[0:6]
    flat = lambda a: a.reshape(s, AT_W)
    aq, ak, av = [[flat(a) for a in outs[j:j + 3]] for j in (6, 9, 12)]

    odn = _dn_call(qn, kn, v, bg, zdn, dn_norm_w[0].reshape(1, DN_HD), rows=512)
    oat = _attn_call(aq, ak, av, zat, jnp.tile(at_norm_w[0], AT_HEADS).reshape(1, AT_W))

    out = _out_call(x2, odn, oat, gate, w_out_b, final_norm_w.reshape(1, d), tm=1024)
    return out.reshape(b, s, d)
```

```python
import math

import jax
import jax.numpy as jnp
from jax import lax
from jax.experimental import pallas as pl
from jax.experimental.pallas import tpu as pltpu

D_MODEL = 1024
DN_HEADS = 4
DN_HD = 128
DN_W = DN_HEADS * DN_HD
AT_HEADS = 8
AT_HD = 64
AT_W = AT_HEADS * AT_HD
CONV_K = 4
CHUNK = 64
DILATIONS = (1, 4, 16)
W_SUB = 128
Q_BLOCK = 128
SUPER = Q_BLOCK * DILATIONS[-1]
OUT_IN_BUFS = 3
ATT_PAIRS = 2
ATT_LAG = 2
ROPE_THETA = 10000.0
EPS = 1e-6
NEG = -1e30

LANES = 128
SUBLANES = 8
MXU_COLS = 256
PROJ_COLS = 512
VMEM_LIMIT = 48 * 1024 * 1024

C_QKV, C_Z, C_AQ, C_AK, C_AV, C_AZ, C_BA = 0, 1536, 2048, 2560, 3072, 3584, 4096
W_COLS = C_BA + LANES

F32 = jnp.float32
BF16 = jnp.bfloat16
NT_DIMS = (((1,), (1,)), ((), ()))


def _sigmoid(v):
    return 1.0 / (1.0 + jnp.exp2(v * (-math.log2(math.e))))


def _iota(shape, dim):
    return lax.broadcasted_iota(jnp.int32, shape, dim)


def _params(sem):
    return pltpu.CompilerParams(dimension_semantics=sem, vmem_limit_bytes=VMEM_LIMIT)


def _mod_body(c_ref, w_ref, b_ref, o_ref):
    c = c_ref[...]
    sc = c * _sigmoid(c)
    o_ref[...] = jnp.sum(sc * w_ref[...], axis=0, keepdims=True) + b_ref[...]


def _mod_call(c_col, w_mod, b_mod):
    d, n = w_mod.shape
    tn = 1024
    return pl.pallas_call(
        _mod_body,
        out_shape=jax.ShapeDtypeStruct((1, n), F32),
        grid=(n // tn,),
        in_specs=[pl.BlockSpec((d, 1), lambda j: (0, 0)),
                  pl.BlockSpec((d, tn), lambda j: (0, j)),
                  pl.BlockSpec((1, tn), lambda j: (0, j))],
        out_specs=pl.BlockSpec((1, tn), lambda j: (0, j)),
        compiler_params=_params(("arbitrary",)),
        name="mod",
    )(c_col, w_mod, b_mod)


def _wpack_body(wt_ref, wo_ref, o_ref, oo_ref):
    kb = wt_ref.shape[2]
    o_b = 4 * DN_W
    o_q = o_b + 2 * DN_HEADS
    half = AT_HD // 2

    def put(col0, src):
        o_ref[:, col0:col0 + LANES] = src.T.astype(BF16)

    for g in range(o_b // LANES):
        put(C_QKV + g * LANES, wt_ref[0, g * LANES:(g + 1) * LANES, :])
    for g in range(2 * AT_W // LANES):
        base = o_q + g * LANES
        put(C_AQ + g * LANES, jnp.concatenate([wt_ref[0, base:base + half, :],
                                               wt_ref[0, base + 2 * half:base + 3 * half, :],
                                               wt_ref[0, base + half:base + 2 * half, :],
                                               wt_ref[0, base + 3 * half:base + 4 * half, :]], axis=0))
    for g in range(2 * AT_W // LANES):
        base = o_q + 2 * AT_W + g * LANES
        put(C_AV + g * LANES, wt_ref[0, base:base + LANES, :])
    put(C_BA, jnp.concatenate([wt_ref[0, o_b:o_b + 2 * DN_HEADS, :],
                               jnp.zeros((LANES - 2 * DN_HEADS, kb), F32)], axis=0))
    oo_ref[...] = wo_ref[0].astype(BF16)


def _wpack_call(w_in, w_out):
    _, d, n_in = w_in.shape
    n_out = w_out.shape[2]
    kb = 512
    w_t = jnp.swapaxes(w_in, 1, 2)
    return pl.pallas_call(
        _wpack_body,
        out_shape=[jax.ShapeDtypeStruct((d, W_COLS), BF16), jax.ShapeDtypeStruct((w_out.shape[1], n_out), BF16)],
        grid=(d // kb,),
        in_specs=[pl.BlockSpec((1, n_in, kb), lambda i: (0, 0, i)),
                  pl.BlockSpec((1, kb, n_out), lambda i: (0, i, 0))],
        out_specs=[pl.BlockSpec((kb, W_COLS), lambda i: (i, 0)), pl.BlockSpec((kb, n_out), lambda i: (i, 0))],
        compiler_params=_params(("arbitrary",)),
        name="wpack",
    )(w_t, w_out)


def _proj_body(x_ref, pos_ref, nw_ref, sc_ref, sh_ref, w_ref, cw_ref, alog_ref, dtb_ref, invf_ref,
               sgn_ref, qn_ref, kn_ref, v_ref, zdn_ref, bg_ref, zat_ref,
               aq1_ref, aq4_ref, aq16_ref, ak1_ref, ak4_ref, ak16_ref, av1_ref, av4_ref, av16_ref,
               cbuf_ref, hb_ref, pbuf_ref, p4buf_ref):
    i = pl.program_id(0)
    tm = x_ref.shape[0]
    x = x_ref[...]
    ms = jnp.mean(x * x, axis=-1, keepdims=True)
    hb_ref[...] = ((x * lax.rsqrt(ms + EPS)) * (nw_ref[...] * (1.0 + sc_ref[...])) + sh_ref[...]).astype(BF16)

    def proj(c0, width=PROJ_COLS):
        return jnp.dot(hb_ref[...], w_ref[:, c0:c0 + width], preferred_element_type=F32)

    def silu(v):
        return v * _sigmoid(v)

    @pl.when(i == 0)
    def _():
        cbuf_ref[:, 0:SUBLANES, :] = jnp.zeros((cbuf_ref.shape[0], SUBLANES, LANES), F32)

    dyn0 = jnp.minimum(i, 0)
    for c0 in range(0, 3 * DN_W, PROJ_COLS):
        xg = proj(C_QKV + c0)
        halves = []
        for hh in range(PROJ_COLS // LANES):
            grp = c0 // LANES + hh
            ls = slice(c0 + hh * LANES, c0 + (hh + 1) * LANES)
            xh = xg[:, hh * LANES:(hh + 1) * LANES]
            cbuf_ref[grp, SUBLANES:SUBLANES + tm, :] = xh
            acc = xh * cw_ref[CONV_K - 1:CONV_K, ls]
            for j in range(1, CONV_K):
                acc = acc + cbuf_ref[grp, pl.ds(dyn0 + SUBLANES - j, tm), :] * cw_ref[CONV_K - 1 - j:CONV_K - j, ls]
            cbuf_ref[grp, 0:SUBLANES, :] = xh[tm - SUBLANES:tm]
            halves.append(acc)
        conv = jnp.concatenate(halves, axis=1)
        y = silu(conv)
        kind, off = c0 // DN_W, c0 % DN_W
        for hh in range(PROJ_COLS // DN_HD):
            yh = y[:, hh * DN_HD:(hh + 1) * DN_HD]
            dst = slice(off + hh * DN_HD, off + (hh + 1) * DN_HD)
            if kind == 2:
                v_ref[:, dst] = yh.astype(BF16)
            else:
                inv = lax.rsqrt(jnp.sum(yh * yh, axis=-1, keepdims=True) + EPS)
                if kind == 0:
                    qn_ref[:, dst] = (yh * (inv * (DN_HD ** -0.5))).astype(BF16)
                else:
                    kn_ref[:, dst] = (yh * inv).astype(BF16)

    for c0 in range(0, DN_W, PROJ_COLS):
        zdn_ref[:, c0:c0 + PROJ_COLS] = silu(proj(C_Z + c0)).astype(BF16)
        zat_ref[:, c0:c0 + PROJ_COLS] = silu(proj(C_AZ + c0)).astype(BF16)

    ba = proj(C_BA, LANES)
    lane = _iota(ba.shape, 1)
    a = ba + dtb_ref[...]
    softplus = jnp.maximum(a, 0.0) + jnp.log(1.0 + jnp.exp(-jnp.abs(a)))
    g = -jnp.exp(alog_ref[...]) * softplus
    bg_ref[...] = jnp.where(lane < DN_HEADS, _sigmoid(ba), jnp.where(lane < 2 * DN_HEADS, g, 0.0))

    ang = invf_ref[...] * pos_ref[...].astype(F32)
    reps = LANES // (AT_HD // 2)
    cos = jnp.concatenate([jnp.cos(ang)] * reps, axis=0).T
    sin = jnp.concatenate([jnp.sin(ang)] * reps, axis=0).T * sgn_ref[...]
    qscale = (AT_HD ** -0.5) * math.log2(math.e)
    d4, d16 = DILATIONS[1], DILATIONS[2]
    n4, n16 = tm // d4, tm // d16

    def emit(val, gp, nat_ref, p4_ref, p16_ref):
        sl = slice(gp * LANES, (gp + 1) * LANES)
        nat_ref[:, sl] = val.astype(BF16)
        pbuf_ref[gp] = val
        for c in range(d4):
            seg = pbuf_ref[gp, pl.ds(c, n4, stride=d4), :]
            p4_ref[0, c, :, sl] = seg.astype(BF16)
            p4buf_ref[gp, c * n4:(c + 1) * n4, :] = seg
        for r in range(d16):
            seg = p4buf_ref[gp, pl.ds((r % d4) * n4 + r // d4, n16, stride=d4), :]
            p16_ref[0, r, :, sl] = seg.astype(BF16)

    def rope(v):
        return v * cos + pltpu.roll(v, LANES // 2, axis=1) * sin

    for c0 in range(0, AT_W, PROJ_COLS):
        aq = proj(C_AQ + c0)
        ak = proj(C_AK + c0)
        av = proj(C_AV + c0)
        for hh in range(PROJ_COLS // LANES):
            gp = c0 // LANES + hh
            ls = slice(hh * LANES, (hh + 1) * LANES)
            emit(rope(aq[:, ls]) * qscale, gp, aq1_ref, aq4_ref, aq16_ref)
            emit(rope(ak[:, ls]), gp, ak1_ref, ak4_ref, ak16_ref)
            emit(av[:, ls], gp, av1_ref, av4_ref, av16_ref)


def _proj_call(x2, pos_row, nw, scale, shift, w_all, cw, alog_row, dtb_row, invf, sgn, tm):
    s, d = x2.shape
    row = lambda n: pl.BlockSpec((1, n), lambda i: (0, 0))
    full = lambda a: pl.BlockSpec(a.shape, lambda i: (0, 0))
    tile = lambda n: pl.BlockSpec((tm, n), lambda i: (i, 0))
    spb = SUPER // tm
    res_shape = lambda dil: jax.ShapeDtypeStruct((s // SUPER, dil, SUPER // dil, AT_W), BF16)
    res_spec = lambda dil: pl.BlockSpec((1, dil, tm // dil, AT_W), lambda i: (i // spb, 0, i % spb, 0))
    bf = lambda n: jax.ShapeDtypeStruct((s, n), BF16)
    outs = [bf(DN_W)] * 4 + [jax.ShapeDtypeStruct((s, LANES), F32), bf(AT_W)] + \
           [bf(AT_W), res_shape(DILATIONS[1]), res_shape(DILATIONS[2])] * 3
    out_specs = [tile(DN_W)] * 4 + [tile(LANES), tile(AT_W)] + \
                [tile(AT_W), res_spec(DILATIONS[1]), res_spec(DILATIONS[2])] * 3
    return pl.pallas_call(
        _proj_body,
        out_shape=outs,
        grid=(s // tm,),
        in_specs=[tile(d), pl.BlockSpec((1, tm), lambda i: (0, i)), row(d), row(d), row(d),
                  pl.BlockSpec(w_all.shape, lambda i: (0, 0), pipeline_mode=pl.Buffered(1)),
                  full(cw), row(LANES), row(LANES), full(invf), row(LANES)],
        out_specs=out_specs,
        scratch_shapes=[pltpu.VMEM((3 * DN_W // LANES, SUBLANES + tm, LANES), F32), pltpu.VMEM((tm, d), BF16),
                        pltpu.VMEM((AT_W // LANES, tm, LANES), F32),
                        pltpu.VMEM((AT_W // LANES, tm, LANES), F32)],
        compiler_params=_params(("arbitrary",)),
        name="proj",
    )(x2, pos_row, nw, scale, shift, w_all, cw, alog_row, dtb_row, invf, sgn)


def _dn_body(q_ref, k_ref, v_ref, bg_ref, z_ref, nw_ref, o_ref, w_s, bt_s, op_s, gl_s, st_ref):
    step = pl.program_id(0)
    cur = step % 2
    prv = 1 - cur

    @pl.when(step == 0)
    def _():
        st_ref[...] = jnp.zeros(st_ref.shape, F32)
        w_s[1] = jnp.zeros(w_s.shape[1:], BF16)
        bt_s[1] = jnp.zeros(bt_s.shape[1:], BF16)
        op_s[1] = jnp.zeros(op_s.shape[1:], BF16)
        gl_s[1] = jnp.zeros(gl_s.shape[1:], F32)

    rows = q_ref.shape[0]
    nchunk = rows // CHUNK
    hsl = lambda hd: slice(hd * DN_HD, (hd + 1) * DN_HD)

    def scan_chunk(c):
        rs = slice(c * CHUNK, (c + 1) * CHUNK)
        for hd in range(DN_HEADS):
            st = st_ref[hd]
            r = jnp.dot(st.astype(BF16), w_s[prv, c, hd], preferred_element_type=F32)
            st_ref[hd] = gl_s[prv, c, hd:hd + 1, :] * st - r[:, 0:DN_HD] + bt_s[prv, c, hd].astype(F32)
            half = (hd % 2) * CHUNK
            o = r[:, DN_HD:2 * DN_HD].T[half:half + CHUNK] + op_s[prv, rs, hsl(hd)].astype(F32)
            ms = jnp.mean(o * o, axis=-1, keepdims=True)
            o_ref[rs, hsl(hd)] = (o * lax.rsqrt(ms + EPS) * nw_ref[...] * z_ref[rs, hsl(hd)].astype(F32)).astype(BF16)

    pending = list(range(nchunk))

    def tick():
        if pending:
            scan_chunk(pending.pop(0))

    wide = DN_HEADS * CHUNK
    bg = bg_ref[...]
    cs_rows = min(rows, MXU_COLS)
    ri = _iota((cs_rows, cs_rows), 0)
    ci = _iota((cs_rows, cs_rows), 1)
    tril_blk = jnp.where((ri // CHUNK == ci // CHUNK) & (ri >= ci), 1.0, 0.0).astype(BF16)
    g_hi = bg.astype(BF16)
    g_r1 = bg - g_hi.astype(F32)
    g_mid = g_r1.astype(BF16)
    g_lo = (g_r1 - g_mid.astype(F32)).astype(BF16)
    g3 = jnp.concatenate([g_hi, g_mid, g_lo], axis=1)
    csum = jnp.concatenate([jnp.dot(tril_blk, g3[r0:r0 + cs_rows], preferred_element_type=F32)
                            for r0 in range(0, rows, cs_rows)], axis=0)
    gcum = csum[:, 0:LANES] + csum[:, LANES:2 * LANES] + csum[:, 2 * LANES:3 * LANES]
    bfull = jnp.concatenate([jnp.broadcast_to(bg[:, hd:hd + 1], (rows, DN_HD)) for hd in range(DN_HEADS)], axis=1)
    gfull = jnp.concatenate([jnp.broadcast_to(gcum[:, DN_HEADS + hd:DN_HEADS + hd + 1], (rows, DN_HD))
                             for hd in range(DN_HEADS)], axis=1)
    eg = jnp.exp(gfull)
    q = q_ref[...].astype(F32)
    k = k_ref[...].astype(F32)
    kb = k * bfull
    vb = v_ref[...].astype(F32) * bfull
    q_dec = q * eg
    kbg = kb * eg

    r64 = _iota((CHUNK, wide), 0)
    j64 = _iota((CHUNK, wide), 1) % CHUNK
    tril = r64 >= j64
    strict = r64 > j64
    eye = jnp.where(r64 == j64, 1.0, 0.0).astype(F32)
    low_half = _iota((CHUNK, LANES), 1) < CHUNK
    bd64 = _iota((wide, wide), 0) // CHUNK == _iota((wide, wide), 1) // CHUNK
    bd_k = _iota((wide, DN_W), 0) // CHUNK == _iota((wide, DN_W), 1) // DN_HD

    def block_diag(m):
        return jnp.where(bd64, jnp.concatenate([m.astype(BF16)] * DN_HEADS, axis=0), jnp.zeros((wide, wide), BF16))

    rsl = lambda c: slice(c * CHUNK, (c + 1) * CHUNK)
    chunks = range(nchunk)
    g_last = [gfull[c * CHUNK + CHUNK - 1:(c + 1) * CHUNK, :] for c in chunks]
    k_dec = [k[rsl(c)] * jnp.exp(g_last[c] - gfull[rsl(c)]) for c in chunks]
    g_row = [jnp.concatenate([gcum[rsl(c)], gcum[rsl(c)]], axis=0).T for c in chunks]
    dec = []
    for c in chunks:
        halves = []
        for pr in range(DN_HEADS // 2):
            ha, hb = 2 * pr, 2 * pr + 1
            col = jnp.where(low_half, gfull[rsl(c), hsl(ha)], gfull[rsl(c), hsl(hb)])
            row = jnp.where(low_half[0:1], g_row[c][DN_HEADS + ha:DN_HEADS + ha + 1, :],
                            g_row[c][DN_HEADS + hb:DN_HEADS + hb + 1, :])
            halves.append(col - row)
        dec.append(jnp.exp(jnp.where(tril, jnp.concatenate(halves, axis=1), NEG)))
    kq = [lax.dot_general(jnp.concatenate([kb[rsl(c)].astype(BF16), q_ref[rsl(c), :]], axis=0),
                          jnp.where(bd_k, jnp.concatenate([k_ref[rsl(c), :]] * DN_HEADS, axis=0),
                                    jnp.zeros((wide, DN_W), BF16)),
                          NT_DIMS, preferred_element_type=F32) for c in chunks]
    tick()
    a_low = [jnp.where(strict, m[0:CHUNK] * d, 0.0) for m, d in zip(kq, dec)]
    attn = [jnp.where(tril, m[CHUNK:2 * CHUNK] * d, 0.0) for m, d in zip(kq, dec)]
    bmat = [-a for a in a_low]
    ymat = [eye + b for b in bmat]
    bmat = [jnp.dot(b.astype(BF16), block_diag(b), preferred_element_type=F32) for b in bmat]
    tick()
    for _ in range(4):
        prod = [jnp.dot(jnp.concatenate([y, b], axis=0).astype(BF16), block_diag(b), preferred_element_type=F32)
                for y, b in zip(ymat, bmat)]
        tick()
        ymat = [y + p[0:CHUNK] for y, p in zip(ymat, prod)]
        bmat = [p[CHUNK:2 * CHUNK] for p in prod]
    ymat = [y + jnp.dot(y.astype(BF16), block_diag(b), preferred_element_type=F32) for y, b in zip(ymat, bmat)]
    tick()
    rhs = [jnp.concatenate([jnp.concatenate([kbg[rsl(c), hsl(hd)], vb[rsl(c), hsl(hd)]], axis=1)
                            for hd in range(DN_HEADS)], axis=0).astype(BF16) for c in chunks]
    wu = [jnp.dot(block_diag(y), r, preferred_element_type=F32) for y, r in zip(ymat, rhs)]
    tick()
    wut = [m.T.astype(BF16) for m in wu]
    kdbd = [jnp.where(bd_k, jnp.concatenate([k_dec[c].astype(BF16)] * DN_HEADS, axis=0),
                      jnp.zeros((wide, DN_W), BF16)) for c in chunks]
    pbt = [jnp.dot(a, b, preferred_element_type=F32) for a, b in zip(wut, kdbd)]
    tick()
    aw = [jnp.dot(block_diag(a), m.astype(BF16), preferred_element_type=F32) for a, m in zip(attn, wu)]
    qpt = [(jnp.concatenate([q_dec[rsl(c), hsl(hd)] for hd in range(DN_HEADS)], axis=0) - a[:, 0:DN_HD]).T
           .astype(BF16) for c, a in zip(chunks, aw)]
    while pending:
        tick()
    for c in chunks:
        for hd in range(DN_HEADS):
            pair = slice((hd // 2) * LANES, (hd // 2 + 1) * LANES)
            w_s[cur, c, hd] = jnp.concatenate([pbt[c][0:DN_HD, hsl(hd)].astype(BF16), qpt[c][:, pair]], axis=1)
            bt_s[cur, c, hd] = pbt[c][DN_HD:2 * DN_HD, hsl(hd)].astype(BF16)
            op_s[cur, rsl(c), hsl(hd)] = aw[c][hd * CHUNK:(hd + 1) * CHUNK, DN_HD:2 * DN_HD].astype(BF16)
        gl_rows = [jnp.exp(g_last[c][:, hsl(hd)]) for hd in range(DN_HEADS)]
        gl_rows.append(jnp.zeros((SUBLANES - DN_HEADS, DN_HD), F32))
        gl_s[cur, c] = jnp.concatenate(gl_rows, axis=0)


def _dn_call(qn, kn, v, bg, zdn, nw_row, rows):
    s = qn.shape[0]
    nblk = s // rows
    ncb = rows // CHUNK
    cur = lambda n: pl.BlockSpec((rows, n), lambda i: (jnp.minimum(i, nblk - 1), 0))
    prv = lambda n: pl.BlockSpec((rows, n), lambda i: (jnp.maximum(i - 1, 0), 0))
    return pl.pallas_call(
        _dn_body,
        out_shape=jax.ShapeDtypeStruct((s, DN_W), BF16),
        grid=(nblk + 1,),
        in_specs=[cur(DN_W), cur(DN_W), cur(DN_W), cur(LANES), prv(DN_W), pl.BlockSpec((1, DN_HD), lambda i: (0, 0))],
        out_specs=prv(DN_W),
        scratch_shapes=[pltpu.VMEM((2, ncb, DN_HEADS, DN_HD, 2 * DN_HD), BF16),
                        pltpu.VMEM((2, ncb, DN_HEADS, DN_HD, DN_HD), BF16),
                        pltpu.VMEM((2, rows, DN_W), BF16),
                        pltpu.VMEM((2, ncb, SUBLANES, DN_HD), F32),
                        pltpu.VMEM((DN_HEADS, DN_HD, DN_HD), F32)],
        compiler_params=_params(("arbitrary",)),
        name="deltanet",
    )(qn, kn, v, bg, zdn, nw_row)


def _attn_body(*refs):
    ins, scratch = refs[:-6], refs[-6:]
    for p in range(ATT_PAIRS):
        _attn_pair(*[r.at[:, p * LANES:(p + 1) * LANES] for r in ins], *scratch)


def _attn_pair(q1, q4, q16, k1, k4, k16, v1, v4, v16, kp1, vp1, kp4a, kp4b, kp4c, kp4d,
               vp4a, vp4b, vp4c, vp4d, kp16, vp16, z_ref, anw_ref, o_ref, m_s, l_s, a_s, m_n, l_n, a_n):
    d4, d16 = DILATIONS[1], DILATIONS[2]
    seg4 = SUPER // d4
    kp4 = (kp4a, kp4b, kp4c, kp4d)
    vp4 = (vp4a, vp4b, vp4c, vp4d)
    kj = _iota((2 * Q_BLOCK, Q_BLOCK), 0)
    qi = _iota((2 * Q_BLOCK, Q_BLOCK), 1)
    rel = Q_BLOCK + qi - kj
    band = (rel >= 0) & (rel <= W_SUB)
    bias_in = jnp.where(band, 0.0, NEG).astype(BF16)
    bias_first = jnp.where(band & (kj >= Q_BLOCK), 0.0, NEG).astype(BF16)
    bias_edge = jnp.where(pl.program_id(0) == 0, bias_first, bias_in)
    one_hot = jnp.where(kj % Q_BLOCK == qi, 1.0, 0.0).astype(BF16)
    lane = _iota((Q_BLOCK, LANES), 1)
    lane_k = _iota((2 * Q_BLOCK, LANES), 1)
    head_a_q = (lane % AT_HD) < (AT_HD // 2)
    head_a = lane < AT_HD
    head_a_k = lane_k < AT_HD
    ro = _iota((4 * Q_BLOCK, LANES), 0)
    ones_rhs = jnp.where((ro < 2 * Q_BLOCK) == (_iota((4 * Q_BLOCK, LANES), 1) < AT_HD), 1.0, 0.0).astype(BF16)
    sr = _iota((2 * LANES, LANES), 0)
    ssq_rhs = jnp.where((sr % LANES) // AT_HD == _iota((2 * LANES, LANES), 1) // AT_HD, 1.0, 0.0).astype(BF16)
    zq = jnp.zeros((Q_BLOCK, LANES), BF16)
    zv = jnp.zeros((2 * Q_BLOCK, LANES), BF16)

    def scores(load):
        q, kb, vb, bias = load()
        q2 = jnp.concatenate([jnp.where(head_a_q, q, zq), jnp.where(head_a_q, zq, q)], axis=0)
        s = lax.dot_general(jnp.concatenate([q2, one_hot], axis=1), jnp.concatenate([kb, bias], axis=1),
                            NT_DIMS, preferred_element_type=F32)
        m = jnp.max(s, axis=-1, keepdims=True)
        return jnp.exp2(s - m).astype(BF16), m, vb

    def values(p, m, vb):
        pc = jnp.concatenate([p[0:Q_BLOCK], p[Q_BLOCK:2 * Q_BLOCK]], axis=1)
        rhs = jnp.concatenate([jnp.concatenate([jnp.where(head_a_k, vb, zv), jnp.where(head_a_k, zv, vb)], axis=0),
                               ones_rhs], axis=1)
        r = jnp.dot(pc, rhs, preferred_element_type=F32)
        return r[:, 0:LANES], r[:, LANES:2 * LANES], jnp.where(head_a, m[0:Q_BLOCK], m[Q_BLOCK:2 * Q_BLOCK])

    def run(loads, finish, post=None):
        n = len(loads)
        pending, done = {}, {}
        for i in range(n + ATT_LAG + 1):
            if i < n:
                pending[i] = scores(loads[i])
            j = i - ATT_LAG
            if 0 <= j < n:
                done[j] = finish(j, *values(*pending.pop(j)))
            if post is not None and 0 <= j - 1 < n:
                post(j - 1, done.pop(j - 1))

    def merged(acc, l, m, m_old, l_old, a_old):
        m_new = jnp.maximum(m_old, m)
        w_old = jnp.exp2(m_old - m_new)
        w_new = jnp.exp2(m - m_new)
        return a_old * w_old + acc * w_new, l_old * w_old + l * w_new, m_new

    def load16(r):
        rs = slice(r * Q_BLOCK, (r + 1) * Q_BLOCK)
        return lambda: (q16[rs, :], jnp.concatenate([kp16[rs, :], k16[rs, :]], axis=0),
                        jnp.concatenate([vp16[rs, :], v16[rs, :]], axis=0), bias_edge)

    def finish16(r, acc, l, m):
        rows = pl.ds((r % d4) * seg4 + r // d4, Q_BLOCK, stride=d4)
        a_s[rows, :] = acc
        l_s[rows, :] = l
        m_s[rows, :] = m

    run([load16(r) for r in range(d16)], finish16)

    def load4(r, b):
        base = r * seg4 + b * Q_BLOCK
        if b == 0:
            return lambda: (q4[base:base + Q_BLOCK, :],
                            jnp.concatenate([kp4[r][...], k4[base:base + Q_BLOCK, :]], axis=0),
                            jnp.concatenate([vp4[r][...], v4[base:base + Q_BLOCK, :]], axis=0), bias_edge)
        return lambda: (q4[base:base + Q_BLOCK, :], k4[base - Q_BLOCK:base + Q_BLOCK, :],
                        v4[base - Q_BLOCK:base + Q_BLOCK, :], bias_in)

    blocks4 = [(r, b) for b in range(seg4 // Q_BLOCK) for r in range(d4)]

    def finish4(j, acc, l, m):
        r, b = blocks4[j]
        rows = slice(r * seg4 + b * Q_BLOCK, r * seg4 + (b + 1) * Q_BLOCK)
        acc, l, m = merged(acc, l, m, m_s[rows, :], l_s[rows, :], a_s[rows, :])
        a_s[rows, :] = acc
        l_s[rows, :] = l
        m_s[rows, :] = m

    run([load4(r, b) for r, b in blocks4], finish4)

    for c in range(d4):
        src = slice(c * seg4, (c + 1) * seg4)
        dst = pl.ds(c, seg4, stride=d4)
        m_n[dst, :] = m_s[src, :]
        l_n[dst, :] = l_s[src, :]
        a_n[dst, :] = a_s[src, :]

    def load1(b):
        rs = slice(b * Q_BLOCK, (b + 1) * Q_BLOCK)
        if b == 0:
            return lambda: (q1[rs, :], jnp.concatenate([kp1[...], k1[rs, :]], axis=0),
                            jnp.concatenate([vp1[...], v1[rs, :]], axis=0), bias_edge)
        ks = slice((b - 1) * Q_BLOCK, (b + 1) * Q_BLOCK)
        return lambda: (q1[rs, :], k1[ks, :], v1[ks, :], bias_in)

    def finish1(b, acc, l, m):
        rs = slice(b * Q_BLOCK, (b + 1) * Q_BLOCK)
        acc, l, _ = merged(acc, l, m, m_n[rs, :], l_n[rs, :], a_n[rs, :])
        o = acc / l
        sq = o * o
        hi = sq.astype(BF16)
        return o, jnp.concatenate([hi, (sq - hi.astype(F32)).astype(BF16)], axis=1)

    def post1(b, res):
        o, sq2 = res
        rs = slice(b * Q_BLOCK, (b + 1) * Q_BLOCK)
        ssq = jnp.dot(sq2, ssq_rhs, preferred_element_type=F32)
        o_ref[rs, :] = (o * lax.rsqrt(ssq * (1.0 / AT_HD) + EPS) * anw_ref[...]
                        * z_ref[rs, :].astype(F32)).astype(BF16)

    run([load1(b) for b in range(SUPER // Q_BLOCK)], finish1, post1)


def _attn_call(aq, ak, av, zat, anw_row):
    s = zat.shape[0]
    d4 = DILATIONS[1]
    wl = ATT_PAIRS * LANES
    per_super = SUPER // Q_BLOCK
    cur = pl.BlockSpec((SUPER, wl), lambda m, g: (m, g))
    prev16 = pl.BlockSpec((SUPER, wl), lambda m, g: (jnp.maximum(m - 1, 0), g))
    prev1 = pl.BlockSpec((Q_BLOCK, wl), lambda m, g: (jnp.maximum(m * per_super - 1, 0), g))
    prev4 = [pl.BlockSpec((Q_BLOCK, wl),
                          lambda m, g, r=r: (jnp.maximum((m - 1) * per_super + (r + 1) * (per_super // d4) - 1, 0), g))
             for r in range(d4)]
    state = pltpu.VMEM((SUPER, LANES), F32)
    return pl.pallas_call(
        _attn_body,
        out_shape=jax.ShapeDtypeStruct((s, AT_W), BF16),
        grid=(s // SUPER, AT_W // wl),
        in_specs=[cur] * 9 + [prev1, prev1] + prev4 + prev4 + [prev16, prev16, cur,
                                                               pl.BlockSpec((1, wl), lambda m, g: (0, g))],
        out_specs=cur,
        scratch_shapes=[state] * 6,
        compiler_params=_params(("arbitrary", "arbitrary")),
        name="attn",
    )(*aq, *ak, *av, ak[0], av[0], *([ak[1]] * d4), *([av[1]] * d4), ak[2], av[2], zat, anw_row)


def _out_body(x_hbm, odn_hbm, oat_hbm, gate_ref, w_ref, fnw_ref, o_hbm, xb, db, ab, ob, in_sem, out_sem):
    tm = xb.shape[1]
    ntile = x_hbm.shape[0] // tm

    def in_copies(t, slot):
        rows = pl.ds(pl.multiple_of(t * tm, tm), tm)
        return (pltpu.make_async_copy(x_hbm.at[rows, :], xb.at[slot], in_sem.at[0, slot]),
                pltpu.make_async_copy(odn_hbm.at[rows, :], db.at[slot], in_sem.at[1, slot]),
                pltpu.make_async_copy(oat_hbm.at[rows, :], ab.at[slot], in_sem.at[2, slot]))

    def out_copy(t, slot):
        rows = pl.ds(pl.multiple_of(t * tm, tm), tm)
        return pltpu.make_async_copy(ob.at[slot], o_hbm.at[rows, :], out_sem.at[slot])

    for t in range(OUT_IN_BUFS):
        for cp in in_copies(t, t):
            cp.start()

    def step(t, carry):
        slot = t % OUT_IN_BUFS
        oslot = t % 2
        for cp in in_copies(t, slot):
            cp.wait()

        @pl.when(t >= 2)
        def _():
            out_copy(t - 2, oslot).wait()

        mixin = jnp.concatenate([db[slot], ab[slot]], axis=1)
        mix = jnp.dot(mixin, w_ref[...], preferred_element_type=F32)
        y = xb[slot] + gate_ref[...] * mix
        ms = jnp.mean(y * y, axis=-1, keepdims=True)
        ob[oslot] = y * lax.rsqrt(ms + EPS) * fnw_ref[...]
        out_copy(t, oslot).start(priority=1)

        @pl.when(t + OUT_IN_BUFS < ntile)
        def _():
            for cp in in_copies(t + OUT_IN_BUFS, slot):
                cp.start()

        return carry

    lax.fori_loop(0, ntile, step, 0)
    out_copy(ntile - 2, (ntile - 2) % 2).wait()
    out_copy(ntile - 1, (ntile - 1) % 2).wait()


def _out_call(x2, odn, oat, gate, w_out, fnw_row, tm):
    s, d = x2.shape
    assert s // tm >= max(OUT_IN_BUFS, 2)
    hbm = pl.BlockSpec(memory_space=pl.ANY)
    vmem = pl.BlockSpec(memory_space=pltpu.VMEM)
    return pl.pallas_call(
        _out_body,
        out_shape=jax.ShapeDtypeStruct((s, d), F32),
        in_specs=[hbm, hbm, hbm, vmem, vmem, vmem],
        out_specs=hbm,
        scratch_shapes=[pltpu.VMEM((OUT_IN_BUFS, tm, d), F32), pltpu.VMEM((OUT_IN_BUFS, tm, DN_W), BF16),
                        pltpu.VMEM((OUT_IN_BUFS, tm, AT_W), BF16), pltpu.VMEM((2, tm, d), F32),
                        pltpu.SemaphoreType.DMA((3, OUT_IN_BUFS)), pltpu.SemaphoreType.DMA((2,))],
        compiler_params=pltpu.CompilerParams(vmem_limit_bytes=VMEM_LIMIT),
        name="out",
    )(x2, odn, oat, gate, w_out, fnw_row)


def kernel(x, c, positions, w_mod, b_mod, norm_w, w_in, conv_w, a_log, dt_bias, dn_norm_w, at_norm_w,
           w_out, final_norm_w):
    b, s, d = x.shape
    assert b == 1 and d == D_MODEL and w_mod.shape[0] == 1
    x2 = x.reshape(s, d)

    mod = _mod_call(c.reshape(d, 1), w_mod[0], b_mod[0].reshape(1, 3 * d))
    shift, scale, gate = mod[:, 0:d], mod[:, d:2 * d], mod[:, 2 * d:3 * d]

    w_all, w_out_b = _wpack_call(w_in, w_out)
    pad_heads = lambda v: jnp.pad(v.reshape(1, DN_HEADS), ((0, 0), (DN_HEADS, LANES - 2 * DN_HEADS)))
    half = AT_HD // 2
    invf = (ROPE_THETA ** (-jnp.arange(half, dtype=F32) / half)).reshape(half, 1)
    sgn = jnp.where(jnp.arange(LANES) < LANES // 2, -1.0, 1.0).astype(F32).reshape(1, LANES)

    assert s % SUPER == 0
    outs = _proj_call(x2, positions.reshape(1, s), norm_w[0].reshape(1, d), scale, shift, w_all, conv_w[0],
                      pad_heads(a_log[0]), pad_heads(dt_bias[0]), invf, sgn, tm=512)
    qn, kn, v, zdn, bg, zat = outs[0:6]
    flat = lambda a: a.reshape(s, AT_W)
    aq, ak, av = [[flat(a) for a in outs[j:j + 3]] for j in (6, 9, 12)]

    odn = _dn_call(qn, kn, v, bg, zdn, dn_norm_w[0].reshape(1, DN_HD), rows=512)
    oat = _attn_call(aq, ak, av, zat, jnp.tile(at_norm_w[0], AT_HEADS).reshape(1, AT_W))

    out = _out_call(x2, odn, oat, gate, w_out_b, final_norm_w.reshape(1, d), tm=1024)
    return out.reshape(b, s, d)
```

```python
import math

import jax
import jax.numpy as jnp
from jax import lax
from jax.experimental import pallas as pl
from jax.experimental.pallas import tpu as pltpu

D_MODEL = 1024
DN_HEADS = 4
DN_HD = 128
DN_W = DN_HEADS * DN_HD
AT_HEADS = 8
AT_HD = 64
AT_W = AT_HEADS * AT_HD
CONV_K = 4
CHUNK = 64
DILATIONS = (1, 4, 16)
W_SUB = 128
Q_BLOCK = 128
SUPER = Q_BLOCK * DILATIONS[-1]
OUT_IN_BUFS = 3
ATT_PAIRS = 2
ATT_LAG = 2
ROPE_THETA = 10000.0
EPS = 1e-6
NEG = -1e30

LANES = 128
SUBLANES = 8
MXU_COLS = 256
PROJ_COLS = 512
VMEM_LIMIT = 48 * 1024 * 1024

C_QKV, C_Z, C_AQ, C_AK, C_AV, C_AZ, C_BA = 0, 1536, 2048, 2560, 3072, 3584, 4096
W_COLS = C_BA + LANES

F32 = jnp.float32
BF16 = jnp.bfloat16
NT_DIMS = (((1,), (1,)), ((), ()))


def _sigmoid(v):
    return 1.0 / (1.0 + jnp.exp2(v * (-math.log2(math.e))))


def _iota(shape, dim):
    return lax.broadcasted_iota(jnp.int32, shape, dim)


def _params(sem):
    return pltpu.CompilerParams(dimension_semantics=sem, vmem_limit_bytes=VMEM_LIMIT)


def _mod_body(c_ref, w_ref, b_ref, o_ref):
    c = c_ref[...]
    sc = c * _sigmoid(c)
    o_ref[...] = jnp.sum(sc * w_ref[...], axis=0, keepdims=True) + b_ref[...]


def _mod_call(c_col, w_mod, b_mod):
    d, n = w_mod.shape
    tn = 1024
    return pl.pallas_call(
        _mod_body,
        out_shape=jax.ShapeDtypeStruct((1, n), F32),
        grid=(n // tn,),
        in_specs=[pl.BlockSpec((d, 1), lambda j: (0, 0)),
                  pl.BlockSpec((d, tn), lambda j: (0, j)),
                  pl.BlockSpec((1, tn), lambda j: (0, j))],
        out_specs=pl.BlockSpec((1, tn), lambda j: (0, j)),
        compiler_params=_params(("arbitrary",)),
        name="mod",
    )(c_col, w_mod, b_mod)


def _wpack_body(wt_ref, wo_ref, o_ref, oo_ref):
    kb = wt_ref.shape[2]
    o_b = 4 * DN_W
    o_q = o_b + 2 * DN_HEADS
    half = AT_HD // 2

    def put(col0, src):
        o_ref[:, col0:col0 + LANES] = src.T.astype(BF16)

    for g in range(o_b // LANES):
        put(C_QKV + g * LANES, wt_ref[0, g * LANES:(g + 1) * LANES, :])
    for g in range(2 * AT_W // LANES):
        base = o_q + g * LANES
        put(C_AQ + g * LANES, jnp.concatenate([wt_ref[0, base:base + half, :],
                                               wt_ref[0, base + 2 * half:base + 3 * half, :],
                                               wt_ref[0, base + half:base + 2 * half, :],
                                               wt_ref[0, base + 3 * half:base + 4 * half, :]], axis=0))
    for g in range(2 * AT_W // LANES):
        base = o_q + 2 * AT_W + g * LANES
        put(C_AV + g * LANES, wt_ref[0, base:base + LANES, :])
    put(C_BA, jnp.concatenate([wt_ref[0, o_b:o_b + 2 * DN_HEADS, :],
                               jnp.zeros((LANES - 2 * DN_HEADS, kb), F32)], axis=0))
    oo_ref[...] = wo_ref[0].astype(BF16)


def _wpack_call(w_in, w_out):
    _, d, n_in = w_in.shape
    n_out = w_out.shape[2]
    kb = 512
    w_t = jnp.swapaxes(w_in, 1, 2)
    return pl.pallas_call(
        _wpack_body,
        out_shape=[jax.ShapeDtypeStruct((d, W_COLS), BF16), jax.ShapeDtypeStruct((w_out.shape[1], n_out), BF16)],
        grid=(d // kb,),
        in_specs=[pl.BlockSpec((1, n_in, kb), lambda i: (0, 0, i)),
                  pl.BlockSpec((1, kb, n_out), lambda i: (0, i, 0))],
        out_specs=[pl.BlockSpec((kb, W_COLS), lambda i: (i, 0)), pl.BlockSpec((kb, n_out), lambda i: (i, 0))],
        compiler_params=_params(("arbitrary",)),
        name="wpack",
    )(w_t, w_out)


def _proj_body(x_ref, pos_ref, nw_ref, sc_ref, sh_ref, w_ref, cw_ref, alog_ref, dtb_ref, invf_ref,
               sgn_ref, qn_ref, kn_ref, v_ref, zdn_ref, bg_ref, zat_ref,
               aq1_ref, aq4_ref, aq16_ref, ak1_ref, ak4_ref, ak16_ref, av1_ref, av4_ref, av16_ref,
               cbuf_ref, hb_ref, pbuf_ref, p4buf_ref):
    i = pl.program_id(0)
    tm = x_ref.shape[0]
    x = x_ref[...]
    ms = jnp.mean(x * x, axis=-1, keepdims=True)
    hb_ref[...] = ((x * lax.rsqrt(ms + EPS)) * (nw_ref[...] * (1.0 + sc_ref[...])) + sh_ref[...]).astype(BF16)

    def proj(c0, width=PROJ_COLS):
        return jnp.dot(hb_ref[...], w_ref[:, c0:c0 + width], preferred_element_type=F32)

    def silu(v):
        return v * _sigmoid(v)

    @pl.when(i == 0)
    def _():
        cbuf_ref[:, 0:SUBLANES, :] = jnp.zeros((cbuf_ref.shape[0], SUBLANES, LANES), F32)

    dyn0 = jnp.minimum(i, 0)
    for c0 in range(0, 3 * DN_W, PROJ_COLS):
        xg = proj(C_QKV + c0)
        halves = []
        for hh in range(PROJ_COLS // LANES):
            grp = c0 // LANES + hh
            ls = slice(c0 + hh * LANES, c0 + (hh + 1) * LANES)
            xh = xg[:, hh * LANES:(hh + 1) * LANES]
            cbuf_ref[grp, SUBLANES:SUBLANES + tm, :] = xh
            acc = xh * cw_ref[CONV_K - 1:CONV_K, ls]
            for j in range(1, CONV_K):
                acc = acc + cbuf_ref[grp, pl.ds(dyn0 + SUBLANES - j, tm), :] * cw_ref[CONV_K - 1 - j:CONV_K - j, ls]
            cbuf_ref[grp, 0:SUBLANES, :] = xh[tm - SUBLANES:tm]
            halves.append(acc)
        conv = jnp.concatenate(halves, axis=1)
        y = silu(conv)
        kind, off = c0 // DN_W, c0 % DN_W
        for hh in range(PROJ_COLS // DN_HD):
            yh = y[:, hh * DN_HD:(hh + 1) * DN_HD]
            dst = slice(off + hh * DN_HD, off + (hh + 1) * DN_HD)
            if kind == 2:
                v_ref[:, dst] = yh.astype(BF16)
            else:
                inv = lax.rsqrt(jnp.sum(yh * yh, axis=-1, keepdims=True) + EPS)
                if kind == 0:
                    qn_ref[:, dst] = (yh * (inv * (DN_HD ** -0.5))).astype(BF16)
                else:
                    kn_ref[:, dst] = (yh * inv).astype(BF16)

    for c0 in range(0, DN_W, PROJ_COLS):
        zdn_ref[:, c0:c0 + PROJ_COLS] = silu(proj(C_Z + c0)).astype(BF16)
        zat_ref[:, c0:c0 + PROJ_COLS] = silu(proj(C_AZ + c0)).astype(BF16)

    ba = proj(C_BA, LANES)
    lane = _iota(ba.shape, 1)
    a = ba + dtb_ref[...]
    softplus = jnp.maximum(a, 0.0) + jnp.log(1.0 + jnp.exp(-jnp.abs(a)))
    g = -jnp.exp(alog_ref[...]) * softplus
    bg_ref[...] = jnp.where(lane < DN_HEADS, _sigmoid(ba), jnp.where(lane < 2 * DN_HEADS, g, 0.0))

    ang = invf_ref[...] * pos_ref[...].astype(F32)
    reps = LANES // (AT_HD // 2)
    cos = jnp.concatenate([jnp.cos(ang)] * reps, axis=0).T
    sin = jnp.concatenate([jnp.sin(ang)] * reps, axis=0).T * sgn_ref[...]
    qscale = (AT_HD ** -0.5) * math.log2(math.e)
    d4, d16 = DILATIONS[1], DILATIONS[2]
    n4, n16 = tm // d4, tm // d16

    def emit(val, gp, nat_ref, p4_ref, p16_ref):
        sl = slice(gp * LANES, (gp + 1) * LANES)
        nat_ref[:, sl] = val.astype(BF16)
        pbuf_ref[gp] = val
        for c in range(d4):
            seg = pbuf_ref[gp, pl.ds(c, n4, stride=d4), :]
            p4_ref[0, c, :, sl] = seg.astype(BF16)
            p4buf_ref[gp, c * n4:(c + 1) * n4, :] = seg
        for r in range(d16):
            seg = p4buf_ref[gp, pl.ds((r % d4) * n4 + r // d4, n16, stride=d4), :]
            p16_ref[0, r, :, sl] = seg.astype(BF16)

    def rope(v):
        return v * cos + pltpu.roll(v, LANES // 2, axis=1) * sin

    for c0 in range(0, AT_W, PROJ_COLS):
        aq = proj(C_AQ + c0)
        ak = proj(C_AK + c0)
        av = proj(C_AV + c0)
        for hh in range(PROJ_COLS // LANES):
            gp = c0 // LANES + hh
            ls = slice(hh * LANES, (hh + 1) * LANES)
            emit(rope(aq[:, ls]) * qscale, gp, aq1_ref, aq4_ref, aq16_ref)
            emit(rope(ak[:, ls]), gp, ak1_ref, ak4_ref, ak16_ref)
            emit(av[:, ls], gp, av1_ref, av4_ref, av16_ref)


def _proj_call(x2, pos_row, nw, scale, shift, w_all, cw, alog_row, dtb_row, invf, sgn, tm):
    s, d = x2.shape
    row = lambda n: pl.BlockSpec((1, n), lambda i: (0, 0))
    full = lambda a: pl.BlockSpec(a.shape, lambda i: (0, 0))
    tile = lambda n: pl.BlockSpec((tm, n), lambda i: (i, 0))
    spb = SUPER // tm
    res_shape = lambda dil: jax.ShapeDtypeStruct((s // SUPER, dil, SUPER // dil, AT_W), BF16)
    res_spec = lambda dil: pl.BlockSpec((1, dil, tm // dil, AT_W), lambda i: (i // spb, 0, i % spb, 0))
    bf = lambda n: jax.ShapeDtypeStruct((s, n), BF16)
    outs = [bf(DN_W)] * 4 + [jax.ShapeDtypeStruct((s, LANES), F32), bf(AT_W)] + \
           [bf(AT_W), res_shape(DILATIONS[1]), res_shape(DILATIONS[2])] * 3
    out_specs = [tile(DN_W)] * 4 + [tile(LANES), tile(AT_W)] + \
                [tile(AT_W), res_spec(DILATIONS[1]), res_spec(DILATIONS[2])] * 3
    return pl.pallas_call(
        _proj_body,
        out_shape=outs,
        grid=(s // tm,),
        in_specs=[tile(d), pl.BlockSpec((1, tm), lambda i: (0, i)), row(d), row(d), row(d),
                  pl.BlockSpec(w_all.shape, lambda i: (0, 0), pipeline_mode=pl.Buffered(1)),
                  full(cw), row(LANES), row(LANES), full(invf), row(LANES)],
        out_specs=out_specs,
        scratch_shapes=[pltpu.VMEM((3 * DN_W // LANES, SUBLANES + tm, LANES), F32), pltpu.VMEM((tm, d), BF16),
                        pltpu.VMEM((AT_W // LANES, tm, LANES), F32),
                        pltpu.VMEM((AT_W // LANES, tm, LANES), F32)],
        compiler_params=_params(("arbitrary",)),
        name="proj",
    )(x2, pos_row, nw, scale, shift, w_all, cw, alog_row, dtb_row, invf, sgn)


def _dn_body(q_ref, k_ref, v_ref, bg_ref, z_ref, nw_ref, o_ref, w_s, bt_s, op_s, gl_s, st_ref):
    step = pl.program_id(0)
    cur = step % 2
    prv = 1 - cur

    @pl.when(step == 0)
    def _():
        st_ref[...] = jnp.zeros(st_ref.shape, F32)
        w_s[1] = jnp.zeros(w_s.shape[1:], BF16)
        bt_s[1] = jnp.zeros(bt_s.shape[1:], BF16)
        op_s[1] = jnp.zeros(op_s.shape[1:], BF16)
        gl_s[1] = jnp.zeros(gl_s.shape[1:], F32)

    rows = q_ref.shape[0]
    nchunk = rows // CHUNK
    hsl = lambda hd: slice(hd * DN_HD, (hd + 1) * DN_HD)

    def scan_chunk(c):
        rs = slice(c * CHUNK, (c + 1) * CHUNK)
        for hd in range(DN_HEADS):
            st = st_ref[hd]
            r = jnp.dot(st.astype(BF16), w_s[prv, c, hd], preferred_element_type=F32)
            st_ref[hd] = gl_s[prv, c, hd:hd + 1, :] * st - r[:, 0:DN_HD] + bt_s[prv, c, hd].astype(F32)
            half = (hd % 2) * CHUNK
            o = r[:, DN_HD:2 * DN_HD].T[half:half + CHUNK] + op_s[prv, rs, hsl(hd)].astype(F32)
            ms = jnp.mean(o * o, axis=-1, keepdims=True)
            o_ref[rs, hsl(hd)] = (o * lax.rsqrt(ms + EPS) * nw_ref[...] * z_ref[rs, hsl(hd)].astype(F32)).astype(BF16)

    pending = list(range(nchunk))

    def tick():
        if pending:
            scan_chunk(pending.pop(0))

    wide = DN_HEADS * CHUNK
    bg = bg_ref[...]
    cs_rows = min(rows, MXU_COLS)
    ri = _iota((cs_rows, cs_rows), 0)
    ci = _iota((cs_rows, cs_rows), 1)
    tril_blk = jnp.where((ri // CHUNK == ci // CHUNK) & (ri >= ci), 1.0, 0.0).astype(BF16)
    g_hi = bg.astype(BF16)
    g_r1 = bg - g_hi.astype(F32)
    g_mid = g_r1.astype(BF16)
    g_lo = (g_r1 - g_mid.astype(F32)).astype(BF16)
    g3 = jnp.concatenate([g_hi, g_mid, g_lo], axis=1)
    csum = jnp.concatenate([jnp.dot(tril_blk, g3[r0:r0 + cs_rows], preferred_element_type=F32)
                            for r0 in range(0, rows, cs_rows)], axis=0)
    gcum = csum[:, 0:LANES] + csum[:, LANES:2 * LANES] + csum[:, 2 * LANES:3 * LANES]
    bfull = jnp.concatenate([jnp.broadcast_to(bg[:, hd:hd + 1], (rows, DN_HD)) for hd in range(DN_HEADS)], axis=1)
    gfull = jnp.concatenate([jnp.broadcast_to(gcum[:, DN_HEADS + hd:DN_HEADS + hd + 1], (rows, DN_HD))
                             for hd in range(DN_HEADS)], axis=1)
    eg = jnp.exp(gfull)
    q = q_ref[...].astype(F32)
    k = k_ref[...].astype(F32)
    kb = k * bfull
    vb = v_ref[...].astype(F32) * bfull
    q_dec = q * eg
    kbg = kb * eg

    r64 = _iota((CHUNK, wide), 0)
    j64 = _iota((CHUNK, wide), 1) % CHUNK
    tril = r64 >= j64
    strict = r64 > j64
    eye = jnp.where(r64 == j64, 1.0, 0.0).astype(F32)
    low_half = _iota((CHUNK, LANES), 1) < CHUNK
    bd64 = _iota((wide, wide), 0) // CHUNK == _iota((wide, wide), 1) // CHUNK
    bd_k = _iota((wide, DN_W), 0) // CHUNK == _iota((wide, DN_W), 1) // DN_HD

    def block_diag(m):
        return jnp.where(bd64, jnp.concatenate([m.astype(BF16)] * DN_HEADS, axis=0), jnp.zeros((wide, wide), BF16))

    rsl = lambda c: slice(c * CHUNK, (c + 1) * CHUNK)
    chunks = range(nchunk)
    g_last = [gfull[c * CHUNK + CHUNK - 1:(c + 1) * CHUNK, :] for c in chunks]
    k_dec = [k[rsl(c)] * jnp.exp(g_last[c] - gfull[rsl(c)]) for c in chunks]
    g_row = [jnp.concatenate([gcum[rsl(c)], gcum[rsl(c)]], axis=0).T for c in chunks]
    dec = []
    for c in chunks:
        halves = []
        for pr in range(DN_HEADS // 2):
            ha, hb = 2 * pr, 2 * pr + 1
            col = jnp.where(low_half, gfull[rsl(c), hsl(ha)], gfull[rsl(c), hsl(hb)])
            row = jnp.where(low_half[0:1], g_row[c][DN_HEADS + ha:DN_HEADS + ha + 1, :],
                            g_row[c][DN_HEADS + hb:DN_HEADS + hb + 1, :])
            halves.append(col - row)
        dec.append(jnp.exp(jnp.where(tril, jnp.concatenate(halves, axis=1), NEG)))
    kq = [lax.dot_general(jnp.concatenate([kb[rsl(c)].astype(BF16), q_ref[rsl(c), :]], axis=0),
                          jnp.where(bd_k, jnp.concatenate([k_ref[rsl(c), :]] * DN_HEADS, axis=0),
                                    jnp.zeros((wide, DN_W), BF16)),
                          NT_DIMS, preferred_element_type=F32) for c in chunks]
    tick()
    a_low = [jnp.where(strict, m[0:CHUNK] * d, 0.0) for m, d in zip(kq, dec)]
    attn = [jnp.where(tril, m[CHUNK:2 * CHUNK] * d, 0.0) for m, d in zip(kq, dec)]
    bmat = [-a for a in a_low]
    ymat = [eye + b for b in bmat]
    bmat = [jnp.dot(b.astype(BF16), block_diag(b), preferred_element_type=F32) for b in bmat]
    tick()
    for _ in range(4):
        prod = [jnp.dot(jnp.concatenate([y, b], axis=0).astype(BF16), block_diag(b), preferred_element_type=F32)
                for y, b in zip(ymat, bmat)]
        tick()
        ymat = [y + p[0:CHUNK] for y, p in zip(ymat, prod)]
        bmat = [p[CHUNK:2 * CHUNK] for p in prod]
    ymat = [y + jnp.dot(y.astype(BF16), block_diag(b), preferred_element_type=F32) for y, b in zip(ymat, bmat)]
    tick()
    rhs = [jnp.concatenate([jnp.concatenate([kbg[rsl(c), hsl(hd)], vb[rsl(c), hsl(hd)]], axis=1)
                            for hd in range(DN_HEADS)], axis=0).astype(BF16) for c in chunks]
    wu = [jnp.dot(block_diag(y), r, preferred_element_type=F32) for y, r in zip(ymat, rhs)]
    tick()
    wut = [m.T.astype(BF16) for m in wu]
    kdbd = [jnp.where(bd_k, jnp.concatenate([k_dec[c].astype(BF16)] * DN_HEADS, axis=0),
                      jnp.zeros((wide, DN_W), BF16)) for c in chunks]
    pbt = [jnp.dot(a, b, preferred_element_type=F32) for a, b in zip(wut, kdbd)]
    tick()
    aw = [jnp.dot(block_diag(a), m.astype(BF16), preferred_element_type=F32) for a, m in zip(attn, wu)]
    qpt = [(jnp.concatenate([q_dec[rsl(c), hsl(hd)] for hd in range(DN_HEADS)], axis=0) - a[:, 0:DN_HD]).T
           .astype(BF16) for c, a in zip(chunks, aw)]
    while pending:
        tick()
    for c in chunks:
        for hd in range(DN_HEADS):
            pair = slice((hd // 2) * LANES, (hd // 2 + 1) * LANES)
            w_s[cur, c, hd] = jnp.concatenate([pbt[c][0:DN_HD, hsl(hd)].astype(BF16), qpt[c][:, pair]], axis=1)
            bt_s[cur, c, hd] = pbt[c][DN_HD:2 * DN_HD, hsl(hd)].astype(BF16)
            op_s[cur, rsl(c), hsl(hd)] = aw[c][hd * CHUNK:(hd + 1) * CHUNK, DN_HD:2 * DN_HD].astype(BF16)
        gl_rows = [jnp.exp(g_last[c][:, hsl(hd)]) for hd in range(DN_HEADS)]
        gl_rows.append(jnp.zeros((SUBLANES - DN_HEADS, DN_HD), F32))
        gl_s[cur, c] = jnp.concatenate(gl_rows, axis=0)


def _dn_call(qn, kn, v, bg, zdn, nw_row, rows):
    s = qn.shape[0]
    nblk = s // rows
    ncb = rows // CHUNK
    cur = lambda n: pl.BlockSpec((rows, n), lambda i: (jnp.minimum(i, nblk - 1), 0))
    prv = lambda n: pl.BlockSpec((rows, n), lambda i: (jnp.maximum(i - 1, 0), 0))
    return pl.pallas_call(
        _dn_body,
        out_shape=jax.ShapeDtypeStruct((s, DN_W), BF16),
        grid=(nblk + 1,),
        in_specs=[cur(DN_W), cur(DN_W), cur(DN_W), cur(LANES), prv(DN_W), pl.BlockSpec((1, DN_HD), lambda i: (0, 0))],
        out_specs=prv(DN_W),
        scratch_shapes=[pltpu.VMEM((2, ncb, DN_HEADS, DN_HD, 2 * DN_HD), BF16),
                        pltpu.VMEM((2, ncb, DN_HEADS, DN_HD, DN_HD), BF16),
                        pltpu.VMEM((2, rows, DN_W), BF16),
                        pltpu.VMEM((2, ncb, SUBLANES, DN_HD), F32),
                        pltpu.VMEM((DN_HEADS, DN_HD, DN_HD), F32)],
        compiler_params=_params(("arbitrary",)),
        name="deltanet",
    )(qn, kn, v, bg, zdn, nw_row)


def _attn_body(*refs):
    ins, scratch = refs[:-6], refs[-6:]
    for p in range(ATT_PAIRS):
        _attn_pair(*[r.at[:, p * LANES:(p + 1) * LANES] for r in ins], *scratch)


def _attn_pair(q1, q4, q16, k1, k4, k16, v1, v4, v16, kp1, vp1, kp4a, kp4b, kp4c, kp4d,
               vp4a, vp4b, vp4c, vp4d, kp16, vp16, z_ref, anw_ref, o_ref, m_s, l_s, a_s, m_n, l_n, a_n):
    d4, d16 = DILATIONS[1], DILATIONS[2]
    seg4 = SUPER // d4
    kp4 = (kp4a, kp4b, kp4c, kp4d)
    vp4 = (vp4a, vp4b, vp4c, vp4d)
    kj = _iota((2 * Q_BLOCK, Q_BLOCK), 0)
    qi = _iota((2 * Q_BLOCK, Q_BLOCK), 1)
    rel = Q_BLOCK + qi - kj
    band = (rel >= 0) & (rel <= W_SUB)
    bias_in = jnp.where(band, 0.0, NEG).astype(BF16)
    bias_first = jnp.where(band & (kj >= Q_BLOCK), 0.0, NEG).astype(BF16)
    bias_edge = jnp.where(pl.program_id(0) == 0, bias_first, bias_in)
    one_hot = jnp.where(kj % Q_BLOCK == qi, 1.0, 0.0).astype(BF16)
    lane = _iota((Q_BLOCK, LANES), 1)
    lane_k = _iota((2 * Q_BLOCK, LANES), 1)
    head_a_q = (lane % AT_HD) < (AT_HD // 2)
    head_a = lane < AT_HD
    head_a_k = lane_k < AT_HD
    ro = _iota((4 * Q_BLOCK, LANES), 0)
    ones_rhs = jnp.where((ro < 2 * Q_BLOCK) == (_iota((4 * Q_BLOCK, LANES), 1) < AT_HD), 1.0, 0.0).astype(BF16)
    sr = _iota((2 * LANES, LANES), 0)
    ssq_rhs = jnp.where((sr % LANES) // AT_HD == _iota((2 * LANES, LANES), 1) // AT_HD, 1.0, 0.0).astype(BF16)
    zq = jnp.zeros((Q_BLOCK, LANES), BF16)
    zv = jnp.zeros((2 * Q_BLOCK, LANES), BF16)

    def scores(load):
        q, kb, vb, bias = load()
        q2 = jnp.concatenate([jnp.where(head_a_q, q, zq), jnp.where(head_a_q, zq, q)], axis=0)
        s = lax.dot_general(jnp.concatenate([q2, one_hot], axis=1), jnp.concatenate([kb, bias], axis=1),
                            NT_DIMS, preferred_element_type=F32)
        m = jnp.max(s, axis=-1, keepdims=True)
        return jnp.exp2(s - m).astype(BF16), m, vb

    def values(p, m, vb):
        pc = jnp.concatenate([p[0:Q_BLOCK], p[Q_BLOCK:2 * Q_BLOCK]], axis=1)
        rhs = jnp.concatenate([jnp.concatenate([jnp.where(head_a_k, vb, zv), jnp.where(head_a_k, zv, vb)], axis=0),
                               ones_rhs], axis=1)
        r = jnp.dot(pc, rhs, preferred_element_type=F32)
        return r[:, 0:LANES], r[:, LANES:2 * LANES], jnp.where(head_a, m[0:Q_BLOCK], m[Q_BLOCK:2 * Q_BLOCK])

    def run(loads, finish, post=None):
        n = len(loads)
        pending, done = {}, {}
        for i in range(n + ATT_LAG + 1):
            if i < n:
                pending[i] = scores(loads[i])
            j = i - ATT_LAG
            if 0 <= j < n:
                done[j] = finish(j, *values(*pending.pop(j)))
            if post is not None and 0 <= j - 1 < n:
                post(j - 1, done.pop(j - 1))

    def merged(acc, l, m, m_old, l_old, a_old):
        m_new = jnp.maximum(m_old, m)
        w_old = jnp.exp2(m_old - m_new)
        w_new = jnp.exp2(m - m_new)
        return a_old * w_old + acc * w_new, l_old * w_old + l * w_new, m_new

    def load16(r):
        rs = slice(r * Q_BLOCK, (r + 1) * Q_BLOCK)
        return lambda: (q16[rs, :], jnp.concatenate([kp16[rs, :], k16[rs, :]], axis=0),
                        jnp.concatenate([vp16[rs, :], v16[rs, :]], axis=0), bias_edge)

    def finish16(r, acc, l, m):
        rows = pl.ds((r % d4) * seg4 + r // d4, Q_BLOCK, stride=d4)
        a_s[rows, :] = acc
        l_s[rows, :] = l
        m_s[rows, :] = m

    run([load16(r) for r in range(d16)], finish16)

    def load4(r, b):
        base = r * seg4 + b * Q_BLOCK
        if b == 0:
            return lambda: (q4[base:base + Q_BLOCK, :],
                            jnp.concatenate([kp4[r][...], k4[base:base + Q_BLOCK, :]], axis=0),
                            jnp.concatenate([vp4[r][...], v4[base:base + Q_BLOCK, :]], axis=0), bias_edge)
        return lambda: (q4[base:base + Q_BLOCK, :], k4[base - Q_BLOCK:base + Q_BLOCK, :],
                        v4[base - Q_BLOCK:base + Q_BLOCK, :], bias_in)

    blocks4 = [(r, b) for b in range(seg4 // Q_BLOCK) for r in range(d4)]

    def finish4(j, acc, l, m):
        r, b = blocks4[j]
        rows = slice(r * seg4 + b * Q_BLOCK, r * seg4 + (b + 1) * Q_BLOCK)
        acc, l, m = merged(acc, l, m, m_s[rows, :], l_s[rows, :], a_s[rows, :])
        a_s[rows, :] = acc
        l_s[rows, :] = l
        m_s[rows, :] = m

    run([load4(r, b) for r, b in blocks4], finish4)

    for c in range(d4):
        src = slice(c * seg4, (c + 1) * seg4)
        dst = pl.ds(c, seg4, stride=d4)
        m_n[dst, :] = m_s[src, :]
        l_n[dst, :] = l_s[src, :]
        a_n[dst, :] = a_s[src, :]

    def load1(b):
        rs = slice(b * Q_BLOCK, (b + 1) * Q_BLOCK)
        if b == 0:
            return lambda: (q1[rs, :], jnp.concatenate([kp1[...], k1[rs, :]], axis=0),
                            jnp.concatenate([vp1[...], v1[rs, :]], axis=0), bias_edge)
        ks = slice((b - 1) * Q_BLOCK, (b + 1) * Q_BLOCK)
        return lambda: (q1[rs, :], k1[ks, :], v1[ks, :], bias_in)

    def finish1(b, acc, l, m):
        rs = slice(b * Q_BLOCK, (b + 1) * Q_BLOCK)
        acc, l, _ = merged(acc, l, m, m_n[rs, :], l_n[rs, :], a_n[rs, :])
        o = acc / l
        sq = o * o
        hi = sq.astype(BF16)
        return o, jnp.concatenate([hi, (sq - hi.astype(F32)).astype(BF16)], axis=1)

    def post1(b, res):
        o, sq2 = res
        rs = slice(b * Q_BLOCK, (b + 1) * Q_BLOCK)
        ssq = jnp.dot(sq2, ssq_rhs, preferred_element_type=F32)
        o_ref[rs, :] = (o * lax.rsqrt(ssq * (1.0 / AT_HD) + EPS) * anw_ref[...]
                        * z_ref[rs, :].astype(F32)).astype(BF16)

    run([load1(b) for b in range(SUPER // Q_BLOCK)], finish1, post1)


def _attn_call(aq, ak, av, zat, anw_row):
    s = zat.shape[0]
    d4 = DILATIONS[1]
    wl = ATT_PAIRS * LANES
    per_super = SUPER // Q_BLOCK
    cur = pl.BlockSpec((SUPER, wl), lambda m, g: (m, g))
    prev16 = pl.BlockSpec((SUPER, wl), lambda m, g: (jnp.maximum(m - 1, 0), g))
    prev1 = pl.BlockSpec((Q_BLOCK, wl), lambda m, g: (jnp.maximum(m * per_super - 1, 0), g))
    prev4 = [pl.BlockSpec((Q_BLOCK, wl),
                          lambda m, g, r=r: (jnp.maximum((m - 1) * per_super + (r + 1) * (per_super // d4) - 1, 0), g))
             for r in range(d4)]
    state = pltpu.VMEM((SUPER, LANES), F32)
    return pl.pallas_call(
        _attn_body,
        out_shape=jax.ShapeDtypeStruct((s, AT_W), BF16),
        grid=(s // SUPER, AT_W // wl),
        in_specs=[cur] * 9 + [prev1, prev1] + prev4 + prev4 + [prev16, prev16, cur,
                                                               pl.BlockSpec((1, wl), lambda m, g: (0, g))],
        out_specs=cur,
        scratch_shapes=[state] * 6,
        compiler_params=_params(("arbitrary", "arbitrary")),
        name="attn",
    )(*aq, *ak, *av, ak[0], av[0], *([ak[1]] * d4), *([av[1]] * d4), ak[2], av[2], zat, anw_row)


def _out_body(x_hbm, odn_hbm, oat_hbm, gate_ref, w_ref, fnw_ref, o_hbm, xb, db, ab, ob, in_sem, out_sem):
    tm = xb.shape[1]
    ntile = x_hbm.shape[0] // tm

    def in_copies(t, slot):
        rows = pl.ds(pl.multiple_of(t * tm, tm), tm)
        return (pltpu.make_async_copy(x_hbm.at[rows, :], xb.at[slot], in_sem.at[0, slot]),
                pltpu.make_async_copy(odn_hbm.at[rows, :], db.at[slot], in_sem.at[1, slot]),
                pltpu.make_async_copy(oat_hbm.at[rows, :], ab.at[slot], in_sem.at[2, slot]))

    def out_copy(t, slot):
        rows = pl.ds(pl.multiple_of(t * tm, tm), tm)
        return pltpu.make_async_copy(ob.at[slot], o_hbm.at[rows, :], out_sem.at[slot])

    for t in range(OUT_IN_BUFS):
        for n, cp in enumerate(in_copies(t, t)):
            cp.start(priority=min(n, 1))

    def step(t, carry):
        slot = t % OUT_IN_BUFS
        oslot = t % 2
        for cp in in_copies(t, slot):
            cp.wait()

        @pl.when(t >= 2)
        def _():
            out_copy(t - 2, oslot).wait()

        mixin = jnp.concatenate([db[slot], ab[slot]], axis=1)
        mix = jnp.dot(mixin, w_ref[...], preferred_element_type=F32)
        y = xb[slot] + gate_ref[...] * mix
        ms = jnp.mean(y * y, axis=-1, keepdims=True)
        ob[oslot] = y * lax.rsqrt(ms + EPS) * fnw_ref[...]
        out_copy(t, oslot).start(priority=1)

        @pl.when(t + OUT_IN_BUFS < ntile)
        def _():
            for n, cp in enumerate(in_copies(t + OUT_IN_BUFS, slot)):
                cp.start(priority=min(n, 1))

        return carry

    lax.fori_loop(0, ntile, step, 0)
    out_copy(ntile - 2, (ntile - 2) % 2).wait()
    out_copy(ntile - 1, (ntile - 1) % 2).wait()


def _out_call(x2, odn, oat, gate, w_out, fnw_row, tm):
    s, d = x2.shape
    assert s // tm >= max(OUT_IN_BUFS, 2)
    hbm = pl.BlockSpec(memory_space=pl.ANY)
    vmem = pl.BlockSpec(memory_space=pltpu.VMEM)
    return pl.pallas_call(
        _out_body,
        out_shape=jax.ShapeDtypeStruct((s, d), F32),
        in_specs=[hbm, hbm, hbm, vmem, vmem, vmem],
        out_specs=hbm,
        scratch_shapes=[pltpu.VMEM((OUT_IN_BUFS, tm, d), F32), pltpu.VMEM((OUT_IN_BUFS, tm, DN_W), BF16),
                        pltpu.VMEM((OUT_IN_BUFS, tm, AT_W), BF16), pltpu.VMEM((2, tm, d), F32),
                        pltpu.SemaphoreType.DMA((3, OUT_IN_BUFS)), pltpu.SemaphoreType.DMA((2,))],
        compiler_params=pltpu.CompilerParams(vmem_limit_bytes=VMEM_LIMIT),
        name="out",
    )(x2, odn, oat, gate, w_out, fnw_row)


def kernel(x, c, positions, w_mod, b_mod, norm_w, w_in, conv_w, a_log, dt_bias, dn_norm_w, at_norm_w,
           w_out, final_norm_w):
    b, s, d = x.shape
    assert b == 1 and d == D_MODEL and w_mod.shape[0] == 1
    x2 = x.reshape(s, d)

    mod = _mod_call(c.reshape(d, 1), w_mod[0], b_mod[0].reshape(1, 3 * d))
    shift, scale, gate = mod[:, 0:d], mod[:, d:2 * d], mod[:, 2 * d:3 * d]

    w_all, w_out_b = _wpack_call(w_in, w_out)
    pad_heads = lambda v: jnp.pad(v.reshape(1, DN_HEADS), ((0, 0), (DN_HEADS, LANES - 2 * DN_HEADS)))
    half = AT_HD // 2
    invf = (ROPE_THETA ** (-jnp.arange(half, dtype=F32) / half)).reshape(half, 1)
    sgn = jnp.where(jnp.arange(LANES) < LANES // 2, -1.0, 1.0).astype(F32).reshape(1, LANES)

    assert s % SUPER == 0
    outs = _proj_call(x2, positions.reshape(1, s), norm_w[0].reshape(1, d), scale, shift, w_all, conv_w[0],
                      pad_heads(a_log[0]), pad_heads(dt_bias[0]), invf, sgn, tm=512)
    qn, kn, v, zdn, bg, zat = outs[0:6]
    flat = lambda a: a.reshape(s, AT_W)
    aq, ak, av = [[flat(a) for a in outs[j:j + 3]] for j in (6, 9, 12)]

    odn = _dn_call(qn, kn, v, bg, zdn, dn_norm_w[0].reshape(1, DN_HD), rows=512)
    oat = _attn_call(aq, ak, av, zat, jnp.tile(at_norm_w[0], AT_HEADS).reshape(1, AT_W))

    out = _out_call(x2, odn, oat, gate, w_out_b, final_norm_w.reshape(1, d), tm=1024)
    return out.reshape(b, s, d)
```
